```python
import jax, jax.numpy as jnp
from jax import lax
import numpy as np

D_MODEL = 1024
BATCH = 2
SEQ = 8192
DEPTH = 1

D_POOL = D_MODEL // 2
POOL_WINDOWS = (2, 4, 8, 16)
N_POOL_GROUPS = len(POOL_WINDOWS)
POOL_GROUP_DIM = D_POOL // N_POOL_GROUPS
V_HEAD_DIM = 64
N_HEADS = (D_MODEL // 2) // V_HEAD_DIM
D_ATTN = N_HEADS * V_HEAD_DIM
D_MIX = D_POOL + D_ATTN
QK_NOPE_DIM = 64
QK_ROPE_DIM = 32
QK_DIM = QK_NOPE_DIM + QK_ROPE_DIM
Q_LORA_RANK = D_MODEL // 4
KV_LORA_RANK = D_MODEL // 8
ROPE_THETA = 10000.0
Q_BLOCK = 128
D_IN_PROJ = D_POOL + Q_LORA_RANK + KV_LORA_RANK + QK_ROPE_DIM
N_EXPERTS = 32
TOP_K = 4
D_EXPERT = D_MODEL
SWIGLU_ALPHA = 1.702
SWIGLU_LIMIT = 7.0
MOE_BLOCK = 128
RMS_EPS = 1e-6

kernel_name = 'hymba_pool_mla_moe_block'


def rmsnorm(x, g):
    xf = x.astype(jnp.float32)
    y = xf * lax.rsqrt(jnp.mean(xf * xf, axis=-1, keepdims=True) + RMS_EPS)
    return (y * g.astype(jnp.float32)).astype(x.dtype)


def rope_tables(positions):
    half = QK_ROPE_DIM // 2
    inv = ROPE_THETA ** (-jnp.arange(half, dtype=jnp.float32) / half)
    ang = positions.astype(jnp.float32)[..., None] * inv
    return jnp.cos(ang), jnp.sin(ang)


def apply_rope(x, cos, sin):
    half = QK_ROPE_DIM // 2
    xf = x.astype(jnp.float32)
    x1, x2 = xf[..., :half], xf[..., half:]
    out = jnp.concatenate([x1 * cos - x2 * sin, x2 * cos + x1 * sin], axis=-1)
    return out.astype(x.dtype)


def pool_mixer(u, w_pool, b_pool, pool_scale):
    B, S, _ = u.shape
    uf = u.reshape(B, S, N_POOL_GROUPS, POOL_GROUP_DIM).astype(jnp.float32)
    csum = jnp.pad(jnp.cumsum(uf, axis=1), ((0, 0), (1, 0), (0, 0), (0, 0)))
    t = jnp.arange(S)
    pooled = []
    for g, w in enumerate(POOL_WINDOWS):
        c = csum[:, :, g]
        lo = jnp.pad(c[:, :S + 1 - w], ((0, 0), (w - 1, 0), (0, 0)))
        cnt = jnp.minimum(t + 1, w).astype(jnp.float32)[None, :, None]
        pooled.append((c[:, 1:] - lo) / cnt)
    pooled = jnp.stack(pooled, axis=2)
    diff = (pooled - uf).astype(u.dtype)
    mixed = jnp.einsum('bsgc,gcd->bsgd', diff, w_pool) + b_pool
    return mixed.reshape(B, S, D_POOL) * pool_scale


def mla_attention(q_lat, kv_lat, k_rope_raw, cos, sin, g_q_a, w_q_b, g_kv_a, w_kv_b):
    B, S, _ = q_lat.shape
    q = (rmsnorm(q_lat, g_q_a) @ w_q_b).reshape(B, S, N_HEADS, QK_DIM)
    q_nope = q[..., :QK_NOPE_DIM]
    q_rope = apply_rope(q[..., QK_NOPE_DIM:], cos[:, :, None, :], sin[:, :, None, :])
    kv = (rmsnorm(kv_lat, g_kv_a) @ w_kv_b).reshape(B, S, N_HEADS, QK_NOPE_DIM + V_HEAD_DIM)
    k_nope, v = kv[..., :QK_NOPE_DIM], kv[..., QK_NOPE_DIM:]
    k_rope = apply_rope(k_rope_raw, cos, sin)
    scale = QK_DIM ** -0.5
    nb = S // Q_BLOCK
    qn_b = q_nope.reshape(B, nb, Q_BLOCK, N_HEADS, QK_NOPE_DIM).transpose(1, 0, 2, 3, 4)
    qr_b = q_rope.reshape(B, nb, Q_BLOCK, N_HEADS, QK_ROPE_DIM).transpose(1, 0, 2, 3, 4)
    k_idx = jnp.arange(S)

    def attend(args):
        qn, qr, blk = args
        s = (jnp.einsum('bqhd,bkhd->bhqk', qn, k_nope, preferred_element_type=jnp.float32)
             + jnp.einsum('bqhr,bkr->bhqk', qr, k_rope, preferred_element_type=jnp.float32)) * scale
        q_idx = blk * Q_BLOCK + jnp.arange(Q_BLOCK)
        s = jnp.where(k_idx[None, :] <= q_idx[:, None], s, -jnp.inf)
        p = jax.nn.softmax(s, axis=-1).astype(v.dtype)
        return jnp.einsum('bhqk,bkhd->bqhd', p, v)

    out = lax.map(attend, (qn_b, qr_b, jnp.arange(nb)))
    return out.transpose(1, 0, 2, 3, 4).reshape(B, S, D_ATTN)


def moe(h, w_router, b_router, w_gate_up, b_gate_up, w_down, b_down):
    B, S, D = h.shape
    T = B * S
    ht = h.reshape(T, D)
    logits = (ht @ w_router + b_router).astype(jnp.float32)
    top_vals, top_idx = lax.top_k(logits, TOP_K)
    gates = jax.nn.softmax(top_vals, axis=-1)
    n_assign = T * TOP_K
    e_flat = top_idx.reshape(-1).astype(jnp.int32)
    tok_flat = jnp.repeat(jnp.arange(T, dtype=jnp.int32), TOP_K)
    g_flat = gates.reshape(-1)
    order = jnp.argsort(e_flat)
    e_sorted = e_flat[order]
    counts = jnp.zeros((N_EXPERTS,), jnp.int32).at[e_flat].add(1)
    padded = ((counts + MOE_BLOCK - 1) // MOE_BLOCK) * MOE_BLOCK
    start = jnp.cumsum(counts) - counts
    pad_end = jnp.cumsum(padded)
    pad_start = pad_end - padded
    dest = pad_start[e_sorted] + (jnp.arange(n_assign, dtype=jnp.int32) - start[e_sorted])
    n_blocks = n_assign // MOE_BLOCK + N_EXPERTS
    n_rows = n_blocks * MOE_BLOCK
    row_tok = jnp.full((n_rows,), T, jnp.int32).at[dest].set(tok_flat[order])
    row_gate = jnp.zeros((n_rows,), jnp.float32).at[dest].set(g_flat[order])
    block_expert = jnp.minimum(
        jnp.searchsorted(pad_end, jnp.arange(n_blocks, dtype=jnp.int32) * MOE_BLOCK, side='right'),
        N_EXPERTS - 1)
    h_pad = jnp.concatenate([ht, jnp.zeros((1, D), ht.dtype)], axis=0)
    xb = h_pad[row_tok].reshape(n_blocks, MOE_BLOCK, D)

    def expert_block(args):
        xblk, e = args
        gu = xblk @ w_gate_up[e] + b_gate_up[e]
        gate = jnp.minimum(gu[:, :D_EXPERT], SWIGLU_LIMIT)
        up = jnp.clip(gu[:, D_EXPERT:], -SWIGLU_LIMIT, SWIGLU_LIMIT)
        act = gate * jax.nn.sigmoid(SWIGLU_ALPHA * gate) * (up + 1)
        return act @ w_down[e] + b_down[e]

    yb = lax.map(expert_block, (xb, block_expert)).reshape(n_rows, D)
    y = jnp.zeros((T + 1, D), h.dtype).at[row_tok].add(row_gate[:, None].astype(h.dtype) * yb)
    return y[:T].reshape(B, S, D)


def setup_inputs(seed: int = 0) -> dict:
    key = jax.random.key(seed)
    ks = jax.random.split(key, 24)
    L, D, E, F = DEPTH, D_MODEL, N_EXPERTS, D_EXPERT
    nrm = lambda k, shape, fan_in: jax.random.normal(k, shape, jnp.float32) * (fan_in ** -0.5)
    gain = lambda k, shape: 1.0 + 0.05 * jax.random.normal(k, shape, jnp.float32)
    small = lambda k, shape: 0.01 * jax.random.normal(k, shape, jnp.float32)
    x = jax.random.normal(ks[0], (BATCH, SEQ, D), jnp.float32)
    positions = jnp.broadcast_to(jnp.arange(SEQ, dtype=jnp.int32), (BATCH, SEQ))
    return {
        'x': x,
        'positions': positions,
        'g_attn_norm': gain(ks[1], (L, D)),
        'w_in': nrm(ks[2], (L, D, D_IN_PROJ), D),
        'w_pool': nrm(ks[3], (L, N_POOL_GROUPS, POOL_GROUP_DIM, POOL_GROUP_DIM), POOL_GROUP_DIM),
        'b_pool': small(ks[4], (L, N_POOL_GROUPS, POOL_GROUP_DIM)),
        'pool_scale': gain(ks[5], (L, D_POOL)),
        'g_q_a': gain(ks[6], (L, Q_LORA_RANK)),
        'w_q_b': nrm(ks[7], (L, Q_LORA_RANK, N_HEADS * QK_DIM), Q_LORA_RANK),
        'g_kv_a': gain(ks[8], (L, KV_LORA_RANK)),
        'w_kv_b': nrm(ks[9], (L, KV_LORA_RANK, N_HEADS * (QK_NOPE_DIM + V_HEAD_DIM)), KV_LORA_RANK),
        'g_out_pool': gain(ks[10], (L, D_POOL)),
        'g_out_attn': gain(ks[11], (L, D_ATTN)),
        'w_out': nrm(ks[12], (L, D_MIX, D), D_MIX),
        'g_ffn_norm': gain(ks[13], (L, D)),
        'w_router': nrm(ks[14], (L, D, E), D),
        'b_router': small(ks[15], (L, E)),
        'w_gate_up': nrm(ks[16], (L, E, D, 2 * F), D),
        'b_gate_up': small(ks[17], (L, E, 2 * F)),
        'w_down': nrm(ks[18], (L, E, F, D), F),
        'b_down': small(ks[19], (L, E, D)),
        'g_final': gain(ks[20], (D,)),
    }


def reference(x, positions, g_attn_norm, w_in, w_pool, b_pool, pool_scale, g_q_a, w_q_b,
              g_kv_a, w_kv_b, g_out_pool, g_out_attn, w_out, g_ffn_norm, w_router, b_router,
              w_gate_up, b_gate_up, w_down, b_down, g_final):
    cos, sin = rope_tables(positions)
    o1 = D_POOL
    o2 = o1 + Q_LORA_RANK
    o3 = o2 + KV_LORA_RANK
    for l in range(DEPTH):
        h = rmsnorm(x, g_attn_norm[l])
        proj = h @ w_in[l]
        u, q_lat, kv_lat, k_r = proj[..., :o1], proj[..., o1:o2], proj[..., o2:o3], proj[..., o3:]
        y_pool = pool_mixer(u, w_pool[l], b_pool[l], pool_scale[l])
        y_attn = mla_attention(q_lat, kv_lat, k_r, cos, sin, g_q_a[l], w_q_b[l], g_kv_a[l], w_kv_b[l])
        mix = jnp.concatenate([rmsnorm(y_pool, g_out_pool[l]), rmsnorm(y_attn, g_out_attn[l])], axis=-1)
        x = x + mix @ w_out[l]
        h = rmsnorm(x, g_ffn_norm[l])
        x = x + moe(h, w_router[l], b_router[l], w_gate_up[l], b_gate_up[l], w_down[l], b_down[l])
    return rmsnorm(x, g_final)
```

```python
import functools

import jax
import jax.numpy as jnp
from jax import lax
from jax.experimental import pallas as pl
from jax.experimental.pallas import tpu as pltpu

POOL_WINDOWS = (2, 4, 8, 16)
N_POOL_GROUPS = 4
N_HEADS = 8
V_HEAD_DIM = 64
QK_NOPE_DIM = 64
QK_ROPE_DIM = 32
QK_DIM = QK_NOPE_DIM + QK_ROPE_DIM
ROPE_THETA = 10000.0
N_EXPERTS = 32
TOP_K = 4
SWIGLU_ALPHA = 1.702
SWIGLU_LIMIT = 7.0
RMS_EPS = 1e-6

LANE = 128
HEAD_PAD = 128
POOL_HALO = 16

VMEM_LIMIT = 56 * 1024 * 1024

F32 = jnp.float32
BF16 = jnp.bfloat16


def _rms(x, g, n):
    ms = jnp.sum(x * x, axis=-1, keepdims=True) * (1.0 / n)
    return x * lax.rsqrt(ms + RMS_EPS) * g


def _proj_kernel(blocks_per_seq, x_ref, pos_ref, g_attn_ref, w_in_ref, inv_ref, wpool_ref,
                 bpool_ref, pscale_ref, gpool_ref, gq_ref, wq_ref, gkv_ref, wkv_ref,
                 pool_out, q_out, k_out, v_out, ext_scr):
    tm = x_ref.shape[0]
    d_model = x_ref.shape[1]
    d_pool = pool_out.shape[1]
    gdim = d_pool // N_POOL_GROUPS
    q_rank = gq_ref.shape[1]
    kv_rank = gkv_ref.shape[1]
    i = pl.program_id(0)
    blk_in_seq = i % blocks_per_seq

    x = x_ref[...]
    h = _rms(x, g_attn_ref[...], d_model).astype(BF16)
    proj = jnp.dot(h, w_in_ref[...], preferred_element_type=F32)
    u = proj[:, :d_pool]
    qlat = proj[:, d_pool:d_pool + q_rank]
    kvlat = proj[:, d_pool + q_rank:d_pool + q_rank + kv_rank]
    krb = proj[:, d_pool + q_rank + kv_rank:]

    @pl.when(blk_in_seq == 0)
    def _():
        ext_scr[0:POOL_HALO, :] = jnp.zeros((POOL_HALO, d_pool), F32)

    @pl.when(blk_in_seq != 0)
    def _():
        ext_scr[0:POOL_HALO, :] = ext_scr[tm:tm + POOL_HALO, :]

    ext_scr[POOL_HALO:, :] = u
    t_in_seq = (blk_in_seq * tm + lax.broadcasted_iota(jnp.int32, (tm, gdim), 0) + 1).astype(F32)
    ys = []
    ssq = jnp.zeros((tm, 1), F32)
    for g, w in enumerate(POOL_WINDOWS):
        a = ext_scr[:, g * gdim:(g + 1) * gdim]
        shift = 1
        while shift < w:
            a = a + pltpu.roll(a, shift, axis=0)
            shift *= 2
        win = a[POOL_HALO:, :]
        cnt = jnp.minimum(t_in_seq, float(w))
        ug = u[:, g * gdim:(g + 1) * gdim]
        diff = (win / cnt - ug).astype(BF16)
        mixed = jnp.dot(diff, wpool_ref[g], preferred_element_type=F32) + bpool_ref[g]
        y = mixed * pscale_ref[g]
        ssq = ssq + jnp.sum(y * y, axis=-1, keepdims=True)
        ys.append(y)
    rinv = lax.rsqrt(ssq * (1.0 / d_pool) + RMS_EPS)
    for g in range(N_POOL_GROUPS):
        pool_out[:, g * gdim:(g + 1) * gdim] = (ys[g] * rinv * gpool_ref[g]).astype(pool_out.dtype)

    pos = pos_ref[0].astype(F32)
    ang_t = inv_ref[...] * pos
    cos_t = jnp.cos(ang_t)
    sin_t = jnp.sin(ang_t)
    half = QK_ROPE_DIM // 2
    ones = jnp.ones((QK_NOPE_DIM, tm), F32)
    c_t = jnp.concatenate([ones, cos_t, cos_t, jnp.ones((LANE - QK_DIM, tm), F32)], axis=0)
    sa_t = jnp.concatenate([jnp.zeros((QK_NOPE_DIM + half, tm), F32), sin_t,
                            jnp.zeros((LANE - QK_DIM, tm), F32)], axis=0)
    sb_t = jnp.concatenate([jnp.zeros((QK_NOPE_DIM, tm), F32), sin_t,
                            jnp.zeros((LANE - QK_DIM + half, tm), F32)], axis=0)
    c = c_t.T
    sa = sa_t.T
    sb = sb_t.T

    def rope(blk, c_, sa_, sb_):
        return blk * c_ + pltpu.roll(blk, half, axis=1) * sa_ - pltpu.roll(blk, LANE - half, axis=1) * sb_

    qn = _rms(qlat, gq_ref[...], q_rank).astype(BF16)
    q = jnp.dot(qn, wq_ref[...], preferred_element_type=F32)
    scale = QK_DIM ** -0.5
    cq, saq, sbq = c * scale, sa * scale, sb * scale
    for hh in range(N_HEADS):
        sl = slice(hh * HEAD_PAD, (hh + 1) * HEAD_PAD)
        q_out[:, sl] = rope(q[:, sl], cq, saq, sbq).astype(q_out.dtype)

    kvn = _rms(kvlat, gkv_ref[...], kv_rank).astype(BF16)
    kv = jnp.dot(kvn, wkv_ref[...], preferred_element_type=F32)
    kr = rope(krb, c, sa, sb)
    for hh in range(N_HEADS):
        sl = slice(hh * HEAD_PAD, (hh + 1) * HEAD_PAD)
        k_out[:, sl] = (kv[:, sl] + kr).astype(k_out.dtype)
    v_out[...] = kv[:, N_HEADS * HEAD_PAD:].astype(v_out.dtype)


def _proj_call(x2, pos3, g_attn, w_in_p, inv_col, wpool, bpool, pscale, gpool, gq, wq_p, gkv, wkv_p,
               seq, tm):
    t, d = x2.shape
    d_pool = pscale.shape[0] * pscale.shape[2]
    hp = N_HEADS * HEAD_PAD
    nblk = t // tm
    const2 = lambda i: (0, 0)
    const3 = lambda i: (0, 0, 0)
    in_specs = [
        pl.BlockSpec((tm, d), lambda i: (i, 0)),
        pl.BlockSpec((1, 1, tm), lambda i: (i, 0, 0)),
        pl.BlockSpec(g_attn.shape, const2),
        pl.BlockSpec(w_in_p.shape, const2),
        pl.BlockSpec(inv_col.shape, const2),
        pl.BlockSpec(wpool.shape, const3),
        pl.BlockSpec(bpool.shape, const3),
        pl.BlockSpec(pscale.shape, const3),
        pl.BlockSpec(gpool.shape, const3),
        pl.BlockSpec(gq.shape, const2),
        pl.BlockSpec(wq_p.shape, const2),
        pl.BlockSpec(gkv.shape, const2),
        pl.BlockSpec(wkv_p.shape, const2),
    ]
    out_shape = [
        jax.ShapeDtypeStruct((t, d_pool), BF16),
        jax.ShapeDtypeStruct((t, hp), BF16),
        jax.ShapeDtypeStruct((t, hp), BF16),
        jax.ShapeDtypeStruct((t, hp), BF16),
    ]
    out_specs = [
        pl.BlockSpec((tm, d_pool), lambda i: (i, 0)),
        pl.BlockSpec((tm, hp), lambda i: (i, 0)),
        pl.BlockSpec((tm, hp), lambda i: (i, 0)),
        pl.BlockSpec((tm, hp), lambda i: (i, 0)),
    ]
    return pl.pallas_call(
        functools.partial(_proj_kernel, seq // tm),
        grid=(nblk,),
        in_specs=in_specs,
        out_specs=out_specs,
        out_shape=out_shape,
        scratch_shapes=[pltpu.VMEM((tm + POOL_HALO, d_pool), F32)],
        compiler_params=pltpu.CompilerParams(dimension_semantics=("arbitrary",),
                                             vmem_limit_bytes=VMEM_LIMIT),
        name="proj",
    )(x2, pos3, g_attn, w_in_p, inv_col, wpool, bpool, pscale, gpool, gq, wq_p, gkv, wkv_p)


def _attn_kernel(q_ref, k_ref, v_ref, o_ref, m_scr, l_scr, acc_scr):
    tq = q_ref.shape[0]
    tk = tq
    qi = pl.program_id(2)
    q = q_ref[...]
    m_scr[...] = jnp.full(m_scr.shape, -jnp.inf, F32)
    l_scr[...] = jnp.zeros(l_scr.shape, F32)
    acc_scr[...] = jnp.zeros(acc_scr.shape, F32)

    def chunk(kc, masked):
        start = pl.multiple_of(kc * tk, tk)
        k = k_ref[pl.ds(start, tk), :]
        v = v_ref[pl.ds(start, tk), :]
        s = lax.dot_general(q, k, (((1,), (1,)), ((), ())), preferred_element_type=F32)
        if masked:
            row = lax.broadcasted_iota(jnp.int32, (tq, tk), 0)
            col = lax.broadcasted_iota(jnp.int32, (tq, tk), 1)
            s = jnp.where(col <= row, s, -jnp.inf)
        m_prev = m_scr[...]
        m_new = jnp.maximum(m_prev, jnp.max(s, axis=-1, keepdims=True))
        p = jnp.exp(s - m_new)
        alpha = jnp.exp(m_prev - m_new)
        l_scr[...] = alpha * l_scr[...] + jnp.sum(p, axis=-1, keepdims=True)
        acc_scr[...] = alpha * acc_scr[...] + jnp.dot(p.astype(BF16), v, preferred_element_type=F32)
        m_scr[...] = m_new

    def body(kc, carry):
        chunk(kc, False)
        return carry

    lax.fori_loop(0, qi, body, 0)
    chunk(qi, True)
    lane = lax.broadcasted_iota(jnp.int32, acc_scr.shape, 1)
    o_ref[...] = jnp.where(lane < V_HEAD_DIM, acc_scr[...] / l_scr[...], 0.0).astype(o_ref.dtype)


def _attn_call(q, k, v, batch, seq, tq):
    t, hp = q.shape
    nq = seq // tq
    return pl.pallas_call(
        _attn_kernel,
        grid=(batch, N_HEADS, nq),
        in_specs=[
            pl.BlockSpec((tq, HEAD_PAD), lambda b, h, qi: (b * nq + qi, h)),
            pl.BlockSpec((seq, HEAD_PAD), lambda b, h, qi: (b, h)),
            pl.BlockSpec((seq, HEAD_PAD), lambda b, h, qi: (b, h)),
        ],
        out_specs=pl.BlockSpec((tq, HEAD_PAD), lambda b, h, qi: (b * nq + qi, h)),
        out_shape=jax.ShapeDtypeStruct((t, hp), F32),
        scratch_shapes=[pltpu.VMEM((tq, 1), F32), pltpu.VMEM((tq, 1), F32),
                        pltpu.VMEM((tq, HEAD_PAD), F32)],
        compiler_params=pltpu.CompilerParams(
            dimension_semantics=("arbitrary", "arbitrary", "arbitrary"),
            vmem_limit_bytes=VMEM_LIMIT),
        name="attn",
    )(q, k, v)


def _outproj_kernel(pool_ref, attn_ref, x_ref, gattn_ref, wop_ref, woa_ref, gffn_ref, wr_ref,
                    br_ref, x1_out, hp_out, idx_out, gate_out, rank_out, cnt_out, run_scr):
    tm, d = x_ref.shape
    d_attn = N_HEADS * V_HEAD_DIM
    i = pl.program_id(0)

    @pl.when(i == 0)
    def _():
        run_scr[...] = jnp.zeros(run_scr.shape, F32)

    attn = attn_ref[...]
    attn_n = _rms(attn, gattn_ref[...], d_attn).astype(BF16)
    mix = (jnp.dot(pool_ref[...], wop_ref[...], preferred_element_type=F32)
           + jnp.dot(attn_n, woa_ref[...], preferred_element_type=F32))
    x1 = x_ref[...] + mix
    x1_out[...] = x1
    h2 = _rms(x1, gffn_ref[...], d)

    hb = h2.astype(BF16).astype(F32)
    lo = lax.bitcast_convert_type(hb[:, :d // 2], jnp.uint32)
    hi = lax.bitcast_convert_type(hb[:, d // 2:], jnp.uint32)
    hp_out[...] = (lo >> 16) | (hi & jnp.uint32(0xFFFF0000))

    logits = lax.dot_general(wr_ref[...], h2, (((1,), (1,)), ((), ())),
                             preferred_element_type=F32,
                             precision=lax.Precision.HIGHEST) + br_ref[...]
    e_iota = lax.broadcasted_iota(jnp.int32, (N_EXPERTS, tm), 0)
    work = logits
    idxs, vals = [], []
    for _ in range(TOP_K):
        mx = jnp.max(work, axis=0, keepdims=True)
        sel = jnp.min(jnp.where(work == mx, e_iota, N_EXPERTS), axis=0, keepdims=True)
        idxs.append(sel)
        vals.append(mx)
        work = jnp.where(e_iota == sel, -jnp.inf, work)
    exps = [jnp.exp(v - vals[0]) for v in vals]
    denom = exps[0] + exps[1] + exps[2] + exps[3]
    gate_out[...] = jnp.concatenate([e / denom for e in exps], axis=0)
    idx_out[...] = jnp.concatenate(idxs, axis=0)

    onehot = jnp.zeros((N_EXPERTS, tm), F32)
    for sel in idxs:
        onehot = onehot + (e_iota == sel).astype(F32)
    r_io = lax.broadcasted_iota(jnp.int32, (tm, tm), 0)
    c_io = lax.broadcasted_iota(jnp.int32, (tm, tm), 1)
    upper = (r_io < c_io).astype(BF16)
    prefix = jnp.dot(onehot.astype(BF16), upper, preferred_element_type=F32) + run_scr[:, 0:1]
    ranks = [jnp.sum(jnp.where(e_iota == sel, prefix, 0.0), axis=0, keepdims=True) for sel in idxs]
    rank_out[...] = jnp.concatenate(ranks, axis=0).astype(jnp.int32)
    run_new = run_scr[...] + jnp.sum(onehot, axis=1, keepdims=True)
    run_scr[...] = run_new
    cnt_out[...] = run_new.astype(jnp.int32)


def _outproj_call(pool_n, attn, x2, gattn_p, wo_pool, wo_attn_p, gffn, wr_t, br_col, tm):
    t, d = x2.shape
    nblk = t // tm
    const2 = lambda i: (0, 0)
    row = lambda i: (i, 0)
    col = lambda i: (0, i)
    in_specs = [
        pl.BlockSpec((tm, pool_n.shape[1]), row),
        pl.BlockSpec((tm, attn.shape[1]), row),
        pl.BlockSpec((tm, d), row),
        pl.BlockSpec(gattn_p.shape, const2),
        pl.BlockSpec(wo_pool.shape, const2),
        pl.BlockSpec(wo_attn_p.shape, const2),
        pl.BlockSpec(gffn.shape, const2),
        pl.BlockSpec(wr_t.shape, const2),
        pl.BlockSpec(br_col.shape, const2),
    ]
    out_shape = [
        jax.ShapeDtypeStruct((t, d), F32),
        jax.ShapeDtypeStruct((t, d // 2), jnp.uint32),
        jax.ShapeDtypeStruct((TOP_K, t), jnp.int32),
        jax.ShapeDtypeStruct((TOP_K, t), F32),
        jax.ShapeDtypeStruct((TOP_K, t), jnp.int32),
        jax.ShapeDtypeStruct((N_EXPERTS, LANE), jnp.int32),
    ]
    out_specs = [
        pl.BlockSpec((tm, d), row),
        pl.BlockSpec((tm, d // 2), row),
        pl.BlockSpec((TOP_K, tm), col),
        pl.BlockSpec((TOP_K, tm), col),
        pl.BlockSpec((TOP_K, tm), col),
        pl.BlockSpec((N_EXPERTS, LANE), const2),
    ]
    return pl.pallas_call(
        _outproj_kernel,
        grid=(nblk,),
        in_specs=in_specs,
        out_specs=out_specs,
        out_shape=out_shape,
        scratch_shapes=[pltpu.VMEM((N_EXPERTS, LANE), F32)],
        compiler_params=pltpu.CompilerParams(dimension_semantics=("arbitrary",),
                                             vmem_limit_bytes=VMEM_LIMIT),
        name="outproj",
    )(pool_n, attn, x2, gattn_p, wo_pool, wo_attn_p, gffn, wr_t, br_col)


def _dispatch_kernel(dest_ref, h_ref, xs_in, xs_out, sem):
    del xs_in
    td = h_ref.shape[0]

    def row_copy(i, k):
        d = dest_ref[k, i]
        return pltpu.make_async_copy(h_ref.at[pl.ds(i, 1)], xs_out.at[pl.ds(d, 1)], sem)

    def body(i, carry):
        for k in range(TOP_K):
            row_copy(i, k).start()
        return carry

    lax.fori_loop(0, td, body, 0)
    for _ in range(TOP_K):
        pltpu.make_async_copy(h_ref, xs_out.at[pl.ds(0, td)], sem).wait()


def _dispatch_call(dest, hp, xs_init, td):
    t, dw = hp.shape
    return pl.pallas_call(
        _dispatch_kernel,
        grid=(t // td,),
        in_specs=[
            pl.BlockSpec((TOP_K, td), lambda i: (0, i), memory_space=pltpu.SMEM),
            pl.BlockSpec((td, dw), lambda i: (i, 0)),
            pl.BlockSpec(memory_space=pl.ANY),
        ],
        out_specs=pl.BlockSpec(memory_space=pl.ANY),
        out_shape=jax.ShapeDtypeStruct(xs_init.shape, xs_init.dtype),
        scratch_shapes=[pltpu.SemaphoreType.DMA],
        input_output_aliases={2: 0},
        compiler_params=pltpu.CompilerParams(dimension_semantics=("arbitrary",),
                                             vmem_limit_bytes=VMEM_LIMIT),
        name="dispatch",
    )(dest, hp, xs_init)


CAST_ROWS = 64


def _expert_kernel(be_ref, nv_ref, xs_ref, wgu_ref, bgu_ref, wd_ref, bd_ref, y_ref, wgu_bf, wd_bf):
    i = pl.program_id(0)
    d = wgu_ref.shape[1]
    f = wd_ref.shape[1]

    @pl.when(i < nv_ref[0])
    def _():
        prev = be_ref[jnp.maximum(i - 1, 0)]
        changed = jnp.logical_or(i == 0, be_ref[i] != prev)

        @pl.when(changed)
        def _():
            def cast_gu(r, c):
                rows = pl.ds(pl.multiple_of(r * CAST_ROWS, CAST_ROWS), CAST_ROWS)
                wgu_bf[rows, :] = wgu_ref[0, rows, :].astype(BF16)
                return c

            def cast_d(r, c):
                rows = pl.ds(pl.multiple_of(r * CAST_ROWS, CAST_ROWS), CAST_ROWS)
                wd_bf[rows, :] = wd_ref[0, rows, :].astype(BF16)
                return c

            lax.fori_loop(0, d // CAST_ROWS, cast_gu, 0)
            lax.fori_loop(0, f // CAST_ROWS, cast_d, 0)

        w = xs_ref[...]
        x_lo = lax.bitcast_convert_type(w << 16, F32).astype(BF16)
        x_hi = lax.bitcast_convert_type(w & jnp.uint32(0xFFFF0000), F32).astype(BF16)
        gu = (jnp.dot(x_lo, wgu_bf[:d // 2, :], preferred_element_type=F32)
              + jnp.dot(x_hi, wgu_bf[d // 2:, :], preferred_element_type=F32) + bgu_ref[0])
        gate = jnp.minimum(gu[:, :f], SWIGLU_LIMIT)
        up = jnp.clip(gu[:, f:], -SWIGLU_LIMIT, SWIGLU_LIMIT)
        act = gate * jax.nn.sigmoid(SWIGLU_ALPHA * gate) * (up + 1.0)
        y_ref[...] = jnp.dot(act.astype(BF16), wd_bf[...], preferred_element_type=F32) + bd_ref[0]

    @pl.when(i >= nv_ref[0])
    def _():
        y_ref[...] = jnp.zeros(y_ref.shape, y_ref.dtype)


def _expert_call(block_expert, n_valid, xs, wgu, bgu, wd, bd, bm):
    r, dw = xs.shape
    e, d, f2 = wgu.shape
    f = wd.shape[1]
    nb = r // bm

    def xrow(i, be, nv):
        return (jnp.minimum(i, nv[0] - 1), 0)

    def wsel(i, be, nv):
        return (be[i], 0, 0)

    grid_spec = pltpu.PrefetchScalarGridSpec(
        num_scalar_prefetch=2,
        grid=(nb,),
        in_specs=[
            pl.BlockSpec((bm, dw), xrow),
            pl.BlockSpec((1, d, f2), wsel),
            pl.BlockSpec((1, 1, f2), wsel),
            pl.BlockSpec((1, f, d), wsel),
            pl.BlockSpec((1, 1, d), wsel),
        ],
        out_specs=pl.BlockSpec((bm, d), lambda i, be, nv: (i, 0)),
        scratch_shapes=[pltpu.VMEM((d, f2), BF16), pltpu.VMEM((f, d), BF16)],
    )
    return pl.pallas_call(
        _expert_kernel,
        grid_spec=grid_spec,
        out_shape=jax.ShapeDtypeStruct((r, d), F32),
        compiler_params=pltpu.CompilerParams(dimension_semantics=("arbitrary",),
                                             vmem_limit_bytes=VMEM_LIMIT),
        name="experts",
    )(block_expert, n_valid, xs, wgu, bgu, wd, bd)


def _combine_kernel(dest_ref, x1_ref, gate_ref, gfin_ref, yb_hbm, o_ref, ybuf, sem):
    tf, d = x1_ref.shape

    def body(i, carry):
        for k in range(TOP_K):
            r = dest_ref[k, i]
            pltpu.make_async_copy(yb_hbm.at[pl.ds(r, 1)], ybuf.at[k, pl.ds(i, 1)], sem).start()
        return carry

    lax.fori_loop(0, tf, body, 0)
    for k in range(TOP_K):
        pltpu.make_async_copy(yb_hbm.at[pl.ds(0, tf)], ybuf.at[k], sem).wait()
    g = gate_ref[...]
    y = x1_ref[...]
    for k in range(TOP_K):
        y = y + g[:, k:k + 1] * ybuf[k]
    o_ref[...] = _rms(y, gfin_ref[...], d)


def _combine_call(dest, x1, gates_t, gfin, yb, tf):
    t, d = x1.shape
    return pl.pallas_call(
        _combine_kernel,
        grid=(t // tf,),
        in_specs=[
            pl.BlockSpec((TOP_K, tf), lambda i: (0, i), memory_space=pltpu.SMEM),
            pl.BlockSpec((tf, d), lambda i: (i, 0)),
            pl.BlockSpec((tf, TOP_K), lambda i: (i, 0)),
            pl.BlockSpec(gfin.shape, lambda i: (0, 0)),
            pl.BlockSpec(memory_space=pl.ANY),
        ],
        out_specs=pl.BlockSpec((tf, d), lambda i: (i, 0)),
        out_shape=jax.ShapeDtypeStruct((t, d), F32),
        scratch_shapes=[pltpu.VMEM((TOP_K, tf, d), F32), pltpu.SemaphoreType.DMA],
        compiler_params=pltpu.CompilerParams(dimension_semantics=("arbitrary",),
                                             vmem_limit_bytes=VMEM_LIMIT),
        name="combine",
    )(dest, x1, gates_t, gfin, yb)


def _pad_head_cols(w, widths_in, place):
    k = w.shape[0]
    per = sum(widths_in)
    blocks = []
    for hh in range(N_HEADS):
        blocks.append(place(w[:, hh * per:(hh + 1) * per]))
    return jnp.concatenate(blocks, axis=1)


def _layer(x2, pos3, batch, seq, g_attn_norm, w_in, w_pool, b_pool, pool_scale, g_q_a, w_q_b,
           g_kv_a, w_kv_b, g_out_pool, g_out_attn, w_out, g_ffn_norm, w_router, b_router,
           w_gate_up, b_gate_up, w_down, b_down, tiles):
    tm, tq, tmo, td, bm, tf = tiles
    t, d = x2.shape
    d_pool = pool_scale.shape[0]
    gdim = d_pool // N_POOL_GROUPS
    q_rank = g_q_a.shape[0]
    kv_rank = g_kv_a.shape[0]
    half = QK_ROPE_DIM // 2

    o1, o2, o3 = d_pool, d_pool + q_rank, d_pool + q_rank + kv_rank
    zk = lambda n: jnp.zeros((d, n), F32)
    w_in_p = jnp.concatenate([w_in[:, :o3], zk(QK_NOPE_DIM), w_in[:, o3:], zk(LANE - QK_DIM)],
                             axis=1).astype(BF16)
    zq = lambda n: jnp.zeros((q_rank, n), F32)
    wq_p = _pad_head_cols(w_q_b, (QK_DIM,), lambda c: jnp.concatenate([c, zq(LANE - QK_DIM)], axis=1)
                          ).astype(BF16)
    zkv = lambda n: jnp.zeros((kv_rank, n), F32)
    per = QK_NOPE_DIM + V_HEAD_DIM
    wk_p = _pad_head_cols(w_kv_b, (per,), lambda c: jnp.concatenate(
        [c[:, :QK_NOPE_DIM], zkv(LANE - QK_NOPE_DIM)], axis=1))
    wv_p = _pad_head_cols(w_kv_b, (per,), lambda c: jnp.concatenate(
        [c[:, QK_NOPE_DIM:], zkv(LANE - V_HEAD_DIM)], axis=1))
    wkv_p = jnp.concatenate([wk_p, wv_p], axis=1).astype(BF16)
    inv = ROPE_THETA ** (-jnp.arange(half, dtype=F32) / half)
    inv_col = inv.reshape(half, 1)

    row = lambda v: v.reshape(1, -1)
    grp = lambda v: v.reshape(N_POOL_GROUPS, 1, gdim)

    pool_n, q, k, v = _proj_call(
        x2, pos3, row(g_attn_norm), w_in_p, inv_col, w_pool.astype(BF16), grp(b_pool),
        grp(pool_scale), grp(g_out_pool), row(g_q_a), wq_p, row(g_kv_a), wkv_p, seq, tm)

    attn = _attn_call(q, k, v, batch, seq, tq)

    d_attn = N_HEADS * V_HEAD_DIM
    wo_pool = w_out[:d_pool].astype(BF16)
    wo_attn = w_out[d_pool:].reshape(N_HEADS, V_HEAD_DIM, d)
    wo_attn_p = jnp.concatenate([wo_attn, jnp.zeros((N_HEADS, HEAD_PAD - V_HEAD_DIM, d), F32)],
                                axis=1).reshape(N_HEADS * HEAD_PAD, d).astype(BF16)
    gattn_p = jnp.concatenate([g_out_attn.reshape(N_HEADS, V_HEAD_DIM),
                               jnp.zeros((N_HEADS, HEAD_PAD - V_HEAD_DIM), F32)], axis=1).reshape(1, -1)

    x1, hp, idx, gates, rank, counts = _outproj_call(
        pool_n, attn, x2, gattn_p, wo_pool, wo_attn_p, row(g_ffn_norm), w_router.T,
        b_router.reshape(-1, 1), tmo)

    n_assign = t * TOP_K
    cnt = counts[:, 0]
    padded = ((cnt + bm - 1) // bm) * bm
    pad_end = jnp.cumsum(padded)
    pad_start = pad_end - padded
    dest = pad_start[idx] + rank
    nb = n_assign // bm + N_EXPERTS
    n_rows = nb * bm
    block_expert = jnp.minimum(
        jnp.searchsorted(pad_end, jnp.arange(nb, dtype=jnp.int32) * bm, side='right'),
        N_EXPERTS - 1).astype(jnp.int32)
    n_valid = (pad_end[-1] // bm).astype(jnp.int32).reshape(1)

    xs = _dispatch_call(dest, hp, jnp.zeros((n_rows, d // 2), jnp.uint32), td)
    yb = _expert_call(block_expert, n_valid, xs, w_gate_up, b_gate_up.reshape(N_EXPERTS, 1, -1),
                      w_down, b_down.reshape(N_EXPERTS, 1, -1), bm)
    return x1, dest, gates, yb


def _forward(x, positions, g_attn_norm, w_in, w_pool, b_pool, pool_scale, g_q_a, w_q_b, g_kv_a,
             w_kv_b, g_out_pool, g_out_attn, w_out, g_ffn_norm, w_router, b_router, w_gate_up,
             b_gate_up, w_down, b_down, g_final, tiles):
    batch, seq, d = x.shape
    t = batch * seq
    tm, tq, tmo, td, bm, tf = tiles
    x2 = x.reshape(t, d)
    pos3 = positions.reshape(t // tm, 1, tm)
    depth = g_attn_norm.shape[0]
    assert depth == 1
    l = 0
    x1, dest, gates, yb = _layer(
        x2, pos3, batch, seq, g_attn_norm[l], w_in[l], w_pool[l], b_pool[l], pool_scale[l],
        g_q_a[l], w_q_b[l], g_kv_a[l], w_kv_b[l], g_out_pool[l], g_out_attn[l], w_out[l],
        g_ffn_norm[l], w_router[l], b_router[l], w_gate_up[l], b_gate_up[l], w_down[l],
        b_down[l], tiles)
    out = _combine_call(dest, x1, gates.T, g_final.reshape(1, -1), yb, tf)
    return out.reshape(batch, seq, d)


def _tiles_for(seq):
    tm = min(512, seq)
    tq = min(512, seq)
    tmo = min(512, seq)
    td = min(512, seq)
    bm = 256 if seq >= 2048 else 128
    tf = min(256, seq)
    return (tm, tq, tmo, td, bm, tf)


def kernel(x, positions, g_attn_norm, w_in, w_pool, b_pool, pool_scale, g_q_a, w_q_b, g_kv_a, w_kv_b, g_out_pool, g_out_attn, w_out, g_ffn_norm, w_router, b_router, w_gate_up, b_gate_up, w_down, b_down, g_final):
    tiles = _tiles_for(x.shape[1])
    return _forward(x, positions, g_attn_norm, w_in, w_pool, b_pool, pool_scale, g_q_a, w_q_b,
                    g_kv_a, w_kv_b, g_out_pool, g_out_attn, w_out, g_ffn_norm, w_router, b_router,
                    w_gate_up, b_gate_up, w_down, b_down, g_final, tiles)
```

```python
import functools

import jax
import jax.numpy as jnp
from jax import lax
from jax.experimental import pallas as pl
from jax.experimental.pallas import tpu as pltpu

POOL_WINDOWS = (2, 4, 8, 16)
N_POOL_GROUPS = 4
N_HEADS = 8
V_HEAD_DIM = 64
QK_NOPE_DIM = 64
QK_ROPE_DIM = 32
QK_DIM = QK_NOPE_DIM + QK_ROPE_DIM
ROPE_THETA = 10000.0
N_EXPERTS = 32
TOP_K = 4
SWIGLU_ALPHA = 1.702
SWIGLU_LIMIT = 7.0
RMS_EPS = 1e-6
LOG2_E = 1.4426950408889634

LANE = 128
HEAD_PAD = 128
POOL_HALO = 16

VMEM_LIMIT = 56 * 1024 * 1024

F32 = jnp.float32
BF16 = jnp.bfloat16


def _rms(x, g, n):
    ms = jnp.sum(x * x, axis=-1, keepdims=True) * (1.0 / n)
    return x * lax.rsqrt(ms + RMS_EPS) * g


def _proj_kernel(blocks_per_seq, x_ref, pos_ref, g_attn_ref, w_in_ref, inv_ref, wpool_ref,
                 bpool_ref, pscale_ref, gpool_ref, gq_ref, wq_ref, gkv_ref, wkv_ref,
                 pool_out, q_out, k_out, v_out, ext_scr):
    tm = x_ref.shape[0]
    d_model = x_ref.shape[1]
    d_pool = pool_out.shape[1]
    gdim = d_pool // N_POOL_GROUPS
    q_rank = gq_ref.shape[1]
    kv_rank = gkv_ref.shape[1]
    i = pl.program_id(0)
    blk_in_seq = i % blocks_per_seq

    x = x_ref[...]
    h = _rms(x, g_attn_ref[...], d_model).astype(BF16)
    proj = jnp.dot(h, w_in_ref[...], preferred_element_type=F32)
    u = proj[:, :d_pool]
    qlat = proj[:, d_pool:d_pool + q_rank]
    kvlat = proj[:, d_pool + q_rank:d_pool + q_rank + kv_rank]
    krb = proj[:, d_pool + q_rank + kv_rank:]

    @pl.when(blk_in_seq == 0)
    def _():
        ext_scr[0:POOL_HALO, :] = jnp.zeros((POOL_HALO, d_pool), F32)

    @pl.when(blk_in_seq != 0)
    def _():
        ext_scr[0:POOL_HALO, :] = ext_scr[tm:tm + POOL_HALO, :]

    ext_scr[POOL_HALO:, :] = u
    t_in_seq = (blk_in_seq * tm + lax.broadcasted_iota(jnp.int32, (tm, gdim), 0) + 1).astype(F32)
    ys = []
    ssq = jnp.zeros((tm, 1), F32)
    for g, w in enumerate(POOL_WINDOWS):
        a = ext_scr[:, g * gdim:(g + 1) * gdim]
        shift = 1
        while shift < w:
            a = a + pltpu.roll(a, shift, axis=0)
            shift *= 2
        win = a[POOL_HALO:, :]
        cnt = jnp.minimum(t_in_seq, float(w))
        ug = u[:, g * gdim:(g + 1) * gdim]
        diff = (win / cnt - ug).astype(BF16)
        mixed = jnp.dot(diff, wpool_ref[g], preferred_element_type=F32) + bpool_ref[g]
        y = mixed * pscale_ref[g]
        ssq = ssq + jnp.sum(y * y, axis=-1, keepdims=True)
        ys.append(y)
    rinv = lax.rsqrt(ssq * (1.0 / d_pool) + RMS_EPS)
    for g in range(N_POOL_GROUPS):
        pool_out[:, g * gdim:(g + 1) * gdim] = (ys[g] * rinv * gpool_ref[g]).astype(pool_out.dtype)

    pos = pos_ref[0].astype(F32)
    ang_t = inv_ref[...] * pos
    cos_t = jnp.cos(ang_t)
    sin_t = jnp.sin(ang_t)
    half = QK_ROPE_DIM // 2
    ones = jnp.ones((QK_NOPE_DIM, tm), F32)
    c_t = jnp.concatenate([ones, cos_t, cos_t, jnp.ones((LANE - QK_DIM, tm), F32)], axis=0)
    sa_t = jnp.concatenate([jnp.zeros((QK_NOPE_DIM + half, tm), F32), sin_t,
                            jnp.zeros((LANE - QK_DIM, tm), F32)], axis=0)
    sb_t = jnp.concatenate([jnp.zeros((QK_NOPE_DIM, tm), F32), sin_t,
                            jnp.zeros((LANE - QK_DIM + half, tm), F32)], axis=0)
    c = c_t.T
    sa = sa_t.T
    sb = sb_t.T

    def rope(blk, c_, sa_, sb_):
        return blk * c_ + pltpu.roll(blk, half, axis=1) * sa_ - pltpu.roll(blk, LANE - half, axis=1) * sb_

    qn = _rms(qlat, gq_ref[...], q_rank).astype(BF16)
    q = jnp.dot(qn, wq_ref[...], preferred_element_type=F32)
    scale = QK_DIM ** -0.5 * LOG2_E
    cq, saq, sbq = c * scale, sa * scale, sb * scale
    for hh in range(N_HEADS):
        sl = slice(hh * HEAD_PAD, (hh + 1) * HEAD_PAD)
        q_out[:, sl] = rope(q[:, sl], cq, saq, sbq).astype(q_out.dtype)

    kvn = _rms(kvlat, gkv_ref[...], kv_rank).astype(BF16)
    kv = jnp.dot(kvn, wkv_ref[...], preferred_element_type=F32)
    kr = rope(krb, c, sa, sb)
    for hh in range(N_HEADS):
        sl = slice(hh * HEAD_PAD, (hh + 1) * HEAD_PAD)
        k_out[:, sl] = (kv[:, sl] + kr).astype(k_out.dtype)
    vlane = lax.broadcasted_iota(jnp.int32, (1, N_HEADS * HEAD_PAD), 1) % HEAD_PAD
    ones_col = (vlane == V_HEAD_DIM).astype(F32)
    v_out[...] = (kv[:, N_HEADS * HEAD_PAD:] + ones_col).astype(v_out.dtype)


def _proj_call(x2, pos3, g_attn, w_in_p, inv_col, wpool, bpool, pscale, gpool, gq, wq_p, gkv, wkv_p,
               seq, tm):
    t, d = x2.shape
    d_pool = pscale.shape[0] * pscale.shape[2]
    hp = N_HEADS * HEAD_PAD
    nblk = t // tm
    const2 = lambda i: (0, 0)
    const3 = lambda i: (0, 0, 0)
    in_specs = [
        pl.BlockSpec((tm, d), lambda i: (i, 0)),
        pl.BlockSpec((1, 1, tm), lambda i: (i, 0, 0)),
        pl.BlockSpec(g_attn.shape, const2),
        pl.BlockSpec(w_in_p.shape, const2),
        pl.BlockSpec(inv_col.shape, const2),
        pl.BlockSpec(wpool.shape, const3),
        pl.BlockSpec(bpool.shape, const3),
        pl.BlockSpec(pscale.shape, const3),
        pl.BlockSpec(gpool.shape, const3),
        pl.BlockSpec(gq.shape, const2),
        pl.BlockSpec(wq_p.shape, const2),
        pl.BlockSpec(gkv.shape, const2),
        pl.BlockSpec(wkv_p.shape, const2),
    ]
    out_shape = [
        jax.ShapeDtypeStruct((t, d_pool), BF16),
        jax.ShapeDtypeStruct((t, hp), BF16),
        jax.ShapeDtypeStruct((t, hp), BF16),
        jax.ShapeDtypeStruct((t, hp), BF16),
    ]
    out_specs = [
        pl.BlockSpec((tm, d_pool), lambda i: (i, 0)),
        pl.BlockSpec((tm, hp), lambda i: (i, 0)),
        pl.BlockSpec((tm, hp), lambda i: (i, 0)),
        pl.BlockSpec((tm, hp), lambda i: (i, 0)),
    ]
    return pl.pallas_call(
        functools.partial(_proj_kernel, seq // tm),
        grid=(nblk,),
        in_specs=in_specs,
        out_specs=out_specs,
        out_shape=out_shape,
        scratch_shapes=[pltpu.VMEM((tm + POOL_HALO, d_pool), F32)],
        compiler_params=pltpu.CompilerParams(dimension_semantics=("arbitrary",),
                                             vmem_limit_bytes=VMEM_LIMIT),
        name="proj",
    )(x2, pos3, g_attn, w_in_p, inv_col, wpool, bpool, pscale, gpool, gq, wq_p, gkv, wkv_p)


ROWSUM_LANE = V_HEAD_DIM


def _attn_kernel(q_ref, k_ref, v_ref, o_ref, m_a, acc_a, m_b, acc_b):
    th = q_ref.shape[0] // 2
    qi = pl.program_id(2)
    nt = th // LANE
    for m_ref, acc_ref in ((m_a, acc_a), (m_b, acc_b)):
        m_ref[...] = jnp.full(m_ref.shape, -jnp.inf, F32)
        acc_ref[...] = jnp.zeros(acc_ref.shape, F32)

    def load_kv(kc):
        start = pl.multiple_of(kc * th, th)
        return k_ref[pl.ds(start, th), :], v_ref[pl.ds(start, th), :]

    def half_step(row0, k, v, m_ref, acc_ref, masked):
        q = q_ref[row0:row0 + th, :]
        s = lax.dot_general(q, k, (((1,), (1,)), ((), ())), preferred_element_type=F32)
        if masked:
            row = lax.broadcasted_iota(jnp.int32, (th, th), 0)
            col = lax.broadcasted_iota(jnp.int32, (th, th), 1)
            s = jnp.where(col <= row, s, -jnp.inf)
        tiles = [s[:, j * LANE:(j + 1) * LANE] for j in range(nt)]
        mt = tiles[0]
        for tl in tiles[1:]:
            mt = jnp.maximum(mt, tl)
        m_prev = m_ref[...]
        m_new = jnp.maximum(m_prev, jnp.max(mt, axis=-1, keepdims=True))
        alpha = jnp.exp2(m_prev - m_new)
        p = jnp.concatenate([jnp.exp2(tl - m_new).astype(BF16) for tl in tiles], axis=1)
        acc_ref[...] = alpha * acc_ref[...] + jnp.dot(p, v, preferred_element_type=F32)
        m_ref[...] = m_new

    def body(kc, carry):
        k, v = load_kv(kc)
        half_step(0, k, v, m_a, acc_a, False)
        half_step(th, k, v, m_b, acc_b, False)
        return carry

    lax.fori_loop(0, 2 * qi, body, 0)
    k, v = load_kv(2 * qi)
    half_step(0, k, v, m_a, acc_a, True)
    half_step(th, k, v, m_b, acc_b, False)
    k, v = load_kv(2 * qi + 1)
    half_step(th, k, v, m_b, acc_b, True)

    lane = lax.broadcasted_iota(jnp.int32, (th, HEAD_PAD), 1)
    for row0, acc_ref in ((0, acc_a), (th, acc_b)):
        acc = acc_ref[...]
        out = acc / acc[:, ROWSUM_LANE:ROWSUM_LANE + 1]
        o_ref[row0:row0 + th, :] = jnp.where(lane < V_HEAD_DIM, out, 0.0).astype(o_ref.dtype)


def _attn_call(q, k, v, batch, seq, th):
    t, hp = q.shape
    tq = 2 * th
    nq = seq // tq
    return pl.pallas_call(
        _attn_kernel,
        grid=(batch, N_HEADS, nq),
        in_specs=[
            pl.BlockSpec((tq, HEAD_PAD), lambda b, h, qi: (b * nq + qi, h)),
            pl.BlockSpec((seq, HEAD_PAD), lambda b, h, qi: (b, h)),
            pl.BlockSpec((seq, HEAD_PAD), lambda b, h, qi: (b, h)),
        ],
        out_specs=pl.BlockSpec((tq, HEAD_PAD), lambda b, h, qi: (b * nq + qi, h)),
        out_shape=jax.ShapeDtypeStruct((t, hp), F32),
        scratch_shapes=[pltpu.VMEM((th, HEAD_PAD), F32), pltpu.VMEM((th, HEAD_PAD), F32),
                        pltpu.VMEM((th, HEAD_PAD), F32), pltpu.VMEM((th, HEAD_PAD), F32)],
        compiler_params=pltpu.CompilerParams(
            dimension_semantics=("arbitrary", "arbitrary", "arbitrary"),
            vmem_limit_bytes=VMEM_LIMIT),
        name="attn",
    )(q, k, v)


def _outproj_kernel(pool_ref, attn_ref, x_ref, gattn_ref, wop_ref, woa_ref, gffn_ref, wr_ref,
                    br_ref, x1_out, hp_out, idx_out, gate_out, rank_out, cnt_out, run_scr):
    tm, d = x_ref.shape
    d_attn = N_HEADS * V_HEAD_DIM
    i = pl.program_id(0)

    @pl.when(i == 0)
    def _():
        run_scr[...] = jnp.zeros(run_scr.shape, F32)

    attn = attn_ref[...]
    attn_n = _rms(attn, gattn_ref[...], d_attn).astype(BF16)
    mix = (jnp.dot(pool_ref[...], wop_ref[...], preferred_element_type=F32)
           + jnp.dot(attn_n, woa_ref[...], preferred_element_type=F32))
    x1 = x_ref[...] + mix
    x1_out[...] = x1
    h2 = _rms(x1, gffn_ref[...], d)

    hb = h2.astype(BF16).astype(F32)
    lo = lax.bitcast_convert_type(hb[:, :d // 2], jnp.uint32)
    hi = lax.bitcast_convert_type(hb[:, d // 2:], jnp.uint32)
    hp_out[...] = (lo >> 16) | (hi & jnp.uint32(0xFFFF0000))

    logits = lax.dot_general(wr_ref[...], h2, (((1,), (1,)), ((), ())),
                             preferred_element_type=F32,
                             precision=lax.Precision.HIGHEST) + br_ref[...]
    e_iota = lax.broadcasted_iota(jnp.int32, (N_EXPERTS, tm), 0)
    work = logits
    idxs, vals = [], []
    for _ in range(TOP_K):
        mx = jnp.max(work, axis=0, keepdims=True)
        sel = jnp.min(jnp.where(work == mx, e_iota, N_EXPERTS), axis=0, keepdims=True)
        idxs.append(sel)
        vals.append(mx)
        work = jnp.where(e_iota == sel, -jnp.inf, work)
    exps = [jnp.exp(v - vals[0]) for v in vals]
    denom = exps[0] + exps[1] + exps[2] + exps[3]
    gate_out[...] = jnp.concatenate([e / denom for e in exps], axis=0)
    idx_out[...] = jnp.concatenate(idxs, axis=0)

    onehot = jnp.zeros((N_EXPERTS, tm), F32)
    for sel in idxs:
        onehot = onehot + (e_iota == sel).astype(F32)
    r_io = lax.broadcasted_iota(jnp.int32, (tm, tm), 0)
    c_io = lax.broadcasted_iota(jnp.int32, (tm, tm), 1)
    upper = (r_io < c_io).astype(BF16)
    prefix = jnp.dot(onehot.astype(BF16), upper, preferred_element_type=F32) + run_scr[:, 0:1]
    ranks = [jnp.sum(jnp.where(e_iota == sel, prefix, 0.0), axis=0, keepdims=True) for sel in idxs]
    rank_out[...] = jnp.concatenate(ranks, axis=0).astype(jnp.int32)
    run_new = run_scr[...] + jnp.sum(onehot, axis=1, keepdims=True)
    run_scr[...] = run_new
    cnt_out[...] = run_new.astype(jnp.int32)


def _outproj_call(pool_n, attn, x2, gattn_p, wo_pool, wo_attn_p, gffn, wr_t, br_col, tm):
    t, d = x2.shape
    nblk = t // tm
    const2 = lambda i: (0, 0)
    row = lambda i: (i, 0)
    col = lambda i: (0, i)
    in_specs = [
        pl.BlockSpec((tm, pool_n.shape[1]), row),
        pl.BlockSpec((tm, attn.shape[1]), row),
        pl.BlockSpec((tm, d), row),
        pl.BlockSpec(gattn_p.shape, const2),
        pl.BlockSpec(wo_pool.shape, const2),
        pl.BlockSpec(wo_attn_p.shape, const2),
        pl.BlockSpec(gffn.shape, const2),
        pl.BlockSpec(wr_t.shape, const2),
        pl.BlockSpec(br_col.shape, const2),
    ]
    out_shape = [
        jax.ShapeDtypeStruct((t, d), F32),
        jax.ShapeDtypeStruct((t, d // 2), jnp.uint32),
        jax.ShapeDtypeStruct((TOP_K, t), jnp.int32),
        jax.ShapeDtypeStruct((TOP_K, t), F32),
        jax.ShapeDtypeStruct((TOP_K, t), jnp.int32),
        jax.ShapeDtypeStruct((N_EXPERTS, LANE), jnp.int32),
    ]
    out_specs = [
        pl.BlockSpec((tm, d), row),
        pl.BlockSpec((tm, d // 2), row),
        pl.BlockSpec((TOP_K, tm), col),
        pl.BlockSpec((TOP_K, tm), col),
        pl.BlockSpec((TOP_K, tm), col),
        pl.BlockSpec((N_EXPERTS, LANE), const2),
    ]
    return pl.pallas_call(
        _outproj_kernel,
        grid=(nblk,),
        in_specs=in_specs,
        out_specs=out_specs,
        out_shape=out_shape,
        scratch_shapes=[pltpu.VMEM((N_EXPERTS, LANE), F32)],
        compiler_params=pltpu.CompilerParams(dimension_semantics=("arbitrary",),
                                             vmem_limit_bytes=VMEM_LIMIT),
        name="outproj",
    )(pool_n, attn, x2, gattn_p, wo_pool, wo_attn_p, gffn, wr_t, br_col)


def _dispatch_kernel(dest_ref, h_ref, xs_in, xs_out, sem):
    del xs_in
    td = h_ref.shape[0]

    def row_copy(i, k):
        d = dest_ref[k, i]
        return pltpu.make_async_copy(h_ref.at[pl.ds(i, 1)], xs_out.at[pl.ds(d, 1)], sem)

    def body(i, carry):
        for k in range(TOP_K):
            row_copy(i, k).start()
        return carry

    lax.fori_loop(0, td, body, 0)
    for _ in range(TOP_K):
        pltpu.make_async_copy(h_ref, xs_out.at[pl.ds(0, td)], sem).wait()


def _dispatch_call(dest, hp, xs_init, td):
    t, dw = hp.shape
    return pl.pallas_call(
        _dispatch_kernel,
        grid=(t // td,),
        in_specs=[
            pl.BlockSpec((TOP_K, td), lambda i: (0, i), memory_space=pltpu.SMEM),
            pl.BlockSpec((td, dw), lambda i: (i, 0)),
            pl.BlockSpec(memory_space=pl.ANY),
        ],
        out_specs=pl.BlockSpec(memory_space=pl.ANY),
        out_shape=jax.ShapeDtypeStruct(xs_init.shape, xs_init.dtype),
        scratch_shapes=[pltpu.SemaphoreType.DMA],
        input_output_aliases={2: 0},
        compiler_params=pltpu.CompilerParams(dimension_semantics=("arbitrary",),
                                             vmem_limit_bytes=VMEM_LIMIT),
        name="dispatch",
    )(dest, hp, xs_init)


CAST_ROWS = 64


def _expert_kernel(be_ref, nv_ref, xs_ref, wgu_ref, bgu_ref, wd_ref, bd_ref, y_ref, wgu_bf, wd_bf):
    i = pl.program_id(0)
    d = wgu_ref.shape[1]
    f = wd_ref.shape[1]

    @pl.when(i < nv_ref[0])
    def _():
        prev = be_ref[jnp.maximum(i - 1, 0)]
        changed = jnp.logical_or(i == 0, be_ref[i] != prev)

        @pl.when(changed)
        def _():
            def cast_gu(r, c):
                rows = pl.ds(pl.multiple_of(r * CAST_ROWS, CAST_ROWS), CAST_ROWS)
                wgu_bf[rows, :] = wgu_ref[0, rows, :].astype(BF16)
                return c

            def cast_d(r, c):
                rows = pl.ds(pl.multiple_of(r * CAST_ROWS, CAST_ROWS), CAST_ROWS)
                wd_bf[rows, :] = wd_ref[0, rows, :].astype(BF16)
                return c

            lax.fori_loop(0, d // CAST_ROWS, cast_gu, 0)
            lax.fori_loop(0, f // CAST_ROWS, cast_d, 0)

        w = xs_ref[...]
        x_lo = lax.bitcast_convert_type(w << 16, F32).astype(BF16)
        x_hi = lax.bitcast_convert_type(w & jnp.uint32(0xFFFF0000), F32).astype(BF16)
        gu = (jnp.dot(x_lo, wgu_bf[:d // 2, :], preferred_element_type=F32)
              + jnp.dot(x_hi, wgu_bf[d // 2:, :], preferred_element_type=F32) + bgu_ref[0])
        gate = jnp.minimum(gu[:, :f], SWIGLU_LIMIT)
        up = jnp.clip(gu[:, f:], -SWIGLU_LIMIT, SWIGLU_LIMIT)
        act = gate * jax.nn.sigmoid(SWIGLU_ALPHA * gate) * (up + 1.0)
        y_ref[...] = jnp.dot(act.astype(BF16), wd_bf[...], preferred_element_type=F32) + bd_ref[0]

    @pl.when(i >= nv_ref[0])
    def _():
        y_ref[...] = jnp.zeros(y_ref.shape, y_ref.dtype)


def _expert_call(block_expert, n_valid, xs, wgu, bgu, wd, bd, bm):
    r, dw = xs.shape
    e, d, f2 = wgu.shape
    f = wd.shape[1]
    nb = r // bm

    def xrow(i, be, nv):
        return (jnp.minimum(i, nv[0] - 1), 0)

    def wsel(i, be, nv):
        return (be[i], 0, 0)

    grid_spec = pltpu.PrefetchScalarGridSpec(
        num_scalar_prefetch=2,
        grid=(nb,),
        in_specs=[
            pl.BlockSpec((bm, dw), xrow),
            pl.BlockSpec((1, d, f2), wsel),
            pl.BlockSpec((1, 1, f2), wsel),
            pl.BlockSpec((1, f, d), wsel),
            pl.BlockSpec((1, 1, d), wsel),
        ],
        out_specs=pl.BlockSpec((bm, d), lambda i, be, nv: (i, 0)),
        scratch_shapes=[pltpu.VMEM((d, f2), BF16), pltpu.VMEM((f, d), BF16)],
    )
    return pl.pallas_call(
        _expert_kernel,
        grid_spec=grid_spec,
        out_shape=jax.ShapeDtypeStruct((r, d), F32),
        compiler_params=pltpu.CompilerParams(dimension_semantics=("arbitrary",),
                                             vmem_limit_bytes=VMEM_LIMIT),
        name="experts",
    )(block_expert, n_valid, xs, wgu, bgu, wd, bd)


def _combine_kernel(dest_ref, x1_ref, gate_ref, gfin_ref, yb_hbm, o_ref, ybuf, sem):
    tf, d = x1_ref.shape

    def body(i, carry):
        for k in range(TOP_K):
            r = dest_ref[k, i]
            pltpu.make_async_copy(yb_hbm.at[pl.ds(r, 1)], ybuf.at[k, pl.ds(i, 1)], sem).start()
        return carry

    lax.fori_loop(0, tf, body, 0)
    for k in range(TOP_K):
        pltpu.make_async_copy(yb_hbm.at[pl.ds(0, tf)], ybuf.at[k], sem).wait()
    g = gate_ref[...]
    y = x1_ref[...]
    for k in range(TOP_K):
        y = y + g[:, k:k + 1] * ybuf[k]
    o_ref[...] = _rms(y, gfin_ref[...], d)


def _combine_call(dest, x1, gates_t, gfin, yb, tf):
    t, d = x1.shape
    return pl.pallas_call(
        _combine_kernel,
        grid=(t // tf,),
        in_specs=[
            pl.BlockSpec((TOP_K, tf), lambda i: (0, i), memory_space=pltpu.SMEM),
            pl.BlockSpec((tf, d), lambda i: (i, 0)),
            pl.BlockSpec((tf, TOP_K), lambda i: (i, 0)),
            pl.BlockSpec(gfin.shape, lambda i: (0, 0)),
            pl.BlockSpec(memory_space=pl.ANY),
        ],
        out_specs=pl.BlockSpec((tf, d), lambda i: (i, 0)),
        out_shape=jax.ShapeDtypeStruct((t, d), F32),
        scratch_shapes=[pltpu.VMEM((TOP_K, tf, d), F32), pltpu.SemaphoreType.DMA],
        compiler_params=pltpu.CompilerParams(dimension_semantics=("arbitrary",),
                                             vmem_limit_bytes=VMEM_LIMIT),
        name="combine",
    )(dest, x1, gates_t, gfin, yb)


def _pad_head_cols(w, widths_in, place):
    k = w.shape[0]
    per = sum(widths_in)
    blocks = []
    for hh in range(N_HEADS):
        blocks.append(place(w[:, hh * per:(hh + 1) * per]))
    return jnp.concatenate(blocks, axis=1)


def _layer(x2, pos3, batch, seq, g_attn_norm, w_in, w_pool, b_pool, pool_scale, g_q_a, w_q_b,
           g_kv_a, w_kv_b, g_out_pool, g_out_attn, w_out, g_ffn_norm, w_router, b_router,
           w_gate_up, b_gate_up, w_down, b_down, tiles):
    tm, tq, tmo, td, bm, tf = tiles
    t, d = x2.shape
    d_pool = pool_scale.shape[0]
    gdim = d_pool // N_POOL_GROUPS
    q_rank = g_q_a.shape[0]
    kv_rank = g_kv_a.shape[0]
    half = QK_ROPE_DIM // 2

    o1, o2, o3 = d_pool, d_pool + q_rank, d_pool + q_rank + kv_rank
    zk = lambda n: jnp.zeros((d, n), F32)
    w_in_p = jnp.concatenate([w_in[:, :o3], zk(QK_NOPE_DIM), w_in[:, o3:], zk(LANE - QK_DIM)],
                             axis=1).astype(BF16)
    zq = lambda n: jnp.zeros((q_rank, n), F32)
    wq_p = _pad_head_cols(w_q_b, (QK_DIM,), lambda c: jnp.concatenate([c, zq(LANE - QK_DIM)], axis=1)
                          ).astype(BF16)
    zkv = lambda n: jnp.zeros((kv_rank, n), F32)
    per = QK_NOPE_DIM + V_HEAD_DIM
    wk_p = _pad_head_cols(w_kv_b, (per,), lambda c: jnp.concatenate(
        [c[:, :QK_NOPE_DIM], zkv(LANE - QK_NOPE_DIM)], axis=1))
    wv_p = _pad_head_cols(w_kv_b, (per,), lambda c: jnp.concatenate(
        [c[:, QK_NOPE_DIM:], zkv(LANE - V_HEAD_DIM)], axis=1))
    wkv_p = jnp.concatenate([wk_p, wv_p], axis=1).astype(BF16)
    inv = ROPE_THETA ** (-jnp.arange(half, dtype=F32) / half)
    inv_col = inv.reshape(half, 1)

    row = lambda v: v.reshape(1, -1)
    grp = lambda v: v.reshape(N_POOL_GROUPS, 1, gdim)

    pool_n, q, k, v = _proj_call(
        x2, pos3, row(g_attn_norm), w_in_p, inv_col, w_pool.astype(BF16), grp(b_pool),
        grp(pool_scale), grp(g_out_pool), row(g_q_a), wq_p, row(g_kv_a), wkv_p, seq, tm)

    attn = _attn_call(q, k, v, batch, seq, tq)

    d_attn = N_HEADS * V_HEAD_DIM
    wo_pool = w_out[:d_pool].astype(BF16)
    wo_attn = w_out[d_pool:].reshape(N_HEADS, V_HEAD_DIM, d)
    wo_attn_p = jnp.concatenate([wo_attn, jnp.zeros((N_HEADS, HEAD_PAD - V_HEAD_DIM, d), F32)],
                                axis=1).reshape(N_HEADS * HEAD_PAD, d).astype(BF16)
    gattn_p = jnp.concatenate([g_out_attn.reshape(N_HEADS, V_HEAD_DIM),
                               jnp.zeros((N_HEADS, HEAD_PAD - V_HEAD_DIM), F32)], axis=1).reshape(1, -1)

    x1, hp, idx, gates, rank, counts = _outproj_call(
        pool_n, attn, x2, gattn_p, wo_pool, wo_attn_p, row(g_ffn_norm), w_router.T,
        b_router.reshape(-1, 1), tmo)

    n_assign = t * TOP_K
    cnt = counts[:, 0]
    padded = ((cnt + bm - 1) // bm) * bm
    pad_end = jnp.cumsum(padded)
    pad_start = pad_end - padded
    e_ids = jnp.arange(N_EXPERTS, dtype=jnp.int32)
    dest = rank + jnp.sum(jnp.where(idx[None] == e_ids[:, None, None], pad_start[:, None, None], 0),
                          axis=0)
    nb = n_assign // bm + N_EXPERTS
    n_rows = nb * bm
    blk_row0 = jnp.arange(nb, dtype=jnp.int32) * bm
    block_expert = jnp.minimum(jnp.sum((pad_end[None, :] <= blk_row0[:, None]).astype(jnp.int32), axis=1),
                               N_EXPERTS - 1)
    n_valid = (pad_end[-1] // bm).astype(jnp.int32).reshape(1)

    xs = _dispatch_call(dest, hp, jnp.zeros((n_rows, d // 2), jnp.uint32), td)
    yb = _expert_call(block_expert, n_valid, xs, w_gate_up, b_gate_up.reshape(N_EXPERTS, 1, -1),
                      w_down, b_down.reshape(N_EXPERTS, 1, -1), bm)
    return x1, dest, gates, yb


def _forward(x, positions, g_attn_norm, w_in, w_pool, b_pool, pool_scale, g_q_a, w_q_b, g_kv_a,
             w_kv_b, g_out_pool, g_out_attn, w_out, g_ffn_norm, w_router, b_router, w_gate_up,
             b_gate_up, w_down, b_down, g_final, tiles):
    batch, seq, d = x.shape
    t = batch * seq
    tm, tq, tmo, td, bm, tf = tiles
    x2 = x.reshape(t, d)
    pos3 = positions.reshape(t // tm, 1, tm)
    depth = g_attn_norm.shape[0]
    assert depth == 1
    l = 0
    x1, dest, gates, yb = _layer(
        x2, pos3, batch, seq, g_attn_norm[l], w_in[l], w_pool[l], b_pool[l], pool_scale[l],
        g_q_a[l], w_q_b[l], g_kv_a[l], w_kv_b[l], g_out_pool[l], g_out_attn[l], w_out[l],
        g_ffn_norm[l], w_router[l], b_router[l], w_gate_up[l], b_gate_up[l], w_down[l],
        b_down[l], tiles)
    out = _combine_call(dest, x1, gates.T, g_final.reshape(1, -1), yb, tf)
    return out.reshape(batch, seq, d)


def _tiles_for(seq):
    tm = min(512, seq)
    tq = min(512, seq // 2)
    tmo = min(512, seq)
    td = min(512, seq)
    bm = 256 if seq >= 2048 else 128
    tf = min(256, seq)
    return (tm, tq, tmo, td, bm, tf)


def kernel(x, positions, g_attn_norm, w_in, w_pool, b_pool, pool_scale, g_q_a, w_q_b, g_kv_a, w_kv_b, g_out_pool, g_out_attn, w_out, g_ffn_norm, w_router, b_router, w_gate_up, b_gate_up, w_down, b_down, g_final):
    tiles = _tiles_for(x.shape[1])
    return _forward(x, positions, g_attn_norm, w_in, w_pool, b_pool, pool_scale, g_q_a, w_q_b,
                    g_kv_a, w_kv_b, g_out_pool, g_out_attn, w_out, g_ffn_norm, w_router, b_router,
                    w_gate_up, b_gate_up, w_down, b_down, g_final, tiles)
```

```python
import functools

import jax
import jax.numpy as jnp
from jax import lax
from jax.experimental import pallas as pl
from jax.experimental.pallas import tpu as pltpu

POOL_WINDOWS = (2, 4, 8, 16)
N_POOL_GROUPS = 4
N_HEADS = 8
V_HEAD_DIM = 64
QK_NOPE_DIM = 64
QK_ROPE_DIM = 32
QK_DIM = QK_NOPE_DIM + QK_ROPE_DIM
ROPE_THETA = 10000.0
N_EXPERTS = 32
TOP_K = 4
SWIGLU_ALPHA = 1.702
SWIGLU_LIMIT = 7.0
RMS_EPS = 1e-6
LOG2_E = 1.4426950408889634

LANE = 128
HEAD_PAD = 128
POOL_HALO = 16

VMEM_LIMIT = 56 * 1024 * 1024

F32 = jnp.float32
BF16 = jnp.bfloat16


def _rms(x, g, n):
    ms = jnp.sum(x * x, axis=-1, keepdims=True) * (1.0 / n)
    return x * lax.rsqrt(ms + RMS_EPS) * g


def _proj_kernel(blocks_per_seq, x_ref, pos_ref, g_attn_ref, w_in_ref, inv_ref, wpool_ref,
                 bpool_ref, pscale_ref, gpool_ref, gq_ref, wq_ref, gkv_ref, wkv_ref,
                 pool_out, q_out, k_out, v_out, ext_scr):
    tm = x_ref.shape[0]
    d_model = x_ref.shape[1]
    d_pool = pool_out.shape[1]
    gdim = d_pool // N_POOL_GROUPS
    q_rank = gq_ref.shape[1]
    kv_rank = gkv_ref.shape[1]
    i = pl.program_id(0)
    blk_in_seq = i % blocks_per_seq

    x = x_ref[...]
    h = _rms(x, g_attn_ref[...], d_model).astype(BF16)
    proj = jnp.dot(h, w_in_ref[...], preferred_element_type=F32)
    u = proj[:, :d_pool]
    qlat = proj[:, d_pool:d_pool + q_rank]
    kvlat = proj[:, d_pool + q_rank:d_pool + q_rank + kv_rank]
    krb = proj[:, d_pool + q_rank + kv_rank:]

    @pl.when(blk_in_seq == 0)
    def _():
        ext_scr[0:POOL_HALO, :] = jnp.zeros((POOL_HALO, d_pool), F32)

    @pl.when(blk_in_seq != 0)
    def _():
        ext_scr[0:POOL_HALO, :] = ext_scr[tm:tm + POOL_HALO, :]

    ext_scr[POOL_HALO:, :] = u
    t_in_seq = (blk_in_seq * tm + lax.broadcasted_iota(jnp.int32, (tm, gdim), 0) + 1).astype(F32)
    ys = []
    ssq = jnp.zeros((tm, 1), F32)
    for g, w in enumerate(POOL_WINDOWS):
        a = ext_scr[:, g * gdim:(g + 1) * gdim]
        shift = 1
        while shift < w:
            a = a + pltpu.roll(a, shift, axis=0)
            shift *= 2
        win = a[POOL_HALO:, :]
        cnt = jnp.minimum(t_in_seq, float(w))
        ug = u[:, g * gdim:(g + 1) * gdim]
        diff = (win / cnt - ug).astype(BF16)
        mixed = jnp.dot(diff, wpool_ref[g], preferred_element_type=F32) + bpool_ref[g]
        y = mixed * pscale_ref[g]
        ssq = ssq + jnp.sum(y * y, axis=-1, keepdims=True)
        ys.append(y)
    rinv = lax.rsqrt(ssq * (1.0 / d_pool) + RMS_EPS)
    for g in range(N_POOL_GROUPS):
        pool_out[:, g * gdim:(g + 1) * gdim] = (ys[g] * rinv * gpool_ref[g]).astype(pool_out.dtype)

    pos = pos_ref[0].astype(F32)
    ang_t = inv_ref[...] * pos
    cos_t = jnp.cos(ang_t)
    sin_t = jnp.sin(ang_t)
    half = QK_ROPE_DIM // 2
    ones = jnp.ones((QK_NOPE_DIM, tm), F32)
    c_t = jnp.concatenate([ones, cos_t, cos_t, jnp.ones((LANE - QK_DIM, tm), F32)], axis=0)
    sa_t = jnp.concatenate([jnp.zeros((QK_NOPE_DIM + half, tm), F32), sin_t,
                            jnp.zeros((LANE - QK_DIM, tm), F32)], axis=0)
    sb_t = jnp.concatenate([jnp.zeros((QK_NOPE_DIM, tm), F32), sin_t,
                            jnp.zeros((LANE - QK_DIM + half, tm), F32)], axis=0)
    c = c_t.T
    sa = sa_t.T
    sb = sb_t.T

    def rope(blk, c_, sa_, sb_):
        return blk * c_ + pltpu.roll(blk, half, axis=1) * sa_ - pltpu.roll(blk, LANE - half, axis=1) * sb_

    qn = _rms(qlat, gq_ref[...], q_rank).astype(BF16)
    q = jnp.dot(qn, wq_ref[...], preferred_element_type=F32)
    scale = QK_DIM ** -0.5 * LOG2_E
    cq, saq, sbq = c * scale, sa * scale, sb * scale
    for hh in range(N_HEADS):
        sl = slice(hh * HEAD_PAD, (hh + 1) * HEAD_PAD)
        q_out[:, sl] = rope(q[:, sl], cq, saq, sbq).astype(q_out.dtype)

    kvn = _rms(kvlat, gkv_ref[...], kv_rank).astype(BF16)
    kv = jnp.dot(kvn, wkv_ref[...], preferred_element_type=F32)
    kr = rope(krb, c, sa, sb)
    for hh in range(N_HEADS):
        sl = slice(hh * HEAD_PAD, (hh + 1) * HEAD_PAD)
        k_out[:, sl] = (kv[:, sl] + kr).astype(k_out.dtype)
    vlane = lax.broadcasted_iota(jnp.int32, (1, N_HEADS * HEAD_PAD), 1) % HEAD_PAD
    ones_col = (vlane == V_HEAD_DIM).astype(F32)
    v_out[...] = (kv[:, N_HEADS * HEAD_PAD:] + ones_col).astype(v_out.dtype)


def _proj_call(x2, pos3, g_attn, w_in_p, inv_col, wpool, bpool, pscale, gpool, gq, wq_p, gkv, wkv_p,
               seq, tm):
    t, d = x2.shape
    d_pool = pscale.shape[0] * pscale.shape[2]
    hp = N_HEADS * HEAD_PAD
    nblk = t // tm
    const2 = lambda i: (0, 0)
    const3 = lambda i: (0, 0, 0)
    in_specs = [
        pl.BlockSpec((tm, d), lambda i: (i, 0)),
        pl.BlockSpec((1, 1, tm), lambda i: (i, 0, 0)),
        pl.BlockSpec(g_attn.shape, const2),
        pl.BlockSpec(w_in_p.shape, const2),
        pl.BlockSpec(inv_col.shape, const2),
        pl.BlockSpec(wpool.shape, const3),
        pl.BlockSpec(bpool.shape, const3),
        pl.BlockSpec(pscale.shape, const3),
        pl.BlockSpec(gpool.shape, const3),
        pl.BlockSpec(gq.shape, const2),
        pl.BlockSpec(wq_p.shape, const2),
        pl.BlockSpec(gkv.shape, const2),
        pl.BlockSpec(wkv_p.shape, const2),
    ]
    out_shape = [
        jax.ShapeDtypeStruct((t, d_pool), BF16),
        jax.ShapeDtypeStruct((t, hp), BF16),
        jax.ShapeDtypeStruct((t, hp), BF16),
        jax.ShapeDtypeStruct((t, hp), BF16),
    ]
    out_specs = [
        pl.BlockSpec((tm, d_pool), lambda i: (i, 0)),
        pl.BlockSpec((tm, hp), lambda i: (i, 0)),
        pl.BlockSpec((tm, hp), lambda i: (i, 0)),
        pl.BlockSpec((tm, hp), lambda i: (i, 0)),
    ]
    return pl.pallas_call(
        functools.partial(_proj_kernel, seq // tm),
        grid=(nblk,),
        in_specs=in_specs,
        out_specs=out_specs,
        out_shape=out_shape,
        scratch_shapes=[pltpu.VMEM((tm + POOL_HALO, d_pool), F32)],
        compiler_params=pltpu.CompilerParams(dimension_semantics=("arbitrary",),
                                             vmem_limit_bytes=VMEM_LIMIT),
        name="proj",
    )(x2, pos3, g_attn, w_in_p, inv_col, wpool, bpool, pscale, gpool, gq, wq_p, gkv, wkv_p)


ROWSUM_LANE = V_HEAD_DIM


def _attn_kernel(q_ref, k_ref, v_ref, o_ref, m_a, acc_a, m_b, acc_b):
    th = q_ref.shape[0] // 2
    qi = pl.program_id(2)
    nt = th // LANE
    for m_ref, acc_ref in ((m_a, acc_a), (m_b, acc_b)):
        m_ref[...] = jnp.full(m_ref.shape, -jnp.inf, F32)
        acc_ref[...] = jnp.zeros(acc_ref.shape, F32)

    def load_kv(kc):
        start = pl.multiple_of(kc * th, th)
        return k_ref[pl.ds(start, th), :], v_ref[pl.ds(start, th), :]

    def half_step(row0, k, v, m_ref, acc_ref, masked):
        q = q_ref[row0:row0 + th, :]
        s = lax.dot_general(q, k, (((1,), (1,)), ((), ())), preferred_element_type=F32)
        if masked:
            row = lax.broadcasted_iota(jnp.int32, (th, th), 0)
            col = lax.broadcasted_iota(jnp.int32, (th, th), 1)
            s = jnp.where(col <= row, s, -jnp.inf)
        tiles = [s[:, j * LANE:(j + 1) * LANE] for j in range(nt)]
        mt = tiles[0]
        for tl in tiles[1:]:
            mt = jnp.maximum(mt, tl)
        m_prev = m_ref[...]
        m_new = jnp.maximum(m_prev, jnp.max(mt, axis=-1, keepdims=True))
        alpha = jnp.exp2(m_prev - m_new)
        p = jnp.concatenate([jnp.exp2(tl - m_new).astype(BF16) for tl in tiles], axis=1)
        acc_ref[...] = alpha * acc_ref[...] + jnp.dot(p, v, preferred_element_type=F32)
        m_ref[...] = m_new

    def body(kc, carry):
        k, v = load_kv(kc)
        half_step(0, k, v, m_a, acc_a, False)
        half_step(th, k, v, m_b, acc_b, False)
        return carry

    lax.fori_loop(0, 2 * qi, body, 0)
    k, v = load_kv(2 * qi)
    half_step(0, k, v, m_a, acc_a, True)
    half_step(th, k, v, m_b, acc_b, False)
    k, v = load_kv(2 * qi + 1)
    half_step(th, k, v, m_b, acc_b, True)

    lane = lax.broadcasted_iota(jnp.int32, (th, HEAD_PAD), 1)
    for row0, acc_ref in ((0, acc_a), (th, acc_b)):
        acc = acc_ref[...]
        out = acc / acc[:, ROWSUM_LANE:ROWSUM_LANE + 1]
        o_ref[row0:row0 + th, :] = jnp.where(lane < V_HEAD_DIM, out, 0.0).astype(o_ref.dtype)


def _attn_call(q, k, v, batch, seq, th):
    t, hp = q.shape
    tq = 2 * th
    nq = seq // tq
    return pl.pallas_call(
        _attn_kernel,
        grid=(batch, N_HEADS, nq),
        in_specs=[
            pl.BlockSpec((tq, HEAD_PAD), lambda b, h, qi: (b * nq + qi, h)),
            pl.BlockSpec((seq, HEAD_PAD), lambda b, h, qi: (b, h)),
            pl.BlockSpec((seq, HEAD_PAD), lambda b, h, qi: (b, h)),
        ],
        out_specs=pl.BlockSpec((tq, HEAD_PAD), lambda b, h, qi: (b * nq + qi, h)),
        out_shape=jax.ShapeDtypeStruct((t, hp), F32),
        scratch_shapes=[pltpu.VMEM((th, HEAD_PAD), F32), pltpu.VMEM((th, HEAD_PAD), F32),
                        pltpu.VMEM((th, HEAD_PAD), F32), pltpu.VMEM((th, HEAD_PAD), F32)],
        compiler_params=pltpu.CompilerParams(
            dimension_semantics=("arbitrary", "arbitrary", "arbitrary"),
            vmem_limit_bytes=VMEM_LIMIT),
        name="attn",
    )(q, k, v)


def _outproj_kernel(pool_ref, attn_ref, x_ref, gattn_ref, wop_ref, woa_ref, gffn_ref, wr_ref,
                    br_ref, x1_out, hp_out, idx_out, gate_out, rank_out, cnt_out, run_scr):
    tm, d = x_ref.shape
    d_attn = N_HEADS * V_HEAD_DIM
    i = pl.program_id(0)

    @pl.when(i == 0)
    def _():
        run_scr[...] = jnp.zeros(run_scr.shape, F32)

    attn = attn_ref[...]
    attn_n = _rms(attn, gattn_ref[...], d_attn).astype(BF16)
    mix = (jnp.dot(pool_ref[...], wop_ref[...], preferred_element_type=F32)
           + jnp.dot(attn_n, woa_ref[...], preferred_element_type=F32))
    x1 = x_ref[...] + mix
    x1_out[...] = x1
    h2 = _rms(x1, gffn_ref[...], d)

    hb = h2.astype(BF16).astype(F32)
    lo = lax.bitcast_convert_type(hb[:, :d // 2], jnp.uint32)
    hi = lax.bitcast_convert_type(hb[:, d // 2:], jnp.uint32)
    hp_out[...] = (lo >> 16) | (hi & jnp.uint32(0xFFFF0000))

    logits = lax.dot_general(wr_ref[...], h2, (((1,), (1,)), ((), ())),
                             preferred_element_type=F32,
                             precision=lax.Precision.HIGHEST) + br_ref[...]
    e_iota = lax.broadcasted_iota(jnp.int32, (N_EXPERTS, tm), 0)
    work = logits
    idxs, vals = [], []
    for _ in range(TOP_K):
        mx = jnp.max(work, axis=0, keepdims=True)
        sel = jnp.min(jnp.where(work == mx, e_iota, N_EXPERTS), axis=0, keepdims=True)
        idxs.append(sel)
        vals.append(mx)
        work = jnp.where(e_iota == sel, -jnp.inf, work)
    exps = [jnp.exp(v - vals[0]) for v in vals]
    denom = exps[0] + exps[1] + exps[2] + exps[3]
    gate_out[...] = jnp.concatenate([e / denom for e in exps], axis=0)
    idx_out[...] = jnp.concatenate(idxs, axis=0)

    onehot = jnp.zeros((N_EXPERTS, tm), F32)
    for sel in idxs:
        onehot = onehot + (e_iota == sel).astype(F32)
    r_io = lax.broadcasted_iota(jnp.int32, (tm, tm), 0)
    c_io = lax.broadcasted_iota(jnp.int32, (tm, tm), 1)
    upper = (r_io < c_io).astype(BF16)
    prefix = jnp.dot(onehot.astype(BF16), upper, preferred_element_type=F32) + run_scr[:, 0:1]
    ranks = [jnp.sum(jnp.where(e_iota == sel, prefix, 0.0), axis=0, keepdims=True) for sel in idxs]
    rank_out[...] = jnp.concatenate(ranks, axis=0).astype(jnp.int32)
    run_new = run_scr[...] + jnp.sum(onehot, axis=1, keepdims=True)
    run_scr[...] = run_new
    cnt_out[...] = run_new.astype(jnp.int32)


def _outproj_call(pool_n, attn, x2, gattn_p, wo_pool, wo_attn_p, gffn, wr_t, br_col, tm):
    t, d = x2.shape
    nblk = t // tm
    const2 = lambda i: (0, 0)
    row = lambda i: (i, 0)
    col = lambda i: (0, i)
    in_specs = [
        pl.BlockSpec((tm, pool_n.shape[1]), row),
        pl.BlockSpec((tm, attn.shape[1]), row),
        pl.BlockSpec((tm, d), row),
        pl.BlockSpec(gattn_p.shape, const2),
        pl.BlockSpec(wo_pool.shape, const2),
        pl.BlockSpec(wo_attn_p.shape, const2),
        pl.BlockSpec(gffn.shape, const2),
        pl.BlockSpec(wr_t.shape, const2),
        pl.BlockSpec(br_col.shape, const2),
    ]
    out_shape = [
        jax.ShapeDtypeStruct((t, d), F32),
        jax.ShapeDtypeStruct((t, d // 2), jnp.uint32),
        jax.ShapeDtypeStruct((TOP_K, t), jnp.int32),
        jax.ShapeDtypeStruct((TOP_K, t), F32),
        jax.ShapeDtypeStruct((TOP_K, t), jnp.int32),
        jax.ShapeDtypeStruct((N_EXPERTS, LANE), jnp.int32),
    ]
    out_specs = [
        pl.BlockSpec((tm, d), row),
        pl.BlockSpec((tm, d // 2), row),
        pl.BlockSpec((TOP_K, tm), col),
        pl.BlockSpec((TOP_K, tm), col),
        pl.BlockSpec((TOP_K, tm), col),
        pl.BlockSpec((N_EXPERTS, LANE), const2),
    ]
    return pl.pallas_call(
        _outproj_kernel,
        grid=(nblk,),
        in_specs=in_specs,
        out_specs=out_specs,
        out_shape=out_shape,
        scratch_shapes=[pltpu.VMEM((N_EXPERTS, LANE), F32)],
        compiler_params=pltpu.CompilerParams(dimension_semantics=("arbitrary",),
                                             vmem_limit_bytes=VMEM_LIMIT),
        name="outproj",
    )(pool_n, attn, x2, gattn_p, wo_pool, wo_attn_p, gffn, wr_t, br_col)


ROW_UNROLL = 8


def _dispatch_kernel(bs_ref, nblk_ref, dest_ref, h_ref, xs_out, zbuf, sem, zsem):
    td = h_ref.shape[0]
    bm = zbuf.shape[0]
    n_exp = bs_ref.shape[0]

    @pl.when(pl.program_id(0) == 0)
    def _():
        zbuf[...] = jnp.zeros(zbuf.shape, zbuf.dtype)
        total = xs_out.shape[0] // bm
        first_tail = bs_ref[n_exp - 1] + nblk_ref[n_exp - 1]

        def z_copy(blk):
            row0 = pl.multiple_of(blk * bm, bm)
            return pltpu.make_async_copy(zbuf, xs_out.at[pl.ds(row0, bm)], zsem)

        def each_partial_block(fn):
            def per_expert(e, c):
                @pl.when(nblk_ref[e] > 0)
                def _():
                    fn(z_copy(bs_ref[e] + nblk_ref[e] - 1))
                return c

            def per_tail(b, c):
                fn(z_copy(b))
                return c

            lax.fori_loop(0, n_exp, per_expert, 0)
            lax.fori_loop(first_tail, total, per_tail, 0)

        each_partial_block(lambda cp: cp.start())
        each_partial_block(lambda cp: cp.wait())

    def body(g, carry):
        base = g * ROW_UNROLL
        for u in range(ROW_UNROLL):
            i = base + u
            for k in range(TOP_K):
                d = dest_ref[i * TOP_K + k]
                pltpu.make_async_copy(h_ref.at[pl.ds(i, 1)], xs_out.at[pl.ds(d, 1)], sem).start(
                    priority=k % 2)
        return carry

    lax.fori_loop(0, td // ROW_UNROLL, body, 0)
    for _ in range(TOP_K):
        pltpu.make_async_copy(h_ref, xs_out.at[pl.ds(0, td)], sem).wait()


def _dispatch_call(blk_start, nblk, dest, hp, n_rows, td, bm):
    t, dw = hp.shape
    grid_spec = pltpu.PrefetchScalarGridSpec(
        num_scalar_prefetch=2,
        grid=(t // td,),
        in_specs=[
            pl.BlockSpec((TOP_K * td,), lambda i, bs, nb: (i,), memory_space=pltpu.SMEM),
            pl.BlockSpec((td, dw), lambda i, bs, nb: (i, 0)),
        ],
        out_specs=pl.BlockSpec(memory_space=pl.ANY),
        scratch_shapes=[pltpu.VMEM((bm, dw), hp.dtype), pltpu.SemaphoreType.DMA,
                        pltpu.SemaphoreType.DMA],
    )
    return pl.pallas_call(
        _dispatch_kernel,
        grid_spec=grid_spec,
        out_shape=jax.ShapeDtypeStruct((n_rows, dw), hp.dtype),
        compiler_params=pltpu.CompilerParams(dimension_semantics=("arbitrary",),
                                             vmem_limit_bytes=VMEM_LIMIT),
        name="dispatch",
    )(blk_start, nblk, dest, hp)


CAST_ROWS = 64


def _expert_kernel(bs_ref, nblk_ref, wgu_ref, bgu_ref, wd_ref, bd_ref, xs_hbm, y_hbm,
                   wgu_bf, wd_bf, xbuf, ybuf, xsem, ysem):
    e = pl.program_id(0)
    n_exp = pl.num_programs(0)
    bm = xbuf.shape[1]
    d = wgu_ref.shape[1]
    f = wd_ref.shape[1]
    n = nblk_ref[e]
    blk0 = bs_ref[e]

    def x_copy(j, slot):
        row0 = pl.multiple_of((blk0 + j) * bm, bm)
        return pltpu.make_async_copy(xs_hbm.at[pl.ds(row0, bm)], xbuf.at[slot], xsem.at[slot])

    def y_copy(blk, slot):
        row0 = pl.multiple_of(blk * bm, bm)
        return pltpu.make_async_copy(ybuf.at[slot], y_hbm.at[pl.ds(row0, bm)], ysem.at[slot])

    @pl.when(n > 0)
    def _():
        x_copy(0, 0).start()

        def cast_gu(r, c):
            rows = pl.ds(pl.multiple_of(r * CAST_ROWS, CAST_ROWS), CAST_ROWS)
            wgu_bf[rows, :] = wgu_ref[0, rows, :].astype(BF16)
            return c

        def cast_d(r, c):
            rows = pl.ds(pl.multiple_of(r * CAST_ROWS, CAST_ROWS), CAST_ROWS)
            wd_bf[rows, :] = wd_ref[0, rows, :].astype(BF16)
            return c

        lax.fori_loop(0, d // CAST_ROWS, cast_gu, 0)
        lax.fori_loop(0, f // CAST_ROWS, cast_d, 0)

        def block(j, c):
            slot = j % 2
            x_copy(j, slot).wait()

            @pl.when(j + 1 < n)
            def _():
                x_copy(j + 1, 1 - slot).start()

            @pl.when(j >= 2)
            def _():
                y_copy(blk0 + j - 2, slot).wait()

            w = xbuf[slot]
            x_lo = lax.bitcast_convert_type(w << 16, F32).astype(BF16)
            x_hi = lax.bitcast_convert_type(w & jnp.uint32(0xFFFF0000), F32).astype(BF16)
            gu = (jnp.dot(x_lo, wgu_bf[:d // 2, :], preferred_element_type=F32)
                  + jnp.dot(x_hi, wgu_bf[d // 2:, :], preferred_element_type=F32) + bgu_ref[0])
            gate = jnp.minimum(gu[:, :f], SWIGLU_LIMIT)
            up = jnp.clip(gu[:, f:], -SWIGLU_LIMIT, SWIGLU_LIMIT)
            act = gate * jax.nn.sigmoid(SWIGLU_ALPHA * gate) * (up + 1.0)
            ybuf[slot] = jnp.dot(act.astype(BF16), wd_bf[...], preferred_element_type=F32) + bd_ref[0]
            y_copy(blk0 + j, slot).start()
            return c

        lax.fori_loop(0, n, block, 0)

        @pl.when(n >= 2)
        def _():
            y_copy(blk0 + n - 2, n % 2).wait()

        y_copy(blk0 + n - 1, (n - 1) % 2).wait()

    @pl.when(e == n_exp - 1)
    def _():
        first = blk0 + n
        total = y_hbm.shape[0] // bm
        ybuf[0] = jnp.zeros(ybuf.shape[1:], ybuf.dtype)

        def fill(b, c):
            y_copy(b, 0).start()
            return c

        def drain(b, c):
            y_copy(b, 0).wait()
            return c

        lax.fori_loop(first, total, fill, 0)
        lax.fori_loop(first, total, drain, 0)


def _expert_call(blk_start, nblk, xs, wgu, bgu, wd, bd, bm):
    r, dw = xs.shape
    e, d, f2 = wgu.shape
    f = wd.shape[1]

    def wsel(i, bs, nb):
        return (i, 0, 0)

    grid_spec = pltpu.PrefetchScalarGridSpec(
        num_scalar_prefetch=2,
        grid=(e,),
        in_specs=[
            pl.BlockSpec((1, d, f2), wsel),
            pl.BlockSpec((1, 1, f2), wsel),
            pl.BlockSpec((1, f, d), wsel),
            pl.BlockSpec((1, 1, d), wsel),
            pl.BlockSpec(memory_space=pl.ANY),
        ],
        out_specs=pl.BlockSpec(memory_space=pl.ANY),
        scratch_shapes=[pltpu.VMEM((d, f2), BF16), pltpu.VMEM((f, d), BF16),
                        pltpu.VMEM((2, bm, dw), xs.dtype), pltpu.VMEM((2, bm, d), F32),
                        pltpu.SemaphoreType.DMA((2,)), pltpu.SemaphoreType.DMA((2,))],
    )
    return pl.pallas_call(
        _expert_kernel,
        grid_spec=grid_spec,
        out_shape=jax.ShapeDtypeStruct((r, d), F32),
        compiler_params=pltpu.CompilerParams(dimension_semantics=("arbitrary",),
                                             vmem_limit_bytes=VMEM_LIMIT),
        name="experts",
    )(blk_start, nblk, wgu, bgu, wd, bd, xs)


def _combine_kernel(dest_ref, x1_ref, gate_ref, gfin_ref, yb_hbm, o_ref, ybuf, sem):
    tf, d = x1_ref.shape

    def body(g, carry):
        base = g * ROW_UNROLL
        for u in range(ROW_UNROLL):
            i = base + u
            for k in range(TOP_K):
                r = dest_ref[i * TOP_K + k]
                pltpu.make_async_copy(yb_hbm.at[pl.ds(r, 1)], ybuf.at[k, pl.ds(i, 1)], sem).start(
                    priority=k % 2)
        return carry

    lax.fori_loop(0, tf // ROW_UNROLL, body, 0)
    for k in range(TOP_K):
        pltpu.make_async_copy(yb_hbm.at[pl.ds(0, tf)], ybuf.at[k], sem).wait()
    g = gate_ref[...]
    y = x1_ref[...]
    for k in range(TOP_K):
        y = y + g[:, k:k + 1] * ybuf[k]
    o_ref[...] = _rms(y, gfin_ref[...], d)


def _combine_call(dest, x1, gates_t, gfin, yb, tf):
    t, d = x1.shape
    return pl.pallas_call(
        _combine_kernel,
        grid=(t // tf,),
        in_specs=[
            pl.BlockSpec((TOP_K * tf,), lambda i: (i,), memory_space=pltpu.SMEM),
            pl.BlockSpec((tf, d), lambda i: (i, 0)),
            pl.BlockSpec((tf, TOP_K), lambda i: (i, 0)),
            pl.BlockSpec(gfin.shape, lambda i: (0, 0)),
            pl.BlockSpec(memory_space=pl.ANY),
        ],
        out_specs=pl.BlockSpec((tf, d), lambda i: (i, 0)),
        out_shape=jax.ShapeDtypeStruct((t, d), F32),
        scratch_shapes=[pltpu.VMEM((TOP_K, tf, d), F32), pltpu.SemaphoreType.DMA],
        compiler_params=pltpu.CompilerParams(dimension_semantics=("arbitrary",),
                                             vmem_limit_bytes=VMEM_LIMIT),
        name="combine",
    )(dest, x1, gates_t, gfin, yb)


def _pad_head_cols(w, widths_in, place):
    k = w.shape[0]
    per = sum(widths_in)
    blocks = []
    for hh in range(N_HEADS):
        blocks.append(place(w[:, hh * per:(hh + 1) * per]))
    return jnp.concatenate(blocks, axis=1)


def _layer(x2, pos3, batch, seq, g_attn_norm, w_in, w_pool, b_pool, pool_scale, g_q_a, w_q_b,
           g_kv_a, w_kv_b, g_out_pool, g_out_attn, w_out, g_ffn_norm, w_router, b_router,
           w_gate_up, b_gate_up, w_down, b_down, tiles):
    tm, tq, tmo, td, bm, tf = tiles
    t, d = x2.shape
    d_pool = pool_scale.shape[0]
    gdim = d_pool // N_POOL_GROUPS
    q_rank = g_q_a.shape[0]
    kv_rank = g_kv_a.shape[0]
    half = QK_ROPE_DIM // 2

    o1, o2, o3 = d_pool, d_pool + q_rank, d_pool + q_rank + kv_rank
    zk = lambda n: jnp.zeros((d, n), F32)
    w_in_p = jnp.concatenate([w_in[:, :o3], zk(QK_NOPE_DIM), w_in[:, o3:], zk(LANE - QK_DIM)],
                             axis=1).astype(BF16)
    zq = lambda n: jnp.zeros((q_rank, n), F32)
    wq_p = _pad_head_cols(w_q_b, (QK_DIM,), lambda c: jnp.concatenate([c, zq(LANE - QK_DIM)], axis=1)
                          ).astype(BF16)
    zkv = lambda n: jnp.zeros((kv_rank, n), F32)
    per = QK_NOPE_DIM + V_HEAD_DIM
    wk_p = _pad_head_cols(w_kv_b, (per,), lambda c: jnp.concatenate(
        [c[:, :QK_NOPE_DIM], zkv(LANE - QK_NOPE_DIM)], axis=1))
    wv_p = _pad_head_cols(w_kv_b, (per,), lambda c: jnp.concatenate(
        [c[:, QK_NOPE_DIM:], zkv(LANE - V_HEAD_DIM)], axis=1))
    wkv_p = jnp.concatenate([wk_p, wv_p], axis=1).astype(BF16)
    inv = ROPE_THETA ** (-jnp.arange(half, dtype=F32) / half)
    inv_col = inv.reshape(half, 1)

    row = lambda v: v.reshape(1, -1)
    grp = lambda v: v.reshape(N_POOL_GROUPS, 1, gdim)

    pool_n, q, k, v = _proj_call(
        x2, pos3, row(g_attn_norm), w_in_p, inv_col, w_pool.astype(BF16), grp(b_pool),
        grp(pool_scale), grp(g_out_pool), row(g_q_a), wq_p, row(g_kv_a), wkv_p, seq, tm)

    attn = _attn_call(q, k, v, batch, seq, tq)

    d_attn = N_HEADS * V_HEAD_DIM
    wo_pool = w_out[:d_pool].astype(BF16)
    wo_attn = w_out[d_pool:].reshape(N_HEADS, V_HEAD_DIM, d)
    wo_attn_p = jnp.concatenate([wo_attn, jnp.zeros((N_HEADS, HEAD_PAD - V_HEAD_DIM, d), F32)],
                                axis=1).reshape(N_HEADS * HEAD_PAD, d).astype(BF16)
    gattn_p = jnp.concatenate([g_out_attn.reshape(N_HEADS, V_HEAD_DIM),
                               jnp.zeros((N_HEADS, HEAD_PAD - V_HEAD_DIM), F32)], axis=1).reshape(1, -1)

    x1, hp, idx, gates, rank, counts = _outproj_call(
        pool_n, attn, x2, gattn_p, wo_pool, wo_attn_p, row(g_ffn_norm), w_router.T,
        b_router.reshape(-1, 1), tmo)

    n_assign = t * TOP_K
    cnt = counts[:, 0]
    padded = ((cnt + bm - 1) // bm) * bm
    pad_end = jnp.cumsum(padded)
    pad_start = pad_end - padded
    e_ids = jnp.arange(N_EXPERTS, dtype=jnp.int32)
    dest = rank + jnp.sum(jnp.where(idx[None] == e_ids[:, None, None], pad_start[:, None, None], 0),
                          axis=0)
    dest = dest.T.reshape(-1)
    n_rows = (n_assign // bm + N_EXPERTS) * bm
    blk_start = (pad_start // bm).astype(jnp.int32)
    nblk = (padded // bm).astype(jnp.int32)

    xs = _dispatch_call(blk_start, nblk, dest, hp, n_rows, td, bm)
    yb = _expert_call(blk_start, nblk, xs, w_gate_up, b_gate_up.reshape(N_EXPERTS, 1, -1),
                      w_down, b_down.reshape(N_EXPERTS, 1, -1), bm)
    return x1, dest, gates, yb


def _forward(x, positions, g_attn_norm, w_in, w_pool, b_pool, pool_scale, g_q_a, w_q_b, g_kv_a,
             w_kv_b, g_out_pool, g_out_attn, w_out, g_ffn_norm, w_router, b_router, w_gate_up,
             b_gate_up, w_down, b_down, g_final, tiles):
    batch, seq, d = x.shape
    t = batch * seq
    tm, tq, tmo, td, bm, tf = tiles
    x2 = x.reshape(t, d)
    pos3 = positions.reshape(t // tm, 1, tm)
    depth = g_attn_norm.shape[0]
    assert depth == 1
    l = 0
    x1, dest, gates, yb = _layer(
        x2, pos3, batch, seq, g_attn_norm[l], w_in[l], w_pool[l], b_pool[l], pool_scale[l],
        g_q_a[l], w_q_b[l], g_kv_a[l], w_kv_b[l], g_out_pool[l], g_out_attn[l], w_out[l],
        g_ffn_norm[l], w_router[l], b_router[l], w_gate_up[l], b_gate_up[l], w_down[l],
        b_down[l], tiles)
    out = _combine_call(dest, x1, gates.T, g_final.reshape(1, -1), yb, tf)
    return out.reshape(batch, seq, d)


def _tiles_for(seq):
    tm = min(512, seq)
    tq = min(512, seq // 2)
    tmo = min(512, seq)
    td = min(512, seq)
    bm = 256 if seq >= 2048 else 128
    tf = min(256, seq)
    return (tm, tq, tmo, td, bm, tf)


def kernel(x, positions, g_attn_norm, w_in, w_pool, b_pool, pool_scale, g_q_a, w_q_b, g_kv_a, w_kv_b, g_out_pool, g_out_attn, w_out, g_ffn_norm, w_router, b_router, w_gate_up, b_gate_up, w_down, b_down, g_final):
    tiles = _tiles_for(x.shape[1])
    return _forward(x, positions, g_attn_norm, w_in, w_pool, b_pool, pool_scale, g_q_a, w_q_b,
                    g_kv_a, w_kv_b, g_out_pool, g_out_attn, w_out, g_ffn_norm, w_router, b_router,
                    w_gate_up, b_gate_up, w_down, b_down, g_final, tiles)
```

```python
import functools

import jax
import jax.numpy as jnp
from jax import lax
from jax.experimental import pallas as pl
from jax.experimental.pallas import tpu as pltpu

POOL_WINDOWS = (2, 4, 8, 16)
N_POOL_GROUPS = 4
N_HEADS = 8
V_HEAD_DIM = 64
QK_NOPE_DIM = 64
QK_ROPE_DIM = 32
QK_DIM = QK_NOPE_DIM + QK_ROPE_DIM
ROPE_THETA = 10000.0
N_EXPERTS = 32
TOP_K = 4
SWIGLU_ALPHA = 1.702
SWIGLU_LIMIT = 7.0
RMS_EPS = 1e-6
LOG2_E = 1.4426950408889634

LANE = 128
HEAD_PAD = 128
POOL_HALO = 16

VMEM_LIMIT = 56 * 1024 * 1024

F32 = jnp.float32
BF16 = jnp.bfloat16


def _rms(x, g, n):
    ms = jnp.sum(x * x, axis=-1, keepdims=True) * (1.0 / n)
    return x * lax.rsqrt(ms + RMS_EPS) * g


def _proj_kernel(blocks_per_seq, x_ref, pos_ref, g_attn_ref, w_in_ref, inv_ref, wpool_ref,
                 bpool_ref, pscale_ref, gpool_ref, gq_ref, wq_ref, gkv_ref, wkv_ref,
                 pool_out, q_out, k_out, v_out, ext_scr):
    tm = x_ref.shape[0]
    d_model = x_ref.shape[1]
    d_pool = pool_out.shape[1]
    gdim = d_pool // N_POOL_GROUPS
    q_rank = gq_ref.shape[1]
    kv_rank = gkv_ref.shape[1]
    i = pl.program_id(0)
    blk_in_seq = i % blocks_per_seq

    x = x_ref[...]
    h = _rms(x, g_attn_ref[...], d_model).astype(BF16)
    proj = jnp.dot(h, w_in_ref[...], preferred_element_type=F32)
    u = proj[:, :d_pool]
    qlat = proj[:, d_pool:d_pool + q_rank]
    kvlat = proj[:, d_pool + q_rank:d_pool + q_rank + kv_rank]
    krb = proj[:, d_pool + q_rank + kv_rank:]

    @pl.when(blk_in_seq == 0)
    def _():
        ext_scr[0:POOL_HALO, :] = jnp.zeros((POOL_HALO, d_pool), F32)

    @pl.when(blk_in_seq != 0)
    def _():
        ext_scr[0:POOL_HALO, :] = ext_scr[tm:tm + POOL_HALO, :]

    ext_scr[POOL_HALO:, :] = u
    t_in_seq = (blk_in_seq * tm + lax.broadcasted_iota(jnp.int32, (tm, gdim), 0) + 1).astype(F32)
    ys = []
    ssq = jnp.zeros((tm, 1), F32)
    for g, w in enumerate(POOL_WINDOWS):
        a = ext_scr[:, g * gdim:(g + 1) * gdim]
        shift = 1
        while shift < w:
            a = a + pltpu.roll(a, shift, axis=0)
            shift *= 2
        win = a[POOL_HALO:, :]
        cnt = jnp.minimum(t_in_seq, float(w))
        ug = u[:, g * gdim:(g + 1) * gdim]
        diff = (win / cnt - ug).astype(BF16)
        mixed = jnp.dot(diff, wpool_ref[g], preferred_element_type=F32) + bpool_ref[g]
        y = mixed * pscale_ref[g]
        ssq = ssq + jnp.sum(y * y, axis=-1, keepdims=True)
        ys.append(y)
    rinv = lax.rsqrt(ssq * (1.0 / d_pool) + RMS_EPS)
    for g in range(N_POOL_GROUPS):
        pool_out[:, g * gdim:(g + 1) * gdim] = (ys[g] * rinv * gpool_ref[g]).astype(pool_out.dtype)

    pos = pos_ref[0].astype(F32)
    ang_t = inv_ref[...] * pos
    cos_t = jnp.cos(ang_t)
    sin_t = jnp.sin(ang_t)
    half = QK_ROPE_DIM // 2
    ones = jnp.ones((QK_NOPE_DIM, tm), F32)
    c_t = jnp.concatenate([ones, cos_t, cos_t, jnp.ones((LANE - QK_DIM, tm), F32)], axis=0)
    sa_t = jnp.concatenate([jnp.zeros((QK_NOPE_DIM + half, tm), F32), sin_t,
                            jnp.zeros((LANE - QK_DIM, tm), F32)], axis=0)
    sb_t = jnp.concatenate([jnp.zeros((QK_NOPE_DIM, tm), F32), sin_t,
                            jnp.zeros((LANE - QK_DIM + half, tm), F32)], axis=0)
    c = c_t.T
    sa = sa_t.T
    sb = sb_t.T

    def rope(blk, c_, sa_, sb_):
        return blk * c_ + pltpu.roll(blk, half, axis=1) * sa_ - pltpu.roll(blk, LANE - half, axis=1) * sb_

    qn = _rms(qlat, gq_ref[...], q_rank).astype(BF16)
    q = jnp.dot(qn, wq_ref[...], preferred_element_type=F32)
    scale = QK_DIM ** -0.5 * LOG2_E
    cq, saq, sbq = c * scale, sa * scale, sb * scale
    for hh in range(N_HEADS):
        sl = slice(hh * HEAD_PAD, (hh + 1) * HEAD_PAD)
        q_out[:, sl] = rope(q[:, sl], cq, saq, sbq).astype(q_out.dtype)

    kvn = _rms(kvlat, gkv_ref[...], kv_rank).astype(BF16)
    kv = jnp.dot(kvn, wkv_ref[...], preferred_element_type=F32)
    kr = rope(krb, c, sa, sb)
    for hh in range(N_HEADS):
        sl = slice(hh * HEAD_PAD, (hh + 1) * HEAD_PAD)
        k_out[:, sl] = (kv[:, sl] + kr).astype(k_out.dtype)
    vlane = lax.broadcasted_iota(jnp.int32, (1, N_HEADS * HEAD_PAD), 1) % HEAD_PAD
    ones_col = (vlane == V_HEAD_DIM).astype(F32)
    v_out[...] = (kv[:, N_HEADS * HEAD_PAD:] + ones_col).astype(v_out.dtype)


def _proj_call(x2, pos3, g_attn, w_in_p, inv_col, wpool, bpool, pscale, gpool, gq, wq_p, gkv, wkv_p,
               seq, tm):
    t, d = x2.shape
    d_pool = pscale.shape[0] * pscale.shape[2]
    hp = N_HEADS * HEAD_PAD
    nblk = t // tm
    const2 = lambda i: (0, 0)
    const3 = lambda i: (0, 0, 0)
    in_specs = [
        pl.BlockSpec((tm, d), lambda i: (i, 0)),
        pl.BlockSpec((1, 1, tm), lambda i: (i, 0, 0)),
        pl.BlockSpec(g_attn.shape, const2),
        pl.BlockSpec(w_in_p.shape, const2),
        pl.BlockSpec(inv_col.shape, const2),
        pl.BlockSpec(wpool.shape, const3),
        pl.BlockSpec(bpool.shape, const3),
        pl.BlockSpec(pscale.shape, const3),
        pl.BlockSpec(gpool.shape, const3),
        pl.BlockSpec(gq.shape, const2),
        pl.BlockSpec(wq_p.shape, const2),
        pl.BlockSpec(gkv.shape, const2),
        pl.BlockSpec(wkv_p.shape, const2),
    ]
    out_shape = [
        jax.ShapeDtypeStruct((t, d_pool), BF16),
        jax.ShapeDtypeStruct((t, hp), BF16),
        jax.ShapeDtypeStruct((t, hp), BF16),
        jax.ShapeDtypeStruct((t, hp), BF16),
    ]
    out_specs = [
        pl.BlockSpec((tm, d_pool), lambda i: (i, 0)),
        pl.BlockSpec((tm, hp), lambda i: (i, 0)),
        pl.BlockSpec((tm, hp), lambda i: (i, 0)),
        pl.BlockSpec((tm, hp), lambda i: (i, 0)),
    ]
    return pl.pallas_call(
        functools.partial(_proj_kernel, seq // tm),
        grid=(nblk,),
        in_specs=in_specs,
        out_specs=out_specs,
        out_shape=out_shape,
        scratch_shapes=[pltpu.VMEM((tm + POOL_HALO, d_pool), F32)],
        compiler_params=pltpu.CompilerParams(dimension_semantics=("arbitrary",),
                                             vmem_limit_bytes=VMEM_LIMIT),
        name="proj",
    )(x2, pos3, g_attn, w_in_p, inv_col, wpool, bpool, pscale, gpool, gq, wq_p, gkv, wkv_p)


ROWSUM_LANE = V_HEAD_DIM


def _attn_kernel(q_ref, k_ref, v_ref, o_ref, m_a, acc_a, m_b, acc_b):
    th = q_ref.shape[0] // 2
    qi = pl.program_id(2)
    nt = th // LANE
    for m_ref, acc_ref in ((m_a, acc_a), (m_b, acc_b)):
        m_ref[...] = jnp.full(m_ref.shape, -jnp.inf, F32)
        acc_ref[...] = jnp.zeros(acc_ref.shape, F32)

    def load_kv(kc):
        start = pl.multiple_of(kc * th, th)
        return k_ref[pl.ds(start, th), :], v_ref[pl.ds(start, th), :]

    def half_step(row0, k, v, m_ref, acc_ref, masked):
        q = q_ref[row0:row0 + th, :]
        s = lax.dot_general(q, k, (((1,), (1,)), ((), ())), preferred_element_type=F32)
        if masked:
            row = lax.broadcasted_iota(jnp.int32, (th, th), 0)
            col = lax.broadcasted_iota(jnp.int32, (th, th), 1)
            s = jnp.where(col <= row, s, -jnp.inf)
        tiles = [s[:, j * LANE:(j + 1) * LANE] for j in range(nt)]
        mt = tiles[0]
        for tl in tiles[1:]:
            mt = jnp.maximum(mt, tl)
        m_prev = m_ref[...]
        m_new = jnp.maximum(m_prev, jnp.max(mt, axis=-1, keepdims=True))
        alpha = jnp.exp2(m_prev - m_new)
        p = jnp.concatenate([jnp.exp2(tl - m_new).astype(BF16) for tl in tiles], axis=1)
        acc_ref[...] = alpha * acc_ref[...] + jnp.dot(p, v, preferred_element_type=F32)
        m_ref[...] = m_new

    def body(kc, carry):
        k, v = load_kv(kc)
        half_step(0, k, v, m_a, acc_a, False)
        half_step(th, k, v, m_b, acc_b, False)
        return carry

    lax.fori_loop(0, 2 * qi, body, 0)
    k, v = load_kv(2 * qi)
    half_step(0, k, v, m_a, acc_a, True)
    half_step(th, k, v, m_b, acc_b, False)
    k, v = load_kv(2 * qi + 1)
    half_step(th, k, v, m_b, acc_b, True)

    lane = lax.broadcasted_iota(jnp.int32, (th, HEAD_PAD), 1)
    for row0, acc_ref in ((0, acc_a), (th, acc_b)):
        acc = acc_ref[...]
        out = acc / acc[:, ROWSUM_LANE:ROWSUM_LANE + 1]
        o_ref[row0:row0 + th, :] = jnp.where(lane < V_HEAD_DIM, out, 0.0).astype(o_ref.dtype)


def _attn_call(q, k, v, batch, seq, th):
    t, hp = q.shape
    tq = 2 * th
    nq = seq // tq
    return pl.pallas_call(
        _attn_kernel,
        grid=(batch, N_HEADS, nq),
        in_specs=[
            pl.BlockSpec((tq, HEAD_PAD), lambda b, h, qi: (b * nq + qi, h)),
            pl.BlockSpec((seq, HEAD_PAD), lambda b, h, qi: (b, h)),
            pl.BlockSpec((seq, HEAD_PAD), lambda b, h, qi: (b, h)),
        ],
        out_specs=pl.BlockSpec((tq, HEAD_PAD), lambda b, h, qi: (b * nq + qi, h)),
        out_shape=jax.ShapeDtypeStruct((t, hp), F32),
        scratch_shapes=[pltpu.VMEM((th, HEAD_PAD), F32), pltpu.VMEM((th, HEAD_PAD), F32),
                        pltpu.VMEM((th, HEAD_PAD), F32), pltpu.VMEM((th, HEAD_PAD), F32)],
        compiler_params=pltpu.CompilerParams(
            dimension_semantics=("arbitrary", "arbitrary", "arbitrary"),
            vmem_limit_bytes=VMEM_LIMIT),
        name="attn",
    )(q, k, v)


def _outproj_kernel(pool_ref, attn_ref, x_ref, gattn_ref, wop_ref, woa_ref, gffn_ref, wr_ref,
                    br_ref, x1_out, hp_out, idx_out, gate_out, rank_out, cnt_out, run_scr):
    tm, d = x_ref.shape
    d_attn = N_HEADS * V_HEAD_DIM
    i = pl.program_id(0)

    @pl.when(i == 0)
    def _():
        run_scr[...] = jnp.zeros(run_scr.shape, F32)

    attn = attn_ref[...]
    attn_n = _rms(attn, gattn_ref[...], d_attn).astype(BF16)
    mix = (jnp.dot(pool_ref[...], wop_ref[...], preferred_element_type=F32)
           + jnp.dot(attn_n, woa_ref[...], preferred_element_type=F32))
    x1 = x_ref[...] + mix
    x1_out[...] = x1
    h2 = _rms(x1, gffn_ref[...], d)

    hb = h2.astype(BF16).astype(F32)
    lo = lax.bitcast_convert_type(hb[:, :d // 2], jnp.uint32)
    hi = lax.bitcast_convert_type(hb[:, d // 2:], jnp.uint32)
    hp_out[...] = (lo >> 16) | (hi & jnp.uint32(0xFFFF0000))

    logits = lax.dot_general(wr_ref[...], h2, (((1,), (1,)), ((), ())),
                             preferred_element_type=F32,
                             precision=lax.Precision.HIGHEST) + br_ref[...]
    e_iota = lax.broadcasted_iota(jnp.int32, (N_EXPERTS, tm), 0)
    work = logits
    idxs, vals = [], []
    for _ in range(TOP_K):
        mx = jnp.max(work, axis=0, keepdims=True)
        sel = jnp.min(jnp.where(work == mx, e_iota, N_EXPERTS), axis=0, keepdims=True)
        idxs.append(sel)
        vals.append(mx)
        work = jnp.where(e_iota == sel, -jnp.inf, work)
    exps = [jnp.exp(v - vals[0]) for v in vals]
    denom = exps[0] + exps[1] + exps[2] + exps[3]
    gate_out[...] = jnp.concatenate([e / denom for e in exps], axis=0)
    idx_out[...] = jnp.concatenate(idxs, axis=0)

    onehot = jnp.zeros((N_EXPERTS, tm), F32)
    for sel in idxs:
        onehot = onehot + (e_iota == sel).astype(F32)
    r_io = lax.broadcasted_iota(jnp.int32, (tm, tm), 0)
    c_io = lax.broadcasted_iota(jnp.int32, (tm, tm), 1)
    upper = (r_io < c_io).astype(BF16)
    prefix = jnp.dot(onehot.astype(BF16), upper, preferred_element_type=F32) + run_scr[:, 0:1]
    ranks = [jnp.sum(jnp.where(e_iota == sel, prefix, 0.0), axis=0, keepdims=True) for sel in idxs]
    rank_out[...] = jnp.concatenate(ranks, axis=0).astype(jnp.int32)
    run_new = run_scr[...] + jnp.sum(onehot, axis=1, keepdims=True)
    run_scr[...] = run_new
    cnt_out[...] = run_new.astype(jnp.int32)


def _outproj_call(pool_n, attn, x2, gattn_p, wo_pool, wo_attn_p, gffn, wr_t, br_col, tm):
    t, d = x2.shape
    nblk = t // tm
    const2 = lambda i: (0, 0)
    row = lambda i: (i, 0)
    col = lambda i: (0, i)
    in_specs = [
        pl.BlockSpec((tm, pool_n.shape[1]), row),
        pl.BlockSpec((tm, attn.shape[1]), row),
        pl.BlockSpec((tm, d), row),
        pl.BlockSpec(gattn_p.shape, const2),
        pl.BlockSpec(wo_pool.shape, const2),
        pl.BlockSpec(wo_attn_p.shape, const2),
        pl.BlockSpec(gffn.shape, const2),
        pl.BlockSpec(wr_t.shape, const2),
        pl.BlockSpec(br_col.shape, const2),
    ]
    out_shape = [
        jax.ShapeDtypeStruct((t, d), F32),
        jax.ShapeDtypeStruct((t, d // 2), jnp.uint32),
        jax.ShapeDtypeStruct((TOP_K, t), jnp.int32),
        jax.ShapeDtypeStruct((TOP_K, t), F32),
        jax.ShapeDtypeStruct((TOP_K, t), jnp.int32),
        jax.ShapeDtypeStruct((N_EXPERTS, LANE), jnp.int32),
    ]
    out_specs = [
        pl.BlockSpec((tm, d), row),
        pl.BlockSpec((tm, d // 2), row),
        pl.BlockSpec((TOP_K, tm), col),
        pl.BlockSpec((TOP_K, tm), col),
        pl.BlockSpec((TOP_K, tm), col),
        pl.BlockSpec((N_EXPERTS, LANE), const2),
    ]
    return pl.pallas_call(
        _outproj_kernel,
        grid=(nblk,),
        in_specs=in_specs,
        out_specs=out_specs,
        out_shape=out_shape,
        scratch_shapes=[pltpu.VMEM((N_EXPERTS, LANE), F32)],
        compiler_params=pltpu.CompilerParams(dimension_semantics=("arbitrary",),
                                             vmem_limit_bytes=VMEM_LIMIT),
        name="outproj",
    )(pool_n, attn, x2, gattn_p, wo_pool, wo_attn_p, gffn, wr_t, br_col)


ROW_UNROLL = 8


def _dispatch_kernel(bs_ref, nblk_ref, dest_ref, h_ref, xs_out, zbuf, sem, zsem):
    td = h_ref.shape[0]
    bm = zbuf.shape[0]
    n_exp = bs_ref.shape[0]

    @pl.when(pl.program_id(0) == 0)
    def _():
        zbuf[...] = jnp.zeros(zbuf.shape, zbuf.dtype)
        total = xs_out.shape[0] // bm
        first_tail = bs_ref[n_exp - 1] + nblk_ref[n_exp - 1]

        def z_copy(blk):
            row0 = pl.multiple_of(blk * bm, bm)
            return pltpu.make_async_copy(zbuf, xs_out.at[pl.ds(row0, bm)], zsem)

        def each_partial_block(fn):
            def per_expert(e, c):
                @pl.when(nblk_ref[e] > 0)
                def _():
                    fn(z_copy(bs_ref[e] + nblk_ref[e] - 1))
                return c

            def per_tail(b, c):
                fn(z_copy(b))
                return c

            lax.fori_loop(0, n_exp, per_expert, 0)
            lax.fori_loop(first_tail, total, per_tail, 0)

        each_partial_block(lambda cp: cp.start())
        each_partial_block(lambda cp: cp.wait())

    def body(g, carry):
        base = g * ROW_UNROLL
        for u in range(ROW_UNROLL):
            i = base + u
            for k in range(TOP_K):
                d = dest_ref[i * TOP_K + k]
                pltpu.make_async_copy(h_ref.at[pl.ds(i, 1)], xs_out.at[pl.ds(d, 1)], sem).start(
                    priority=k % 2)
        return carry

    lax.fori_loop(0, td // ROW_UNROLL, body, 0)
    for _ in range(TOP_K):
        pltpu.make_async_copy(h_ref, xs_out.at[pl.ds(0, td)], sem).wait()


def _dispatch_call(blk_start, nblk, dest, hp, n_rows, td, bm):
    t, dw = hp.shape
    grid_spec = pltpu.PrefetchScalarGridSpec(
        num_scalar_prefetch=2,
        grid=(t // td,),
        in_specs=[
            pl.BlockSpec((TOP_K * td,), lambda i, bs, nb: (i,), memory_space=pltpu.SMEM),
            pl.BlockSpec((td, dw), lambda i, bs, nb: (i, 0)),
        ],
        out_specs=pl.BlockSpec(memory_space=pl.ANY),
        scratch_shapes=[pltpu.VMEM((bm, dw), hp.dtype), pltpu.SemaphoreType.DMA,
                        pltpu.SemaphoreType.DMA],
    )
    return pl.pallas_call(
        _dispatch_kernel,
        grid_spec=grid_spec,
        out_shape=jax.ShapeDtypeStruct((n_rows, dw), hp.dtype),
        compiler_params=pltpu.CompilerParams(dimension_semantics=("arbitrary",),
                                             vmem_limit_bytes=VMEM_LIMIT),
        name="dispatch",
    )(blk_start, nblk, dest, hp)


CAST_ROWS = 64


def _expert_kernel(bs_ref, nblk_ref, wgu_ref, bgu_ref, wd_ref, bd_ref, xs_hbm, y_hbm,
                   wgu_bf, wd_bf, xbuf, ybuf, xsem, ysem):
    e = pl.program_id(0)
    n_exp = pl.num_programs(0)
    bm = xbuf.shape[1]
    d = wgu_ref.shape[1]
    f = wd_ref.shape[1]
    n = nblk_ref[e]
    blk0 = bs_ref[e]

    def x_copy(j, slot):
        row0 = pl.multiple_of((blk0 + j) * bm, bm)
        return pltpu.make_async_copy(xs_hbm.at[pl.ds(row0, bm)], xbuf.at[slot], xsem.at[slot])

    def y_copy(blk, slot):
        row0 = pl.multiple_of(blk * bm, bm)
        return pltpu.make_async_copy(ybuf.at[slot], y_hbm.at[pl.ds(row0, bm)], ysem.at[slot])

    @pl.when(n > 0)
    def _():
        x_copy(0, 0).start()

        def cast_gu(r, c):
            rows = pl.ds(pl.multiple_of(r * CAST_ROWS, CAST_ROWS), CAST_ROWS)
            wgu_bf[rows, :] = wgu_ref[0, rows, :].astype(BF16)
            return c

        def cast_d(r, c):
            rows = pl.ds(pl.multiple_of(r * CAST_ROWS, CAST_ROWS), CAST_ROWS)
            wd_bf[rows, :] = wd_ref[0, rows, :].astype(BF16)
            return c

        lax.fori_loop(0, d // CAST_ROWS, cast_gu, 0)
        lax.fori_loop(0, f // CAST_ROWS, cast_d, 0)

        def block(j, c):
            slot = j % 2
            x_copy(j, slot).wait()

            @pl.when(j + 1 < n)
            def _():
                x_copy(j + 1, 1 - slot).start()

            @pl.when(j >= 2)
            def _():
                y_copy(blk0 + j - 2, slot).wait()

            w = xbuf[slot]
            x_lo = lax.bitcast_convert_type(w << 16, F32).astype(BF16)
            x_hi = lax.bitcast_convert_type(w & jnp.uint32(0xFFFF0000), F32).astype(BF16)
            gu = (jnp.dot(x_lo, wgu_bf[:d // 2, :], preferred_element_type=F32)
                  + jnp.dot(x_hi, wgu_bf[d // 2:, :], preferred_element_type=F32) + bgu_ref[0])
            gate = jnp.minimum(gu[:, :f], SWIGLU_LIMIT)
            up = jnp.clip(gu[:, f:], -SWIGLU_LIMIT, SWIGLU_LIMIT)
            act = gate * jax.nn.sigmoid(SWIGLU_ALPHA * gate) * (up + 1.0)
            ybuf[slot] = jnp.dot(act.astype(BF16), wd_bf[...], preferred_element_type=F32) + bd_ref[0]
            y_copy(blk0 + j, slot).start()
            return c

        lax.fori_loop(0, n, block, 0)

        @pl.when(n >= 2)
        def _():
            y_copy(blk0 + n - 2, n % 2).wait()

        y_copy(blk0 + n - 1, (n - 1) % 2).wait()

    @pl.when(e == n_exp - 1)
    def _():
        first = blk0 + n
        total = y_hbm.shape[0] // bm
        ybuf[0] = jnp.zeros(ybuf.shape[1:], ybuf.dtype)

        def fill(b, c):
            y_copy(b, 0).start()
            return c

        def drain(b, c):
            y_copy(b, 0).wait()
            return c

        lax.fori_loop(first, total, fill, 0)
        lax.fori_loop(first, total, drain, 0)


def _expert_call(blk_start, nblk, xs, wgu, bgu, wd, bd, bm):
    r, dw = xs.shape
    e, d, f2 = wgu.shape
    f = wd.shape[1]

    def wsel(i, bs, nb):
        return (i, 0, 0)

    grid_spec = pltpu.PrefetchScalarGridSpec(
        num_scalar_prefetch=2,
        grid=(e,),
        in_specs=[
            pl.BlockSpec((1, d, f2), wsel),
            pl.BlockSpec((1, 1, f2), wsel),
            pl.BlockSpec((1, f, d), wsel),
            pl.BlockSpec((1, 1, d), wsel),
            pl.BlockSpec(memory_space=pl.ANY),
        ],
        out_specs=pl.BlockSpec(memory_space=pl.ANY),
        scratch_shapes=[pltpu.VMEM((d, f2), BF16), pltpu.VMEM((f, d), BF16),
                        pltpu.VMEM((2, bm, dw), xs.dtype), pltpu.VMEM((2, bm, d), F32),
                        pltpu.SemaphoreType.DMA((2,)), pltpu.SemaphoreType.DMA((2,))],
    )
    return pl.pallas_call(
        _expert_kernel,
        grid_spec=grid_spec,
        out_shape=jax.ShapeDtypeStruct((r, d), F32),
        compiler_params=pltpu.CompilerParams(dimension_semantics=("arbitrary",),
                                             vmem_limit_bytes=VMEM_LIMIT),
        name="experts",
    )(blk_start, nblk, wgu, bgu, wd, bd, xs)


def _combine_kernel(dest_ref, x1_ref, gate_ref, gfin_ref, yb_hbm, o_ref, ybuf, sem):
    tf, d = x1_ref.shape

    def body(g, carry):
        base = g * ROW_UNROLL
        for u in range(ROW_UNROLL):
            i = base + u
            for k in range(TOP_K):
                r = dest_ref[i * TOP_K + k]
                pltpu.make_async_copy(yb_hbm.at[pl.ds(r, 1)], ybuf.at[k, pl.ds(i, 1)], sem).start(
                    priority=k % 2)
        return carry

    lax.fori_loop(0, tf // ROW_UNROLL, body, 0)
    for k in range(TOP_K):
        pltpu.make_async_copy(yb_hbm.at[pl.ds(0, tf)], ybuf.at[k], sem).wait()
    g = gate_ref[...]
    y = x1_ref[...]
    for k in range(TOP_K):
        y = y + g[:, k:k + 1] * ybuf[k]
    o_ref[...] = _rms(y, gfin_ref[...], d)


def _combine_call(dest, x1, gates_t, gfin, yb, tf):
    t, d = x1.shape
    return pl.pallas_call(
        _combine_kernel,
        grid=(t // tf,),
        in_specs=[
            pl.BlockSpec((TOP_K * tf,), lambda i: (i,), memory_space=pltpu.SMEM),
            pl.BlockSpec((tf, d), lambda i: (i, 0)),
            pl.BlockSpec((tf, TOP_K), lambda i: (i, 0)),
            pl.BlockSpec(gfin.shape, lambda i: (0, 0)),
            pl.BlockSpec(memory_space=pl.ANY),
        ],
        out_specs=pl.BlockSpec((tf, d), lambda i: (i, 0)),
        out_shape=jax.ShapeDtypeStruct((t, d), F32),
        scratch_shapes=[pltpu.VMEM((TOP_K, tf, d), F32), pltpu.SemaphoreType.DMA],
        compiler_params=pltpu.CompilerParams(dimension_semantics=("arbitrary",),
                                             vmem_limit_bytes=VMEM_LIMIT),
        name="combine",
    )(dest, x1, gates_t, gfin, yb)


def _pad_head_cols(w, widths_in, place):
    k = w.shape[0]
    per = sum(widths_in)
    blocks = []
    for hh in range(N_HEADS):
        blocks.append(place(w[:, hh * per:(hh + 1) * per]))
    return jnp.concatenate(blocks, axis=1)


def _layer(x2, pos3, batch, seq, g_attn_norm, w_in, w_pool, b_pool, pool_scale, g_q_a, w_q_b,
           g_kv_a, w_kv_b, g_out_pool, g_out_attn, w_out, g_ffn_norm, w_router, b_router,
           w_gate_up, b_gate_up, w_down, b_down, tiles):
    tm, tq, tmo, td, bm, tf = tiles
    t, d = x2.shape
    d_pool = pool_scale.shape[0]
    gdim = d_pool // N_POOL_GROUPS
    q_rank = g_q_a.shape[0]
    kv_rank = g_kv_a.shape[0]
    half = QK_ROPE_DIM // 2

    o1, o2, o3 = d_pool, d_pool + q_rank, d_pool + q_rank + kv_rank
    zk = lambda n: jnp.zeros((d, n), F32)
    w_in_p = jnp.concatenate([w_in[:, :o3], zk(QK_NOPE_DIM), w_in[:, o3:], zk(LANE - QK_DIM)],
                             axis=1).astype(BF16)
    zq = lambda n: jnp.zeros((q_rank, n), F32)
    wq_p = _pad_head_cols(w_q_b, (QK_DIM,), lambda c: jnp.concatenate([c, zq(LANE - QK_DIM)], axis=1)
                          ).astype(BF16)
    zkv = lambda n: jnp.zeros((kv_rank, n), F32)
    per = QK_NOPE_DIM + V_HEAD_DIM
    wk_p = _pad_head_cols(w_kv_b, (per,), lambda c: jnp.concatenate(
        [c[:, :QK_NOPE_DIM], zkv(LANE - QK_NOPE_DIM)], axis=1))
    wv_p = _pad_head_cols(w_kv_b, (per,), lambda c: jnp.concatenate(
        [c[:, QK_NOPE_DIM:], zkv(LANE - V_HEAD_DIM)], axis=1))
    wkv_p = jnp.concatenate([wk_p, wv_p], axis=1).astype(BF16)
    inv = ROPE_THETA ** (-jnp.arange(half, dtype=F32) / half)
    inv_col = inv.reshape(half, 1)

    row = lambda v: v.reshape(1, -1)
    grp = lambda v: v.reshape(N_POOL_GROUPS, 1, gdim)

    pool_n, q, k, v = _proj_call(
        x2, pos3, row(g_attn_norm), w_in_p, inv_col, w_pool.astype(BF16), grp(b_pool),
        grp(pool_scale), grp(g_out_pool), row(g_q_a), wq_p, row(g_kv_a), wkv_p, seq, tm)

    attn = _attn_call(q, k, v, batch, seq, tq)

    d_attn = N_HEADS * V_HEAD_DIM
    wo_pool = w_out[:d_pool].astype(BF16)
    wo_attn = w_out[d_pool:].reshape(N_HEADS, V_HEAD_DIM, d)
    wo_attn_p = jnp.concatenate([wo_attn, jnp.zeros((N_HEADS, HEAD_PAD - V_HEAD_DIM, d), F32)],
                                axis=1).reshape(N_HEADS * HEAD_PAD, d).astype(BF16)
    gattn_p = jnp.concatenate([g_out_attn.reshape(N_HEADS, V_HEAD_DIM),
                               jnp.zeros((N_HEADS, HEAD_PAD - V_HEAD_DIM), F32)], axis=1).reshape(1, -1)

    x1, hp, idx, gates, rank, counts = _outproj_call(
        pool_n, attn, x2, gattn_p, wo_pool, wo_attn_p, row(g_ffn_norm), w_router.T,
        b_router.reshape(-1, 1), tmo)

    n_assign = t * TOP_K
    cnt = counts[:, 0]
    padded = ((cnt + bm - 1) // bm) * bm
    pad_end = jnp.cumsum(padded)
    pad_start = pad_end - padded
    e_ids = jnp.arange(N_EXPERTS, dtype=jnp.int32)
    dest = rank + jnp.sum(jnp.where(idx[None] == e_ids[:, None, None], pad_start[:, None, None], 0),
                          axis=0)
    dest = dest.T.reshape(-1)
    n_rows = (n_assign // bm + N_EXPERTS) * bm
    blk_start = (pad_start // bm).astype(jnp.int32)
    nblk = (padded // bm).astype(jnp.int32)

    xs = _dispatch_call(blk_start, nblk, dest, hp, n_rows, td, bm)
    yb = _expert_call(blk_start, nblk, xs, w_gate_up, b_gate_up.reshape(N_EXPERTS, 1, -1),
                      w_down, b_down.reshape(N_EXPERTS, 1, -1), bm)
    return x1, dest, gates, yb


def _forward(x, positions, g_attn_norm, w_in, w_pool, b_pool, pool_scale, g_q_a, w_q_b, g_kv_a,
             w_kv_b, g_out_pool, g_out_attn, w_out, g_ffn_norm, w_router, b_router, w_gate_up,
             b_gate_up, w_down, b_down, g_final, tiles):
    batch, seq, d = x.shape
    t = batch * seq
    tm, tq, tmo, td, bm, tf = tiles
    x2 = x.reshape(t, d)
    pos3 = positions.reshape(t // tm, 1, tm)
    depth = g_attn_norm.shape[0]
    assert depth == 1
    l = 0
    x1, dest, gates, yb = _layer(
        x2, pos3, batch, seq, g_attn_norm[l], w_in[l], w_pool[l], b_pool[l], pool_scale[l],
        g_q_a[l], w_q_b[l], g_kv_a[l], w_kv_b[l], g_out_pool[l], g_out_attn[l], w_out[l],
        g_ffn_norm[l], w_router[l], b_router[l], w_gate_up[l], b_gate_up[l], w_down[l],
        b_down[l], tiles)
    out = _combine_call(dest, x1, gates.T, g_final.reshape(1, -1), yb, tf)
    return out.reshape(batch, seq, d)


def _tiles_for(seq):
    tm = min(512, seq)
    tq = min(512, seq // 2)
    tmo = min(512, seq)
    td = min(512, seq)
    bm = 512 if seq >= 2048 else 128
    tf = min(256, seq)
    return (tm, tq, tmo, td, bm, tf)


def kernel(x, positions, g_attn_norm, w_in, w_pool, b_pool, pool_scale, g_q_a, w_q_b, g_kv_a, w_kv_b, g_out_pool, g_out_attn, w_out, g_ffn_norm, w_router, b_router, w_gate_up, b_gate_up, w_down, b_down, g_final):
    tiles = _tiles_for(x.shape[1])
    return _forward(x, positions, g_attn_norm, w_in, w_pool, b_pool, pool_scale, g_q_a, w_q_b,
                    g_kv_a, w_kv_b, g_out_pool, g_out_attn, w_out, g_ffn_norm, w_router, b_router,
                    w_gate_up, b_gate_up, w_down, b_down, g_final, tiles)
```

```python
import functools

import jax
import jax.numpy as jnp
from jax import lax
from jax.experimental import pallas as pl
from jax.experimental.pallas import tpu as pltpu

POOL_WINDOWS = (2, 4, 8, 16)
N_POOL_GROUPS = 4
N_HEADS = 8
V_HEAD_DIM = 64
QK_NOPE_DIM = 64
QK_ROPE_DIM = 32
QK_DIM = QK_NOPE_DIM + QK_ROPE_DIM
ROPE_THETA = 10000.0
N_EXPERTS = 32
TOP_K = 4
SWIGLU_ALPHA = 1.702
SWIGLU_LIMIT = 7.0
RMS_EPS = 1e-6
LOG2_E = 1.4426950408889634

LANE = 128
HEAD_PAD = 128
POOL_HALO = 16

VMEM_LIMIT = 56 * 1024 * 1024

F32 = jnp.float32
BF16 = jnp.bfloat16


def _rms(x, g, n):
    ms = jnp.sum(x * x, axis=-1, keepdims=True) * (1.0 / n)
    return x * lax.rsqrt(ms + RMS_EPS) * g


def _proj_kernel(blocks_per_seq, x_ref, pos_ref, g_attn_ref, w_in_ref, inv_ref, wpool_ref,
                 bpool_ref, pscale_ref, gpool_ref, gq_ref, wq_ref, gkv_ref, wkv_ref,
                 pool_out, q_out, k_out, v_out, ext_scr):
    tm = x_ref.shape[0]
    d_model = x_ref.shape[1]
    d_pool = pool_out.shape[1]
    gdim = d_pool // N_POOL_GROUPS
    q_rank = gq_ref.shape[1]
    kv_rank = gkv_ref.shape[1]
    i = pl.program_id(0)
    blk_in_seq = i % blocks_per_seq

    x = x_ref[...]
    h = _rms(x, g_attn_ref[...], d_model).astype(BF16)
    proj = jnp.dot(h, w_in_ref[...], preferred_element_type=F32)
    u = proj[:, :d_pool]
    qlat = proj[:, d_pool:d_pool + q_rank]
    kvlat = proj[:, d_pool + q_rank:d_pool + q_rank + kv_rank]
    krb = proj[:, d_pool + q_rank + kv_rank:]

    @pl.when(blk_in_seq == 0)
    def _():
        ext_scr[0:POOL_HALO, :] = jnp.zeros((POOL_HALO, d_pool), F32)

    @pl.when(blk_in_seq != 0)
    def _():
        ext_scr[0:POOL_HALO, :] = ext_scr[tm:tm + POOL_HALO, :]

    ext_scr[POOL_HALO:, :] = u
    t_in_seq = (blk_in_seq * tm + lax.broadcasted_iota(jnp.int32, (tm, gdim), 0) + 1).astype(F32)
    ys = []
    ssq = jnp.zeros((tm, 1), F32)
    for g, w in enumerate(POOL_WINDOWS):
        a = ext_scr[:, g * gdim:(g + 1) * gdim]
        shift = 1
        while shift < w:
            a = a + pltpu.roll(a, shift, axis=0)
            shift *= 2
        win = a[POOL_HALO:, :]
        cnt = jnp.minimum(t_in_seq, float(w))
        ug = u[:, g * gdim:(g + 1) * gdim]
        diff = (win / cnt - ug).astype(BF16)
        mixed = jnp.dot(diff, wpool_ref[g], preferred_element_type=F32) + bpool_ref[g]
        y = mixed * pscale_ref[g]
        ssq = ssq + jnp.sum(y * y, axis=-1, keepdims=True)
        ys.append(y)
    rinv = lax.rsqrt(ssq * (1.0 / d_pool) + RMS_EPS)
    for g in range(N_POOL_GROUPS):
        pool_out[:, g * gdim:(g + 1) * gdim] = (ys[g] * rinv * gpool_ref[g]).astype(pool_out.dtype)

    pos = pos_ref[0].astype(F32)
    ang_t = inv_ref[...] * pos
    cos_t = jnp.cos(ang_t)
    sin_t = jnp.sin(ang_t)
    half = QK_ROPE_DIM // 2
    ones = jnp.ones((QK_NOPE_DIM, tm), F32)
    c_t = jnp.concatenate([ones, cos_t, cos_t, jnp.ones((LANE - QK_DIM, tm), F32)], axis=0)
    sa_t = jnp.concatenate([jnp.zeros((QK_NOPE_DIM + half, tm), F32), sin_t,
                            jnp.zeros((LANE - QK_DIM, tm), F32)], axis=0)
    sb_t = jnp.concatenate([jnp.zeros((QK_NOPE_DIM, tm), F32), sin_t,
                            jnp.zeros((LANE - QK_DIM + half, tm), F32)], axis=0)
    c = c_t.T
    sa = sa_t.T
    sb = sb_t.T

    def rope(blk, c_, sa_, sb_):
        return blk * c_ + pltpu.roll(blk, half, axis=1) * sa_ - pltpu.roll(blk, LANE - half, axis=1) * sb_

    qn = _rms(qlat, gq_ref[...], q_rank).astype(BF16)
    q = jnp.dot(qn, wq_ref[...], preferred_element_type=F32)
    scale = QK_DIM ** -0.5 * LOG2_E
    cq, saq, sbq = c * scale, sa * scale, sb * scale
    for hh in range(N_HEADS):
        sl = slice(hh * HEAD_PAD, (hh + 1) * HEAD_PAD)
        q_out[:, sl] = rope(q[:, sl], cq, saq, sbq).astype(q_out.dtype)

    kvn = _rms(kvlat, gkv_ref[...], kv_rank).astype(BF16)
    kv = jnp.dot(kvn, wkv_ref[...], preferred_element_type=F32)
    kr = rope(krb, c, sa, sb)
    for hh in range(N_HEADS):
        sl = slice(hh * HEAD_PAD, (hh + 1) * HEAD_PAD)
        k_out[:, sl] = (kv[:, sl] + kr).astype(k_out.dtype)
    vlane = lax.broadcasted_iota(jnp.int32, (1, N_HEADS * HEAD_PAD), 1) % HEAD_PAD
    ones_col = (vlane == V_HEAD_DIM).astype(F32)
    v_out[...] = (kv[:, N_HEADS * HEAD_PAD:] + ones_col).astype(v_out.dtype)


def _proj_call(x2, pos3, g_attn, w_in_p, inv_col, wpool, bpool, pscale, gpool, gq, wq_p, gkv, wkv_p,
               seq, tm):
    t, d = x2.shape
    d_pool = pscale.shape[0] * pscale.shape[2]
    hp = N_HEADS * HEAD_PAD
    nblk = t // tm
    const2 = lambda i: (0, 0)
    const3 = lambda i: (0, 0, 0)
    in_specs = [
        pl.BlockSpec((tm, d), lambda i: (i, 0)),
        pl.BlockSpec((1, 1, tm), lambda i: (i, 0, 0)),
        pl.BlockSpec(g_attn.shape, const2),
        pl.BlockSpec(w_in_p.shape, const2),
        pl.BlockSpec(inv_col.shape, const2),
        pl.BlockSpec(wpool.shape, const3),
        pl.BlockSpec(bpool.shape, const3),
        pl.BlockSpec(pscale.shape, const3),
        pl.BlockSpec(gpool.shape, const3),
        pl.BlockSpec(gq.shape, const2),
        pl.BlockSpec(wq_p.shape, const2),
        pl.BlockSpec(gkv.shape, const2),
        pl.BlockSpec(wkv_p.shape, const2),
    ]
    out_shape = [
        jax.ShapeDtypeStruct((t, d_pool), BF16),
        jax.ShapeDtypeStruct((t, hp), BF16),
        jax.ShapeDtypeStruct((t, hp), BF16),
        jax.ShapeDtypeStruct((t, hp), BF16),
    ]
    out_specs = [
        pl.BlockSpec((tm, d_pool), lambda i: (i, 0)),
        pl.BlockSpec((tm, hp), lambda i: (i, 0)),
        pl.BlockSpec((tm, hp), lambda i: (i, 0)),
        pl.BlockSpec((tm, hp), lambda i: (i, 0)),
    ]
    return pl.pallas_call(
        functools.partial(_proj_kernel, seq // tm),
        grid=(nblk,),
        in_specs=in_specs,
        out_specs=out_specs,
        out_shape=out_shape,
        scratch_shapes=[pltpu.VMEM((tm + POOL_HALO, d_pool), F32)],
        compiler_params=pltpu.CompilerParams(dimension_semantics=("arbitrary",),
                                             vmem_limit_bytes=VMEM_LIMIT),
        name="proj",
    )(x2, pos3, g_attn, w_in_p, inv_col, wpool, bpool, pscale, gpool, gq, wq_p, gkv, wkv_p)


ROWSUM_LANE = V_HEAD_DIM


def _attn_kernel(q_ref, k_ref, v_ref, o_ref, m_a, acc_a, m_b, acc_b):
    th = q_ref.shape[0] // 2
    qi = pl.program_id(2)
    nt = th // LANE
    for m_ref, acc_ref in ((m_a, acc_a), (m_b, acc_b)):
        m_ref[...] = jnp.full(m_ref.shape, -jnp.inf, F32)
        acc_ref[...] = jnp.zeros(acc_ref.shape, F32)

    def load_kv(kc):
        start = pl.multiple_of(kc * th, th)
        return k_ref[pl.ds(start, th), :], v_ref[pl.ds(start, th), :]

    def half_step(row0, k, v, m_ref, acc_ref, masked):
        q = q_ref[row0:row0 + th, :]
        s = lax.dot_general(q, k, (((1,), (1,)), ((), ())), preferred_element_type=F32)
        if masked:
            row = lax.broadcasted_iota(jnp.int32, (th, th), 0)
            col = lax.broadcasted_iota(jnp.int32, (th, th), 1)
            s = jnp.where(col <= row, s, -jnp.inf)
        tiles = [s[:, j * LANE:(j + 1) * LANE] for j in range(nt)]
        mt = tiles[0]
        for tl in tiles[1:]:
            mt = jnp.maximum(mt, tl)
        m_prev = m_ref[...]
        m_new = jnp.maximum(m_prev, jnp.max(mt, axis=-1, keepdims=True))
        alpha = jnp.exp2(m_prev - m_new)
        p = jnp.concatenate([jnp.exp2(tl - m_new).astype(BF16) for tl in tiles], axis=1)
        acc_ref[...] = alpha * acc_ref[...] + jnp.dot(p, v, preferred_element_type=F32)
        m_ref[...] = m_new

    def body(kc, carry):
        k, v = load_kv(kc)
        half_step(0, k, v, m_a, acc_a, False)
        half_step(th, k, v, m_b, acc_b, False)
        return carry

    lax.fori_loop(0, 2 * qi, body, 0)
    k, v = load_kv(2 * qi)
    half_step(0, k, v, m_a, acc_a, True)
    half_step(th, k, v, m_b, acc_b, False)
    k, v = load_kv(2 * qi + 1)
    half_step(th, k, v, m_b, acc_b, True)

    lane = lax.broadcasted_iota(jnp.int32, (th, HEAD_PAD), 1)
    for row0, acc_ref in ((0, acc_a), (th, acc_b)):
        acc = acc_ref[...]
        out = acc / acc[:, ROWSUM_LANE:ROWSUM_LANE + 1]
        o_ref[row0:row0 + th, :] = jnp.where(lane < V_HEAD_DIM, out, 0.0).astype(o_ref.dtype)


def _attn_call(q, k, v, batch, seq, th):
    t, hp = q.shape
    tq = 2 * th
    nq = seq // tq
    return pl.pallas_call(
        _attn_kernel,
        grid=(batch, N_HEADS, nq),
        in_specs=[
            pl.BlockSpec((tq, HEAD_PAD), lambda b, h, qi: (b * nq + qi, h)),
            pl.BlockSpec((seq, HEAD_PAD), lambda b, h, qi: (b, h)),
            pl.BlockSpec((seq, HEAD_PAD), lambda b, h, qi: (b, h)),
        ],
        out_specs=pl.BlockSpec((tq, HEAD_PAD), lambda b, h, qi: (b * nq + qi, h)),
        out_shape=jax.ShapeDtypeStruct((t, hp), F32),
        scratch_shapes=[pltpu.VMEM((th, HEAD_PAD), F32), pltpu.VMEM((th, HEAD_PAD), F32),
                        pltpu.VMEM((th, HEAD_PAD), F32), pltpu.VMEM((th, HEAD_PAD), F32)],
        compiler_params=pltpu.CompilerParams(
            dimension_semantics=("arbitrary", "arbitrary", "arbitrary"),
            vmem_limit_bytes=VMEM_LIMIT),
        name="attn",
    )(q, k, v)


def _outproj_kernel(pool_ref, attn_ref, x_ref, gattn_ref, wop_ref, woa_ref, gffn_ref, wr_ref,
                    br_ref, x1_out, hp_out, idx_out, gate_out, rank_out, cnt_out, run_scr):
    tm, d = x_ref.shape
    d_attn = N_HEADS * V_HEAD_DIM
    i = pl.program_id(0)

    @pl.when(i == 0)
    def _():
        run_scr[...] = jnp.zeros(run_scr.shape, F32)

    attn = attn_ref[...]
    attn_n = _rms(attn, gattn_ref[...], d_attn).astype(BF16)
    mix = (jnp.dot(pool_ref[...], wop_ref[...], preferred_element_type=F32)
           + jnp.dot(attn_n, woa_ref[...], preferred_element_type=F32))
    x1 = x_ref[...] + mix
    x1_out[...] = x1
    h2 = _rms(x1, gffn_ref[...], d)

    hb = h2.astype(BF16).astype(F32)
    lo = lax.bitcast_convert_type(hb[:, :d // 2], jnp.uint32)
    hi = lax.bitcast_convert_type(hb[:, d // 2:], jnp.uint32)
    hp_out[...] = (lo >> 16) | (hi & jnp.uint32(0xFFFF0000))

    logits = lax.dot_general(wr_ref[...], h2, (((1,), (1,)), ((), ())),
                             preferred_element_type=F32,
                             precision=lax.Precision.HIGHEST) + br_ref[...]
    e_iota = lax.broadcasted_iota(jnp.int32, (N_EXPERTS, tm), 0)
    work = logits
    idxs, vals = [], []
    for _ in range(TOP_K):
        mx = jnp.max(work, axis=0, keepdims=True)
        sel = jnp.min(jnp.where(work == mx, e_iota, N_EXPERTS), axis=0, keepdims=True)
        idxs.append(sel)
        vals.append(mx)
        work = jnp.where(e_iota == sel, -jnp.inf, work)
    exps = [jnp.exp(v - vals[0]) for v in vals]
    denom = exps[0] + exps[1] + exps[2] + exps[3]
    gate_out[...] = jnp.concatenate([e / denom for e in exps], axis=0)
    idx_out[...] = jnp.concatenate(idxs, axis=0)

    onehot = jnp.zeros((N_EXPERTS, tm), F32)
    for sel in idxs:
        onehot = onehot + (e_iota == sel).astype(F32)
    r_io = lax.broadcasted_iota(jnp.int32, (tm, tm), 0)
    c_io = lax.broadcasted_iota(jnp.int32, (tm, tm), 1)
    upper = (r_io < c_io).astype(BF16)
    prefix = jnp.dot(onehot.astype(BF16), upper, preferred_element_type=F32) + run_scr[:, 0:1]
    ranks = [jnp.sum(jnp.where(e_iota == sel, prefix, 0.0), axis=0, keepdims=True) for sel in idxs]
    rank_out[...] = jnp.concatenate(ranks, axis=0).astype(jnp.int32)
    run_new = run_scr[...] + jnp.sum(onehot, axis=1, keepdims=True)
    run_scr[...] = run_new
    cnt_out[...] = run_new.astype(jnp.int32)


def _outproj_call(pool_n, attn, x2, gattn_p, wo_pool, wo_attn_p, gffn, wr_t, br_col, tm):
    t, d = x2.shape
    nblk = t // tm
    const2 = lambda i: (0, 0)
    row = lambda i: (i, 0)
    col = lambda i: (0, i)
    in_specs = [
        pl.BlockSpec((tm, pool_n.shape[1]), row),
        pl.BlockSpec((tm, attn.shape[1]), row),
        pl.BlockSpec((tm, d), row),
        pl.BlockSpec(gattn_p.shape, const2),
        pl.BlockSpec(wo_pool.shape, const2),
        pl.BlockSpec(wo_attn_p.shape, const2),
        pl.BlockSpec(gffn.shape, const2),
        pl.BlockSpec(wr_t.shape, const2),
        pl.BlockSpec(br_col.shape, const2),
    ]
    out_shape = [
        jax.ShapeDtypeStruct((t, d), F32),
        jax.ShapeDtypeStruct((t, d // 2), jnp.uint32),
        jax.ShapeDtypeStruct((TOP_K, t), jnp.int32),
        jax.ShapeDtypeStruct((TOP_K, t), F32),
        jax.ShapeDtypeStruct((TOP_K, t), jnp.int32),
        jax.ShapeDtypeStruct((N_EXPERTS, LANE), jnp.int32),
    ]
    out_specs = [
        pl.BlockSpec((tm, d), row),
        pl.BlockSpec((tm, d // 2), row),
        pl.BlockSpec((TOP_K, tm), col),
        pl.BlockSpec((TOP_K, tm), col),
        pl.BlockSpec((TOP_K, tm), col),
        pl.BlockSpec((N_EXPERTS, LANE), const2),
    ]
    return pl.pallas_call(
        _outproj_kernel,
        grid=(nblk,),
        in_specs=in_specs,
        out_specs=out_specs,
        out_shape=out_shape,
        scratch_shapes=[pltpu.VMEM((N_EXPERTS, LANE), F32)],
        compiler_params=pltpu.CompilerParams(dimension_semantics=("arbitrary",),
                                             vmem_limit_bytes=VMEM_LIMIT),
        name="outproj",
    )(pool_n, attn, x2, gattn_p, wo_pool, wo_attn_p, gffn, wr_t, br_col)


ROW_UNROLL = 8


def _dispatch_kernel(bs_ref, nblk_ref, dest_ref, h_ref, xs_out, zbuf, sem, zsem):
    td = h_ref.shape[0]
    bm = zbuf.shape[0]
    n_exp = bs_ref.shape[0]

    @pl.when(pl.program_id(0) == 0)
    def _():
        zbuf[...] = jnp.zeros(zbuf.shape, zbuf.dtype)
        total = xs_out.shape[0] // bm
        first_tail = bs_ref[n_exp - 1] + nblk_ref[n_exp - 1]

        def z_copy(blk):
            row0 = pl.multiple_of(blk * bm, bm)
            return pltpu.make_async_copy(zbuf, xs_out.at[pl.ds(row0, bm)], zsem)

        def each_partial_block(fn):
            def per_expert(e, c):
                @pl.when(nblk_ref[e] > 0)
                def _():
                    fn(z_copy(bs_ref[e] + nblk_ref[e] - 1))
                return c

            def per_tail(b, c):
                fn(z_copy(b))
                return c

            lax.fori_loop(0, n_exp, per_expert, 0)
            lax.fori_loop(first_tail, total, per_tail, 0)

        each_partial_block(lambda cp: cp.start())
        each_partial_block(lambda cp: cp.wait())

    def body(g, carry):
        base = g * ROW_UNROLL
        for u in range(ROW_UNROLL):
            i = base + u
            for k in range(TOP_K):
                d = dest_ref[i * TOP_K + k]
                pltpu.make_async_copy(h_ref.at[pl.ds(i, 1)], xs_out.at[pl.ds(d, 1)], sem).start(
                    priority=k % 2)
        return carry

    lax.fori_loop(0, td // ROW_UNROLL, body, 0)
    for _ in range(TOP_K):
        pltpu.make_async_copy(h_ref, xs_out.at[pl.ds(0, td)], sem).wait()


def _dispatch_call(blk_start, nblk, dest, hp, n_rows, td, bm):
    t, dw = hp.shape
    grid_spec = pltpu.PrefetchScalarGridSpec(
        num_scalar_prefetch=2,
        grid=(t // td,),
        in_specs=[
            pl.BlockSpec((TOP_K * td,), lambda i, bs, nb: (i,), memory_space=pltpu.SMEM),
            pl.BlockSpec((td, dw), lambda i, bs, nb: (i, 0)),
        ],
        out_specs=pl.BlockSpec(memory_space=pl.ANY),
        scratch_shapes=[pltpu.VMEM((bm, dw), hp.dtype), pltpu.SemaphoreType.DMA,
                        pltpu.SemaphoreType.DMA],
    )
    return pl.pallas_call(
        _dispatch_kernel,
        grid_spec=grid_spec,
        out_shape=jax.ShapeDtypeStruct((n_rows, dw), hp.dtype),
        compiler_params=pltpu.CompilerParams(dimension_semantics=("arbitrary",),
                                             vmem_limit_bytes=VMEM_LIMIT),
        name="dispatch",
    )(blk_start, nblk, dest, hp)


CAST_ROWS = 64


def _expert_kernel(bs_ref, nblk_ref, wgu_ref, bgu_ref, wd_ref, bd_ref, xs_hbm, y_hbm,
                   wgu_bf, wd_bf, xbuf, ybuf, xsem, ysem):
    e = pl.program_id(0)
    n_exp = pl.num_programs(0)
    bm = xbuf.shape[1]
    d = wgu_ref.shape[1]
    f = wd_ref.shape[1]
    n = nblk_ref[e]
    blk0 = bs_ref[e]

    def x_copy(j, slot):
        row0 = pl.multiple_of((blk0 + j) * bm, bm)
        return pltpu.make_async_copy(xs_hbm.at[pl.ds(row0, bm)], xbuf.at[slot], xsem.at[slot])

    def y_copy(blk, slot):
        row0 = pl.multiple_of(blk * bm, bm)
        return pltpu.make_async_copy(ybuf.at[slot], y_hbm.at[pl.ds(row0, bm)], ysem.at[slot])

    @pl.when(n > 0)
    def _():
        x_copy(0, 0).start()

        def cast_gu(r, c):
            rows = pl.ds(pl.multiple_of(r * CAST_ROWS, CAST_ROWS), CAST_ROWS)
            wgu_bf[rows, :] = wgu_ref[0, rows, :].astype(BF16)
            return c

        def cast_d(r, c):
            rows = pl.ds(pl.multiple_of(r * CAST_ROWS, CAST_ROWS), CAST_ROWS)
            wd_bf[rows, :] = wd_ref[0, rows, :].astype(BF16)
            return c

        lax.fori_loop(0, d // CAST_ROWS, cast_gu, 0)
        lax.fori_loop(0, f // CAST_ROWS, cast_d, 0)

        def block(j, c):
            slot = j % 2
            x_copy(j, slot).wait()

            @pl.when(j + 1 < n)
            def _():
                x_copy(j + 1, 1 - slot).start()

            @pl.when(j >= 2)
            def _():
                y_copy(blk0 + j - 2, slot).wait()

            w = xbuf[slot]
            x_lo = lax.bitcast_convert_type(w << 16, F32).astype(BF16)
            x_hi = lax.bitcast_convert_type(w & jnp.uint32(0xFFFF0000), F32).astype(BF16)
            gu = (jnp.dot(x_lo, wgu_bf[:d // 2, :], preferred_element_type=F32)
                  + jnp.dot(x_hi, wgu_bf[d // 2:, :], preferred_element_type=F32) + bgu_ref[0])
            gate = jnp.minimum(gu[:, :f], SWIGLU_LIMIT)
            up = jnp.clip(gu[:, f:], -SWIGLU_LIMIT, SWIGLU_LIMIT)
            act = gate * jax.nn.sigmoid(SWIGLU_ALPHA * gate) * (up + 1.0)
            ybuf[slot] = jnp.dot(act.astype(BF16), wd_bf[...], preferred_element_type=F32) + bd_ref[0]
            y_copy(blk0 + j, slot).start()
            return c

        lax.fori_loop(0, n, block, 0)

        @pl.when(n >= 2)
        def _():
            y_copy(blk0 + n - 2, n % 2).wait()

        y_copy(blk0 + n - 1, (n - 1) % 2).wait()

    @pl.when(e == n_exp - 1)
    def _():
        first = blk0 + n
        total = y_hbm.shape[0] // bm
        ybuf[0] = jnp.zeros(ybuf.shape[1:], ybuf.dtype)

        def fill(b, c):
            y_copy(b, 0).start()
            return c

        def drain(b, c):
            y_copy(b, 0).wait()
            return c

        lax.fori_loop(first, total, fill, 0)
        lax.fori_loop(first, total, drain, 0)


def _expert_call(blk_start, nblk, xs, wgu, bgu, wd, bd, bm):
    r, dw = xs.shape
    e, d, f2 = wgu.shape
    f = wd.shape[1]

    def wsel(i, bs, nb):
        return (i, 0, 0)

    grid_spec = pltpu.PrefetchScalarGridSpec(
        num_scalar_prefetch=2,
        grid=(e,),
        in_specs=[
            pl.BlockSpec((1, d, f2), wsel),
            pl.BlockSpec((1, 1, f2), wsel),
            pl.BlockSpec((1, f, d), wsel),
            pl.BlockSpec((1, 1, d), wsel),
            pl.BlockSpec(memory_space=pl.ANY),
        ],
        out_specs=pl.BlockSpec(memory_space=pl.ANY),
        scratch_shapes=[pltpu.VMEM((d, f2), BF16), pltpu.VMEM((f, d), BF16),
                        pltpu.VMEM((2, bm, dw), xs.dtype), pltpu.VMEM((2, bm, d), F32),
                        pltpu.SemaphoreType.DMA((2,)), pltpu.SemaphoreType.DMA((2,))],
    )
    return pl.pallas_call(
        _expert_kernel,
        grid_spec=grid_spec,
        out_shape=jax.ShapeDtypeStruct((r, d), F32),
        compiler_params=pltpu.CompilerParams(dimension_semantics=("arbitrary",),
                                             vmem_limit_bytes=VMEM_LIMIT),
        name="experts",
    )(blk_start, nblk, wgu, bgu, wd, bd, xs)


def _combine_kernel(dest_ref, x1_ref, gate_ref, gfin_ref, yb_hbm, o_ref, ybuf, sem):
    tf, d = x1_ref.shape

    def body(g, carry):
        base = g * ROW_UNROLL
        for u in range(ROW_UNROLL):
            i = base + u
            for k in range(TOP_K):
                r = dest_ref[i * TOP_K + k]
                pltpu.make_async_copy(yb_hbm.at[pl.ds(r, 1)], ybuf.at[k, pl.ds(i, 1)], sem).start(
                    priority=k % 2)
        return carry

    lax.fori_loop(0, tf // ROW_UNROLL, body, 0)
    for k in range(TOP_K):
        pltpu.make_async_copy(yb_hbm.at[pl.ds(0, tf)], ybuf.at[k], sem).wait()
    g = gate_ref[...]
    y = x1_ref[...]
    for k in range(TOP_K):
        y = y + g[:, k:k + 1] * ybuf[k]
    o_ref[...] = _rms(y, gfin_ref[...], d)


def _combine_call(dest, x1, gates_t, gfin, yb, tf):
    t, d = x1.shape
    return pl.pallas_call(
        _combine_kernel,
        grid=(t // tf,),
        in_specs=[
            pl.BlockSpec((TOP_K * tf,), lambda i: (i,), memory_space=pltpu.SMEM),
            pl.BlockSpec((tf, d), lambda i: (i, 0)),
            pl.BlockSpec((tf, TOP_K), lambda i: (i, 0)),
            pl.BlockSpec(gfin.shape, lambda i: (0, 0)),
            pl.BlockSpec(memory_space=pl.ANY),
        ],
        out_specs=pl.BlockSpec((tf, d), lambda i: (i, 0)),
        out_shape=jax.ShapeDtypeStruct((t, d), F32),
        scratch_shapes=[pltpu.VMEM((TOP_K, tf, d), F32), pltpu.SemaphoreType.DMA],
        compiler_params=pltpu.CompilerParams(dimension_semantics=("arbitrary",),
                                             vmem_limit_bytes=VMEM_LIMIT),
        name="combine",
    )(dest, x1, gates_t, gfin, yb)


def _pad_head_cols(w, widths_in, place):
    k = w.shape[0]
    per = sum(widths_in)
    blocks = []
    for hh in range(N_HEADS):
        blocks.append(place(w[:, hh * per:(hh + 1) * per]))
    return jnp.concatenate(blocks, axis=1)


def _layer(x2, pos3, batch, seq, g_attn_norm, w_in, w_pool, b_pool, pool_scale, g_q_a, w_q_b,
           g_kv_a, w_kv_b, g_out_pool, g_out_attn, w_out, g_ffn_norm, w_router, b_router,
           w_gate_up, b_gate_up, w_down, b_down, tiles):
    tm, tq, tmo, td, bm, tf = tiles
    t, d = x2.shape
    d_pool = pool_scale.shape[0]
    gdim = d_pool // N_POOL_GROUPS
    q_rank = g_q_a.shape[0]
    kv_rank = g_kv_a.shape[0]
    half = QK_ROPE_DIM // 2

    o1, o2, o3 = d_pool, d_pool + q_rank, d_pool + q_rank + kv_rank
    zk = lambda n: jnp.zeros((d, n), F32)
    w_in_p = jnp.concatenate([w_in[:, :o3], zk(QK_NOPE_DIM), w_in[:, o3:], zk(LANE - QK_DIM)],
                             axis=1).astype(BF16)
    zq = lambda n: jnp.zeros((q_rank, n), F32)
    wq_p = _pad_head_cols(w_q_b, (QK_DIM,), lambda c: jnp.concatenate([c, zq(LANE - QK_DIM)], axis=1)
                          ).astype(BF16)
    zkv = lambda n: jnp.zeros((kv_rank, n), F32)
    per = QK_NOPE_DIM + V_HEAD_DIM
    wk_p = _pad_head_cols(w_kv_b, (per,), lambda c: jnp.concatenate(
        [c[:, :QK_NOPE_DIM], zkv(LANE - QK_NOPE_DIM)], axis=1))
    wv_p = _pad_head_cols(w_kv_b, (per,), lambda c: jnp.concatenate(
        [c[:, QK_NOPE_DIM:], zkv(LANE - V_HEAD_DIM)], axis=1))
    wkv_p = jnp.concatenate([wk_p, wv_p], axis=1).astype(BF16)
    inv = ROPE_THETA ** (-jnp.arange(half, dtype=F32) / half)
    inv_col = inv.reshape(half, 1)

    row = lambda v: v.reshape(1, -1)
    grp = lambda v: v.reshape(N_POOL_GROUPS, 1, gdim)

    pool_n, q, k, v = _proj_call(
        x2, pos3, row(g_attn_norm), w_in_p, inv_col, w_pool.astype(BF16), grp(b_pool),
        grp(pool_scale), grp(g_out_pool), row(g_q_a), wq_p, row(g_kv_a), wkv_p, seq, tm)

    attn = _attn_call(q, k, v, batch, seq, tq)

    d_attn = N_HEADS * V_HEAD_DIM
    wo_pool = w_out[:d_pool].astype(BF16)
    wo_attn = w_out[d_pool:].reshape(N_HEADS, V_HEAD_DIM, d)
    wo_attn_p = jnp.concatenate([wo_attn, jnp.zeros((N_HEADS, HEAD_PAD - V_HEAD_DIM, d), F32)],
                                axis=1).reshape(N_HEADS * HEAD_PAD, d).astype(BF16)
    gattn_p = jnp.concatenate([g_out_attn.reshape(N_HEADS, V_HEAD_DIM),
                               jnp.zeros((N_HEADS, HEAD_PAD - V_HEAD_DIM), F32)], axis=1).reshape(1, -1)

    x1, hp, idx, gates, rank, counts = _outproj_call(
        pool_n, attn, x2, gattn_p, wo_pool, wo_attn_p, row(g_ffn_norm), w_router.T,
        b_router.reshape(-1, 1), tmo)

    n_assign = t * TOP_K
    cnt = counts[:, 0]
    padded = ((cnt + bm - 1) // bm) * bm
    pad_end = jnp.cumsum(padded)
    pad_start = pad_end - padded
    e_ids = jnp.arange(N_EXPERTS, dtype=jnp.int32)
    dest = rank + jnp.sum(jnp.where(idx[None] == e_ids[:, None, None], pad_start[:, None, None], 0),
                          axis=0)
    dest = dest.T.reshape(-1)
    n_rows = (n_assign // bm + N_EXPERTS) * bm
    blk_start = (pad_start // bm).astype(jnp.int32)
    nblk = (padded // bm).astype(jnp.int32)

    xs = _dispatch_call(blk_start, nblk, dest, hp, n_rows, td, bm)
    yb = _expert_call(blk_start, nblk, xs, w_gate_up.astype(BF16), b_gate_up.reshape(N_EXPERTS, 1, -1),
                      w_down.astype(BF16), b_down.reshape(N_EXPERTS, 1, -1), bm)
    return x1, dest, gates, yb


def _forward(x, positions, g_attn_norm, w_in, w_pool, b_pool, pool_scale, g_q_a, w_q_b, g_kv_a,
             w_kv_b, g_out_pool, g_out_attn, w_out, g_ffn_norm, w_router, b_router, w_gate_up,
             b_gate_up, w_down, b_down, g_final, tiles):
    batch, seq, d = x.shape
    t = batch * seq
    tm, tq, tmo, td, bm, tf = tiles
    x2 = x.reshape(t, d)
    pos3 = positions.reshape(t // tm, 1, tm)
    depth = g_attn_norm.shape[0]
    assert depth == 1
    l = 0
    x1, dest, gates, yb = _layer(
        x2, pos3, batch, seq, g_attn_norm[l], w_in[l], w_pool[l], b_pool[l], pool_scale[l],
        g_q_a[l], w_q_b[l], g_kv_a[l], w_kv_b[l], g_out_pool[l], g_out_attn[l], w_out[l],
        g_ffn_norm[l], w_router[l], b_router[l], w_gate_up[l], b_gate_up[l], w_down[l],
        b_down[l], tiles)
    out = _combine_call(dest, x1, gates.T, g_final.reshape(1, -1), yb, tf)
    return out.reshape(batch, seq, d)


def _tiles_for(seq):
    tm = min(512, seq)
    tq = min(512, seq // 2)
    tmo = min(512, seq)
    td = min(512, seq)
    bm = 256 if seq >= 2048 else 128
    tf = min(256, seq)
    return (tm, tq, tmo, td, bm, tf)


def kernel(x, positions, g_attn_norm, w_in, w_pool, b_pool, pool_scale, g_q_a, w_q_b, g_kv_a, w_kv_b, g_out_pool, g_out_attn, w_out, g_ffn_norm, w_router, b_router, w_gate_up, b_gate_up, w_down, b_down, g_final):
    tiles = _tiles_for(x.shape[1])
    return _forward(x, positions, g_attn_norm, w_in, w_pool, b_pool, pool_scale, g_q_a, w_q_b,
                    g_kv_a, w_kv_b, g_out_pool, g_out_attn, w_out, g_ffn_norm, w_router, b_router,
                    w_gate_up, b_gate_up, w_down, b_down, g_final, tiles)
```

```python
import functools

import jax
import jax.numpy as jnp
from jax import lax
from jax.experimental import pallas as pl
from jax.experimental.pallas import tpu as pltpu

POOL_WINDOWS = (2, 4, 8, 16)
N_POOL_GROUPS = 4
N_HEADS = 8
V_HEAD_DIM = 64
QK_NOPE_DIM = 64
QK_ROPE_DIM = 32
QK_DIM = QK_NOPE_DIM + QK_ROPE_DIM
ROPE_THETA = 10000.0
N_EXPERTS = 32
TOP_K = 4
SWIGLU_ALPHA = 1.702
SWIGLU_LIMIT = 7.0
RMS_EPS = 1e-6
LOG2_E = 1.4426950408889634

LANE = 128
HEAD_PAD = 128
POOL_HALO = 16

VMEM_LIMIT = 56 * 1024 * 1024

F32 = jnp.float32
BF16 = jnp.bfloat16


def _rms(x, g, n):
    ms = jnp.sum(x * x, axis=-1, keepdims=True) * (1.0 / n)
    return x * lax.rsqrt(ms + RMS_EPS) * g


def _proj_kernel(blocks_per_seq, x_ref, pos_ref, g_attn_ref, w_in_ref, inv_ref, wpool_ref,
                 bpool_ref, pscale_ref, gpool_ref, gq_ref, wq_ref, gkv_ref, wkv_ref,
                 pool_out, q_out, k_out, v_out, ext_scr):
    tm = x_ref.shape[0]
    d_model = x_ref.shape[1]
    d_pool = pool_out.shape[1]
    gdim = d_pool // N_POOL_GROUPS
    q_rank = gq_ref.shape[1]
    kv_rank = gkv_ref.shape[1]
    i = pl.program_id(0)
    blk_in_seq = i % blocks_per_seq

    x = x_ref[...]
    h = _rms(x, g_attn_ref[...], d_model).astype(BF16)
    proj = jnp.dot(h, w_in_ref[...], preferred_element_type=F32)
    u = proj[:, :d_pool]
    qlat = proj[:, d_pool:d_pool + q_rank]
    kvlat = proj[:, d_pool + q_rank:d_pool + q_rank + kv_rank]
    krb = proj[:, d_pool + q_rank + kv_rank:]

    @pl.when(blk_in_seq == 0)
    def _():
        ext_scr[0:POOL_HALO, :] = jnp.zeros((POOL_HALO, d_pool), F32)

    @pl.when(blk_in_seq != 0)
    def _():
        ext_scr[0:POOL_HALO, :] = ext_scr[tm:tm + POOL_HALO, :]

    ext_scr[POOL_HALO:, :] = u
    t_in_seq = (blk_in_seq * tm + lax.broadcasted_iota(jnp.int32, (tm, gdim), 0) + 1).astype(F32)
    ys = []
    ssq = jnp.zeros((tm, 1), F32)
    for g, w in enumerate(POOL_WINDOWS):
        a = ext_scr[:, g * gdim:(g + 1) * gdim]
        shift = 1
        while shift < w:
            a = a + pltpu.roll(a, shift, axis=0)
            shift *= 2
        win = a[POOL_HALO:, :]
        cnt = jnp.minimum(t_in_seq, float(w))
        ug = u[:, g * gdim:(g + 1) * gdim]
        diff = (win / cnt - ug).astype(BF16)
        mixed = jnp.dot(diff, wpool_ref[g], preferred_element_type=F32) + bpool_ref[g]
        y = mixed * pscale_ref[g]
        ssq = ssq + jnp.sum(y * y, axis=-1, keepdims=True)
        ys.append(y)
    rinv = lax.rsqrt(ssq * (1.0 / d_pool) + RMS_EPS)
    for g in range(N_POOL_GROUPS):
        pool_out[:, g * gdim:(g + 1) * gdim] = (ys[g] * rinv * gpool_ref[g]).astype(pool_out.dtype)

    pos = pos_ref[0].astype(F32)
    ang_t = inv_ref[...] * pos
    cos_t = jnp.cos(ang_t)
    sin_t = jnp.sin(ang_t)
    half = QK_ROPE_DIM // 2
    ones = jnp.ones((QK_NOPE_DIM, tm), F32)
    c_t = jnp.concatenate([ones, cos_t, cos_t, jnp.ones((LANE - QK_DIM, tm), F32)], axis=0)
    sa_t = jnp.concatenate([jnp.zeros((QK_NOPE_DIM + half, tm), F32), sin_t,
                            jnp.zeros((LANE - QK_DIM, tm), F32)], axis=0)
    sb_t = jnp.concatenate([jnp.zeros((QK_NOPE_DIM, tm), F32), sin_t,
                            jnp.zeros((LANE - QK_DIM + half, tm), F32)], axis=0)
    c = c_t.T
    sa = sa_t.T
    sb = sb_t.T

    def rope(blk, c_, sa_, sb_):
        return blk * c_ + pltpu.roll(blk, half, axis=1) * sa_ - pltpu.roll(blk, LANE - half, axis=1) * sb_

    qn = _rms(qlat, gq_ref[...], q_rank).astype(BF16)
    q = jnp.dot(qn, wq_ref[...], preferred_element_type=F32)
    scale = QK_DIM ** -0.5 * LOG2_E
    cq, saq, sbq = c * scale, sa * scale, sb * scale
    for hh in range(N_HEADS):
        sl = slice(hh * HEAD_PAD, (hh + 1) * HEAD_PAD)
        q_out[:, sl] = rope(q[:, sl], cq, saq, sbq).astype(q_out.dtype)

    kvn = _rms(kvlat, gkv_ref[...], kv_rank).astype(BF16)
    kv = jnp.dot(kvn, wkv_ref[...], preferred_element_type=F32)
    kr = rope(krb, c, sa, sb)
    for hh in range(N_HEADS):
        sl = slice(hh * HEAD_PAD, (hh + 1) * HEAD_PAD)
        k_out[:, sl] = (kv[:, sl] + kr).astype(k_out.dtype)
    vlane = lax.broadcasted_iota(jnp.int32, (1, N_HEADS * HEAD_PAD), 1) % HEAD_PAD
    ones_col = (vlane == V_HEAD_DIM).astype(F32)
    v_out[0] = (kv[:, N_HEADS * HEAD_PAD:] + ones_col).T.astype(v_out.dtype)


def _proj_call(x2, pos3, g_attn, w_in_p, inv_col, wpool, bpool, pscale, gpool, gq, wq_p, gkv, wkv_p,
               seq, tm):
    t, d = x2.shape
    d_pool = pscale.shape[0] * pscale.shape[2]
    hp = N_HEADS * HEAD_PAD
    nblk = t // tm
    const2 = lambda i: (0, 0)
    const3 = lambda i: (0, 0, 0)
    in_specs = [
        pl.BlockSpec((tm, d), lambda i: (i, 0)),
        pl.BlockSpec((1, 1, tm), lambda i: (i, 0, 0)),
        pl.BlockSpec(g_attn.shape, const2),
        pl.BlockSpec(w_in_p.shape, const2),
        pl.BlockSpec(inv_col.shape, const2),
        pl.BlockSpec(wpool.shape, const3),
        pl.BlockSpec(bpool.shape, const3),
        pl.BlockSpec(pscale.shape, const3),
        pl.BlockSpec(gpool.shape, const3),
        pl.BlockSpec(gq.shape, const2),
        pl.BlockSpec(wq_p.shape, const2),
        pl.BlockSpec(gkv.shape, const2),
        pl.BlockSpec(wkv_p.shape, const2),
    ]
    out_shape = [
        jax.ShapeDtypeStruct((t, d_pool), BF16),
        jax.ShapeDtypeStruct((t, hp), BF16),
        jax.ShapeDtypeStruct((t, hp), BF16),
        jax.ShapeDtypeStruct((nblk, hp, tm), BF16),
    ]
    out_specs = [
        pl.BlockSpec((tm, d_pool), lambda i: (i, 0)),
        pl.BlockSpec((tm, hp), lambda i: (i, 0)),
        pl.BlockSpec((tm, hp), lambda i: (i, 0)),
        pl.BlockSpec((1, hp, tm), lambda i: (i, 0, 0)),
    ]
    return pl.pallas_call(
        functools.partial(_proj_kernel, seq // tm),
        grid=(nblk,),
        in_specs=in_specs,
        out_specs=out_specs,
        out_shape=out_shape,
        scratch_shapes=[pltpu.VMEM((tm + POOL_HALO, d_pool), F32)],
        compiler_params=pltpu.CompilerParams(dimension_semantics=("arbitrary",),
                                             vmem_limit_bytes=VMEM_LIMIT),
        name="proj",
    )(x2, pos3, g_attn, w_in_p, inv_col, wpool, bpool, pscale, gpool, gq, wq_p, gkv, wkv_p)


ROWSUM_ROW = V_HEAD_DIM


def _attn_kernel(q_ref, k_ref, vt_ref, o_ref, acc_ref):
    tq = q_ref.shape[0]
    th = vt_ref.shape[2]
    nsub = tq // th
    qi = pl.program_id(2)
    acc_ref[...] = jnp.zeros(acc_ref.shape, F32)

    def step(kc, m_prev, q_lo, masked):
        width = tq - q_lo
        start = pl.multiple_of(kc * th, th)
        k = k_ref[pl.ds(start, th), :]
        st = lax.dot_general(k, q_ref[q_lo:, :], (((1,), (1,)), ((), ())),
                             preferred_element_type=F32)
        if masked:
            key = lax.broadcasted_iota(jnp.int32, (th, width), 0)
            qry = lax.broadcasted_iota(jnp.int32, (th, width), 1)
            st = jnp.where(key <= qry, st, -jnp.inf)
        m_new = jnp.maximum(m_prev, jnp.max(st, axis=0, keepdims=True))
        alpha = jnp.exp2(m_prev - m_new)
        pt = jnp.exp2(st - m_new).astype(BF16)
        acc_ref[:, q_lo:] = alpha * acc_ref[:, q_lo:] + jnp.dot(vt_ref[kc], pt,
                                                                preferred_element_type=F32)
        return m_new

    def two_chunks(j, mp):
        return step(2 * j + 1, step(2 * j, mp, 0, False), 0, False)

    m = lax.fori_loop(0, (nsub // 2) * qi, two_chunks, jnp.full((1, tq), -jnp.inf, F32))
    for c in range(nsub):
        m = step(nsub * qi + c, m, c * th, True)[:, th:]

    acc = acc_ref[...]
    out = (acc / acc[ROWSUM_ROW:ROWSUM_ROW + 1, :]).T
    lane = lax.broadcasted_iota(jnp.int32, out.shape, 1)
    o_ref[...] = jnp.where(lane < V_HEAD_DIM, out, 0.0).astype(o_ref.dtype)


ATTN_Q_CHUNKS = 4


def _attn_call(q, k, vt, batch, seq, th):
    t, hp = q.shape
    tq = min(ATTN_Q_CHUNKS * th, seq)
    nq = seq // tq
    nchunk = seq // th
    return pl.pallas_call(
        _attn_kernel,
        grid=(batch, N_HEADS, nq),
        in_specs=[
            pl.BlockSpec((tq, HEAD_PAD), lambda b, h, qi: (b * nq + qi, h)),
            pl.BlockSpec((seq, HEAD_PAD), lambda b, h, qi: (b, h)),
            pl.BlockSpec((nchunk, HEAD_PAD, th), lambda b, h, qi: (b, h, 0)),
        ],
        out_specs=pl.BlockSpec((tq, HEAD_PAD), lambda b, h, qi: (b * nq + qi, h)),
        out_shape=jax.ShapeDtypeStruct((t, hp), F32),
        scratch_shapes=[pltpu.VMEM((HEAD_PAD, tq), F32)],
        compiler_params=pltpu.CompilerParams(
            dimension_semantics=("arbitrary", "arbitrary", "arbitrary"),
            vmem_limit_bytes=VMEM_LIMIT),
        name="attn",
    )(q, k, vt)


def _outproj_kernel(pool_ref, attn_ref, x_ref, gattn_ref, wop_ref, woa_ref, gffn_ref, wr_ref,
                    br_ref, x1_out, hp_out, idx_out, gate_out, rank_out, cnt_out, run_scr):
    tm, d = x_ref.shape
    d_attn = N_HEADS * V_HEAD_DIM
    i = pl.program_id(0)

    @pl.when(i == 0)
    def _():
        run_scr[...] = jnp.zeros(run_scr.shape, F32)

    attn = attn_ref[...]
    attn_n = _rms(attn, gattn_ref[...], d_attn).astype(BF16)
    mix = (jnp.dot(pool_ref[...], wop_ref[...], preferred_element_type=F32)
           + jnp.dot(attn_n, woa_ref[...], preferred_element_type=F32))
    x1 = x_ref[...] + mix
    x1_out[...] = x1
    h2 = _rms(x1, gffn_ref[...], d)

    hb = h2.astype(BF16).astype(F32)
    lo = lax.bitcast_convert_type(hb[:, :d // 2], jnp.uint32)
    hi = lax.bitcast_convert_type(hb[:, d // 2:], jnp.uint32)
    hp_out[...] = (lo >> 16) | (hi & jnp.uint32(0xFFFF0000))

    logits = lax.dot_general(wr_ref[...], h2, (((1,), (1,)), ((), ())),
                             preferred_element_type=F32,
                             precision=lax.Precision.HIGHEST) + br_ref[...]
    e_iota = lax.broadcasted_iota(jnp.int32, (N_EXPERTS, tm), 0)
    work = logits
    idxs, vals = [], []
    for _ in range(TOP_K):
        mx = jnp.max(work, axis=0, keepdims=True)
        sel = jnp.min(jnp.where(work == mx, e_iota, N_EXPERTS), axis=0, keepdims=True)
        idxs.append(sel)
        vals.append(mx)
        work = jnp.where(e_iota == sel, -jnp.inf, work)
    exps = [jnp.exp(v - vals[0]) for v in vals]
    denom = exps[0] + exps[1] + exps[2] + exps[3]
    gate_out[...] = jnp.concatenate([e / denom for e in exps], axis=0)
    idx_out[...] = jnp.concatenate(idxs, axis=0)

    onehot = jnp.zeros((N_EXPERTS, tm), F32)
    for sel in idxs:
        onehot = onehot + (e_iota == sel).astype(F32)
    r_io = lax.broadcasted_iota(jnp.int32, (tm, tm), 0)
    c_io = lax.broadcasted_iota(jnp.int32, (tm, tm), 1)
    upper = (r_io < c_io).astype(BF16)
    prefix = jnp.dot(onehot.astype(BF16), upper, preferred_element_type=F32) + run_scr[:, 0:1]
    ranks = [jnp.sum(jnp.where(e_iota == sel, prefix, 0.0), axis=0, keepdims=True) for sel in idxs]
    rank_out[...] = jnp.concatenate(ranks, axis=0).astype(jnp.int32)
    run_new = run_scr[...] + jnp.sum(onehot, axis=1, keepdims=True)
    run_scr[...] = run_new
    cnt_out[...] = run_new.astype(jnp.int32)


def _outproj_call(pool_n, attn, x2, gattn_p, wo_pool, wo_attn_p, gffn, wr_t, br_col, tm):
    t, d = x2.shape
    nblk = t // tm
    const2 = lambda i: (0, 0)
    row = lambda i: (i, 0)
    col = lambda i: (0, i)
    in_specs = [
        pl.BlockSpec((tm, pool_n.shape[1]), row),
        pl.BlockSpec((tm, attn.shape[1]), row),
        pl.BlockSpec((tm, d), row),
        pl.BlockSpec(gattn_p.shape, const2),
        pl.BlockSpec(wo_pool.shape, const2),
        pl.BlockSpec(wo_attn_p.shape, const2),
        pl.BlockSpec(gffn.shape, const2),
        pl.BlockSpec(wr_t.shape, const2),
        pl.BlockSpec(br_col.shape, const2),
    ]
    out_shape = [
        jax.ShapeDtypeStruct((t, d), F32),
        jax.ShapeDtypeStruct((t, d // 2), jnp.uint32),
        jax.ShapeDtypeStruct((TOP_K, t), jnp.int32),
        jax.ShapeDtypeStruct((TOP_K, t), F32),
        jax.ShapeDtypeStruct((TOP_K, t), jnp.int32),
        jax.ShapeDtypeStruct((N_EXPERTS, LANE), jnp.int32),
    ]
    out_specs = [
        pl.BlockSpec((tm, d), row),
        pl.BlockSpec((tm, d // 2), row),
        pl.BlockSpec((TOP_K, tm), col),
        pl.BlockSpec((TOP_K, tm), col),
        pl.BlockSpec((TOP_K, tm), col),
        pl.BlockSpec((N_EXPERTS, LANE), const2),
    ]
    return pl.pallas_call(
        _outproj_kernel,
        grid=(nblk,),
        in_specs=in_specs,
        out_specs=out_specs,
        out_shape=out_shape,
        scratch_shapes=[pltpu.VMEM((N_EXPERTS, LANE), F32)],
        compiler_params=pltpu.CompilerParams(dimension_semantics=("arbitrary",),
                                             vmem_limit_bytes=VMEM_LIMIT),
        name="outproj",
    )(pool_n, attn, x2, gattn_p, wo_pool, wo_attn_p, gffn, wr_t, br_col)


ROW_UNROLL = 8


def _dispatch_kernel(bs_ref, nblk_ref, dest_ref, h_ref, xs_out, zbuf, sem, zsem):
    td = h_ref.shape[0]
    bm = zbuf.shape[0]
    n_exp = bs_ref.shape[0]

    @pl.when(pl.program_id(0) == 0)
    def _():
        zbuf[...] = jnp.zeros(zbuf.shape, zbuf.dtype)
        total = xs_out.shape[0] // bm
        first_tail = bs_ref[n_exp - 1] + nblk_ref[n_exp - 1]

        def z_copy(blk):
            row0 = pl.multiple_of(blk * bm, bm)
            return pltpu.make_async_copy(zbuf, xs_out.at[pl.ds(row0, bm)], zsem)

        def each_partial_block(fn):
            def per_expert(e, c):
                @pl.when(nblk_ref[e] > 0)
                def _():
                    fn(z_copy(bs_ref[e] + nblk_ref[e] - 1))
                return c

            def per_tail(b, c):
                fn(z_copy(b))
                return c

            lax.fori_loop(0, n_exp, per_expert, 0)
            lax.fori_loop(first_tail, total, per_tail, 0)

        each_partial_block(lambda cp: cp.start())
        each_partial_block(lambda cp: cp.wait())

    def body(g, carry):
        base = g * ROW_UNROLL
        for u in range(ROW_UNROLL):
            i = base + u
            for k in range(TOP_K):
                d = dest_ref[i * TOP_K + k]
                pltpu.make_async_copy(h_ref.at[pl.ds(i, 1)], xs_out.at[pl.ds(d, 1)], sem).start(
                    priority=k % 2)
        return carry

    lax.fori_loop(0, td // ROW_UNROLL, body, 0)
    for _ in range(TOP_K):
        pltpu.make_async_copy(h_ref, xs_out.at[pl.ds(0, td)], sem).wait()


def _dispatch_call(blk_start, nblk, dest, hp, n_rows, td, bm):
    t, dw = hp.shape
    grid_spec = pltpu.PrefetchScalarGridSpec(
        num_scalar_prefetch=2,
        grid=(t // td,),
        in_specs=[
            pl.BlockSpec((TOP_K * td,), lambda i, bs, nb: (i,), memory_space=pltpu.SMEM),
            pl.BlockSpec((td, dw), lambda i, bs, nb: (i, 0)),
        ],
        out_specs=pl.BlockSpec(memory_space=pl.ANY),
        scratch_shapes=[pltpu.VMEM((bm, dw), hp.dtype), pltpu.SemaphoreType.DMA,
                        pltpu.SemaphoreType.DMA],
    )
    return pl.pallas_call(
        _dispatch_kernel,
        grid_spec=grid_spec,
        out_shape=jax.ShapeDtypeStruct((n_rows, dw), hp.dtype),
        compiler_params=pltpu.CompilerParams(dimension_semantics=("arbitrary",),
                                             vmem_limit_bytes=VMEM_LIMIT),
        name="dispatch",
    )(blk_start, nblk, dest, hp)


CAST_ROWS = 64


def _expert_kernel(bs_ref, nblk_ref, wgu_ref, bgu_ref, wd_ref, bd_ref, xs_hbm, y_hbm,
                   wgu_bf, wd_bf, xbuf, ybuf, xsem, ysem):
    e = pl.program_id(0)
    n_exp = pl.num_programs(0)
    bm = xbuf.shape[1]
    d = wgu_ref.shape[1]
    f = wd_ref.shape[1]
    n = nblk_ref[e]
    blk0 = bs_ref[e]

    def x_copy(j, slot):
        row0 = pl.multiple_of((blk0 + j) * bm, bm)
        return pltpu.make_async_copy(xs_hbm.at[pl.ds(row0, bm)], xbuf.at[slot], xsem.at[slot])

    def y_copy(blk, slot):
        row0 = pl.multiple_of(blk * bm, bm)
        return pltpu.make_async_copy(ybuf.at[slot], y_hbm.at[pl.ds(row0, bm)], ysem.at[slot])

    @pl.when(n > 0)
    def _():
        x_copy(0, 0).start()

        def cast_gu(r, c):
            rows = pl.ds(pl.multiple_of(r * CAST_ROWS, CAST_ROWS), CAST_ROWS)
            wgu_bf[rows, :] = wgu_ref[0, rows, :].astype(BF16)
            return c

        def cast_d(r, c):
            rows = pl.ds(pl.multiple_of(r * CAST_ROWS, CAST_ROWS), CAST_ROWS)
            wd_bf[rows, :] = wd_ref[0, rows, :].astype(BF16)
            return c

        lax.fori_loop(0, d // CAST_ROWS, cast_gu, 0)
        lax.fori_loop(0, f // CAST_ROWS, cast_d, 0)

        def block(j, c):
            slot = j % 2
            x_copy(j, slot).wait()

            @pl.when(j + 1 < n)
            def _():
                x_copy(j + 1, 1 - slot).start()

            @pl.when(j >= 2)
            def _():
                y_copy(blk0 + j - 2, slot).wait()

            w = xbuf[slot]
            x_lo = lax.bitcast_convert_type(w << 16, F32).astype(BF16)
            x_hi = lax.bitcast_convert_type(w & jnp.uint32(0xFFFF0000), F32).astype(BF16)
            gu = (jnp.dot(x_lo, wgu_bf[:d // 2, :], preferred_element_type=F32)
                  + jnp.dot(x_hi, wgu_bf[d // 2:, :], preferred_element_type=F32) + bgu_ref[0])
            gate = jnp.minimum(gu[:, :f], SWIGLU_LIMIT)
            up = jnp.clip(gu[:, f:], -SWIGLU_LIMIT, SWIGLU_LIMIT)
            act = gate * jax.nn.sigmoid(SWIGLU_ALPHA * gate) * (up + 1.0)
            ybuf[slot] = jnp.dot(act.astype(BF16), wd_bf[...], preferred_element_type=F32) + bd_ref[0]
            y_copy(blk0 + j, slot).start()
            return c

        lax.fori_loop(0, n, block, 0)

        @pl.when(n >= 2)
        def _():
            y_copy(blk0 + n - 2, n % 2).wait()

        y_copy(blk0 + n - 1, (n - 1) % 2).wait()

    @pl.when(e == n_exp - 1)
    def _():
        first = blk0 + n
        total = y_hbm.shape[0] // bm
        ybuf[0] = jnp.zeros(ybuf.shape[1:], ybuf.dtype)

        def fill(b, c):
            y_copy(b, 0).start()
            return c

        def drain(b, c):
            y_copy(b, 0).wait()
            return c

        lax.fori_loop(first, total, fill, 0)
        lax.fori_loop(first, total, drain, 0)


def _expert_call(blk_start, nblk, xs, wgu, bgu, wd, bd, bm):
    r, dw = xs.shape
    e, d, f2 = wgu.shape
    f = wd.shape[1]

    def wsel(i, bs, nb):
        return (i, 0, 0)

    grid_spec = pltpu.PrefetchScalarGridSpec(
        num_scalar_prefetch=2,
        grid=(e,),
        in_specs=[
            pl.BlockSpec((1, d, f2), wsel),
            pl.BlockSpec((1, 1, f2), wsel),
            pl.BlockSpec((1, f, d), wsel),
            pl.BlockSpec((1, 1, d), wsel),
            pl.BlockSpec(memory_space=pl.ANY),
        ],
        out_specs=pl.BlockSpec(memory_space=pl.ANY),
        scratch_shapes=[pltpu.VMEM((d, f2), BF16), pltpu.VMEM((f, d), BF16),
                        pltpu.VMEM((2, bm, dw), xs.dtype), pltpu.VMEM((2, bm, d), F32),
                        pltpu.SemaphoreType.DMA((2,)), pltpu.SemaphoreType.DMA((2,))],
    )
    return pl.pallas_call(
        _expert_kernel,
        grid_spec=grid_spec,
        out_shape=jax.ShapeDtypeStruct((r, d), F32),
        compiler_params=pltpu.CompilerParams(dimension_semantics=("arbitrary",),
                                             vmem_limit_bytes=VMEM_LIMIT),
        name="experts",
    )(blk_start, nblk, wgu, bgu, wd, bd, xs)


def _combine_kernel(dest_ref, x1_ref, gate_ref, gfin_ref, yb_hbm, o_ref, ybuf, sem):
    tf, d = x1_ref.shape

    def body(g, carry):
        base = g * ROW_UNROLL
        for u in range(ROW_UNROLL):
            i = base + u
            for k in range(TOP_K):
                r = dest_ref[i * TOP_K + k]
                pltpu.make_async_copy(yb_hbm.at[pl.ds(r, 1)], ybuf.at[k, pl.ds(i, 1)], sem).start(
                    priority=k % 2)
        return carry

    lax.fori_loop(0, tf // ROW_UNROLL, body, 0)
    for k in range(TOP_K):
        pltpu.make_async_copy(yb_hbm.at[pl.ds(0, tf)], ybuf.at[k], sem).wait()
    g = gate_ref[...]
    y = x1_ref[...]
    for k in range(TOP_K):
        y = y + g[:, k:k + 1] * ybuf[k]
    o_ref[...] = _rms(y, gfin_ref[...], d)


def _combine_call(dest, x1, gates_t, gfin, yb, tf):
    t, d = x1.shape
    return pl.pallas_call(
        _combine_kernel,
        grid=(t // tf,),
        in_specs=[
            pl.BlockSpec((TOP_K * tf,), lambda i: (i,), memory_space=pltpu.SMEM),
            pl.BlockSpec((tf, d), lambda i: (i, 0)),
            pl.BlockSpec((tf, TOP_K), lambda i: (i, 0)),
            pl.BlockSpec(gfin.shape, lambda i: (0, 0)),
            pl.BlockSpec(memory_space=pl.ANY),
        ],
        out_specs=pl.BlockSpec((tf, d), lambda i: (i, 0)),
        out_shape=jax.ShapeDtypeStruct((t, d), F32),
        scratch_shapes=[pltpu.VMEM((TOP_K, tf, d), F32), pltpu.SemaphoreType.DMA],
        compiler_params=pltpu.CompilerParams(dimension_semantics=("arbitrary",),
                                             vmem_limit_bytes=VMEM_LIMIT),
        name="combine",
    )(dest, x1, gates_t, gfin, yb)


def _pad_head_cols(w, widths_in, place):
    k = w.shape[0]
    per = sum(widths_in)
    blocks = []
    for hh in range(N_HEADS):
        blocks.append(place(w[:, hh * per:(hh + 1) * per]))
    return jnp.concatenate(blocks, axis=1)


def _layer(x2, pos3, batch, seq, g_attn_norm, w_in, w_pool, b_pool, pool_scale, g_q_a, w_q_b,
           g_kv_a, w_kv_b, g_out_pool, g_out_attn, w_out, g_ffn_norm, w_router, b_router,
           w_gate_up, b_gate_up, w_down, b_down, tiles):
    tm, tq, tmo, td, bm, tf = tiles
    t, d = x2.shape
    d_pool = pool_scale.shape[0]
    gdim = d_pool // N_POOL_GROUPS
    q_rank = g_q_a.shape[0]
    kv_rank = g_kv_a.shape[0]
    half = QK_ROPE_DIM // 2

    o1, o2, o3 = d_pool, d_pool + q_rank, d_pool + q_rank + kv_rank
    zk = lambda n: jnp.zeros((d, n), F32)
    w_in_p = jnp.concatenate([w_in[:, :o3], zk(QK_NOPE_DIM), w_in[:, o3:], zk(LANE - QK_DIM)],
                             axis=1).astype(BF16)
    zq = lambda n: jnp.zeros((q_rank, n), F32)
    wq_p = _pad_head_cols(w_q_b, (QK_DIM,), lambda c: jnp.concatenate([c, zq(LANE - QK_DIM)], axis=1)
                          ).astype(BF16)
    zkv = lambda n: jnp.zeros((kv_rank, n), F32)
    per = QK_NOPE_DIM + V_HEAD_DIM
    wk_p = _pad_head_cols(w_kv_b, (per,), lambda c: jnp.concatenate(
        [c[:, :QK_NOPE_DIM], zkv(LANE - QK_NOPE_DIM)], axis=1))
    wv_p = _pad_head_cols(w_kv_b, (per,), lambda c: jnp.concatenate(
        [c[:, QK_NOPE_DIM:], zkv(LANE - V_HEAD_DIM)], axis=1))
    wkv_p = jnp.concatenate([wk_p, wv_p], axis=1).astype(BF16)
    inv = ROPE_THETA ** (-jnp.arange(half, dtype=F32) / half)
    inv_col = inv.reshape(half, 1)

    row = lambda v: v.reshape(1, -1)
    grp = lambda v: v.reshape(N_POOL_GROUPS, 1, gdim)

    pool_n, q, k, v = _proj_call(
        x2, pos3, row(g_attn_norm), w_in_p, inv_col, w_pool.astype(BF16), grp(b_pool),
        grp(pool_scale), grp(g_out_pool), row(g_q_a), wq_p, row(g_kv_a), wkv_p, seq, tm)

    attn = _attn_call(q, k, v, batch, seq, tq)

    d_attn = N_HEADS * V_HEAD_DIM
    wo_pool = w_out[:d_pool].astype(BF16)
    wo_attn = w_out[d_pool:].reshape(N_HEADS, V_HEAD_DIM, d)
    wo_attn_p = jnp.concatenate([wo_attn, jnp.zeros((N_HEADS, HEAD_PAD - V_HEAD_DIM, d), F32)],
                                axis=1).reshape(N_HEADS * HEAD_PAD, d).astype(BF16)
    gattn_p = jnp.concatenate([g_out_attn.reshape(N_HEADS, V_HEAD_DIM),
                               jnp.zeros((N_HEADS, HEAD_PAD - V_HEAD_DIM), F32)], axis=1).reshape(1, -1)

    x1, hp, idx, gates, rank, counts = _outproj_call(
        pool_n, attn, x2, gattn_p, wo_pool, wo_attn_p, row(g_ffn_norm), w_router.T,
        b_router.reshape(-1, 1), tmo)

    n_assign = t * TOP_K
    cnt = counts[:, 0]
    padded = ((cnt + bm - 1) // bm) * bm
    pad_end = jnp.cumsum(padded)
    pad_start = pad_end - padded
    e_ids = jnp.arange(N_EXPERTS, dtype=jnp.int32)
    dest = rank + jnp.sum(jnp.where(idx[None] == e_ids[:, None, None], pad_start[:, None, None], 0),
                          axis=0)
    dest = dest.T.reshape(-1)
    n_rows = (n_assign // bm + N_EXPERTS) * bm
    blk_start = (pad_start // bm).astype(jnp.int32)
    nblk = (padded // bm).astype(jnp.int32)

    xs = _dispatch_call(blk_start, nblk, dest, hp, n_rows, td, bm)
    yb = _expert_call(blk_start, nblk, xs, w_gate_up, b_gate_up.reshape(N_EXPERTS, 1, -1),
                      w_down, b_down.reshape(N_EXPERTS, 1, -1), bm)
    return x1, dest, gates, yb


def _forward(x, positions, g_attn_norm, w_in, w_pool, b_pool, pool_scale, g_q_a, w_q_b, g_kv_a,
             w_kv_b, g_out_pool, g_out_attn, w_out, g_ffn_norm, w_router, b_router, w_gate_up,
             b_gate_up, w_down, b_down, g_final, tiles):
    batch, seq, d = x.shape
    t = batch * seq
    tm, tq, tmo, td, bm, tf = tiles
    x2 = x.reshape(t, d)
    pos3 = positions.reshape(t // tm, 1, tm)
    depth = g_attn_norm.shape[0]
    assert depth == 1
    l = 0
    x1, dest, gates, yb = _layer(
        x2, pos3, batch, seq, g_attn_norm[l], w_in[l], w_pool[l], b_pool[l], pool_scale[l],
        g_q_a[l], w_q_b[l], g_kv_a[l], w_kv_b[l], g_out_pool[l], g_out_attn[l], w_out[l],
        g_ffn_norm[l], w_router[l], b_router[l], w_gate_up[l], b_gate_up[l], w_down[l],
        b_down[l], tiles)
    out = _combine_call(dest, x1, gates.T, g_final.reshape(1, -1), yb, tf)
    return out.reshape(batch, seq, d)


def _tiles_for(seq):
    tm = min(512, seq)
    tq = min(512, seq // 2)
    tmo = min(512, seq)
    td = min(512, seq)
    bm = 256 if seq >= 2048 else 128
    tf = min(256, seq)
    return (tm, tq, tmo, td, bm, tf)


def kernel(x, positions, g_attn_norm, w_in, w_pool, b_pool, pool_scale, g_q_a, w_q_b, g_kv_a, w_kv_b, g_out_pool, g_out_attn, w_out, g_ffn_norm, w_router, b_router, w_gate_up, b_gate_up, w_down, b_down, g_final):
    tiles = _tiles_for(x.shape[1])
    return _forward(x, positions, g_attn_norm, w_in, w_pool, b_pool, pool_scale, g_q_a, w_q_b,
                    g_kv_a, w_kv_b, g_out_pool, g_out_attn, w_out, g_ffn_norm, w_router, b_router,
                    w_gate_up, b_gate_up, w_down, b_down, g_final, tiles)
```

```python
import functools

import jax
import jax.numpy as jnp
from jax import lax
from jax.experimental import pallas as pl
from jax.experimental.pallas import tpu as pltpu

POOL_WINDOWS = (2, 4, 8, 16)
N_POOL_GROUPS = 4
N_HEADS = 8
V_HEAD_DIM = 64
QK_NOPE_DIM = 64
QK_ROPE_DIM = 32
QK_DIM = QK_NOPE_DIM + QK_ROPE_DIM
ROPE_THETA = 10000.0
N_EXPERTS = 32
TOP_K = 4
SWIGLU_ALPHA = 1.702
SWIGLU_LIMIT = 7.0
RMS_EPS = 1e-6
LOG2_E = 1.4426950408889634

LANE = 128
HEAD_PAD = 128
POOL_HALO = 16

VMEM_LIMIT = 56 * 1024 * 1024

F32 = jnp.float32
BF16 = jnp.bfloat16


def _rms(x, g, n):
    ms = jnp.sum(x * x, axis=-1, keepdims=True) * (1.0 / n)
    return x * lax.rsqrt(ms + RMS_EPS) * g


def _proj_kernel(blocks_per_seq, x_ref, pos_ref, g_attn_ref, w_in_ref, inv_ref, wpool_ref,
                 bpool_ref, pscale_ref, gpool_ref, gq_ref, wq_ref, gkv_ref, wkv_ref,
                 pool_out, q_out, k_out, v_out, ext_scr):
    tm = x_ref.shape[0]
    d_model = x_ref.shape[1]
    d_pool = pool_out.shape[1]
    gdim = d_pool // N_POOL_GROUPS
    q_rank = gq_ref.shape[1]
    kv_rank = gkv_ref.shape[1]
    i = pl.program_id(0)
    blk_in_seq = i % blocks_per_seq

    x = x_ref[...]
    h = _rms(x, g_attn_ref[...], d_model).astype(BF16)
    proj = jnp.dot(h, w_in_ref[...], preferred_element_type=F32)
    u = proj[:, :d_pool]
    qlat = proj[:, d_pool:d_pool + q_rank]
    kvlat = proj[:, d_pool + q_rank:d_pool + q_rank + kv_rank]
    krb = proj[:, d_pool + q_rank + kv_rank:]

    @pl.when(blk_in_seq == 0)
    def _():
        ext_scr[0:POOL_HALO, :] = jnp.zeros((POOL_HALO, d_pool), F32)

    @pl.when(blk_in_seq != 0)
    def _():
        ext_scr[0:POOL_HALO, :] = ext_scr[tm:tm + POOL_HALO, :]

    ext_scr[POOL_HALO:, :] = u
    t_in_seq = (blk_in_seq * tm + lax.broadcasted_iota(jnp.int32, (tm, gdim), 0) + 1).astype(F32)
    ys = []
    ssq = jnp.zeros((tm, 1), F32)
    for g, w in enumerate(POOL_WINDOWS):
        a = ext_scr[:, g * gdim:(g + 1) * gdim]
        shift = 1
        while shift < w:
            a = a + pltpu.roll(a, shift, axis=0)
            shift *= 2
        win = a[POOL_HALO:, :]
        cnt = jnp.minimum(t_in_seq, float(w))
        ug = u[:, g * gdim:(g + 1) * gdim]
        diff = (win / cnt - ug).astype(BF16)
        mixed = jnp.dot(diff, wpool_ref[g], preferred_element_type=F32) + bpool_ref[g]
        y = mixed * pscale_ref[g]
        ssq = ssq + jnp.sum(y * y, axis=-1, keepdims=True)
        ys.append(y)
    rinv = lax.rsqrt(ssq * (1.0 / d_pool) + RMS_EPS)
    for g in range(N_POOL_GROUPS):
        pool_out[:, g * gdim:(g + 1) * gdim] = (ys[g] * rinv * gpool_ref[g]).astype(pool_out.dtype)

    pos = pos_ref[0].astype(F32)
    ang_t = inv_ref[...] * pos
    cos_t = jnp.cos(ang_t)
    sin_t = jnp.sin(ang_t)
    half = QK_ROPE_DIM // 2
    ones = jnp.ones((QK_NOPE_DIM, tm), F32)
    c_t = jnp.concatenate([ones, cos_t, cos_t, jnp.ones((LANE - QK_DIM, tm), F32)], axis=0)
    sa_t = jnp.concatenate([jnp.zeros((QK_NOPE_DIM + half, tm), F32), sin_t,
                            jnp.zeros((LANE - QK_DIM, tm), F32)], axis=0)
    sb_t = jnp.concatenate([jnp.zeros((QK_NOPE_DIM, tm), F32), sin_t,
                            jnp.zeros((LANE - QK_DIM + half, tm), F32)], axis=0)
    c = c_t.T
    sa = sa_t.T
    sb = sb_t.T

    def rope(blk, c_, sa_, sb_):
        return blk * c_ + pltpu.roll(blk, half, axis=1) * sa_ - pltpu.roll(blk, LANE - half, axis=1) * sb_

    qn = _rms(qlat, gq_ref[...], q_rank).astype(BF16)
    q = jnp.dot(qn, wq_ref[...], preferred_element_type=F32)
    scale = QK_DIM ** -0.5 * LOG2_E
    cq, saq, sbq = c * scale, sa * scale, sb * scale
    for hh in range(N_HEADS):
        sl = slice(hh * HEAD_PAD, (hh + 1) * HEAD_PAD)
        q_out[:, sl] = rope(q[:, sl], cq, saq, sbq).astype(q_out.dtype)

    kvn = _rms(kvlat, gkv_ref[...], kv_rank).astype(BF16)
    kv = jnp.dot(kvn, wkv_ref[...], preferred_element_type=F32)
    kr = rope(krb, c, sa, sb)
    for hh in range(N_HEADS):
        sl = slice(hh * HEAD_PAD, (hh + 1) * HEAD_PAD)
        k_out[:, sl] = (kv[:, sl] + kr).astype(k_out.dtype)
    vlane = lax.broadcasted_iota(jnp.int32, (1, N_HEADS * HEAD_PAD), 1) % HEAD_PAD
    ones_col = (vlane == V_HEAD_DIM).astype(F32)
    v_out[0] = (kv[:, N_HEADS * HEAD_PAD:] + ones_col).T.astype(v_out.dtype)


def _proj_call(x2, pos3, g_attn, w_in_p, inv_col, wpool, bpool, pscale, gpool, gq, wq_p, gkv, wkv_p,
               seq, tm):
    t, d = x2.shape
    d_pool = pscale.shape[0] * pscale.shape[2]
    hp = N_HEADS * HEAD_PAD
    nblk = t // tm
    const2 = lambda i: (0, 0)
    const3 = lambda i: (0, 0, 0)
    in_specs = [
        pl.BlockSpec((tm, d), lambda i: (i, 0)),
        pl.BlockSpec((1, 1, tm), lambda i: (i, 0, 0)),
        pl.BlockSpec(g_attn.shape, const2),
        pl.BlockSpec(w_in_p.shape, const2),
        pl.BlockSpec(inv_col.shape, const2),
        pl.BlockSpec(wpool.shape, const3),
        pl.BlockSpec(bpool.shape, const3),
        pl.BlockSpec(pscale.shape, const3),
        pl.BlockSpec(gpool.shape, const3),
        pl.BlockSpec(gq.shape, const2),
        pl.BlockSpec(wq_p.shape, const2),
        pl.BlockSpec(gkv.shape, const2),
        pl.BlockSpec(wkv_p.shape, const2),
    ]
    out_shape = [
        jax.ShapeDtypeStruct((t, d_pool), BF16),
        jax.ShapeDtypeStruct((t, hp), BF16),
        jax.ShapeDtypeStruct((t, hp), BF16),
        jax.ShapeDtypeStruct((nblk, hp, tm), BF16),
    ]
    out_specs = [
        pl.BlockSpec((tm, d_pool), lambda i: (i, 0)),
        pl.BlockSpec((tm, hp), lambda i: (i, 0)),
        pl.BlockSpec((tm, hp), lambda i: (i, 0)),
        pl.BlockSpec((1, hp, tm), lambda i: (i, 0, 0)),
    ]
    return pl.pallas_call(
        functools.partial(_proj_kernel, seq // tm),
        grid=(nblk,),
        in_specs=in_specs,
        out_specs=out_specs,
        out_shape=out_shape,
        scratch_shapes=[pltpu.VMEM((tm + POOL_HALO, d_pool), F32)],
        compiler_params=pltpu.CompilerParams(dimension_semantics=("arbitrary",),
                                             vmem_limit_bytes=VMEM_LIMIT),
        name="proj",
    )(x2, pos3, g_attn, w_in_p, inv_col, wpool, bpool, pscale, gpool, gq, wq_p, gkv, wkv_p)


ROWSUM_ROW = V_HEAD_DIM


def _attn_kernel(q_ref, k_ref, vt_ref, o_ref, acc_ref):
    tq = q_ref.shape[0]
    th = vt_ref.shape[2]
    nsub = tq // th
    qi = pl.program_id(2)
    acc_ref[...] = jnp.zeros(acc_ref.shape, F32)

    def step(kc, m_prev, q_lo, masked):
        width = tq - q_lo
        start = pl.multiple_of(kc * th, th)
        k = k_ref[pl.ds(start, th), :]
        st = lax.dot_general(k, q_ref[q_lo:, :], (((1,), (1,)), ((), ())),
                             preferred_element_type=F32)
        if masked:
            key = lax.broadcasted_iota(jnp.int32, (th, width), 0)
            qry = lax.broadcasted_iota(jnp.int32, (th, width), 1)
            st = jnp.where(key <= qry, st, -jnp.inf)
        m_new = jnp.maximum(m_prev, jnp.max(st, axis=0, keepdims=True))
        alpha = jnp.exp2(m_prev - m_new)
        pt = jnp.exp2(st - m_new).astype(BF16)
        acc_ref[:, q_lo:] = alpha * acc_ref[:, q_lo:] + jnp.dot(vt_ref[kc], pt,
                                                                preferred_element_type=F32)
        return m_new

    def two_chunks(j, mp):
        return step(2 * j + 1, step(2 * j, mp, 0, False), 0, False)

    m = lax.fori_loop(0, (nsub // 2) * qi, two_chunks, jnp.full((1, tq), -jnp.inf, F32))
    for c in range(nsub):
        m = step(nsub * qi + c, m, c * th, True)[:, th:]

    acc = acc_ref[...]
    out = (acc / acc[ROWSUM_ROW:ROWSUM_ROW + 1, :]).T
    lane = lax.broadcasted_iota(jnp.int32, out.shape, 1)
    o_ref[...] = jnp.where(lane < V_HEAD_DIM, out, 0.0).astype(o_ref.dtype)


ATTN_Q_CHUNKS = 4


def _attn_call(q, k, vt, batch, seq, th):
    t, hp = q.shape
    tq = min(ATTN_Q_CHUNKS * th, seq)
    nq = seq // tq
    nchunk = seq // th
    return pl.pallas_call(
        _attn_kernel,
        grid=(batch, N_HEADS, nq),
        in_specs=[
            pl.BlockSpec((tq, HEAD_PAD), lambda b, h, qi: (b * nq + qi, h)),
            pl.BlockSpec((seq, HEAD_PAD), lambda b, h, qi: (b, h)),
            pl.BlockSpec((nchunk, HEAD_PAD, th), lambda b, h, qi: (b, h, 0)),
        ],
        out_specs=pl.BlockSpec((tq, HEAD_PAD), lambda b, h, qi: (b * nq + qi, h)),
        out_shape=jax.ShapeDtypeStruct((t, hp), F32),
        scratch_shapes=[pltpu.VMEM((HEAD_PAD, tq), F32)],
        compiler_params=pltpu.CompilerParams(
            dimension_semantics=("arbitrary", "arbitrary", "arbitrary"),
            vmem_limit_bytes=VMEM_LIMIT),
        name="attn",
    )(q, k, vt)


def _outproj_kernel(pool_ref, attn_ref, x_ref, gattn_ref, wop_ref, woa_ref, gffn_ref, wr_ref,
                    br_ref, x1_out, hp_out, idx_out, gate_out, rank_out, cnt_out, run_scr):
    tm, d = x_ref.shape
    d_attn = N_HEADS * V_HEAD_DIM
    i = pl.program_id(0)

    @pl.when(i == 0)
    def _():
        run_scr[...] = jnp.zeros(run_scr.shape, F32)

    attn = attn_ref[...]
    attn_n = _rms(attn, gattn_ref[...], d_attn).astype(BF16)
    mix = (jnp.dot(pool_ref[...], wop_ref[...], preferred_element_type=F32)
           + jnp.dot(attn_n, woa_ref[...], preferred_element_type=F32))
    x1 = x_ref[...] + mix
    x1_out[...] = x1
    h2 = _rms(x1, gffn_ref[...], d)

    hb = h2.astype(BF16).astype(F32)
    lo = lax.bitcast_convert_type(hb[:, :d // 2], jnp.uint32)
    hi = lax.bitcast_convert_type(hb[:, d // 2:], jnp.uint32)
    word = (lo >> 16) | (hi & jnp.uint32(0xFFFF0000))
    nw = d // 2 // LANE
    for c in range(nw):
        hp_out[pl.ds(c, tm, stride=nw), :] = word[:, c * LANE:(c + 1) * LANE]

    logits = lax.dot_general(wr_ref[...], h2, (((1,), (1,)), ((), ())),
                             preferred_element_type=F32,
                             precision=lax.Precision.HIGHEST) + br_ref[...]
    e_iota = lax.broadcasted_iota(jnp.int32, (N_EXPERTS, tm), 0)
    work = logits
    idxs, vals = [], []
    for _ in range(TOP_K):
        mx = jnp.max(work, axis=0, keepdims=True)
        sel = jnp.min(jnp.where(work == mx, e_iota, N_EXPERTS), axis=0, keepdims=True)
        idxs.append(sel)
        vals.append(mx)
        work = jnp.where(e_iota == sel, -jnp.inf, work)
    exps = [jnp.exp(v - vals[0]) for v in vals]
    denom = exps[0] + exps[1] + exps[2] + exps[3]
    gate_out[...] = jnp.concatenate([e / denom for e in exps], axis=0)
    idx_out[...] = jnp.concatenate(idxs, axis=0)

    onehot = jnp.zeros((N_EXPERTS, tm), F32)
    for sel in idxs:
        onehot = onehot + (e_iota == sel).astype(F32)
    r_io = lax.broadcasted_iota(jnp.int32, (tm, tm), 0)
    c_io = lax.broadcasted_iota(jnp.int32, (tm, tm), 1)
    upper = (r_io < c_io).astype(BF16)
    prefix = jnp.dot(onehot.astype(BF16), upper, preferred_element_type=F32) + run_scr[:, 0:1]
    ranks = [jnp.sum(jnp.where(e_iota == sel, prefix, 0.0), axis=0, keepdims=True) for sel in idxs]
    rank_out[...] = jnp.concatenate(ranks, axis=0).astype(jnp.int32)
    run_new = run_scr[...] + jnp.sum(onehot, axis=1, keepdims=True)
    run_scr[...] = run_new
    cnt_out[...] = run_new.astype(jnp.int32)


def _outproj_call(pool_n, attn, x2, gattn_p, wo_pool, wo_attn_p, gffn, wr_t, br_col, tm):
    t, d = x2.shape
    nblk = t // tm
    const2 = lambda i: (0, 0)
    row = lambda i: (i, 0)
    col = lambda i: (0, i)
    in_specs = [
        pl.BlockSpec((tm, pool_n.shape[1]), row),
        pl.BlockSpec((tm, attn.shape[1]), row),
        pl.BlockSpec((tm, d), row),
        pl.BlockSpec(gattn_p.shape, const2),
        pl.BlockSpec(wo_pool.shape, const2),
        pl.BlockSpec(wo_attn_p.shape, const2),
        pl.BlockSpec(gffn.shape, const2),
        pl.BlockSpec(wr_t.shape, const2),
        pl.BlockSpec(br_col.shape, const2),
    ]
    out_shape = [
        jax.ShapeDtypeStruct((t, d), F32),
        jax.ShapeDtypeStruct((t * (d // 2 // LANE), LANE), jnp.uint32),
        jax.ShapeDtypeStruct((TOP_K, t), jnp.int32),
        jax.ShapeDtypeStruct((TOP_K, t), F32),
        jax.ShapeDtypeStruct((TOP_K, t), jnp.int32),
        jax.ShapeDtypeStruct((N_EXPERTS, LANE), jnp.int32),
    ]
    out_specs = [
        pl.BlockSpec((tm, d), row),
        pl.BlockSpec((tm * (d // 2 // LANE), LANE), row),
        pl.BlockSpec((TOP_K, tm), col),
        pl.BlockSpec((TOP_K, tm), col),
        pl.BlockSpec((TOP_K, tm), col),
        pl.BlockSpec((N_EXPERTS, LANE), const2),
    ]
    return pl.pallas_call(
        _outproj_kernel,
        grid=(nblk,),
        in_specs=in_specs,
        out_specs=out_specs,
        out_shape=out_shape,
        scratch_shapes=[pltpu.VMEM((N_EXPERTS, LANE), F32)],
        compiler_params=pltpu.CompilerParams(dimension_semantics=("arbitrary",),
                                             vmem_limit_bytes=VMEM_LIMIT),
        name="outproj",
    )(pool_n, attn, x2, gattn_p, wo_pool, wo_attn_p, gffn, wr_t, br_col)


ROW_UNROLL = 8


def _dispatch_kernel(nw, bs_ref, nblk_ref, dest_ref, h_ref, xs_out, zbuf, sem, zsem):
    td = h_ref.shape[0] // nw
    blk_rows = zbuf.shape[0]
    n_exp = bs_ref.shape[0]

    @pl.when(pl.program_id(0) == 0)
    def _():
        zbuf[...] = jnp.zeros(zbuf.shape, zbuf.dtype)
        total = xs_out.shape[0] // blk_rows
        first_tail = bs_ref[n_exp - 1] + nblk_ref[n_exp - 1]

        def z_copy(blk):
            row0 = pl.multiple_of(blk * blk_rows, blk_rows)
            return pltpu.make_async_copy(zbuf, xs_out.at[pl.ds(row0, blk_rows)], zsem)

        def each_partial_block(fn):
            def per_expert(e, c):
                @pl.when(nblk_ref[e] > 0)
                def _():
                    fn(z_copy(bs_ref[e] + nblk_ref[e] - 1))
                return c

            def per_tail(b, c):
                fn(z_copy(b))
                return c

            lax.fori_loop(0, n_exp, per_expert, 0)
            lax.fori_loop(first_tail, total, per_tail, 0)

        each_partial_block(lambda cp: cp.start())
        each_partial_block(lambda cp: cp.wait())

    def body(g, carry):
        base = g * ROW_UNROLL
        for u in range(ROW_UNROLL):
            i = base + u
            src = h_ref.at[pl.ds(pl.multiple_of(i * nw, nw), nw)]
            for k in range(TOP_K):
                d = dest_ref[i * TOP_K + k]
                pltpu.make_async_copy(src, xs_out.at[pl.ds(pl.multiple_of(d * nw, nw), nw)],
                                      sem).start(priority=k % 2)
        return carry

    lax.fori_loop(0, td // ROW_UNROLL, body, 0)
    for _ in range(TOP_K):
        pltpu.make_async_copy(h_ref, xs_out.at[pl.ds(0, td * nw)], sem).wait()


def _dispatch_call(blk_start, nblk, dest, hp, n_rows, nw, td, bm):
    t = hp.shape[0] // nw
    grid_spec = pltpu.PrefetchScalarGridSpec(
        num_scalar_prefetch=2,
        grid=(t // td,),
        in_specs=[
            pl.BlockSpec((TOP_K * td,), lambda i, bs, nb: (i,), memory_space=pltpu.SMEM),
            pl.BlockSpec((td * nw, LANE), lambda i, bs, nb: (i, 0)),
        ],
        out_specs=pl.BlockSpec(memory_space=pl.ANY),
        scratch_shapes=[pltpu.VMEM((bm * nw, LANE), hp.dtype), pltpu.SemaphoreType.DMA,
                        pltpu.SemaphoreType.DMA],
    )
    return pl.pallas_call(
        functools.partial(_dispatch_kernel, nw),
        grid_spec=grid_spec,
        out_shape=jax.ShapeDtypeStruct((n_rows * nw, LANE), hp.dtype),
        compiler_params=pltpu.CompilerParams(dimension_semantics=("arbitrary",),
                                             vmem_limit_bytes=VMEM_LIMIT),
        name="dispatch",
    )(blk_start, nblk, dest, hp)


CAST_ROWS = 64


def _expert_kernel(bs_ref, nblk_ref, wgu_ref, bgu_ref, wd_ref, bd_ref, xs_hbm, y_hbm,
                   wgu_bf, wd_bf, xbuf, ybuf, xsem, ysem):
    e = pl.program_id(0)
    n_exp = pl.num_programs(0)
    d = wgu_ref.shape[1]
    f = wd_ref.shape[1]
    ns = d // LANE
    nw = ns // 2
    bm = xbuf.shape[1] // nw
    n = nblk_ref[e]
    blk0 = bs_ref[e]

    def x_copy(j, slot):
        row0 = pl.multiple_of((blk0 + j) * (bm * nw), bm * nw)
        return pltpu.make_async_copy(xs_hbm.at[pl.ds(row0, bm * nw)], xbuf.at[slot], xsem.at[slot])

    def y_copy(blk, slot):
        row0 = pl.multiple_of(blk * (bm * ns), bm * ns)
        return pltpu.make_async_copy(ybuf.at[slot], y_hbm.at[pl.ds(row0, bm * ns)], ysem.at[slot])

    @pl.when(n > 0)
    def _():
        x_copy(0, 0).start()

        def cast_gu(r, c):
            rows = pl.ds(pl.multiple_of(r * CAST_ROWS, CAST_ROWS), CAST_ROWS)
            wgu_bf[rows, :] = wgu_ref[0, rows, :].astype(BF16)
            return c

        def cast_d(r, c):
            rows = pl.ds(pl.multiple_of(r * CAST_ROWS, CAST_ROWS), CAST_ROWS)
            wd_bf[rows, :] = wd_ref[0, rows, :].astype(BF16)
            return c

        lax.fori_loop(0, d // CAST_ROWS, cast_gu, 0)
        lax.fori_loop(0, f // CAST_ROWS, cast_d, 0)

        def block(j, c):
            slot = j % 2
            x_copy(j, slot).wait()

            @pl.when(j + 1 < n)
            def _():
                x_copy(j + 1, 1 - slot).start()

            @pl.when(j >= 2)
            def _():
                y_copy(blk0 + j - 2, slot).wait()

            w = jnp.concatenate([xbuf[slot, pl.ds(c, bm, stride=nw), :] for c in range(nw)], axis=1)
            x_lo = lax.bitcast_convert_type(w << 16, F32).astype(BF16)
            x_hi = lax.bitcast_convert_type(w & jnp.uint32(0xFFFF0000), F32).astype(BF16)
            gu = (jnp.dot(x_lo, wgu_bf[:d // 2, :], preferred_element_type=F32)
                  + jnp.dot(x_hi, wgu_bf[d // 2:, :], preferred_element_type=F32) + bgu_ref[0])
            gate = jnp.minimum(gu[:, :f], SWIGLU_LIMIT)
            up = jnp.clip(gu[:, f:], -SWIGLU_LIMIT, SWIGLU_LIMIT)
            act = gate * jax.nn.sigmoid(SWIGLU_ALPHA * gate) * (up + 1.0)
            y = jnp.dot(act.astype(BF16), wd_bf[...], preferred_element_type=F32) + bd_ref[0]
            for c in range(ns):
                ybuf[slot, pl.ds(c, bm, stride=ns), :] = y[:, c * LANE:(c + 1) * LANE]
            y_copy(blk0 + j, slot).start()
            return c

        lax.fori_loop(0, n, block, 0)

        @pl.when(n >= 2)
        def _():
            y_copy(blk0 + n - 2, n % 2).wait()

        y_copy(blk0 + n - 1, (n - 1) % 2).wait()

    @pl.when(e == n_exp - 1)
    def _():
        first = blk0 + n
        total = y_hbm.shape[0] // (bm * ns)
        ybuf[0] = jnp.zeros(ybuf.shape[1:], ybuf.dtype)

        def fill(b, c):
            y_copy(b, 0).start()
            return c

        def drain(b, c):
            y_copy(b, 0).wait()
            return c

        lax.fori_loop(first, total, fill, 0)
        lax.fori_loop(first, total, drain, 0)


def _expert_call(blk_start, nblk, xs, wgu, bgu, wd, bd, bm):
    e, d, f2 = wgu.shape
    f = wd.shape[1]
    nw = d // 2 // LANE
    r = xs.shape[0] // nw

    def wsel(i, bs, nb):
        return (i, 0, 0)

    grid_spec = pltpu.PrefetchScalarGridSpec(
        num_scalar_prefetch=2,
        grid=(e,),
        in_specs=[
            pl.BlockSpec((1, d, f2), wsel),
            pl.BlockSpec((1, 1, f2), wsel),
            pl.BlockSpec((1, f, d), wsel),
            pl.BlockSpec((1, 1, d), wsel),
            pl.BlockSpec(memory_space=pl.ANY),
        ],
        out_specs=pl.BlockSpec(memory_space=pl.ANY),
        scratch_shapes=[pltpu.VMEM((d, f2), BF16), pltpu.VMEM((f, d), BF16),
                        pltpu.VMEM((2, bm * nw, LANE), xs.dtype),
                        pltpu.VMEM((2, bm * (d // LANE), LANE), F32),
                        pltpu.SemaphoreType.DMA((2,)), pltpu.SemaphoreType.DMA((2,))],
    )
    return pl.pallas_call(
        _expert_kernel,
        grid_spec=grid_spec,
        out_shape=jax.ShapeDtypeStruct((r * (d // LANE), LANE), F32),
        compiler_params=pltpu.CompilerParams(dimension_semantics=("arbitrary",),
                                             vmem_limit_bytes=VMEM_LIMIT),
        name="experts",
    )(blk_start, nblk, wgu, bgu, wd, bd, xs)


def _combine_kernel(dest_ref, x1_ref, gate_ref, gfin_ref, yb_hbm, o_ref, ybuf, sem):
    tf, d = x1_ref.shape
    ns = d // LANE

    def body(g, carry):
        base = g * ROW_UNROLL
        for u in range(ROW_UNROLL):
            i = base + u
            for k in range(TOP_K):
                r = dest_ref[i * TOP_K + k]
                pltpu.make_async_copy(yb_hbm.at[pl.ds(pl.multiple_of(r * ns, ns), ns)],
                                      ybuf.at[k, pl.ds(pl.multiple_of(i * ns, ns), ns)],
                                      sem).start(priority=k % 2)
        return carry

    lax.fori_loop(0, tf // ROW_UNROLL, body, 0)
    for k in range(TOP_K):
        pltpu.make_async_copy(yb_hbm.at[pl.ds(0, tf * ns)], ybuf.at[k], sem).wait()
    g = gate_ref[...]
    y = x1_ref[...]
    for k in range(TOP_K):
        yk = jnp.concatenate([ybuf[k, pl.ds(c, tf, stride=ns), :] for c in range(ns)], axis=1)
        y = y + g[:, k:k + 1] * yk
    o_ref[...] = _rms(y, gfin_ref[...], d)


def _combine_call(dest, x1, gates_t, gfin, yb, tf):
    t, d = x1.shape
    return pl.pallas_call(
        _combine_kernel,
        grid=(t // tf,),
        in_specs=[
            pl.BlockSpec((TOP_K * tf,), lambda i: (i,), memory_space=pltpu.SMEM),
            pl.BlockSpec((tf, d), lambda i: (i, 0)),
            pl.BlockSpec((tf, TOP_K), lambda i: (i, 0)),
            pl.BlockSpec(gfin.shape, lambda i: (0, 0)),
            pl.BlockSpec(memory_space=pl.ANY),
        ],
        out_specs=pl.BlockSpec((tf, d), lambda i: (i, 0)),
        out_shape=jax.ShapeDtypeStruct((t, d), F32),
        scratch_shapes=[pltpu.VMEM((TOP_K, tf * (d // LANE), LANE), F32), pltpu.SemaphoreType.DMA],
        compiler_params=pltpu.CompilerParams(dimension_semantics=("arbitrary",),
                                             vmem_limit_bytes=VMEM_LIMIT),
        name="combine",
    )(dest, x1, gates_t, gfin, yb)


def _pad_head_cols(w, widths_in, place):
    k = w.shape[0]
    per = sum(widths_in)
    blocks = []
    for hh in range(N_HEADS):
        blocks.append(place(w[:, hh * per:(hh + 1) * per]))
    return jnp.concatenate(blocks, axis=1)


def _layer(x2, pos3, batch, seq, g_attn_norm, w_in, w_pool, b_pool, pool_scale, g_q_a, w_q_b,
           g_kv_a, w_kv_b, g_out_pool, g_out_attn, w_out, g_ffn_norm, w_router, b_router,
           w_gate_up, b_gate_up, w_down, b_down, tiles):
    tm, tq, tmo, td, bm, tf = tiles
    t, d = x2.shape
    d_pool = pool_scale.shape[0]
    gdim = d_pool // N_POOL_GROUPS
    q_rank = g_q_a.shape[0]
    kv_rank = g_kv_a.shape[0]
    half = QK_ROPE_DIM // 2

    o1, o2, o3 = d_pool, d_pool + q_rank, d_pool + q_rank + kv_rank
    zk = lambda n: jnp.zeros((d, n), F32)
    w_in_p = jnp.concatenate([w_in[:, :o3], zk(QK_NOPE_DIM), w_in[:, o3:], zk(LANE - QK_DIM)],
                             axis=1).astype(BF16)
    zq = lambda n: jnp.zeros((q_rank, n), F32)
    wq_p = _pad_head_cols(w_q_b, (QK_DIM,), lambda c: jnp.concatenate([c, zq(LANE - QK_DIM)], axis=1)
                          ).astype(BF16)
    zkv = lambda n: jnp.zeros((kv_rank, n), F32)
    per = QK_NOPE_DIM + V_HEAD_DIM
    wk_p = _pad_head_cols(w_kv_b, (per,), lambda c: jnp.concatenate(
        [c[:, :QK_NOPE_DIM], zkv(LANE - QK_NOPE_DIM)], axis=1))
    wv_p = _pad_head_cols(w_kv_b, (per,), lambda c: jnp.concatenate(
        [c[:, QK_NOPE_DIM:], zkv(LANE - V_HEAD_DIM)], axis=1))
    wkv_p = jnp.concatenate([wk_p, wv_p], axis=1).astype(BF16)
    inv = ROPE_THETA ** (-jnp.arange(half, dtype=F32) / half)
    inv_col = inv.reshape(half, 1)

    row = lambda v: v.reshape(1, -1)
    grp = lambda v: v.reshape(N_POOL_GROUPS, 1, gdim)

    pool_n, q, k, v = _proj_call(
        x2, pos3, row(g_attn_norm), w_in_p, inv_col, w_pool.astype(BF16), grp(b_pool),
        grp(pool_scale), grp(g_out_pool), row(g_q_a), wq_p, row(g_kv_a), wkv_p, seq, tm)

    attn = _attn_call(q, k, v, batch, seq, tq)

    d_attn = N_HEADS * V_HEAD_DIM
    wo_pool = w_out[:d_pool].astype(BF16)
    wo_attn = w_out[d_pool:].reshape(N_HEADS, V_HEAD_DIM, d)
    wo_attn_p = jnp.concatenate([wo_attn, jnp.zeros((N_HEADS, HEAD_PAD - V_HEAD_DIM, d), F32)],
                                axis=1).reshape(N_HEADS * HEAD_PAD, d).astype(BF16)
    gattn_p = jnp.concatenate([g_out_attn.reshape(N_HEADS, V_HEAD_DIM),
                               jnp.zeros((N_HEADS, HEAD_PAD - V_HEAD_DIM), F32)], axis=1).reshape(1, -1)

    x1, hp, idx, gates, rank, counts = _outproj_call(
        pool_n, attn, x2, gattn_p, wo_pool, wo_attn_p, row(g_ffn_norm), w_router.T,
        b_router.reshape(-1, 1), tmo)

    n_assign = t * TOP_K
    cnt = counts[:, 0]
    padded = ((cnt + bm - 1) // bm) * bm
    pad_end = jnp.cumsum(padded)
    pad_start = pad_end - padded
    e_ids = jnp.arange(N_EXPERTS, dtype=jnp.int32)
    dest = rank + jnp.sum(jnp.where(idx[None] == e_ids[:, None, None], pad_start[:, None, None], 0),
                          axis=0)
    dest = dest.T.reshape(-1)
    n_rows = (n_assign // bm + N_EXPERTS) * bm
    blk_start = (pad_start // bm).astype(jnp.int32)
    nblk = (padded // bm).astype(jnp.int32)

    xs = _dispatch_call(blk_start, nblk, dest, hp, n_rows, d // 2 // LANE, td, bm)
    yb = _expert_call(blk_start, nblk, xs, w_gate_up, b_gate_up.reshape(N_EXPERTS, 1, -1),
                      w_down, b_down.reshape(N_EXPERTS, 1, -1), bm)
    return x1, dest, gates, yb


def _forward(x, positions, g_attn_norm, w_in, w_pool, b_pool, pool_scale, g_q_a, w_q_b, g_kv_a,
             w_kv_b, g_out_pool, g_out_attn, w_out, g_ffn_norm, w_router, b_router, w_gate_up,
             b_gate_up, w_down, b_down, g_final, tiles):
    batch, seq, d = x.shape
    t = batch * seq
    tm, tq, tmo, td, bm, tf = tiles
    x2 = x.reshape(t, d)
    pos3 = positions.reshape(t // tm, 1, tm)
    depth = g_attn_norm.shape[0]
    assert depth == 1
    l = 0
    x1, dest, gates, yb = _layer(
        x2, pos3, batch, seq, g_attn_norm[l], w_in[l], w_pool[l], b_pool[l], pool_scale[l],
        g_q_a[l], w_q_b[l], g_kv_a[l], w_kv_b[l], g_out_pool[l], g_out_attn[l], w_out[l],
        g_ffn_norm[l], w_router[l], b_router[l], w_gate_up[l], b_gate_up[l], w_down[l],
        b_down[l], tiles)
    out = _combine_call(dest, x1, gates.T, g_final.reshape(1, -1), yb, tf)
    return out.reshape(batch, seq, d)


def _tiles_for(seq):
    tm = min(512, seq)
    tq = min(512, seq // 2)
    tmo = min(512, seq)
    td = min(512, seq)
    bm = 256 if seq >= 2048 else 128
    tf = min(256, seq)
    return (tm, tq, tmo, td, bm, tf)


def kernel(x, positions, g_attn_norm, w_in, w_pool, b_pool, pool_scale, g_q_a, w_q_b, g_kv_a, w_kv_b, g_out_pool, g_out_attn, w_out, g_ffn_norm, w_router, b_router, w_gate_up, b_gate_up, w_down, b_down, g_final):
    tiles = _tiles_for(x.shape[1])
    return _forward(x, positions, g_attn_norm, w_in, w_pool, b_pool, pool_scale, g_q_a, w_q_b,
                    g_kv_a, w_kv_b, g_out_pool, g_out_attn, w_out, g_ffn_norm, w_router, b_router,
                    w_gate_up, b_gate_up, w_down, b_down, g_final, tiles)
```

```python
import functools

import jax
import jax.numpy as jnp
from jax import lax
from jax.experimental import pallas as pl
from jax.experimental.pallas import tpu as pltpu

POOL_WINDOWS = (2, 4, 8, 16)
N_POOL_GROUPS = 4
N_HEADS = 8
V_HEAD_DIM = 64
QK_NOPE_DIM = 64
QK_ROPE_DIM = 32
QK_DIM = QK_NOPE_DIM + QK_ROPE_DIM
ROPE_THETA = 10000.0
N_EXPERTS = 32
TOP_K = 4
SWIGLU_ALPHA = 1.702
SWIGLU_LIMIT = 7.0
RMS_EPS = 1e-6
LOG2_E = 1.4426950408889634

LANE = 128
HEAD_PAD = 128
POOL_HALO = 16

VMEM_LIMIT = 56 * 1024 * 1024

F32 = jnp.float32
BF16 = jnp.bfloat16


def _rms(x, g, n):
    ms = jnp.sum(x * x, axis=-1, keepdims=True) * (1.0 / n)
    return x * lax.rsqrt(ms + RMS_EPS) * g


def _proj_kernel(blocks_per_seq, x_ref, pos_ref, g_attn_ref, w_in_ref, inv_ref, wpool_ref,
                 bpool_ref, pscale_ref, gpool_ref, gq_ref, wq_ref, gkv_ref, wkv_ref,
                 pool_out, q_out, k_out, v_out, ext_scr):
    tm = x_ref.shape[0]
    d_model = x_ref.shape[1]
    d_pool = pool_out.shape[1]
    gdim = d_pool // N_POOL_GROUPS
    q_rank = gq_ref.shape[1]
    kv_rank = gkv_ref.shape[1]
    i = pl.program_id(0)
    blk_in_seq = i % blocks_per_seq

    x = x_ref[...]
    h = _rms(x, g_attn_ref[...], d_model).astype(BF16)
    proj = jnp.dot(h, w_in_ref[...], preferred_element_type=F32)
    u = proj[:, :d_pool]
    qlat = proj[:, d_pool:d_pool + q_rank]
    kvlat = proj[:, d_pool + q_rank:d_pool + q_rank + kv_rank]
    o_kr = d_pool + q_rank + kv_rank
    krb = proj[:, o_kr:o_kr + LANE]
    krs = proj[:, o_kr + LANE:]

    @pl.when(blk_in_seq == 0)
    def _():
        ext_scr[0:POOL_HALO, :] = jnp.zeros((POOL_HALO, d_pool), F32)

    @pl.when(blk_in_seq != 0)
    def _():
        ext_scr[0:POOL_HALO, :] = ext_scr[tm:tm + POOL_HALO, :]

    ext_scr[POOL_HALO:, :] = u
    t_in_seq = (blk_in_seq * tm + lax.broadcasted_iota(jnp.int32, (tm, gdim), 0) + 1).astype(F32)
    ys = []
    ssq = jnp.zeros((tm, 1), F32)
    for g, w in enumerate(POOL_WINDOWS):
        a = ext_scr[:, g * gdim:(g + 1) * gdim]
        shift = 1
        while shift < w:
            a = a + pltpu.roll(a, shift, axis=0)
            shift *= 2
        win = a[POOL_HALO:, :]
        cnt = jnp.minimum(t_in_seq, float(w))
        ug = u[:, g * gdim:(g + 1) * gdim]
        diff = (win / cnt - ug).astype(BF16)
        mixed = jnp.dot(diff, wpool_ref[g], preferred_element_type=F32) + bpool_ref[g]
        y = mixed * pscale_ref[g]
        ssq = ssq + jnp.sum(y * y, axis=-1, keepdims=True)
        ys.append(y)
    rinv = lax.rsqrt(ssq * (1.0 / d_pool) + RMS_EPS)
    for g in range(N_POOL_GROUPS):
        pool_out[:, g * gdim:(g + 1) * gdim] = (ys[g] * rinv * gpool_ref[g]).astype(pool_out.dtype)

    pos = pos_ref[0].astype(F32)
    ang_t = inv_ref[...] * pos
    cos_t = jnp.cos(ang_t)
    sin_t = jnp.sin(ang_t)
    half = QK_ROPE_DIM // 2
    ones = jnp.ones((QK_NOPE_DIM, tm), F32)
    c_t = jnp.concatenate([ones, cos_t, cos_t, jnp.ones((LANE - QK_DIM, tm), F32)], axis=0)
    s_t = jnp.concatenate([jnp.zeros((QK_NOPE_DIM, tm), F32), -sin_t, sin_t,
                           jnp.zeros((LANE - QK_DIM, tm), F32)], axis=0)
    c = c_t.T
    s = s_t.T

    def rope(blk, blk_swapped, c_, s_):
        return blk * c_ + blk_swapped * s_

    qn = _rms(qlat, gq_ref[...], q_rank).astype(BF16)
    q2 = jnp.dot(qn, wq_ref[...], preferred_element_type=F32)
    scale = QK_DIM ** -0.5 * LOG2_E
    cq, sq = c * scale, s * scale
    hp = N_HEADS * HEAD_PAD
    for hh in range(N_HEADS):
        sl = slice(hh * HEAD_PAD, (hh + 1) * HEAD_PAD)
        sw = slice(hp + hh * HEAD_PAD, hp + (hh + 1) * HEAD_PAD)
        q_out[:, sl] = rope(q2[:, sl], q2[:, sw], cq, sq).astype(q_out.dtype)

    kvn = _rms(kvlat, gkv_ref[...], kv_rank).astype(BF16)
    kv = jnp.dot(kvn, wkv_ref[...], preferred_element_type=F32)
    kr = rope(krb, krs, c, s)
    for hh in range(N_HEADS):
        sl = slice(hh * HEAD_PAD, (hh + 1) * HEAD_PAD)
        k_out[:, sl] = (kv[:, sl] + kr).astype(k_out.dtype)
    vlane = lax.broadcasted_iota(jnp.int32, (1, N_HEADS * HEAD_PAD), 1) % HEAD_PAD
    ones_col = (vlane == V_HEAD_DIM).astype(F32)
    v_out[0] = (kv[:, N_HEADS * HEAD_PAD:] + ones_col).T.astype(v_out.dtype)


def _proj_call(x2, pos3, g_attn, w_in_p, inv_col, wpool, bpool, pscale, gpool, gq, wq_p, gkv, wkv_p,
               seq, tm):
    t, d = x2.shape
    d_pool = pscale.shape[0] * pscale.shape[2]
    hp = N_HEADS * HEAD_PAD
    nblk = t // tm
    const2 = lambda i: (0, 0)
    const3 = lambda i: (0, 0, 0)
    in_specs = [
        pl.BlockSpec((tm, d), lambda i: (i, 0)),
        pl.BlockSpec((1, 1, tm), lambda i: (i, 0, 0)),
        pl.BlockSpec(g_attn.shape, const2),
        pl.BlockSpec(w_in_p.shape, const2),
        pl.BlockSpec(inv_col.shape, const2),
        pl.BlockSpec(wpool.shape, const3),
        pl.BlockSpec(bpool.shape, const3),
        pl.BlockSpec(pscale.shape, const3),
        pl.BlockSpec(gpool.shape, const3),
        pl.BlockSpec(gq.shape, const2),
        pl.BlockSpec(wq_p.shape, const2),
        pl.BlockSpec(gkv.shape, const2),
        pl.BlockSpec(wkv_p.shape, const2),
    ]
    out_shape = [
        jax.ShapeDtypeStruct((t, d_pool), BF16),
        jax.ShapeDtypeStruct((t, hp), BF16),
        jax.ShapeDtypeStruct((t, hp), BF16),
        jax.ShapeDtypeStruct((nblk, hp, tm), BF16),
    ]
    out_specs = [
        pl.BlockSpec((tm, d_pool), lambda i: (i, 0)),
        pl.BlockSpec((tm, hp), lambda i: (i, 0)),
        pl.BlockSpec((tm, hp), lambda i: (i, 0)),
        pl.BlockSpec((1, hp, tm), lambda i: (i, 0, 0)),
    ]
    return pl.pallas_call(
        functools.partial(_proj_kernel, seq // tm),
        grid=(nblk,),
        in_specs=in_specs,
        out_specs=out_specs,
        out_shape=out_shape,
        scratch_shapes=[pltpu.VMEM((tm + POOL_HALO, d_pool), F32)],
        compiler_params=pltpu.CompilerParams(dimension_semantics=("arbitrary",),
                                             vmem_limit_bytes=VMEM_LIMIT),
        name="proj",
    )(x2, pos3, g_attn, w_in_p, inv_col, wpool, bpool, pscale, gpool, gq, wq_p, gkv, wkv_p)


ROWSUM_ROW = V_HEAD_DIM


def _attn_kernel(q_ref, k_ref, vt_ref, o_ref, acc_ref):
    tq = q_ref.shape[0]
    th = vt_ref.shape[2]
    nsub = tq // th
    qi = pl.program_id(2)
    acc_ref[...] = jnp.zeros(acc_ref.shape, F32)

    def step(kc, m_prev, q_lo, masked):
        width = tq - q_lo
        start = pl.multiple_of(kc * th, th)
        k = k_ref[pl.ds(start, th), :]
        st = lax.dot_general(k, q_ref[q_lo:, :], (((1,), (1,)), ((), ())),
                             preferred_element_type=F32)
        if masked:
            key = lax.broadcasted_iota(jnp.int32, (th, width), 0)
            qry = lax.broadcasted_iota(jnp.int32, (th, width), 1)
            st = jnp.where(key <= qry, st, -jnp.inf)
        m_new = jnp.maximum(m_prev, jnp.max(st, axis=0, keepdims=True))
        alpha = jnp.exp2(m_prev - m_new)
        pt = jnp.exp2(st - m_new).astype(BF16)
        acc_ref[:, q_lo:] = alpha * acc_ref[:, q_lo:] + jnp.dot(vt_ref[kc], pt,
                                                                preferred_element_type=F32)
        return m_new

    def two_chunks(j, mp):
        return step(2 * j + 1, step(2 * j, mp, 0, False), 0, False)

    m = lax.fori_loop(0, (nsub // 2) * qi, two_chunks, jnp.full((1, tq), -jnp.inf, F32))
    for c in range(nsub):
        m = step(nsub * qi + c, m, c * th, True)[:, th:]

    acc = acc_ref[...]
    out = (acc / acc[ROWSUM_ROW:ROWSUM_ROW + 1, :]).T
    lane = lax.broadcasted_iota(jnp.int32, out.shape, 1)
    o_ref[...] = jnp.where(lane < V_HEAD_DIM, out, 0.0).astype(o_ref.dtype)


ATTN_Q_CHUNKS = 4


def _attn_call(q, k, vt, batch, seq, th):
    t, hp = q.shape
    tq = min(ATTN_Q_CHUNKS * th, seq)
    nq = seq // tq
    nchunk = seq // th
    return pl.pallas_call(
        _attn_kernel,
        grid=(batch, N_HEADS, nq),
        in_specs=[
            pl.BlockSpec((tq, HEAD_PAD), lambda b, h, qi: (b * nq + qi, h)),
            pl.BlockSpec((seq, HEAD_PAD), lambda b, h, qi: (b, h)),
            pl.BlockSpec((nchunk, HEAD_PAD, th), lambda b, h, qi: (b, h, 0)),
        ],
        out_specs=pl.BlockSpec((tq, HEAD_PAD), lambda b, h, qi: (b * nq + qi, h)),
        out_shape=jax.ShapeDtypeStruct((t, hp), F32),
        scratch_shapes=[pltpu.VMEM((HEAD_PAD, tq), F32)],
        compiler_params=pltpu.CompilerParams(
            dimension_semantics=("arbitrary", "arbitrary", "arbitrary"),
            vmem_limit_bytes=VMEM_LIMIT),
        name="attn",
    )(q, k, vt)


def _outproj_kernel(pool_ref, attn_ref, x_ref, gattn_ref, wop_ref, woa_ref, gffn_ref, wr_ref,
                    br_ref, x1_out, hp_out, idx_out, gate_out, rank_out, cnt_out, run_scr):
    tm, d = x_ref.shape
    d_attn = N_HEADS * V_HEAD_DIM
    i = pl.program_id(0)

    @pl.when(i == 0)
    def _():
        run_scr[...] = jnp.zeros(run_scr.shape, F32)

    attn = attn_ref[...]
    attn_n = _rms(attn, gattn_ref[...], d_attn).astype(BF16)
    mix = (jnp.dot(pool_ref[...], wop_ref[...], preferred_element_type=F32)
           + jnp.dot(attn_n, woa_ref[...], preferred_element_type=F32))
    x1 = x_ref[...] + mix
    x1_out[...] = x1
    h2 = _rms(x1, gffn_ref[...], d)

    hb = h2.astype(BF16).astype(F32)
    lo = lax.bitcast_convert_type(hb[:, :d // 2], jnp.uint32)
    hi = lax.bitcast_convert_type(hb[:, d // 2:], jnp.uint32)
    word = (lo >> 16) | (hi & jnp.uint32(0xFFFF0000))
    nw = d // 2 // LANE
    for c in range(nw):
        hp_out[pl.ds(c, tm, stride=nw), :] = word[:, c * LANE:(c + 1) * LANE]

    logits = lax.dot_general(wr_ref[...], h2, (((1,), (1,)), ((), ())),
                             preferred_element_type=F32,
                             precision=lax.Precision.HIGHEST) + br_ref[...]
    e_iota = lax.broadcasted_iota(jnp.int32, (N_EXPERTS, tm), 0)
    work = logits
    idxs, vals = [], []
    for _ in range(TOP_K):
        mx = jnp.max(work, axis=0, keepdims=True)
        sel = jnp.min(jnp.where(work == mx, e_iota, N_EXPERTS), axis=0, keepdims=True)
        idxs.append(sel)
        vals.append(mx)
        work = jnp.where(e_iota == sel, -jnp.inf, work)
    exps = [jnp.exp(v - vals[0]) for v in vals]
    denom = exps[0] + exps[1] + exps[2] + exps[3]
    gate_out[...] = jnp.concatenate([e / denom for e in exps], axis=0)
    idx_out[...] = jnp.concatenate(idxs, axis=0)

    onehot = jnp.zeros((N_EXPERTS, tm), F32)
    for sel in idxs:
        onehot = onehot + (e_iota == sel).astype(F32)
    r_io = lax.broadcasted_iota(jnp.int32, (tm, tm), 0)
    c_io = lax.broadcasted_iota(jnp.int32, (tm, tm), 1)
    upper = (r_io < c_io).astype(BF16)
    prefix = jnp.dot(onehot.astype(BF16), upper, preferred_element_type=F32) + run_scr[:, 0:1]
    ranks = [jnp.sum(jnp.where(e_iota == sel, prefix, 0.0), axis=0, keepdims=True) for sel in idxs]
    rank_out[...] = jnp.concatenate(ranks, axis=0).astype(jnp.int32)
    run_new = run_scr[...] + jnp.sum(onehot, axis=1, keepdims=True)
    run_scr[...] = run_new
    cnt_out[...] = run_new.astype(jnp.int32)


def _outproj_call(pool_n, attn, x2, gattn_p, wo_pool, wo_attn_p, gffn, wr_t, br_col, tm):
    t, d = x2.shape
    nblk = t // tm
    const2 = lambda i: (0, 0)
    row = lambda i: (i, 0)
    col = lambda i: (0, i)
    in_specs = [
        pl.BlockSpec((tm, pool_n.shape[1]), row),
        pl.BlockSpec((tm, attn.shape[1]), row),
        pl.BlockSpec((tm, d), row),
        pl.BlockSpec(gattn_p.shape, const2),
        pl.BlockSpec(wo_pool.shape, const2),
        pl.BlockSpec(wo_attn_p.shape, const2),
        pl.BlockSpec(gffn.shape, const2),
        pl.BlockSpec(wr_t.shape, const2),
        pl.BlockSpec(br_col.shape, const2),
    ]
    out_shape = [
        jax.ShapeDtypeStruct((t, d), F32),
        jax.ShapeDtypeStruct((t * (d // 2 // LANE), LANE), jnp.uint32),
        jax.ShapeDtypeStruct((TOP_K, t), jnp.int32),
        jax.ShapeDtypeStruct((TOP_K, t), F32),
        jax.ShapeDtypeStruct((TOP_K, t), jnp.int32),
        jax.ShapeDtypeStruct((N_EXPERTS, LANE), jnp.int32),
    ]
    out_specs = [
        pl.BlockSpec((tm, d), row),
        pl.BlockSpec((tm * (d // 2 // LANE), LANE), row),
        pl.BlockSpec((TOP_K, tm), col),
        pl.BlockSpec((TOP_K, tm), col),
        pl.BlockSpec((TOP_K, tm), col),
        pl.BlockSpec((N_EXPERTS, LANE), const2),
    ]
    return pl.pallas_call(
        _outproj_kernel,
        grid=(nblk,),
        in_specs=in_specs,
        out_specs=out_specs,
        out_shape=out_shape,
        scratch_shapes=[pltpu.VMEM((N_EXPERTS, LANE), F32)],
        compiler_params=pltpu.CompilerParams(dimension_semantics=("arbitrary",),
                                             vmem_limit_bytes=VMEM_LIMIT),
        name="outproj",
    )(pool_n, attn, x2, gattn_p, wo_pool, wo_attn_p, gffn, wr_t, br_col)


ROW_UNROLL = 8


def _dispatch_kernel(nw, bs_ref, nblk_ref, dest_ref, h_ref, xs_out, zbuf, sem, zsem):
    td = h_ref.shape[0] // nw
    blk_rows = zbuf.shape[0]
    n_exp = bs_ref.shape[0]

    @pl.when(pl.program_id(0) == 0)
    def _():
        zbuf[...] = jnp.zeros(zbuf.shape, zbuf.dtype)
        total = xs_out.shape[0] // blk_rows
        first_tail = bs_ref[n_exp - 1] + nblk_ref[n_exp - 1]

        def z_copy(blk):
            row0 = pl.multiple_of(blk * blk_rows, blk_rows)
            return pltpu.make_async_copy(zbuf, xs_out.at[pl.ds(row0, blk_rows)], zsem)

        def each_partial_block(fn):
            def per_expert(e, c):
                @pl.when(nblk_ref[e] > 0)
                def _():
                    fn(z_copy(bs_ref[e] + nblk_ref[e] - 1))
                return c

            def per_tail(b, c):
                fn(z_copy(b))
                return c

            lax.fori_loop(0, n_exp, per_expert, 0)
            lax.fori_loop(first_tail, total, per_tail, 0)

        each_partial_block(lambda cp: cp.start())
        each_partial_block(lambda cp: cp.wait())

    def body(g, carry):
        base = g * ROW_UNROLL
        for u in range(ROW_UNROLL):
            i = base + u
            src = h_ref.at[pl.ds(pl.multiple_of(i * nw, nw), nw)]
            for k in range(TOP_K):
                d = dest_ref[i * TOP_K + k]
                pltpu.make_async_copy(src, xs_out.at[pl.ds(pl.multiple_of(d * nw, nw), nw)],
                                      sem).start(priority=k % 2)
        return carry

    lax.fori_loop(0, td // ROW_UNROLL, body, 0)
    for _ in range(TOP_K):
        pltpu.make_async_copy(h_ref, xs_out.at[pl.ds(0, td * nw)], sem).wait()


def _dispatch_call(blk_start, nblk, dest, hp, n_rows, nw, td, bm):
    t = hp.shape[0] // nw
    grid_spec = pltpu.PrefetchScalarGridSpec(
        num_scalar_prefetch=2,
        grid=(t // td,),
        in_specs=[
            pl.BlockSpec((TOP_K * td,), lambda i, bs, nb: (i,), memory_space=pltpu.SMEM),
            pl.BlockSpec((td * nw, LANE), lambda i, bs, nb: (i, 0)),
        ],
        out_specs=pl.BlockSpec(memory_space=pl.ANY),
        scratch_shapes=[pltpu.VMEM((bm * nw, LANE), hp.dtype), pltpu.SemaphoreType.DMA,
                        pltpu.SemaphoreType.DMA],
    )
    return pl.pallas_call(
        functools.partial(_dispatch_kernel, nw),
        grid_spec=grid_spec,
        out_shape=jax.ShapeDtypeStruct((n_rows * nw, LANE), hp.dtype),
        compiler_params=pltpu.CompilerParams(dimension_semantics=("arbitrary",),
                                             vmem_limit_bytes=VMEM_LIMIT),
        name="dispatch",
    )(blk_start, nblk, dest, hp)


CAST_ROWS = 64


def _expert_kernel(bs_ref, nblk_ref, wgu_ref, bgu_ref, wd_ref, bd_ref, xs_hbm, y_hbm,
                   wgu_bf, wd_bf, xbuf, ybuf, xsem, ysem):
    e = pl.program_id(0)
    n_exp = pl.num_programs(0)
    d = wgu_ref.shape[1]
    f = wd_ref.shape[1]
    ns = d // LANE
    nw = ns // 2
    bm = xbuf.shape[1] // nw
    n = nblk_ref[e]
    blk0 = bs_ref[e]

    def x_copy(j, slot):
        row0 = pl.multiple_of((blk0 + j) * (bm * nw), bm * nw)
        return pltpu.make_async_copy(xs_hbm.at[pl.ds(row0, bm * nw)], xbuf.at[slot], xsem.at[slot])

    def y_copy(blk, slot):
        row0 = pl.multiple_of(blk * (bm * ns), bm * ns)
        return pltpu.make_async_copy(ybuf.at[slot], y_hbm.at[pl.ds(row0, bm * ns)], ysem.at[slot])

    @pl.when(n > 0)
    def _():
        x_copy(0, 0).start()

        def cast_gu(r, c):
            rows = pl.ds(pl.multiple_of(r * CAST_ROWS, CAST_ROWS), CAST_ROWS)
            wgu_bf[rows, :] = wgu_ref[0, rows, :].astype(BF16)
            return c

        def cast_d(r, c):
            rows = pl.ds(pl.multiple_of(r * CAST_ROWS, CAST_ROWS), CAST_ROWS)
            wd_bf[rows, :] = wd_ref[0, rows, :].astype(BF16)
            return c

        lax.fori_loop(0, d // CAST_ROWS, cast_gu, 0)
        lax.fori_loop(0, f // CAST_ROWS, cast_d, 0)

        def block(j, c):
            slot = j % 2
            x_copy(j, slot).wait()

            @pl.when(j + 1 < n)
            def _():
                x_copy(j + 1, 1 - slot).start()

            @pl.when(j >= 2)
            def _():
                y_copy(blk0 + j - 2, slot).wait()

            w = jnp.concatenate([xbuf[slot, pl.ds(c, bm, stride=nw), :] for c in range(nw)], axis=1)
            x_lo = lax.bitcast_convert_type(w << 16, F32).astype(BF16)
            x_hi = lax.bitcast_convert_type(w & jnp.uint32(0xFFFF0000), F32).astype(BF16)
            gu = (jnp.dot(x_lo, wgu_bf[:d // 2, :], preferred_element_type=F32)
                  + jnp.dot(x_hi, wgu_bf[d // 2:, :], preferred_element_type=F32) + bgu_ref[0])
            gate = jnp.minimum(gu[:, :f], SWIGLU_LIMIT)
            up = jnp.clip(gu[:, f:], -SWIGLU_LIMIT, SWIGLU_LIMIT)
            act = gate * jax.nn.sigmoid(SWIGLU_ALPHA * gate) * (up + 1.0)
            y = jnp.dot(act.astype(BF16), wd_bf[...], preferred_element_type=F32) + bd_ref[0]
            for c in range(ns):
                ybuf[slot, pl.ds(c, bm, stride=ns), :] = y[:, c * LANE:(c + 1) * LANE]
            y_copy(blk0 + j, slot).start()
            return c

        lax.fori_loop(0, n, block, 0)

        @pl.when(n >= 2)
        def _():
            y_copy(blk0 + n - 2, n % 2).wait()

        y_copy(blk0 + n - 1, (n - 1) % 2).wait()

    @pl.when(e == n_exp - 1)
    def _():
        first = blk0 + n
        total = y_hbm.shape[0] // (bm * ns)
        ybuf[0] = jnp.zeros(ybuf.shape[1:], ybuf.dtype)

        def fill(b, c):
            y_copy(b, 0).start()
            return c

        def drain(b, c):
            y_copy(b, 0).wait()
            return c

        lax.fori_loop(first, total, fill, 0)
        lax.fori_loop(first, total, drain, 0)


def _expert_call(blk_start, nblk, xs, wgu, bgu, wd, bd, bm):
    e, d, f2 = wgu.shape
    f = wd.shape[1]
    nw = d // 2 // LANE
    r = xs.shape[0] // nw

    def wsel(i, bs, nb):
        return (i, 0, 0)

    grid_spec = pltpu.PrefetchScalarGridSpec(
        num_scalar_prefetch=2,
        grid=(e,),
        in_specs=[
            pl.BlockSpec((1, d, f2), wsel),
            pl.BlockSpec((1, 1, f2), wsel),
            pl.BlockSpec((1, f, d), wsel),
            pl.BlockSpec((1, 1, d), wsel),
            pl.BlockSpec(memory_space=pl.ANY),
        ],
        out_specs=pl.BlockSpec(memory_space=pl.ANY),
        scratch_shapes=[pltpu.VMEM((d, f2), BF16), pltpu.VMEM((f, d), BF16),
                        pltpu.VMEM((2, bm * nw, LANE), xs.dtype),
                        pltpu.VMEM((2, bm * (d // LANE), LANE), F32),
                        pltpu.SemaphoreType.DMA((2,)), pltpu.SemaphoreType.DMA((2,))],
    )
    return pl.pallas_call(
        _expert_kernel,
        grid_spec=grid_spec,
        out_shape=jax.ShapeDtypeStruct((r * (d // LANE), LANE), F32),
        compiler_params=pltpu.CompilerParams(dimension_semantics=("arbitrary",),
                                             vmem_limit_bytes=VMEM_LIMIT),
        name="experts",
    )(blk_start, nblk, wgu, bgu, wd, bd, xs)


def _combine_kernel(dest_ref, dest_next_ref, x1_ref, gate_ref, gfin_ref, yb_hbm, o_ref, ybuf, sem):
    tf, d = x1_ref.shape
    ns = d // LANE
    j = pl.program_id(0)
    nsteps = pl.num_programs(0)
    slot = j % 2

    def issue(idx_ref, s):
        def body(g, carry):
            base = g * ROW_UNROLL
            for u in range(ROW_UNROLL):
                i = base + u
                for k in range(TOP_K):
                    r = idx_ref[i * TOP_K + k]
                    pltpu.make_async_copy(yb_hbm.at[pl.ds(pl.multiple_of(r * ns, ns), ns)],
                                          ybuf.at[s, k, pl.ds(pl.multiple_of(i * ns, ns), ns)],
                                          sem.at[s]).start(priority=k % 2)
            return carry

        lax.fori_loop(0, tf // ROW_UNROLL, body, 0)

    @pl.when(j == 0)
    def _():
        issue(dest_ref, 0)

    @pl.when(j + 1 < nsteps)
    def _():
        issue(dest_next_ref, 1 - slot)

    for k in range(TOP_K):
        pltpu.make_async_copy(yb_hbm.at[pl.ds(0, tf * ns)], ybuf.at[slot, k], sem.at[slot]).wait()
    g = gate_ref[...]
    y = x1_ref[...]
    for k in range(TOP_K):
        yk = jnp.concatenate([ybuf[slot, k, pl.ds(c, tf, stride=ns), :] for c in range(ns)], axis=1)
        y = y + g[:, k:k + 1] * yk
    o_ref[...] = _rms(y, gfin_ref[...], d)


def _combine_call(dest, x1, gates_t, gfin, yb, tf):
    t, d = x1.shape
    nsteps = t // tf
    return pl.pallas_call(
        _combine_kernel,
        grid=(nsteps,),
        in_specs=[
            pl.BlockSpec((TOP_K * tf,), lambda i: (i,), memory_space=pltpu.SMEM),
            pl.BlockSpec((TOP_K * tf,), lambda i: (jnp.minimum(i + 1, nsteps - 1),),
                         memory_space=pltpu.SMEM),
            pl.BlockSpec((tf, d), lambda i: (i, 0)),
            pl.BlockSpec((tf, TOP_K), lambda i: (i, 0)),
            pl.BlockSpec(gfin.shape, lambda i: (0, 0)),
            pl.BlockSpec(memory_space=pl.ANY),
        ],
        out_specs=pl.BlockSpec((tf, d), lambda i: (i, 0)),
        out_shape=jax.ShapeDtypeStruct((t, d), F32),
        scratch_shapes=[pltpu.VMEM((2, TOP_K, tf * (d // LANE), LANE), F32),
                        pltpu.SemaphoreType.DMA((2,))],
        compiler_params=pltpu.CompilerParams(dimension_semantics=("arbitrary",),
                                             vmem_limit_bytes=VMEM_LIMIT),
        name="combine",
    )(dest, dest, x1, gates_t, gfin, yb)


def _pad_head_cols(w, widths_in, place):
    k = w.shape[0]
    per = sum(widths_in)
    blocks = []
    for hh in range(N_HEADS):
        blocks.append(place(w[:, hh * per:(hh + 1) * per]))
    return jnp.concatenate(blocks, axis=1)


def _layer(x2, pos3, batch, seq, g_attn_norm, w_in, w_pool, b_pool, pool_scale, g_q_a, w_q_b,
           g_kv_a, w_kv_b, g_out_pool, g_out_attn, w_out, g_ffn_norm, w_router, b_router,
           w_gate_up, b_gate_up, w_down, b_down, tiles):
    tm, tq, tmo, td, bm, tf = tiles
    t, d = x2.shape
    d_pool = pool_scale.shape[0]
    gdim = d_pool // N_POOL_GROUPS
    q_rank = g_q_a.shape[0]
    kv_rank = g_kv_a.shape[0]
    half = QK_ROPE_DIM // 2

    o1, o2, o3 = d_pool, d_pool + q_rank, d_pool + q_rank + kv_rank
    zk = lambda n: jnp.zeros((d, n), F32)
    w_kr1, w_kr2 = w_in[:, o3:o3 + half], w_in[:, o3 + half:]
    w_in_p = jnp.concatenate([w_in[:, :o3], zk(QK_NOPE_DIM), w_kr1, w_kr2, zk(LANE - QK_DIM),
                              zk(QK_NOPE_DIM), w_kr2, w_kr1, zk(LANE - QK_DIM)], axis=1).astype(BF16)
    zq = lambda n: jnp.zeros((q_rank, n), F32)
    wq_plain = _pad_head_cols(w_q_b, (QK_DIM,),
                              lambda c: jnp.concatenate([c, zq(LANE - QK_DIM)], axis=1))
    wq_swap = _pad_head_cols(w_q_b, (QK_DIM,), lambda c: jnp.concatenate(
        [zq(QK_NOPE_DIM), c[:, QK_NOPE_DIM + half:], c[:, QK_NOPE_DIM:QK_NOPE_DIM + half],
         zq(LANE - QK_DIM)], axis=1))
    wq_p = jnp.concatenate([wq_plain, wq_swap], axis=1).astype(BF16)
    zkv = lambda n: jnp.zeros((kv_rank, n), F32)
    per = QK_NOPE_DIM + V_HEAD_DIM
    wk_p = _pad_head_cols(w_kv_b, (per,), lambda c: jnp.concatenate(
        [c[:, :QK_NOPE_DIM], zkv(LANE - QK_NOPE_DIM)], axis=1))
    wv_p = _pad_head_cols(w_kv_b, (per,), lambda c: jnp.concatenate(
        [c[:, QK_NOPE_DIM:], zkv(LANE - V_HEAD_DIM)], axis=1))
    wkv_p = jnp.concatenate([wk_p, wv_p], axis=1).astype(BF16)
    inv = ROPE_THETA ** (-jnp.arange(half, dtype=F32) / half)
    inv_col = inv.reshape(half, 1)

    row = lambda v: v.reshape(1, -1)
    grp = lambda v: v.reshape(N_POOL_GROUPS, 1, gdim)

    pool_n, q, k, v = _proj_call(
        x2, pos3, row(g_attn_norm), w_in_p, inv_col, w_pool.astype(BF16), grp(b_pool),
        grp(pool_scale), grp(g_out_pool), row(g_q_a), wq_p, row(g_kv_a), wkv_p, seq, tm)

    attn = _attn_call(q, k, v, batch, seq, tq)

    d_attn = N_HEADS * V_HEAD_DIM
    wo_pool = w_out[:d_pool].astype(BF16)
    wo_attn = w_out[d_pool:].reshape(N_HEADS, V_HEAD_DIM, d)
    wo_attn_p = jnp.concatenate([wo_attn, jnp.zeros((N_HEADS, HEAD_PAD - V_HEAD_DIM, d), F32)],
                                axis=1).reshape(N_HEADS * HEAD_PAD, d).astype(BF16)
    gattn_p = jnp.concatenate([g_out_attn.reshape(N_HEADS, V_HEAD_DIM),
                               jnp.zeros((N_HEADS, HEAD_PAD - V_HEAD_DIM), F32)], axis=1).reshape(1, -1)

    x1, hp, idx, gates, rank, counts = _outproj_call(
        pool_n, attn, x2, gattn_p, wo_pool, wo_attn_p, row(g_ffn_norm), w_router.T,
        b_router.reshape(-1, 1), tmo)

    n_assign = t * TOP_K
    cnt = counts[:, 0]
    padded = ((cnt + bm - 1) // bm) * bm
    pad_end = jnp.cumsum(padded)
    pad_start = pad_end - padded
    e_ids = jnp.arange(N_EXPERTS, dtype=jnp.int32)
    dest = rank + jnp.sum(jnp.where(idx[None] == e_ids[:, None, None], pad_start[:, None, None], 0),
                          axis=0)
    dest = dest.T.reshape(-1)
    n_rows = (n_assign // bm + N_EXPERTS) * bm
    blk_start = (pad_start // bm).astype(jnp.int32)
    nblk = (padded // bm).astype(jnp.int32)

    xs = _dispatch_call(blk_start, nblk, dest, hp, n_rows, d // 2 // LANE, td, bm)
    yb = _expert_call(blk_start, nblk, xs, w_gate_up, b_gate_up.reshape(N_EXPERTS, 1, -1),
                      w_down, b_down.reshape(N_EXPERTS, 1, -1), bm)
    return x1, dest, gates, yb


def _forward(x, positions, g_attn_norm, w_in, w_pool, b_pool, pool_scale, g_q_a, w_q_b, g_kv_a,
             w_kv_b, g_out_pool, g_out_attn, w_out, g_ffn_norm, w_router, b_router, w_gate_up,
             b_gate_up, w_down, b_down, g_final, tiles):
    batch, seq, d = x.shape
    t = batch * seq
    tm, tq, tmo, td, bm, tf = tiles
    x2 = x.reshape(t, d)
    pos3 = positions.reshape(t // tm, 1, tm)
    depth = g_attn_norm.shape[0]
    assert depth == 1
    l = 0
    x1, dest, gates, yb = _layer(
        x2, pos3, batch, seq, g_attn_norm[l], w_in[l], w_pool[l], b_pool[l], pool_scale[l],
        g_q_a[l], w_q_b[l], g_kv_a[l], w_kv_b[l], g_out_pool[l], g_out_attn[l], w_out[l],
        g_ffn_norm[l], w_router[l], b_router[l], w_gate_up[l], b_gate_up[l], w_down[l],
        b_down[l], tiles)
    out = _combine_call(dest, x1, gates.T, g_final.reshape(1, -1), yb, tf)
    return out.reshape(batch, seq, d)


def _tiles_for(seq):
    tm = min(512, seq)
    tq = min(512, seq // 2)
    tmo = min(512, seq)
    td = min(512, seq)
    bm = 256 if seq >= 2048 else 128
    tf = min(256, seq)
    return (tm, tq, tmo, td, bm, tf)


def kernel(x, positions, g_attn_norm, w_in, w_pool, b_pool, pool_scale, g_q_a, w_q_b, g_kv_a, w_kv_b, g_out_pool, g_out_attn, w_out, g_ffn_norm, w_router, b_router, w_gate_up, b_gate_up, w_down, b_down, g_final):
    tiles = _tiles_for(x.shape[1])
    return _forward(x, positions, g_attn_norm, w_in, w_pool, b_pool, pool_scale, g_q_a, w_q_b,
                    g_kv_a, w_kv_b, g_out_pool, g_out_attn, w_out, g_ffn_norm, w_router, b_router,
                    w_gate_up, b_gate_up, w_down, b_down, g_final, tiles)
```

```python
import functools

import jax
import jax.numpy as jnp
from jax import lax
from jax.experimental import pallas as pl
from jax.experimental.pallas import tpu as pltpu

POOL_WINDOWS = (2, 4, 8, 16)
N_POOL_GROUPS = 4
N_HEADS = 8
V_HEAD_DIM = 64
QK_NOPE_DIM = 64
QK_ROPE_DIM = 32
QK_DIM = QK_NOPE_DIM + QK_ROPE_DIM
ROPE_THETA = 10000.0
N_EXPERTS = 32
TOP_K = 4
SWIGLU_ALPHA = 1.702
SWIGLU_LIMIT = 7.0
RMS_EPS = 1e-6
LOG2_E = 1.4426950408889634

LANE = 128
HEAD_PAD = 128
POOL_HALO = 16

VMEM_LIMIT = 56 * 1024 * 1024

F32 = jnp.float32
BF16 = jnp.bfloat16


def _rms(x, g, n):
    ms = jnp.sum(x * x, axis=-1, keepdims=True) * (1.0 / n)
    return x * lax.rsqrt(ms + RMS_EPS) * g


def _proj_kernel(blocks_per_seq, x_ref, pos_ref, g_attn_ref, w_in_ref, inv_ref, wpool_ref,
                 bpool_ref, pscale_ref, gpool_ref, gq_ref, wq_ref, gkv_ref, wkv_ref,
                 pool_out, q_out, k_out, v_out, ext_scr):
    tm = x_ref.shape[0]
    d_model = x_ref.shape[1]
    d_pool = pool_out.shape[1]
    gdim = d_pool // N_POOL_GROUPS
    q_rank = gq_ref.shape[1]
    kv_rank = gkv_ref.shape[1]
    i = pl.program_id(0)
    blk_in_seq = i % blocks_per_seq

    x = x_ref[...]
    h = _rms(x, g_attn_ref[...], d_model).astype(BF16)
    proj = jnp.dot(h, w_in_ref[...], preferred_element_type=F32)
    u = proj[:, :d_pool]
    qlat = proj[:, d_pool:d_pool + q_rank]
    kvlat = proj[:, d_pool + q_rank:d_pool + q_rank + kv_rank]
    o_kr = d_pool + q_rank + kv_rank
    krb = proj[:, o_kr:o_kr + LANE]
    krs = proj[:, o_kr + LANE:]

    @pl.when(blk_in_seq == 0)
    def _():
        ext_scr[0:POOL_HALO, :] = jnp.zeros((POOL_HALO, d_pool), F32)

    @pl.when(blk_in_seq != 0)
    def _():
        ext_scr[0:POOL_HALO, :] = ext_scr[tm:tm + POOL_HALO, :]

    ext_scr[POOL_HALO:, :] = u
    t_in_seq = (blk_in_seq * tm + lax.broadcasted_iota(jnp.int32, (tm, gdim), 0) + 1).astype(F32)
    ys = []
    ssq = jnp.zeros((tm, 1), F32)
    for g, w in enumerate(POOL_WINDOWS):
        a = ext_scr[:, g * gdim:(g + 1) * gdim]
        shift = 1
        while shift < w:
            a = a + pltpu.roll(a, shift, axis=0)
            shift *= 2
        win = a[POOL_HALO:, :]
        cnt = jnp.minimum(t_in_seq, float(w))
        ug = u[:, g * gdim:(g + 1) * gdim]
        diff = (win / cnt - ug).astype(BF16)
        mixed = jnp.dot(diff, wpool_ref[g], preferred_element_type=F32) + bpool_ref[g]
        y = mixed * pscale_ref[g]
        ssq = ssq + jnp.sum(y * y, axis=-1, keepdims=True)
        ys.append(y)
    rinv = lax.rsqrt(ssq * (1.0 / d_pool) + RMS_EPS)
    for g in range(N_POOL_GROUPS):
        pool_out[:, g * gdim:(g + 1) * gdim] = (ys[g] * rinv * gpool_ref[g]).astype(pool_out.dtype)

    pos = pos_ref[0].astype(F32)
    ang_t = inv_ref[...] * pos
    cos_t = jnp.cos(ang_t)
    sin_t = jnp.sin(ang_t)
    half = QK_ROPE_DIM // 2
    ones = jnp.ones((QK_NOPE_DIM, tm), F32)
    c_t = jnp.concatenate([ones, cos_t, cos_t, jnp.ones((LANE - QK_DIM, tm), F32)], axis=0)
    s_t = jnp.concatenate([jnp.zeros((QK_NOPE_DIM, tm), F32), -sin_t, sin_t,
                           jnp.zeros((LANE - QK_DIM, tm), F32)], axis=0)
    c = c_t.T
    s = s_t.T

    def rope(blk, blk_swapped, c_, s_):
        return blk * c_ + blk_swapped * s_

    qn = _rms(qlat, gq_ref[...], q_rank).astype(BF16)
    q2 = jnp.dot(qn, wq_ref[...], preferred_element_type=F32)
    scale = QK_DIM ** -0.5 * LOG2_E
    cq, sq = c * scale, s * scale
    hp = N_HEADS * HEAD_PAD
    for hh in range(N_HEADS):
        sl = slice(hh * HEAD_PAD, (hh + 1) * HEAD_PAD)
        sw = slice(hp + hh * HEAD_PAD, hp + (hh + 1) * HEAD_PAD)
        q_out[:, sl] = rope(q2[:, sl], q2[:, sw], cq, sq).astype(q_out.dtype)

    kvn = _rms(kvlat, gkv_ref[...], kv_rank).astype(BF16)
    kv = jnp.dot(kvn, wkv_ref[...], preferred_element_type=F32)
    kr = rope(krb, krs, c, s)
    for hh in range(N_HEADS):
        sl = slice(hh * HEAD_PAD, (hh + 1) * HEAD_PAD)
        k_out[:, sl] = (kv[:, sl] + kr).astype(k_out.dtype)
    vlane = lax.broadcasted_iota(jnp.int32, (1, N_HEADS * HEAD_PAD), 1) % HEAD_PAD
    ones_col = (vlane == V_HEAD_DIM).astype(F32)
    v_out[0] = (kv[:, N_HEADS * HEAD_PAD:] + ones_col).T.astype(v_out.dtype)


def _proj_call(x2, pos3, g_attn, w_in_p, inv_col, wpool, bpool, pscale, gpool, gq, wq_p, gkv, wkv_p,
               seq, tm):
    t, d = x2.shape
    d_pool = pscale.shape[0] * pscale.shape[2]
    hp = N_HEADS * HEAD_PAD
    nblk = t // tm
    const2 = lambda i: (0, 0)
    const3 = lambda i: (0, 0, 0)
    in_specs = [
        pl.BlockSpec((tm, d), lambda i: (i, 0)),
        pl.BlockSpec((1, 1, tm), lambda i: (i, 0, 0)),
        pl.BlockSpec(g_attn.shape, const2),
        pl.BlockSpec(w_in_p.shape, const2),
        pl.BlockSpec(inv_col.shape, const2),
        pl.BlockSpec(wpool.shape, const3),
        pl.BlockSpec(bpool.shape, const3),
        pl.BlockSpec(pscale.shape, const3),
        pl.BlockSpec(gpool.shape, const3),
        pl.BlockSpec(gq.shape, const2),
        pl.BlockSpec(wq_p.shape, const2),
        pl.BlockSpec(gkv.shape, const2),
        pl.BlockSpec(wkv_p.shape, const2),
    ]
    out_shape = [
        jax.ShapeDtypeStruct((t, d_pool), BF16),
        jax.ShapeDtypeStruct((t, hp), BF16),
        jax.ShapeDtypeStruct((t, hp), BF16),
        jax.ShapeDtypeStruct((nblk, hp, tm), BF16),
    ]
    out_specs = [
        pl.BlockSpec((tm, d_pool), lambda i: (i, 0)),
        pl.BlockSpec((tm, hp), lambda i: (i, 0)),
        pl.BlockSpec((tm, hp), lambda i: (i, 0)),
        pl.BlockSpec((1, hp, tm), lambda i: (i, 0, 0)),
    ]
    return pl.pallas_call(
        functools.partial(_proj_kernel, seq // tm),
        grid=(nblk,),
        in_specs=in_specs,
        out_specs=out_specs,
        out_shape=out_shape,
        scratch_shapes=[pltpu.VMEM((tm + POOL_HALO, d_pool), F32)],
        compiler_params=pltpu.CompilerParams(dimension_semantics=("arbitrary",),
                                             vmem_limit_bytes=VMEM_LIMIT),
        name="proj",
    )(x2, pos3, g_attn, w_in_p, inv_col, wpool, bpool, pscale, gpool, gq, wq_p, gkv, wkv_p)


ROWSUM_ROW = V_HEAD_DIM


def _attn_kernel(q_ref, k_ref, vt_ref, o_ref, acc_ref):
    tq = q_ref.shape[0]
    th = vt_ref.shape[2]
    nsub = tq // th
    qi = pl.program_id(2)
    acc_ref[...] = jnp.zeros(acc_ref.shape, F32)

    def step(kc, m_prev, q_lo, masked):
        width = tq - q_lo
        start = pl.multiple_of(kc * th, th)
        k = k_ref[pl.ds(start, th), :]
        st = lax.dot_general(k, q_ref[q_lo:, :], (((1,), (1,)), ((), ())),
                             preferred_element_type=F32)
        if masked:
            key = lax.broadcasted_iota(jnp.int32, (th, width), 0)
            qry = lax.broadcasted_iota(jnp.int32, (th, width), 1)
            st = jnp.where(key <= qry, st, -jnp.inf)
        m_new = jnp.maximum(m_prev, jnp.max(st, axis=0, keepdims=True))
        alpha = jnp.exp2(m_prev - m_new)
        pt = jnp.exp2(st - m_new).astype(BF16)
        acc_ref[:, q_lo:] = alpha * acc_ref[:, q_lo:] + jnp.dot(vt_ref[kc], pt,
                                                                preferred_element_type=F32)
        return m_new

    def two_chunks(j, mp):
        return step(2 * j + 1, step(2 * j, mp, 0, False), 0, False)

    m = lax.fori_loop(0, (nsub // 2) * qi, two_chunks, jnp.full((1, tq), -jnp.inf, F32))
    for c in range(nsub):
        m = step(nsub * qi + c, m, c * th, True)[:, th:]

    acc = acc_ref[...]
    out = (acc / acc[ROWSUM_ROW:ROWSUM_ROW + 1, :]).T
    lane = lax.broadcasted_iota(jnp.int32, out.shape, 1)
    o_ref[...] = jnp.where(lane < V_HEAD_DIM, out, 0.0).astype(o_ref.dtype)


ATTN_Q_CHUNKS = 4


def _attn_call(q, k, vt, batch, seq, th):
    t, hp = q.shape
    tq = min(ATTN_Q_CHUNKS * th, seq)
    nq = seq // tq
    nchunk = seq // th
    return pl.pallas_call(
        _attn_kernel,
        grid=(batch, N_HEADS, nq),
        in_specs=[
            pl.BlockSpec((tq, HEAD_PAD), lambda b, h, qi: (b * nq + qi, h)),
            pl.BlockSpec((seq, HEAD_PAD), lambda b, h, qi: (b, h)),
            pl.BlockSpec((nchunk, HEAD_PAD, th), lambda b, h, qi: (b, h, 0)),
        ],
        out_specs=pl.BlockSpec((tq, HEAD_PAD), lambda b, h, qi: (b * nq + qi, h)),
        out_shape=jax.ShapeDtypeStruct((t, hp), F32),
        scratch_shapes=[pltpu.VMEM((HEAD_PAD, tq), F32)],
        compiler_params=pltpu.CompilerParams(
            dimension_semantics=("arbitrary", "arbitrary", "arbitrary"),
            vmem_limit_bytes=VMEM_LIMIT),
        name="attn",
    )(q, k, vt)


def _outproj_kernel(pool_ref, attn_ref, x_ref, gattn_ref, wop_ref, woa_ref, gffn_ref, wr_ref,
                    br_ref, x1_out, hp_out, idx_out, gate_out, rank_out, cnt_out, run_scr):
    tm, d = x_ref.shape
    d_attn = N_HEADS * V_HEAD_DIM
    i = pl.program_id(0)

    @pl.when(i == 0)
    def _():
        run_scr[...] = jnp.zeros(run_scr.shape, F32)

    attn = attn_ref[...]
    attn_n = _rms(attn, gattn_ref[...], d_attn).astype(BF16)
    mix = (jnp.dot(pool_ref[...], wop_ref[...], preferred_element_type=F32)
           + jnp.dot(attn_n, woa_ref[...], preferred_element_type=F32))
    x1 = x_ref[...] + mix
    x1_out[...] = x1
    h2 = _rms(x1, gffn_ref[...], d)

    hb = h2.astype(BF16).astype(F32)
    lo = lax.bitcast_convert_type(hb[:, :d // 2], jnp.uint32)
    hi = lax.bitcast_convert_type(hb[:, d // 2:], jnp.uint32)
    word = (lo >> 16) | (hi & jnp.uint32(0xFFFF0000))
    nw = d // 2 // LANE
    for c in range(nw):
        hp_out[pl.ds(c, tm, stride=nw), :] = word[:, c * LANE:(c + 1) * LANE]

    logits = lax.dot_general(wr_ref[...], h2, (((1,), (1,)), ((), ())),
                             preferred_element_type=F32,
                             precision=lax.Precision.HIGHEST) + br_ref[...]
    e_iota = lax.broadcasted_iota(jnp.int32, (N_EXPERTS, tm), 0)
    work = logits
    idxs, vals = [], []
    for _ in range(TOP_K):
        mx = jnp.max(work, axis=0, keepdims=True)
        sel = jnp.min(jnp.where(work == mx, e_iota, N_EXPERTS), axis=0, keepdims=True)
        idxs.append(sel)
        vals.append(mx)
        work = jnp.where(e_iota == sel, -jnp.inf, work)
    exps = [jnp.exp(v - vals[0]) for v in vals]
    denom = exps[0] + exps[1] + exps[2] + exps[3]
    gate_out[...] = jnp.concatenate([e / denom for e in exps], axis=0)
    idx_out[...] = jnp.concatenate(idxs, axis=0)

    onehot = jnp.zeros((N_EXPERTS, tm), F32)
    for sel in idxs:
        onehot = onehot + (e_iota == sel).astype(F32)
    r_io = lax.broadcasted_iota(jnp.int32, (tm, tm), 0)
    c_io = lax.broadcasted_iota(jnp.int32, (tm, tm), 1)
    upper = (r_io < c_io).astype(BF16)
    prefix = jnp.dot(onehot.astype(BF16), upper, preferred_element_type=F32) + run_scr[:, 0:1]
    ranks = [jnp.sum(jnp.where(e_iota == sel, prefix, 0.0), axis=0, keepdims=True) for sel in idxs]
    rank_out[...] = jnp.concatenate(ranks, axis=0).astype(jnp.int32)
    run_new = run_scr[...] + jnp.sum(onehot, axis=1, keepdims=True)
    run_scr[...] = run_new
    cnt_out[...] = run_new.astype(jnp.int32)


def _outproj_call(pool_n, attn, x2, gattn_p, wo_pool, wo_attn_p, gffn, wr_t, br_col, tm):
    t, d = x2.shape
    nblk = t // tm
    const2 = lambda i: (0, 0)
    row = lambda i: (i, 0)
    col = lambda i: (0, i)
    in_specs = [
        pl.BlockSpec((tm, pool_n.shape[1]), row),
        pl.BlockSpec((tm, attn.shape[1]), row),
        pl.BlockSpec((tm, d), row),
        pl.BlockSpec(gattn_p.shape, const2),
        pl.BlockSpec(wo_pool.shape, const2),
        pl.BlockSpec(wo_attn_p.shape, const2),
        pl.BlockSpec(gffn.shape, const2),
        pl.BlockSpec(wr_t.shape, const2),
        pl.BlockSpec(br_col.shape, const2),
    ]
    out_shape = [
        jax.ShapeDtypeStruct((t, d), F32),
        jax.ShapeDtypeStruct((t * (d // 2 // LANE), LANE), jnp.uint32),
        jax.ShapeDtypeStruct((TOP_K, t), jnp.int32),
        jax.ShapeDtypeStruct((TOP_K, t), F32),
        jax.ShapeDtypeStruct((TOP_K, t), jnp.int32),
        jax.ShapeDtypeStruct((N_EXPERTS, LANE), jnp.int32),
    ]
    out_specs = [
        pl.BlockSpec((tm, d), row),
        pl.BlockSpec((tm * (d // 2 // LANE), LANE), row),
        pl.BlockSpec((TOP_K, tm), col),
        pl.BlockSpec((TOP_K, tm), col),
        pl.BlockSpec((TOP_K, tm), col),
        pl.BlockSpec((N_EXPERTS, LANE), const2),
    ]
    return pl.pallas_call(
        _outproj_kernel,
        grid=(nblk,),
        in_specs=in_specs,
        out_specs=out_specs,
        out_shape=out_shape,
        scratch_shapes=[pltpu.VMEM((N_EXPERTS, LANE), F32)],
        compiler_params=pltpu.CompilerParams(dimension_semantics=("arbitrary",),
                                             vmem_limit_bytes=VMEM_LIMIT),
        name="outproj",
    )(pool_n, attn, x2, gattn_p, wo_pool, wo_attn_p, gffn, wr_t, br_col)


ROW_UNROLL = 8


def _dispatch_kernel(nw, bs_ref, nblk_ref, dest_ref, h_ref, xs_out, zbuf, sem, zsem):
    td = h_ref.shape[0] // nw
    blk_rows = zbuf.shape[0]
    n_exp = bs_ref.shape[0]

    @pl.when(pl.program_id(0) == 0)
    def _():
        zbuf[...] = jnp.zeros(zbuf.shape, zbuf.dtype)
        total = xs_out.shape[0] // blk_rows
        first_tail = bs_ref[n_exp - 1] + nblk_ref[n_exp - 1]

        def z_copy(blk):
            row0 = pl.multiple_of(blk * blk_rows, blk_rows)
            return pltpu.make_async_copy(zbuf, xs_out.at[pl.ds(row0, blk_rows)], zsem)

        def each_partial_block(fn):
            def per_expert(e, c):
                @pl.when(nblk_ref[e] > 0)
                def _():
                    fn(z_copy(bs_ref[e] + nblk_ref[e] - 1))
                return c

            def per_tail(b, c):
                fn(z_copy(b))
                return c

            lax.fori_loop(0, n_exp, per_expert, 0)
            lax.fori_loop(first_tail, total, per_tail, 0)

        each_partial_block(lambda cp: cp.start())
        each_partial_block(lambda cp: cp.wait())

    def body(g, carry):
        base = g * ROW_UNROLL
        for u in range(ROW_UNROLL):
            i = base + u
            src = h_ref.at[pl.ds(pl.multiple_of(i * nw, nw), nw)]
            for k in range(TOP_K):
                d = dest_ref[i * TOP_K + k]
                pltpu.make_async_copy(src, xs_out.at[pl.ds(pl.multiple_of(d * nw, nw), nw)],
                                      sem).start(priority=k % 2)
        return carry

    lax.fori_loop(0, td // ROW_UNROLL, body, 0)
    for _ in range(TOP_K):
        pltpu.make_async_copy(h_ref, xs_out.at[pl.ds(0, td * nw)], sem).wait()


def _dispatch_call(blk_start, nblk, dest, hp, n_rows, nw, td, bm):
    t = hp.shape[0] // nw
    grid_spec = pltpu.PrefetchScalarGridSpec(
        num_scalar_prefetch=2,
        grid=(t // td,),
        in_specs=[
            pl.BlockSpec((TOP_K * td,), lambda i, bs, nb: (i,), memory_space=pltpu.SMEM),
            pl.BlockSpec((td * nw, LANE), lambda i, bs, nb: (i, 0)),
        ],
        out_specs=pl.BlockSpec(memory_space=pl.ANY),
        scratch_shapes=[pltpu.VMEM((bm * nw, LANE), hp.dtype), pltpu.SemaphoreType.DMA,
                        pltpu.SemaphoreType.DMA],
    )
    return pl.pallas_call(
        functools.partial(_dispatch_kernel, nw),
        grid_spec=grid_spec,
        out_shape=jax.ShapeDtypeStruct((n_rows * nw, LANE), hp.dtype),
        compiler_params=pltpu.CompilerParams(dimension_semantics=("arbitrary",),
                                             vmem_limit_bytes=VMEM_LIMIT),
        name="dispatch",
    )(blk_start, nblk, dest, hp)


CAST_ROWS = 64


def _expert_kernel(bs_ref, nblk_ref, wgu_ref, bgu_ref, wd_ref, bd_ref, xs_hbm, y_hbm,
                   wgu_bf, wd_bf, xbuf, ybuf, xsem, ysem):
    e = pl.program_id(0)
    n_exp = pl.num_programs(0)
    d = wgu_ref.shape[1]
    f = wd_ref.shape[1]
    ns = d // LANE
    nw = ns // 2
    bm = xbuf.shape[1] // nw
    n = nblk_ref[e]
    blk0 = bs_ref[e]

    def x_copy(j, slot):
        row0 = pl.multiple_of((blk0 + j) * (bm * nw), bm * nw)
        return pltpu.make_async_copy(xs_hbm.at[pl.ds(row0, bm * nw)], xbuf.at[slot], xsem.at[slot])

    def y_copy(blk, slot):
        row0 = pl.multiple_of(blk * (bm * ns), bm * ns)
        return pltpu.make_async_copy(ybuf.at[slot], y_hbm.at[pl.ds(row0, bm * ns)], ysem.at[slot])

    @pl.when(n > 0)
    def _():
        x_copy(0, 0).start(priority=1)

        def cast_gu(r, c):
            rows = pl.ds(pl.multiple_of(r * CAST_ROWS, CAST_ROWS), CAST_ROWS)
            wgu_bf[rows, :] = wgu_ref[0, rows, :].astype(BF16)
            return c

        def cast_d(r, c):
            rows = pl.ds(pl.multiple_of(r * CAST_ROWS, CAST_ROWS), CAST_ROWS)
            wd_bf[rows, :] = wd_ref[0, rows, :].astype(BF16)
            return c

        lax.fori_loop(0, d // CAST_ROWS, cast_gu, 0)
        lax.fori_loop(0, f // CAST_ROWS, cast_d, 0)

        def block(j, c):
            slot = j % 2
            x_copy(j, slot).wait()

            @pl.when(j + 1 < n)
            def _():
                x_copy(j + 1, 1 - slot).start(priority=1)

            @pl.when(j >= 2)
            def _():
                y_copy(blk0 + j - 2, slot).wait()

            w = jnp.concatenate([xbuf[slot, pl.ds(c, bm, stride=nw), :] for c in range(nw)], axis=1)
            x_lo = lax.bitcast_convert_type(w << 16, F32).astype(BF16)
            x_hi = lax.bitcast_convert_type(w & jnp.uint32(0xFFFF0000), F32).astype(BF16)
            gu = (jnp.dot(x_lo, wgu_bf[:d // 2, :], preferred_element_type=F32)
                  + jnp.dot(x_hi, wgu_bf[d // 2:, :], preferred_element_type=F32) + bgu_ref[0])
            gate = jnp.minimum(gu[:, :f], SWIGLU_LIMIT)
            up = jnp.clip(gu[:, f:], -SWIGLU_LIMIT, SWIGLU_LIMIT)
            act = gate * jax.nn.sigmoid(SWIGLU_ALPHA * gate) * (up + 1.0)
            y = jnp.dot(act.astype(BF16), wd_bf[...], preferred_element_type=F32) + bd_ref[0]
            for c in range(ns):
                ybuf[slot, pl.ds(c, bm, stride=ns), :] = y[:, c * LANE:(c + 1) * LANE]
            y_copy(blk0 + j, slot).start(priority=1)
            return c

        lax.fori_loop(0, n, block, 0)

        @pl.when(n >= 2)
        def _():
            y_copy(blk0 + n - 2, n % 2).wait()

        y_copy(blk0 + n - 1, (n - 1) % 2).wait()

    @pl.when(e == n_exp - 1)
    def _():
        first = blk0 + n
        total = y_hbm.shape[0] // (bm * ns)
        ybuf[0] = jnp.zeros(ybuf.shape[1:], ybuf.dtype)

        def fill(b, c):
            y_copy(b, 0).start()
            return c

        def drain(b, c):
            y_copy(b, 0).wait()
            return c

        lax.fori_loop(first, total, fill, 0)
        lax.fori_loop(first, total, drain, 0)


def _expert_call(blk_start, nblk, xs, wgu, bgu, wd, bd, bm):
    e, d, f2 = wgu.shape
    f = wd.shape[1]
    nw = d // 2 // LANE
    r = xs.shape[0] // nw

    def wsel(i, bs, nb):
        return (i, 0, 0)

    grid_spec = pltpu.PrefetchScalarGridSpec(
        num_scalar_prefetch=2,
        grid=(e,),
        in_specs=[
            pl.BlockSpec((1, d, f2), wsel),
            pl.BlockSpec((1, 1, f2), wsel),
            pl.BlockSpec((1, f, d), wsel),
            pl.BlockSpec((1, 1, d), wsel),
            pl.BlockSpec(memory_space=pl.ANY),
        ],
        out_specs=pl.BlockSpec(memory_space=pl.ANY),
        scratch_shapes=[pltpu.VMEM((d, f2), BF16), pltpu.VMEM((f, d), BF16),
                        pltpu.VMEM((2, bm * nw, LANE), xs.dtype),
                        pltpu.VMEM((2, bm * (d // LANE), LANE), F32),
                        pltpu.SemaphoreType.DMA((2,)), pltpu.SemaphoreType.DMA((2,))],
    )
    return pl.pallas_call(
        _expert_kernel,
        grid_spec=grid_spec,
        out_shape=jax.ShapeDtypeStruct((r * (d // LANE), LANE), F32),
        compiler_params=pltpu.CompilerParams(dimension_semantics=("arbitrary",),
                                             vmem_limit_bytes=VMEM_LIMIT),
        name="experts",
    )(blk_start, nblk, wgu, bgu, wd, bd, xs)


def _combine_kernel(dest_ref, dest_next_ref, x1_ref, gate_ref, gfin_ref, yb_hbm, o_ref, ybuf, sem):
    tf, d = x1_ref.shape
    ns = d // LANE
    j = pl.program_id(0)
    nsteps = pl.num_programs(0)
    slot = j % 2

    def issue(idx_ref, s):
        def body(g, carry):
            base = g * ROW_UNROLL
            for u in range(ROW_UNROLL):
                i = base + u
                for k in range(TOP_K):
                    r = idx_ref[i * TOP_K + k]
                    pltpu.make_async_copy(yb_hbm.at[pl.ds(pl.multiple_of(r * ns, ns), ns)],
                                          ybuf.at[s, k, pl.ds(pl.multiple_of(i * ns, ns), ns)],
                                          sem.at[s]).start(priority=k % 2)
            return carry

        lax.fori_loop(0, tf // ROW_UNROLL, body, 0)

    @pl.when(j == 0)
    def _():
        issue(dest_ref, 0)

    @pl.when(j + 1 < nsteps)
    def _():
        issue(dest_next_ref, 1 - slot)

    for k in range(TOP_K):
        pltpu.make_async_copy(yb_hbm.at[pl.ds(0, tf * ns)], ybuf.at[slot, k], sem.at[slot]).wait()
    g = gate_ref[...]
    y = x1_ref[...]
    for k in range(TOP_K):
        yk = jnp.concatenate([ybuf[slot, k, pl.ds(c, tf, stride=ns), :] for c in range(ns)], axis=1)
        y = y + g[:, k:k + 1] * yk
    o_ref[...] = _rms(y, gfin_ref[...], d)


def _combine_call(dest, x1, gates_t, gfin, yb, tf):
    t, d = x1.shape
    nsteps = t // tf
    return pl.pallas_call(
        _combine_kernel,
        grid=(nsteps,),
        in_specs=[
            pl.BlockSpec((TOP_K * tf,), lambda i: (i,), memory_space=pltpu.SMEM),
            pl.BlockSpec((TOP_K * tf,), lambda i: (jnp.minimum(i + 1, nsteps - 1),),
                         memory_space=pltpu.SMEM),
            pl.BlockSpec((tf, d), lambda i: (i, 0)),
            pl.BlockSpec((tf, TOP_K), lambda i: (i, 0)),
            pl.BlockSpec(gfin.shape, lambda i: (0, 0)),
            pl.BlockSpec(memory_space=pl.ANY),
        ],
        out_specs=pl.BlockSpec((tf, d), lambda i: (i, 0)),
        out_shape=jax.ShapeDtypeStruct((t, d), F32),
        scratch_shapes=[pltpu.VMEM((2, TOP_K, tf * (d // LANE), LANE), F32),
                        pltpu.SemaphoreType.DMA((2,))],
        compiler_params=pltpu.CompilerParams(dimension_semantics=("arbitrary",),
                                             vmem_limit_bytes=VMEM_LIMIT),
        name="combine",
    )(dest, dest, x1, gates_t, gfin, yb)


def _pad_head_cols(w, widths_in, place):
    k = w.shape[0]
    per = sum(widths_in)
    blocks = []
    for hh in range(N_HEADS):
        blocks.append(place(w[:, hh * per:(hh + 1) * per]))
    return jnp.concatenate(blocks, axis=1)


def _layer(x2, pos3, batch, seq, g_attn_norm, w_in, w_pool, b_pool, pool_scale, g_q_a, w_q_b,
           g_kv_a, w_kv_b, g_out_pool, g_out_attn, w_out, g_ffn_norm, w_router, b_router,
           w_gate_up, b_gate_up, w_down, b_down, tiles):
    tm, tq, tmo, td, bm, tf = tiles
    t, d = x2.shape
    d_pool = pool_scale.shape[0]
    gdim = d_pool // N_POOL_GROUPS
    q_rank = g_q_a.shape[0]
    kv_rank = g_kv_a.shape[0]
    half = QK_ROPE_DIM // 2

    o1, o2, o3 = d_pool, d_pool + q_rank, d_pool + q_rank + kv_rank
    zk = lambda n: jnp.zeros((d, n), F32)
    w_kr1, w_kr2 = w_in[:, o3:o3 + half], w_in[:, o3 + half:]
    w_in_p = jnp.concatenate([w_in[:, :o3], zk(QK_NOPE_DIM), w_kr1, w_kr2, zk(LANE - QK_DIM),
                              zk(QK_NOPE_DIM), w_kr2, w_kr1, zk(LANE - QK_DIM)], axis=1).astype(BF16)
    zq = lambda n: jnp.zeros((q_rank, n), F32)
    wq_plain = _pad_head_cols(w_q_b, (QK_DIM,),
                              lambda c: jnp.concatenate([c, zq(LANE - QK_DIM)], axis=1))
    wq_swap = _pad_head_cols(w_q_b, (QK_DIM,), lambda c: jnp.concatenate(
        [zq(QK_NOPE_DIM), c[:, QK_NOPE_DIM + half:], c[:, QK_NOPE_DIM:QK_NOPE_DIM + half],
         zq(LANE - QK_DIM)], axis=1))
    wq_p = jnp.concatenate([wq_plain, wq_swap], axis=1).astype(BF16)
    zkv = lambda n: jnp.zeros((kv_rank, n), F32)
    per = QK_NOPE_DIM + V_HEAD_DIM
    wk_p = _pad_head_cols(w_kv_b, (per,), lambda c: jnp.concatenate(
        [c[:, :QK_NOPE_DIM], zkv(LANE - QK_NOPE_DIM)], axis=1))
    wv_p = _pad_head_cols(w_kv_b, (per,), lambda c: jnp.concatenate(
        [c[:, QK_NOPE_DIM:], zkv(LANE - V_HEAD_DIM)], axis=1))
    wkv_p = jnp.concatenate([wk_p, wv_p], axis=1).astype(BF16)
    inv = ROPE_THETA ** (-jnp.arange(half, dtype=F32) / half)
    inv_col = inv.reshape(half, 1)

    row = lambda v: v.reshape(1, -1)
    grp = lambda v: v.reshape(N_POOL_GROUPS, 1, gdim)

    pool_n, q, k, v = _proj_call(
        x2, pos3, row(g_attn_norm), w_in_p, inv_col, w_pool.astype(BF16), grp(b_pool),
        grp(pool_scale), grp(g_out_pool), row(g_q_a), wq_p, row(g_kv_a), wkv_p, seq, tm)

    attn = _attn_call(q, k, v, batch, seq, tq)

    d_attn = N_HEADS * V_HEAD_DIM
    wo_pool = w_out[:d_pool].astype(BF16)
    wo_attn = w_out[d_pool:].reshape(N_HEADS, V_HEAD_DIM, d)
    wo_attn_p = jnp.concatenate([wo_attn, jnp.zeros((N_HEADS, HEAD_PAD - V_HEAD_DIM, d), F32)],
                                axis=1).reshape(N_HEADS * HEAD_PAD, d).astype(BF16)
    gattn_p = jnp.concatenate([g_out_attn.reshape(N_HEADS, V_HEAD_DIM),
                               jnp.zeros((N_HEADS, HEAD_PAD - V_HEAD_DIM), F32)], axis=1).reshape(1, -1)

    x1, hp, idx, gates, rank, counts = _outproj_call(
        pool_n, attn, x2, gattn_p, wo_pool, wo_attn_p, row(g_ffn_norm), w_router.T,
        b_router.reshape(-1, 1), tmo)

    n_assign = t * TOP_K
    cnt = counts[:, 0]
    padded = ((cnt + bm - 1) // bm) * bm
    pad_end = jnp.cumsum(padded)
    pad_start = pad_end - padded
    e_ids = jnp.arange(N_EXPERTS, dtype=jnp.int32)
    dest = rank + jnp.sum(jnp.where(idx[None] == e_ids[:, None, None], pad_start[:, None, None], 0),
                          axis=0)
    dest = dest.T.reshape(-1)
    n_rows = (n_assign // bm + N_EXPERTS) * bm
    blk_start = (pad_start // bm).astype(jnp.int32)
    nblk = (padded // bm).astype(jnp.int32)

    xs = _dispatch_call(blk_start, nblk, dest, hp, n_rows, d // 2 // LANE, td, bm)
    yb = _expert_call(blk_start, nblk, xs, w_gate_up, b_gate_up.reshape(N_EXPERTS, 1, -1),
                      w_down, b_down.reshape(N_EXPERTS, 1, -1), bm)
    return x1, dest, gates, yb


def _forward(x, positions, g_attn_norm, w_in, w_pool, b_pool, pool_scale, g_q_a, w_q_b, g_kv_a,
             w_kv_b, g_out_pool, g_out_attn, w_out, g_ffn_norm, w_router, b_router, w_gate_up,
             b_gate_up, w_down, b_down, g_final, tiles):
    batch, seq, d = x.shape
    t = batch * seq
    tm, tq, tmo, td, bm, tf = tiles
    x2 = x.reshape(t, d)
    pos3 = positions.reshape(t // tm, 1, tm)
    depth = g_attn_norm.shape[0]
    assert depth == 1
    l = 0
    x1, dest, gates, yb = _layer(
        x2, pos3, batch, seq, g_attn_norm[l], w_in[l], w_pool[l], b_pool[l], pool_scale[l],
        g_q_a[l], w_q_b[l], g_kv_a[l], w_kv_b[l], g_out_pool[l], g_out_attn[l], w_out[l],
        g_ffn_norm[l], w_router[l], b_router[l], w_gate_up[l], b_gate_up[l], w_down[l],
        b_down[l], tiles)
    out = _combine_call(dest, x1, gates.T, g_final.reshape(1, -1), yb, tf)
    return out.reshape(batch, seq, d)


def _tiles_for(seq):
    tm = min(512, seq)
    tq = min(512, seq // 2)
    tmo = min(512, seq)
    td = min(512, seq)
    bm = 256 if seq >= 2048 else 128
    tf = min(256, seq)
    return (tm, tq, tmo, td, bm, tf)


def kernel(x, positions, g_attn_norm, w_in, w_pool, b_pool, pool_scale, g_q_a, w_q_b, g_kv_a, w_kv_b, g_out_pool, g_out_attn, w_out, g_ffn_norm, w_router, b_router, w_gate_up, b_gate_up, w_down, b_down, g_final):
    tiles = _tiles_for(x.shape[1])
    return _forward(x, positions, g_attn_norm, w_in, w_pool, b_pool, pool_scale, g_q_a, w_q_b,
                    g_kv_a, w_kv_b, g_out_pool, g_out_attn, w_out, g_ffn_norm, w_router, b_router,
                    w_gate_up, b_gate_up, w_down, b_down, g_final, tiles)
```

```python
import functools

import jax
import jax.numpy as jnp
from jax import lax
from jax.experimental import pallas as pl
from jax.experimental.pallas import tpu as pltpu

POOL_WINDOWS = (2, 4, 8, 16)
N_POOL_GROUPS = 4
N_HEADS = 8
V_HEAD_DIM = 64
QK_NOPE_DIM = 64
QK_ROPE_DIM = 32
QK_DIM = QK_NOPE_DIM + QK_ROPE_DIM
ROPE_THETA = 10000.0
N_EXPERTS = 32
TOP_K = 4
SWIGLU_ALPHA = 1.702
SWIGLU_LIMIT = 7.0
RMS_EPS = 1e-6
LOG2_E = 1.4426950408889634

LANE = 128
HEAD_PAD = 128
POOL_HALO = 16

VMEM_LIMIT = 56 * 1024 * 1024

F32 = jnp.float32
BF16 = jnp.bfloat16


def _pack_bf16_pairs(x):
    n = x.shape[1] // 2
    xb = x.astype(BF16).astype(F32)
    lo = lax.bitcast_convert_type(xb[:, :n], jnp.uint32)
    hi = lax.bitcast_convert_type(xb[:, n:], jnp.uint32)
    return (lo >> 16) | (hi & jnp.uint32(0xFFFF0000))


def _unpack_bf16_pairs(w):
    lo = lax.bitcast_convert_type(w << 16, F32)
    hi = lax.bitcast_convert_type(w & jnp.uint32(0xFFFF0000), F32)
    return lo, hi


def _rms(x, g, n):
    ms = jnp.sum(x * x, axis=-1, keepdims=True) * (1.0 / n)
    return x * lax.rsqrt(ms + RMS_EPS) * g


def _proj_kernel(blocks_per_seq, x_ref, pos_ref, g_attn_ref, w_in_ref, inv_ref, wpool_ref,
                 bpool_ref, pscale_ref, gpool_ref, gq_ref, wq_ref, gkv_ref, wkv_ref,
                 pool_out, q_out, k_out, v_out, ext_scr):
    tm = x_ref.shape[0]
    d_model = x_ref.shape[1]
    d_pool = pool_out.shape[1]
    gdim = d_pool // N_POOL_GROUPS
    q_rank = gq_ref.shape[1]
    kv_rank = gkv_ref.shape[1]
    i = pl.program_id(0)
    blk_in_seq = i % blocks_per_seq

    x = x_ref[...]
    h = _rms(x, g_attn_ref[...], d_model).astype(BF16)
    proj = jnp.dot(h, w_in_ref[...], preferred_element_type=F32)
    u = proj[:, :d_pool]
    qlat = proj[:, d_pool:d_pool + q_rank]
    kvlat = proj[:, d_pool + q_rank:d_pool + q_rank + kv_rank]
    o_kr = d_pool + q_rank + kv_rank
    krb = proj[:, o_kr:o_kr + LANE]
    krs = proj[:, o_kr + LANE:]

    @pl.when(blk_in_seq == 0)
    def _():
        ext_scr[0:POOL_HALO, :] = jnp.zeros((POOL_HALO, d_pool), F32)

    @pl.when(blk_in_seq != 0)
    def _():
        ext_scr[0:POOL_HALO, :] = ext_scr[tm:tm + POOL_HALO, :]

    ext_scr[POOL_HALO:, :] = u
    t_in_seq = (blk_in_seq * tm + lax.broadcasted_iota(jnp.int32, (tm, gdim), 0) + 1).astype(F32)
    ys = []
    ssq = jnp.zeros((tm, 1), F32)
    for g, w in enumerate(POOL_WINDOWS):
        a = ext_scr[:, g * gdim:(g + 1) * gdim]
        shift = 1
        while shift < w:
            a = a + pltpu.roll(a, shift, axis=0)
            shift *= 2
        win = a[POOL_HALO:, :]
        cnt = jnp.minimum(t_in_seq, float(w))
        ug = u[:, g * gdim:(g + 1) * gdim]
        diff = (win / cnt - ug).astype(BF16)
        mixed = jnp.dot(diff, wpool_ref[g], preferred_element_type=F32) + bpool_ref[g]
        y = mixed * pscale_ref[g]
        ssq = ssq + jnp.sum(y * y, axis=-1, keepdims=True)
        ys.append(y)
    rinv = lax.rsqrt(ssq * (1.0 / d_pool) + RMS_EPS)
    for g in range(N_POOL_GROUPS):
        pool_out[:, g * gdim:(g + 1) * gdim] = (ys[g] * rinv * gpool_ref[g]).astype(pool_out.dtype)

    pos = pos_ref[0].astype(F32)
    ang_t = inv_ref[...] * pos
    cos_t = jnp.cos(ang_t)
    sin_t = jnp.sin(ang_t)
    half = QK_ROPE_DIM // 2
    ones = jnp.ones((QK_NOPE_DIM, tm), F32)
    c_t = jnp.concatenate([ones, cos_t, cos_t, jnp.ones((LANE - QK_DIM, tm), F32)], axis=0)
    s_t = jnp.concatenate([jnp.zeros((QK_NOPE_DIM, tm), F32), -sin_t, sin_t,
                           jnp.zeros((LANE - QK_DIM, tm), F32)], axis=0)
    c = c_t.T
    s = s_t.T

    def rope(blk, blk_swapped, c_, s_):
        return blk * c_ + blk_swapped * s_

    qn = _rms(qlat, gq_ref[...], q_rank).astype(BF16)
    q2 = jnp.dot(qn, wq_ref[...], preferred_element_type=F32)
    scale = QK_DIM ** -0.5 * LOG2_E
    cq, sq = c * scale, s * scale
    hp = N_HEADS * HEAD_PAD
    for hh in range(N_HEADS):
        sl = slice(hh * HEAD_PAD, (hh + 1) * HEAD_PAD)
        sw = slice(hp + hh * HEAD_PAD, hp + (hh + 1) * HEAD_PAD)
        q_out[:, sl] = rope(q2[:, sl], q2[:, sw], cq, sq).astype(q_out.dtype)

    kvn = _rms(kvlat, gkv_ref[...], kv_rank).astype(BF16)
    kv = jnp.dot(kvn, wkv_ref[...], preferred_element_type=F32)
    kr = rope(krb, krs, c, s)
    for hh in range(N_HEADS):
        sl = slice(hh * HEAD_PAD, (hh + 1) * HEAD_PAD)
        k_out[:, sl] = (kv[:, sl] + kr).astype(k_out.dtype)
    vlane = lax.broadcasted_iota(jnp.int32, (1, N_HEADS * HEAD_PAD), 1) % HEAD_PAD
    ones_col = (vlane == V_HEAD_DIM).astype(F32)
    v_out[0] = (kv[:, N_HEADS * HEAD_PAD:] + ones_col).T.astype(v_out.dtype)


def _proj_call(x2, pos3, g_attn, w_in_p, inv_col, wpool, bpool, pscale, gpool, gq, wq_p, gkv, wkv_p,
               seq, tm):
    t, d = x2.shape
    d_pool = pscale.shape[0] * pscale.shape[2]
    hp = N_HEADS * HEAD_PAD
    nblk = t // tm
    const2 = lambda i: (0, 0)
    const3 = lambda i: (0, 0, 0)
    in_specs = [
        pl.BlockSpec((tm, d), lambda i: (i, 0)),
        pl.BlockSpec((1, 1, tm), lambda i: (i, 0, 0)),
        pl.BlockSpec(g_attn.shape, const2),
        pl.BlockSpec(w_in_p.shape, const2),
        pl.BlockSpec(inv_col.shape, const2),
        pl.BlockSpec(wpool.shape, const3),
        pl.BlockSpec(bpool.shape, const3),
        pl.BlockSpec(pscale.shape, const3),
        pl.BlockSpec(gpool.shape, const3),
        pl.BlockSpec(gq.shape, const2),
        pl.BlockSpec(wq_p.shape, const2),
        pl.BlockSpec(gkv.shape, const2),
        pl.BlockSpec(wkv_p.shape, const2),
    ]
    out_shape = [
        jax.ShapeDtypeStruct((t, d_pool), BF16),
        jax.ShapeDtypeStruct((t, hp), BF16),
        jax.ShapeDtypeStruct((t, hp), BF16),
        jax.ShapeDtypeStruct((nblk, hp, tm), BF16),
    ]
    out_specs = [
        pl.BlockSpec((tm, d_pool), lambda i: (i, 0)),
        pl.BlockSpec((tm, hp), lambda i: (i, 0)),
        pl.BlockSpec((tm, hp), lambda i: (i, 0)),
        pl.BlockSpec((1, hp, tm), lambda i: (i, 0, 0)),
    ]
    return pl.pallas_call(
        functools.partial(_proj_kernel, seq // tm),
        grid=(nblk,),
        in_specs=in_specs,
        out_specs=out_specs,
        out_shape=out_shape,
        scratch_shapes=[pltpu.VMEM((tm + POOL_HALO, d_pool), F32)],
        compiler_params=pltpu.CompilerParams(dimension_semantics=("arbitrary",),
                                             vmem_limit_bytes=VMEM_LIMIT),
        name="proj",
    )(x2, pos3, g_attn, w_in_p, inv_col, wpool, bpool, pscale, gpool, gq, wq_p, gkv, wkv_p)


ROWSUM_ROW = V_HEAD_DIM


def _attn_kernel(q_ref, k_ref, vt_ref, o_ref, acc_ref):
    tq = q_ref.shape[0]
    th = vt_ref.shape[2]
    nsub = tq // th
    qi = pl.program_id(2)
    acc_ref[...] = jnp.zeros(acc_ref.shape, F32)

    def step(kc, m_prev, q_lo, masked):
        width = tq - q_lo
        start = pl.multiple_of(kc * th, th)
        k = k_ref[pl.ds(start, th), :]
        st = lax.dot_general(k, q_ref[q_lo:, :], (((1,), (1,)), ((), ())),
                             preferred_element_type=F32)
        if masked:
            key = lax.broadcasted_iota(jnp.int32, (th, width), 0)
            qry = lax.broadcasted_iota(jnp.int32, (th, width), 1)
            st = jnp.where(key <= qry, st, -jnp.inf)
        m_new = jnp.maximum(m_prev, jnp.max(st, axis=0, keepdims=True))
        alpha = jnp.exp2(m_prev - m_new)
        pt = jnp.exp2(st - m_new).astype(BF16)
        acc_ref[:, q_lo:] = alpha * acc_ref[:, q_lo:] + jnp.dot(vt_ref[kc], pt,
                                                                preferred_element_type=F32)
        return m_new

    def two_chunks(j, mp):
        return step(2 * j + 1, step(2 * j, mp, 0, False), 0, False)

    m = lax.fori_loop(0, (nsub // 2) * qi, two_chunks, jnp.full((1, tq), -jnp.inf, F32))
    for c in range(nsub):
        m = step(nsub * qi + c, m, c * th, True)[:, th:]

    acc = acc_ref[...]
    out = (acc / acc[ROWSUM_ROW:ROWSUM_ROW + 1, :]).T
    lane = lax.broadcasted_iota(jnp.int32, out.shape, 1)
    o_ref[...] = jnp.where(lane < V_HEAD_DIM, out, 0.0).astype(o_ref.dtype)


ATTN_Q_CHUNKS = 4


def _attn_call(q, k, vt, batch, seq, th):
    t, hp = q.shape
    tq = min(ATTN_Q_CHUNKS * th, seq)
    nq = seq // tq
    nchunk = seq // th
    return pl.pallas_call(
        _attn_kernel,
        grid=(batch, N_HEADS, nq),
        in_specs=[
            pl.BlockSpec((tq, HEAD_PAD), lambda b, h, qi: (b * nq + qi, h)),
            pl.BlockSpec((seq, HEAD_PAD), lambda b, h, qi: (b, h)),
            pl.BlockSpec((nchunk, HEAD_PAD, th), lambda b, h, qi: (b, h, 0)),
        ],
        out_specs=pl.BlockSpec((tq, HEAD_PAD), lambda b, h, qi: (b * nq + qi, h)),
        out_shape=jax.ShapeDtypeStruct((t, hp), F32),
        scratch_shapes=[pltpu.VMEM((HEAD_PAD, tq), F32)],
        compiler_params=pltpu.CompilerParams(
            dimension_semantics=("arbitrary", "arbitrary", "arbitrary"),
            vmem_limit_bytes=VMEM_LIMIT),
        name="attn",
    )(q, k, vt)


def _outproj_kernel(pool_ref, attn_ref, x_ref, gattn_ref, wop_ref, woa_ref, gffn_ref, wr_ref,
                    br_ref, x1_out, hp_out, idx_out, gate_out, rank_out, cnt_out, run_scr):
    tm, d = x_ref.shape
    d_attn = N_HEADS * V_HEAD_DIM
    i = pl.program_id(0)

    @pl.when(i == 0)
    def _():
        run_scr[...] = jnp.zeros(run_scr.shape, F32)

    attn = attn_ref[...]
    attn_n = _rms(attn, gattn_ref[...], d_attn).astype(BF16)
    mix = (jnp.dot(pool_ref[...], wop_ref[...], preferred_element_type=F32)
           + jnp.dot(attn_n, woa_ref[...], preferred_element_type=F32))
    x1 = x_ref[...] + mix
    x1_out[...] = x1
    h2 = _rms(x1, gffn_ref[...], d)

    word = _pack_bf16_pairs(h2)
    nw = d // 2 // LANE
    for c in range(nw):
        hp_out[pl.ds(c, tm, stride=nw), :] = word[:, c * LANE:(c + 1) * LANE]

    logits = lax.dot_general(wr_ref[...], h2, (((1,), (1,)), ((), ())),
                             preferred_element_type=F32,
                             precision=lax.Precision.HIGHEST) + br_ref[...]
    e_iota = lax.broadcasted_iota(jnp.int32, (N_EXPERTS, tm), 0)
    work = logits
    idxs, vals = [], []
    for _ in range(TOP_K):
        mx = jnp.max(work, axis=0, keepdims=True)
        sel = jnp.min(jnp.where(work == mx, e_iota, N_EXPERTS), axis=0, keepdims=True)
        idxs.append(sel)
        vals.append(mx)
        work = jnp.where(e_iota == sel, -jnp.inf, work)
    exps = [jnp.exp(v - vals[0]) for v in vals]
    denom = exps[0] + exps[1] + exps[2] + exps[3]
    gate_out[...] = jnp.concatenate([e / denom for e in exps], axis=0)
    idx_out[...] = jnp.concatenate(idxs, axis=0)

    onehot = jnp.zeros((N_EXPERTS, tm), F32)
    for sel in idxs:
        onehot = onehot + (e_iota == sel).astype(F32)
    r_io = lax.broadcasted_iota(jnp.int32, (tm, tm), 0)
    c_io = lax.broadcasted_iota(jnp.int32, (tm, tm), 1)
    upper = (r_io < c_io).astype(BF16)
    prefix = jnp.dot(onehot.astype(BF16), upper, preferred_element_type=F32) + run_scr[:, 0:1]
    ranks = [jnp.sum(jnp.where(e_iota == sel, prefix, 0.0), axis=0, keepdims=True) for sel in idxs]
    rank_out[...] = jnp.concatenate(ranks, axis=0).astype(jnp.int32)
    run_new = run_scr[...] + jnp.sum(onehot, axis=1, keepdims=True)
    run_scr[...] = run_new
    cnt_out[...] = run_new.astype(jnp.int32)


def _outproj_call(pool_n, attn, x2, gattn_p, wo_pool, wo_attn_p, gffn, wr_t, br_col, tm):
    t, d = x2.shape
    nblk = t // tm
    const2 = lambda i: (0, 0)
    row = lambda i: (i, 0)
    col = lambda i: (0, i)
    in_specs = [
        pl.BlockSpec((tm, pool_n.shape[1]), row),
        pl.BlockSpec((tm, attn.shape[1]), row),
        pl.BlockSpec((tm, d), row),
        pl.BlockSpec(gattn_p.shape, const2),
        pl.BlockSpec(wo_pool.shape, const2),
        pl.BlockSpec(wo_attn_p.shape, const2),
        pl.BlockSpec(gffn.shape, const2),
        pl.BlockSpec(wr_t.shape, const2),
        pl.BlockSpec(br_col.shape, const2),
    ]
    out_shape = [
        jax.ShapeDtypeStruct((t, d), F32),
        jax.ShapeDtypeStruct((t * (d // 2 // LANE), LANE), jnp.uint32),
        jax.ShapeDtypeStruct((TOP_K, t), jnp.int32),
        jax.ShapeDtypeStruct((TOP_K, t), F32),
        jax.ShapeDtypeStruct((TOP_K, t), jnp.int32),
        jax.ShapeDtypeStruct((N_EXPERTS, LANE), jnp.int32),
    ]
    out_specs = [
        pl.BlockSpec((tm, d), row),
        pl.BlockSpec((tm * (d // 2 // LANE), LANE), row),
        pl.BlockSpec((TOP_K, tm), col),
        pl.BlockSpec((TOP_K, tm), col),
        pl.BlockSpec((TOP_K, tm), col),
        pl.BlockSpec((N_EXPERTS, LANE), const2),
    ]
    return pl.pallas_call(
        _outproj_kernel,
        grid=(nblk,),
        in_specs=in_specs,
        out_specs=out_specs,
        out_shape=out_shape,
        scratch_shapes=[pltpu.VMEM((N_EXPERTS, LANE), F32)],
        compiler_params=pltpu.CompilerParams(dimension_semantics=("arbitrary",),
                                             vmem_limit_bytes=VMEM_LIMIT),
        name="outproj",
    )(pool_n, attn, x2, gattn_p, wo_pool, wo_attn_p, gffn, wr_t, br_col)


ROW_UNROLL = 8


def _dispatch_kernel(nw, bs_ref, nblk_ref, dest_ref, h_ref, xs_out, zbuf, sem, zsem):
    td = h_ref.shape[0] // nw
    blk_rows = zbuf.shape[0]
    n_exp = bs_ref.shape[0]

    @pl.when(pl.program_id(0) == 0)
    def _():
        zbuf[...] = jnp.zeros(zbuf.shape, zbuf.dtype)
        total = xs_out.shape[0] // blk_rows
        first_tail = bs_ref[n_exp - 1] + nblk_ref[n_exp - 1]

        def z_copy(blk):
            row0 = pl.multiple_of(blk * blk_rows, blk_rows)
            return pltpu.make_async_copy(zbuf, xs_out.at[pl.ds(row0, blk_rows)], zsem)

        def each_partial_block(fn):
            def per_expert(e, c):
                @pl.when(nblk_ref[e] > 0)
                def _():
                    fn(z_copy(bs_ref[e] + nblk_ref[e] - 1))
                return c

            def per_tail(b, c):
                fn(z_copy(b))
                return c

            lax.fori_loop(0, n_exp, per_expert, 0)
            lax.fori_loop(first_tail, total, per_tail, 0)

        each_partial_block(lambda cp: cp.start())
        each_partial_block(lambda cp: cp.wait())

    def body(g, carry):
        base = g * ROW_UNROLL
        for u in range(ROW_UNROLL):
            i = base + u
            src = h_ref.at[pl.ds(pl.multiple_of(i * nw, nw), nw)]
            for k in range(TOP_K):
                d = dest_ref[i * TOP_K + k]
                pltpu.make_async_copy(src, xs_out.at[pl.ds(pl.multiple_of(d * nw, nw), nw)],
                                      sem).start(priority=k % 2)
        return carry

    lax.fori_loop(0, td // ROW_UNROLL, body, 0)
    for _ in range(TOP_K):
        pltpu.make_async_copy(h_ref, xs_out.at[pl.ds(0, td * nw)], sem).wait()


def _dispatch_call(blk_start, nblk, dest, hp, n_rows, nw, td, bm):
    t = hp.shape[0] // nw
    grid_spec = pltpu.PrefetchScalarGridSpec(
        num_scalar_prefetch=2,
        grid=(t // td,),
        in_specs=[
            pl.BlockSpec((TOP_K * td,), lambda i, bs, nb: (i,), memory_space=pltpu.SMEM),
            pl.BlockSpec((td * nw, LANE), lambda i, bs, nb: (i, 0)),
        ],
        out_specs=pl.BlockSpec(memory_space=pl.ANY),
        scratch_shapes=[pltpu.VMEM((bm * nw, LANE), hp.dtype), pltpu.SemaphoreType.DMA,
                        pltpu.SemaphoreType.DMA],
    )
    return pl.pallas_call(
        functools.partial(_dispatch_kernel, nw),
        grid_spec=grid_spec,
        out_shape=jax.ShapeDtypeStruct((n_rows * nw, LANE), hp.dtype),
        compiler_params=pltpu.CompilerParams(dimension_semantics=("arbitrary",),
                                             vmem_limit_bytes=VMEM_LIMIT),
        name="dispatch",
    )(blk_start, nblk, dest, hp)


CAST_ROWS = 64


def _expert_kernel(bs_ref, nblk_ref, wgu_ref, bgu_ref, wd_ref, bd_ref, xs_hbm, y_hbm,
                   wgu_bf, wd_bf, xbuf, ybuf, xsem, ysem):
    e = pl.program_id(0)
    n_exp = pl.num_programs(0)
    d = wgu_ref.shape[1]
    f = wd_ref.shape[1]
    nw = d // 2 // LANE
    bm = xbuf.shape[1] // nw
    n = nblk_ref[e]
    blk0 = bs_ref[e]

    def x_copy(j, slot):
        row0 = pl.multiple_of((blk0 + j) * (bm * nw), bm * nw)
        return pltpu.make_async_copy(xs_hbm.at[pl.ds(row0, bm * nw)], xbuf.at[slot], xsem.at[slot])

    def y_copy(blk, slot):
        row0 = pl.multiple_of(blk * (bm * nw), bm * nw)
        return pltpu.make_async_copy(ybuf.at[slot], y_hbm.at[pl.ds(row0, bm * nw)], ysem.at[slot])

    @pl.when(n > 0)
    def _():
        x_copy(0, 0).start(priority=1)

        def cast_gu(r, c):
            rows = pl.ds(pl.multiple_of(r * CAST_ROWS, CAST_ROWS), CAST_ROWS)
            wgu_bf[rows, :] = wgu_ref[0, rows, :].astype(BF16)
            return c

        def cast_d(r, c):
            rows = pl.ds(pl.multiple_of(r * CAST_ROWS, CAST_ROWS), CAST_ROWS)
            wd_bf[rows, :] = wd_ref[0, rows, :].astype(BF16)
            return c

        lax.fori_loop(0, d // CAST_ROWS, cast_gu, 0)
        lax.fori_loop(0, f // CAST_ROWS, cast_d, 0)

        def block(j, c):
            slot = j % 2
            x_copy(j, slot).wait()

            @pl.when(j + 1 < n)
            def _():
                x_copy(j + 1, 1 - slot).start(priority=1)

            @pl.when(j >= 2)
            def _():
                y_copy(blk0 + j - 2, slot).wait()

            w = jnp.concatenate([xbuf[slot, pl.ds(c, bm, stride=nw), :] for c in range(nw)], axis=1)
            x_lo, x_hi = _unpack_bf16_pairs(w)
            gu = (jnp.dot(x_lo.astype(BF16), wgu_bf[:d // 2, :], preferred_element_type=F32)
                  + jnp.dot(x_hi.astype(BF16), wgu_bf[d // 2:, :], preferred_element_type=F32)
                  + bgu_ref[0])
            gate = jnp.minimum(gu[:, :f], SWIGLU_LIMIT)
            up = jnp.clip(gu[:, f:], -SWIGLU_LIMIT, SWIGLU_LIMIT)
            act = gate * jax.nn.sigmoid(SWIGLU_ALPHA * gate) * (up + 1.0)
            y = jnp.dot(act.astype(BF16), wd_bf[...], preferred_element_type=F32) + bd_ref[0]
            yw = _pack_bf16_pairs(y)
            for c in range(nw):
                ybuf[slot, pl.ds(c, bm, stride=nw), :] = yw[:, c * LANE:(c + 1) * LANE]
            y_copy(blk0 + j, slot).start(priority=1)
            return c

        lax.fori_loop(0, n, block, 0)

        @pl.when(n >= 2)
        def _():
            y_copy(blk0 + n - 2, n % 2).wait()

        y_copy(blk0 + n - 1, (n - 1) % 2).wait()

    @pl.when(e == n_exp - 1)
    def _():
        first = blk0 + n
        total = y_hbm.shape[0] // (bm * nw)
        ybuf[0] = jnp.zeros(ybuf.shape[1:], ybuf.dtype)

        def fill(b, c):
            y_copy(b, 0).start()
            return c

        def drain(b, c):
            y_copy(b, 0).wait()
            return c

        lax.fori_loop(first, total, fill, 0)
        lax.fori_loop(first, total, drain, 0)


def _expert_call(blk_start, nblk, xs, wgu, bgu, wd, bd, bm):
    e, d, f2 = wgu.shape
    f = wd.shape[1]
    nw = d // 2 // LANE
    r = xs.shape[0] // nw

    def wsel(i, bs, nb):
        return (i, 0, 0)

    grid_spec = pltpu.PrefetchScalarGridSpec(
        num_scalar_prefetch=2,
        grid=(e,),
        in_specs=[
            pl.BlockSpec((1, d, f2), wsel),
            pl.BlockSpec((1, 1, f2), wsel),
            pl.BlockSpec((1, f, d), wsel),
            pl.BlockSpec((1, 1, d), wsel),
            pl.BlockSpec(memory_space=pl.ANY),
        ],
        out_specs=pl.BlockSpec(memory_space=pl.ANY),
        scratch_shapes=[pltpu.VMEM((d, f2), BF16), pltpu.VMEM((f, d), BF16),
                        pltpu.VMEM((2, bm * nw, LANE), xs.dtype),
                        pltpu.VMEM((2, bm * nw, LANE), jnp.uint32),
                        pltpu.SemaphoreType.DMA((2,)), pltpu.SemaphoreType.DMA((2,))],
    )
    return pl.pallas_call(
        _expert_kernel,
        grid_spec=grid_spec,
        out_shape=jax.ShapeDtypeStruct((r * nw, LANE), jnp.uint32),
        compiler_params=pltpu.CompilerParams(dimension_semantics=("arbitrary",),
                                             vmem_limit_bytes=VMEM_LIMIT),
        name="experts",
    )(blk_start, nblk, wgu, bgu, wd, bd, xs)


def _combine_kernel(dest_ref, dest_next_ref, x1_ref, gate_ref, gfin_ref, yb_hbm, o_ref, ybuf, sem):
    tf, d = x1_ref.shape
    ns = d // 2 // LANE
    j = pl.program_id(0)
    nsteps = pl.num_programs(0)
    slot = j % 2

    def issue(idx_ref, s):
        def body(g, carry):
            base = g * ROW_UNROLL
            for u in range(ROW_UNROLL):
                i = base + u
                for k in range(TOP_K):
                    r = idx_ref[i * TOP_K + k]
                    pltpu.make_async_copy(yb_hbm.at[pl.ds(pl.multiple_of(r * ns, ns), ns)],
                                          ybuf.at[s, k, pl.ds(pl.multiple_of(i * ns, ns), ns)],
                                          sem.at[s]).start(priority=k % 2)
            return carry

        lax.fori_loop(0, tf // ROW_UNROLL, body, 0)

    @pl.when(j == 0)
    def _():
        issue(dest_ref, 0)

    @pl.when(j + 1 < nsteps)
    def _():
        issue(dest_next_ref, 1 - slot)

    for k in range(TOP_K):
        pltpu.make_async_copy(yb_hbm.at[pl.ds(0, tf * ns)], ybuf.at[slot, k], sem.at[slot]).wait()
    g = gate_ref[...]
    y = x1_ref[...]
    for k in range(TOP_K):
        yw = jnp.concatenate([ybuf[slot, k, pl.ds(c, tf, stride=ns), :] for c in range(ns)], axis=1)
        y = y + g[:, k:k + 1] * jnp.concatenate(_unpack_bf16_pairs(yw), axis=1)
    o_ref[...] = _rms(y, gfin_ref[...], d)


def _combine_call(dest, x1, gates_t, gfin, yb, tf):
    t, d = x1.shape
    nsteps = t // tf
    return pl.pallas_call(
        _combine_kernel,
        grid=(nsteps,),
        in_specs=[
            pl.BlockSpec((TOP_K * tf,), lambda i: (i,), memory_space=pltpu.SMEM),
            pl.BlockSpec((TOP_K * tf,), lambda i: (jnp.minimum(i + 1, nsteps - 1),),
                         memory_space=pltpu.SMEM),
            pl.BlockSpec((tf, d), lambda i: (i, 0)),
            pl.BlockSpec((tf, TOP_K), lambda i: (i, 0)),
            pl.BlockSpec(gfin.shape, lambda i: (0, 0)),
            pl.BlockSpec(memory_space=pl.ANY),
        ],
        out_specs=pl.BlockSpec((tf, d), lambda i: (i, 0)),
        out_shape=jax.ShapeDtypeStruct((t, d), F32),
        scratch_shapes=[pltpu.VMEM((2, TOP_K, tf * (d // 2 // LANE), LANE), jnp.uint32),
                        pltpu.SemaphoreType.DMA((2,))],
        compiler_params=pltpu.CompilerParams(dimension_semantics=("arbitrary",),
                                             vmem_limit_bytes=VMEM_LIMIT),
        name="combine",
    )(dest, dest, x1, gates_t, gfin, yb)


def _pad_head_cols(w, widths_in, place):
    k = w.shape[0]
    per = sum(widths_in)
    blocks = []
    for hh in range(N_HEADS):
        blocks.append(place(w[:, hh * per:(hh + 1) * per]))
    return jnp.concatenate(blocks, axis=1)


def _layer(x2, pos3, batch, seq, g_attn_norm, w_in, w_pool, b_pool, pool_scale, g_q_a, w_q_b,
           g_kv_a, w_kv_b, g_out_pool, g_out_attn, w_out, g_ffn_norm, w_router, b_router,
           w_gate_up, b_gate_up, w_down, b_down, tiles):
    tm, tq, tmo, td, bm, tf = tiles
    t, d = x2.shape
    d_pool = pool_scale.shape[0]
    gdim = d_pool // N_POOL_GROUPS
    q_rank = g_q_a.shape[0]
    kv_rank = g_kv_a.shape[0]
    half = QK_ROPE_DIM // 2

    o1, o2, o3 = d_pool, d_pool + q_rank, d_pool + q_rank + kv_rank
    zk = lambda n: jnp.zeros((d, n), F32)
    w_kr1, w_kr2 = w_in[:, o3:o3 + half], w_in[:, o3 + half:]
    w_in_p = jnp.concatenate([w_in[:, :o3], zk(QK_NOPE_DIM), w_kr1, w_kr2, zk(LANE - QK_DIM),
                              zk(QK_NOPE_DIM), w_kr2, w_kr1, zk(LANE - QK_DIM)], axis=1).astype(BF16)
    zq = lambda n: jnp.zeros((q_rank, n), F32)
    wq_plain = _pad_head_cols(w_q_b, (QK_DIM,),
                              lambda c: jnp.concatenate([c, zq(LANE - QK_DIM)], axis=1))
    wq_swap = _pad_head_cols(w_q_b, (QK_DIM,), lambda c: jnp.concatenate(
        [zq(QK_NOPE_DIM), c[:, QK_NOPE_DIM + half:], c[:, QK_NOPE_DIM:QK_NOPE_DIM + half],
         zq(LANE - QK_DIM)], axis=1))
    wq_p = jnp.concatenate([wq_plain, wq_swap], axis=1).astype(BF16)
    zkv = lambda n: jnp.zeros((kv_rank, n), F32)
    per = QK_NOPE_DIM + V_HEAD_DIM
    wk_p = _pad_head_cols(w_kv_b, (per,), lambda c: jnp.concatenate(
        [c[:, :QK_NOPE_DIM], zkv(LANE - QK_NOPE_DIM)], axis=1))
    wv_p = _pad_head_cols(w_kv_b, (per,), lambda c: jnp.concatenate(
        [c[:, QK_NOPE_DIM:], zkv(LANE - V_HEAD_DIM)], axis=1))
    wkv_p = jnp.concatenate([wk_p, wv_p], axis=1).astype(BF16)
    inv = ROPE_THETA ** (-jnp.arange(half, dtype=F32) / half)
    inv_col = inv.reshape(half, 1)

    row = lambda v: v.reshape(1, -1)
    grp = lambda v: v.reshape(N_POOL_GROUPS, 1, gdim)

    pool_n, q, k, v = _proj_call(
        x2, pos3, row(g_attn_norm), w_in_p, inv_col, w_pool.astype(BF16), grp(b_pool),
        grp(pool_scale), grp(g_out_pool), row(g_q_a), wq_p, row(g_kv_a), wkv_p, seq, tm)

    attn = _attn_call(q, k, v, batch, seq, tq)

    d_attn = N_HEADS * V_HEAD_DIM
    wo_pool = w_out[:d_pool].astype(BF16)
    wo_attn = w_out[d_pool:].reshape(N_HEADS, V_HEAD_DIM, d)
    wo_attn_p = jnp.concatenate([wo_attn, jnp.zeros((N_HEADS, HEAD_PAD - V_HEAD_DIM, d), F32)],
                                axis=1).reshape(N_HEADS * HEAD_PAD, d).astype(BF16)
    gattn_p = jnp.concatenate([g_out_attn.reshape(N_HEADS, V_HEAD_DIM),
                               jnp.zeros((N_HEADS, HEAD_PAD - V_HEAD_DIM), F32)], axis=1).reshape(1, -1)

    x1, hp, idx, gates, rank, counts = _outproj_call(
        pool_n, attn, x2, gattn_p, wo_pool, wo_attn_p, row(g_ffn_norm), w_router.T,
        b_router.reshape(-1, 1), tmo)

    n_assign = t * TOP_K
    cnt = counts[:, 0]
    padded = ((cnt + bm - 1) // bm) * bm
    pad_end = jnp.cumsum(padded)
    pad_start = pad_end - padded
    e_ids = jnp.arange(N_EXPERTS, dtype=jnp.int32)
    dest = rank + jnp.sum(jnp.where(idx[None] == e_ids[:, None, None], pad_start[:, None, None], 0),
                          axis=0)
    dest = dest.T.reshape(-1)
    n_rows = (n_assign // bm + N_EXPERTS) * bm
    blk_start = (pad_start // bm).astype(jnp.int32)
    nblk = (padded // bm).astype(jnp.int32)

    xs = _dispatch_call(blk_start, nblk, dest, hp, n_rows, d // 2 // LANE, td, bm)
    yb = _expert_call(blk_start, nblk, xs, w_gate_up, b_gate_up.reshape(N_EXPERTS, 1, -1),
                      w_down, b_down.reshape(N_EXPERTS, 1, -1), bm)
    return x1, dest, gates, yb


def _forward(x, positions, g_attn_norm, w_in, w_pool, b_pool, pool_scale, g_q_a, w_q_b, g_kv_a,
             w_kv_b, g_out_pool, g_out_attn, w_out, g_ffn_norm, w_router, b_router, w_gate_up,
             b_gate_up, w_down, b_down, g_final, tiles):
    batch, seq, d = x.shape
    t = batch * seq
    tm, tq, tmo, td, bm, tf = tiles
    x2 = x.reshape(t, d)
    pos3 = positions.reshape(t // tm, 1, tm)
    depth = g_attn_norm.shape[0]
    assert depth == 1
    l = 0
    x1, dest, gates, yb = _layer(
        x2, pos3, batch, seq, g_attn_norm[l], w_in[l], w_pool[l], b_pool[l], pool_scale[l],
        g_q_a[l], w_q_b[l], g_kv_a[l], w_kv_b[l], g_out_pool[l], g_out_attn[l], w_out[l],
        g_ffn_norm[l], w_router[l], b_router[l], w_gate_up[l], b_gate_up[l], w_down[l],
        b_down[l], tiles)
    out = _combine_call(dest, x1, gates.T, g_final.reshape(1, -1), yb, tf)
    return out.reshape(batch, seq, d)


def _tiles_for(seq):
    tm = min(512, seq)
    tq = min(512, seq // 2)
    tmo = min(512, seq)
    td = min(512, seq)
    bm = 256 if seq >= 2048 else 128
    tf = min(256, seq)
    return (tm, tq, tmo, td, bm, tf)


def kernel(x, positions, g_attn_norm, w_in, w_pool, b_pool, pool_scale, g_q_a, w_q_b, g_kv_a, w_kv_b, g_out_pool, g_out_attn, w_out, g_ffn_norm, w_router, b_router, w_gate_up, b_gate_up, w_down, b_down, g_final):
    tiles = _tiles_for(x.shape[1])
    return _forward(x, positions, g_attn_norm, w_in, w_pool, b_pool, pool_scale, g_q_a, w_q_b,
                    g_kv_a, w_kv_b, g_out_pool, g_out_attn, w_out, g_ffn_norm, w_router, b_router,
                    w_gate_up, b_gate_up, w_down, b_down, g_final, tiles)
```

```python
import functools

import jax
import jax.numpy as jnp
from jax import lax
from jax.experimental import pallas as pl
from jax.experimental.pallas import tpu as pltpu

POOL_WINDOWS = (2, 4, 8, 16)
N_POOL_GROUPS = 4
N_HEADS = 8
V_HEAD_DIM = 64
QK_NOPE_DIM = 64
QK_ROPE_DIM = 32
QK_DIM = QK_NOPE_DIM + QK_ROPE_DIM
ROPE_THETA = 10000.0
N_EXPERTS = 32
TOP_K = 4
SWIGLU_ALPHA = 1.702
SWIGLU_LIMIT = 7.0
RMS_EPS = 1e-6
LOG2_E = 1.4426950408889634

LANE = 128
HEAD_PAD = 128
POOL_HALO = 16

VMEM_LIMIT = 56 * 1024 * 1024

F32 = jnp.float32
BF16 = jnp.bfloat16


def _pack_bf16_pairs(x):
    n = x.shape[1] // 2
    xb = x.astype(BF16).astype(F32)
    lo = lax.bitcast_convert_type(xb[:, :n], jnp.uint32)
    hi = lax.bitcast_convert_type(xb[:, n:], jnp.uint32)
    return (lo >> 16) | (hi & jnp.uint32(0xFFFF0000))


def _unpack_bf16_pairs(w):
    lo = lax.bitcast_convert_type(w << 16, F32)
    hi = lax.bitcast_convert_type(w & jnp.uint32(0xFFFF0000), F32)
    return lo, hi


def _rms(x, g, n):
    ms = jnp.sum(x * x, axis=-1, keepdims=True) * (1.0 / n)
    return x * lax.rsqrt(ms + RMS_EPS) * g


def _proj_kernel(blocks_per_seq, x_ref, pos_ref, g_attn_ref, w_in_ref, inv_ref, wpool_ref,
                 bpool_ref, pscale_ref, gpool_ref, gq_ref, wq_ref, gkv_ref, wkv_ref,
                 pool_out, q_out, k_out, v_out, ext_scr):
    tm = x_ref.shape[0]
    d_model = x_ref.shape[1]
    d_pool = pool_out.shape[1]
    gdim = d_pool // N_POOL_GROUPS
    q_rank = gq_ref.shape[1]
    kv_rank = gkv_ref.shape[1]
    i = pl.program_id(0)
    blk_in_seq = i % blocks_per_seq

    x = x_ref[...]
    h = _rms(x, g_attn_ref[...], d_model).astype(BF16)
    proj = jnp.dot(h, w_in_ref[...], preferred_element_type=F32)
    u = proj[:, :d_pool]
    qlat = proj[:, d_pool:d_pool + q_rank]
    kvlat = proj[:, d_pool + q_rank:d_pool + q_rank + kv_rank]
    o_kr = d_pool + q_rank + kv_rank
    krb = proj[:, o_kr:o_kr + LANE]
    krs = proj[:, o_kr + LANE:]

    @pl.when(blk_in_seq == 0)
    def _():
        ext_scr[0:POOL_HALO, :] = jnp.zeros((POOL_HALO, d_pool), F32)

    @pl.when(blk_in_seq != 0)
    def _():
        ext_scr[0:POOL_HALO, :] = ext_scr[tm:tm + POOL_HALO, :]

    ext_scr[POOL_HALO:, :] = u
    t_in_seq = (blk_in_seq * tm + lax.broadcasted_iota(jnp.int32, (tm, gdim), 0) + 1).astype(F32)
    ys = []
    ssq = jnp.zeros((tm, 1), F32)
    for g, w in enumerate(POOL_WINDOWS):
        a = ext_scr[:, g * gdim:(g + 1) * gdim]
        shift = 1
        while shift < w:
            a = a + pltpu.roll(a, shift, axis=0)
            shift *= 2
        win = a[POOL_HALO:, :]
        cnt = jnp.minimum(t_in_seq, float(w))
        ug = u[:, g * gdim:(g + 1) * gdim]
        diff = (win / cnt - ug).astype(BF16)
        mixed = jnp.dot(diff, wpool_ref[g], preferred_element_type=F32) + bpool_ref[g]
        y = mixed * pscale_ref[g]
        ssq = ssq + jnp.sum(y * y, axis=-1, keepdims=True)
        ys.append(y)
    rinv = lax.rsqrt(ssq * (1.0 / d_pool) + RMS_EPS)
    for g in range(N_POOL_GROUPS):
        pool_out[:, g * gdim:(g + 1) * gdim] = (ys[g] * rinv * gpool_ref[g]).astype(pool_out.dtype)

    pos = pos_ref[0].astype(F32)
    ang_t = inv_ref[...] * pos
    cos_t = jnp.cos(ang_t)
    sin_t = jnp.sin(ang_t)
    half = QK_ROPE_DIM // 2
    ones = jnp.ones((QK_NOPE_DIM, tm), F32)
    c_t = jnp.concatenate([ones, cos_t, cos_t, jnp.ones((LANE - QK_DIM, tm), F32)], axis=0)
    s_t = jnp.concatenate([jnp.zeros((QK_NOPE_DIM, tm), F32), -sin_t, sin_t,
                           jnp.zeros((LANE - QK_DIM, tm), F32)], axis=0)
    c = c_t.T
    s = s_t.T

    def rope(blk, blk_swapped, c_, s_):
        return blk * c_ + blk_swapped * s_

    qn = _rms(qlat, gq_ref[...], q_rank).astype(BF16)
    q2 = jnp.dot(qn, wq_ref[...], preferred_element_type=F32)
    scale = QK_DIM ** -0.5 * LOG2_E
    cq, sq = c * scale, s * scale
    hp = N_HEADS * HEAD_PAD
    for hh in range(N_HEADS):
        sl = slice(hh * HEAD_PAD, (hh + 1) * HEAD_PAD)
        sw = slice(hp + hh * HEAD_PAD, hp + (hh + 1) * HEAD_PAD)
        q_out[:, sl] = rope(q2[:, sl], q2[:, sw], cq, sq).astype(q_out.dtype)

    kvn = _rms(kvlat, gkv_ref[...], kv_rank).astype(BF16)
    kv = jnp.dot(kvn, wkv_ref[...], preferred_element_type=F32)
    kr = rope(krb, krs, c, s)
    for hh in range(N_HEADS):
        sl = slice(hh * HEAD_PAD, (hh + 1) * HEAD_PAD)
        k_out[:, sl] = (kv[:, sl] + kr).astype(k_out.dtype)
    vlane = lax.broadcasted_iota(jnp.int32, (1, N_HEADS * HEAD_PAD), 1) % HEAD_PAD
    ones_col = (vlane == V_HEAD_DIM).astype(F32)
    v_out[0] = (kv[:, N_HEADS * HEAD_PAD:] + ones_col).T.astype(v_out.dtype)


def _proj_call(x2, pos3, g_attn, w_in_p, inv_col, wpool, bpool, pscale, gpool, gq, wq_p, gkv, wkv_p,
               seq, tm):
    t, d = x2.shape
    d_pool = pscale.shape[0] * pscale.shape[2]
    hp = N_HEADS * HEAD_PAD
    nblk = t // tm
    const2 = lambda i: (0, 0)
    const3 = lambda i: (0, 0, 0)
    in_specs = [
        pl.BlockSpec((tm, d), lambda i: (i, 0)),
        pl.BlockSpec((1, 1, tm), lambda i: (i, 0, 0)),
        pl.BlockSpec(g_attn.shape, const2),
        pl.BlockSpec(w_in_p.shape, const2),
        pl.BlockSpec(inv_col.shape, const2),
        pl.BlockSpec(wpool.shape, const3),
        pl.BlockSpec(bpool.shape, const3),
        pl.BlockSpec(pscale.shape, const3),
        pl.BlockSpec(gpool.shape, const3),
        pl.BlockSpec(gq.shape, const2),
        pl.BlockSpec(wq_p.shape, const2),
        pl.BlockSpec(gkv.shape, const2),
        pl.BlockSpec(wkv_p.shape, const2),
    ]
    out_shape = [
        jax.ShapeDtypeStruct((t, d_pool), BF16),
        jax.ShapeDtypeStruct((t, hp), BF16),
        jax.ShapeDtypeStruct((t, hp), BF16),
        jax.ShapeDtypeStruct((nblk, hp, tm), BF16),
    ]
    out_specs = [
        pl.BlockSpec((tm, d_pool), lambda i: (i, 0)),
        pl.BlockSpec((tm, hp), lambda i: (i, 0)),
        pl.BlockSpec((tm, hp), lambda i: (i, 0)),
        pl.BlockSpec((1, hp, tm), lambda i: (i, 0, 0)),
    ]
    return pl.pallas_call(
        functools.partial(_proj_kernel, seq // tm),
        grid=(nblk,),
        in_specs=in_specs,
        out_specs=out_specs,
        out_shape=out_shape,
        scratch_shapes=[pltpu.VMEM((tm + POOL_HALO, d_pool), F32)],
        compiler_params=pltpu.CompilerParams(dimension_semantics=("arbitrary",),
                                             vmem_limit_bytes=VMEM_LIMIT),
        name="proj",
    )(x2, pos3, g_attn, w_in_p, inv_col, wpool, bpool, pscale, gpool, gq, wq_p, gkv, wkv_p)


ROWSUM_ROW = V_HEAD_DIM


def _attn_kernel(q_ref, k_ref, vt_ref, o_ref, acc_ref):
    tq = q_ref.shape[0]
    th = vt_ref.shape[2]
    nsub = tq // th
    qi = pl.program_id(2)
    acc_ref[...] = jnp.zeros(acc_ref.shape, F32)

    def step(kc, m_prev, q_lo, masked):
        width = tq - q_lo
        start = pl.multiple_of(kc * th, th)
        k = k_ref[pl.ds(start, th), :]
        st = lax.dot_general(k, q_ref[q_lo:, :], (((1,), (1,)), ((), ())),
                             preferred_element_type=F32)
        if masked:
            key = lax.broadcasted_iota(jnp.int32, (th, width), 0)
            qry = lax.broadcasted_iota(jnp.int32, (th, width), 1)
            st = jnp.where(key <= qry, st, -jnp.inf)
        m_new = jnp.maximum(m_prev, jnp.max(st, axis=0, keepdims=True))
        alpha = jnp.exp2(m_prev - m_new)
        pt = jnp.exp2(st - m_new).astype(BF16)
        acc_ref[:, q_lo:] = alpha * acc_ref[:, q_lo:] + jnp.dot(vt_ref[kc], pt,
                                                                preferred_element_type=F32)
        return m_new

    def two_chunks(j, mp):
        return step(2 * j + 1, step(2 * j, mp, 0, False), 0, False)

    m = lax.fori_loop(0, (nsub // 2) * qi, two_chunks, jnp.full((1, tq), -jnp.inf, F32))
    for c in range(nsub):
        m = step(nsub * qi + c, m, c * th, True)[:, th:]

    acc = acc_ref[...]
    out = (acc / acc[ROWSUM_ROW:ROWSUM_ROW + 1, :]).T
    lane = lax.broadcasted_iota(jnp.int32, out.shape, 1)
    o_ref[...] = jnp.where(lane < V_HEAD_DIM, out, 0.0).astype(o_ref.dtype)


ATTN_Q_CHUNKS = 8


def _attn_call(q, k, vt, batch, seq, th):
    t, hp = q.shape
    tq = min(ATTN_Q_CHUNKS * th, seq)
    nq = seq // tq
    nchunk = seq // th
    return pl.pallas_call(
        _attn_kernel,
        grid=(batch, N_HEADS, nq),
        in_specs=[
            pl.BlockSpec((tq, HEAD_PAD), lambda b, h, qi: (b * nq + qi, h)),
            pl.BlockSpec((seq, HEAD_PAD), lambda b, h, qi: (b, h)),
            pl.BlockSpec((nchunk, HEAD_PAD, th), lambda b, h, qi: (b, h, 0)),
        ],
        out_specs=pl.BlockSpec((tq, HEAD_PAD), lambda b, h, qi: (b * nq + qi, h)),
        out_shape=jax.ShapeDtypeStruct((t, hp), F32),
        scratch_shapes=[pltpu.VMEM((HEAD_PAD, tq), F32)],
        compiler_params=pltpu.CompilerParams(
            dimension_semantics=("arbitrary", "arbitrary", "arbitrary"),
            vmem_limit_bytes=VMEM_LIMIT),
        name="attn",
    )(q, k, vt)


def _outproj_kernel(pool_ref, attn_ref, x_ref, gattn_ref, wop_ref, woa_ref, gffn_ref, wr_ref,
                    br_ref, x1_out, hp_out, idx_out, gate_out, rank_out, cnt_out, run_scr):
    tm, d = x_ref.shape
    d_attn = N_HEADS * V_HEAD_DIM
    i = pl.program_id(0)

    @pl.when(i == 0)
    def _():
        run_scr[...] = jnp.zeros(run_scr.shape, F32)

    attn = attn_ref[...]
    attn_n = _rms(attn, gattn_ref[...], d_attn).astype(BF16)
    mix = (jnp.dot(pool_ref[...], wop_ref[...], preferred_element_type=F32)
           + jnp.dot(attn_n, woa_ref[...], preferred_element_type=F32))
    x1 = x_ref[...] + mix
    x1_out[...] = x1
    h2 = _rms(x1, gffn_ref[...], d)

    word = _pack_bf16_pairs(h2)
    nw = d // 2 // LANE
    for c in range(nw):
        hp_out[pl.ds(c, tm, stride=nw), :] = word[:, c * LANE:(c + 1) * LANE]

    logits = lax.dot_general(wr_ref[...], h2, (((1,), (1,)), ((), ())),
                             preferred_element_type=F32,
                             precision=lax.Precision.HIGHEST) + br_ref[...]
    e_iota = lax.broadcasted_iota(jnp.int32, (N_EXPERTS, tm), 0)
    work = logits
    idxs, vals = [], []
    for _ in range(TOP_K):
        mx = jnp.max(work, axis=0, keepdims=True)
        sel = jnp.min(jnp.where(work == mx, e_iota, N_EXPERTS), axis=0, keepdims=True)
        idxs.append(sel)
        vals.append(mx)
        work = jnp.where(e_iota == sel, -jnp.inf, work)
    exps = [jnp.exp(v - vals[0]) for v in vals]
    denom = exps[0] + exps[1] + exps[2] + exps[3]
    gate_out[...] = jnp.concatenate([e / denom for e in exps], axis=0)
    idx_out[...] = jnp.concatenate(idxs, axis=0)

    onehot = jnp.zeros((N_EXPERTS, tm), F32)
    for sel in idxs:
        onehot = onehot + (e_iota == sel).astype(F32)
    r_io = lax.broadcasted_iota(jnp.int32, (tm, tm), 0)
    c_io = lax.broadcasted_iota(jnp.int32, (tm, tm), 1)
    upper = (r_io < c_io).astype(BF16)
    prefix = jnp.dot(onehot.astype(BF16), upper, preferred_element_type=F32) + run_scr[:, 0:1]
    ranks = [jnp.sum(jnp.where(e_iota == sel, prefix, 0.0), axis=0, keepdims=True) for sel in idxs]
    rank_out[...] = jnp.concatenate(ranks, axis=0).astype(jnp.int32)
    run_new = run_scr[...] + jnp.sum(onehot, axis=1, keepdims=True)
    run_scr[...] = run_new
    cnt_out[...] = run_new.astype(jnp.int32)


def _outproj_call(pool_n, attn, x2, gattn_p, wo_pool, wo_attn_p, gffn, wr_t, br_col, tm):
    t, d = x2.shape
    nblk = t // tm
    const2 = lambda i: (0, 0)
    row = lambda i: (i, 0)
    col = lambda i: (0, i)
    in_specs = [
        pl.BlockSpec((tm, pool_n.shape[1]), row),
        pl.BlockSpec((tm, attn.shape[1]), row),
        pl.BlockSpec((tm, d), row),
        pl.BlockSpec(gattn_p.shape, const2),
        pl.BlockSpec(wo_pool.shape, const2),
        pl.BlockSpec(wo_attn_p.shape, const2),
        pl.BlockSpec(gffn.shape, const2),
        pl.BlockSpec(wr_t.shape, const2),
        pl.BlockSpec(br_col.shape, const2),
    ]
    out_shape = [
        jax.ShapeDtypeStruct((t, d), F32),
        jax.ShapeDtypeStruct((t * (d // 2 // LANE), LANE), jnp.uint32),
        jax.ShapeDtypeStruct((TOP_K, t), jnp.int32),
        jax.ShapeDtypeStruct((TOP_K, t), F32),
        jax.ShapeDtypeStruct((TOP_K, t), jnp.int32),
        jax.ShapeDtypeStruct((N_EXPERTS, LANE), jnp.int32),
    ]
    out_specs = [
        pl.BlockSpec((tm, d), row),
        pl.BlockSpec((tm * (d // 2 // LANE), LANE), row),
        pl.BlockSpec((TOP_K, tm), col),
        pl.BlockSpec((TOP_K, tm), col),
        pl.BlockSpec((TOP_K, tm), col),
        pl.BlockSpec((N_EXPERTS, LANE), const2),
    ]
    return pl.pallas_call(
        _outproj_kernel,
        grid=(nblk,),
        in_specs=in_specs,
        out_specs=out_specs,
        out_shape=out_shape,
        scratch_shapes=[pltpu.VMEM((N_EXPERTS, LANE), F32)],
        compiler_params=pltpu.CompilerParams(dimension_semantics=("arbitrary",),
                                             vmem_limit_bytes=VMEM_LIMIT),
        name="outproj",
    )(pool_n, attn, x2, gattn_p, wo_pool, wo_attn_p, gffn, wr_t, br_col)


ROW_UNROLL = 8


def _dispatch_kernel(nw, bs_ref, nblk_ref, dest_ref, h_ref, xs_out, zbuf, sem, zsem):
    td = h_ref.shape[0] // nw
    blk_rows = zbuf.shape[0]
    n_exp = bs_ref.shape[0]

    @pl.when(pl.program_id(0) == 0)
    def _():
        zbuf[...] = jnp.zeros(zbuf.shape, zbuf.dtype)
        total = xs_out.shape[0] // blk_rows
        first_tail = bs_ref[n_exp - 1] + nblk_ref[n_exp - 1]

        def z_copy(blk):
            row0 = pl.multiple_of(blk * blk_rows, blk_rows)
            return pltpu.make_async_copy(zbuf, xs_out.at[pl.ds(row0, blk_rows)], zsem)

        def each_partial_block(fn):
            def per_expert(e, c):
                @pl.when(nblk_ref[e] > 0)
                def _():
                    fn(z_copy(bs_ref[e] + nblk_ref[e] - 1))
                return c

            def per_tail(b, c):
                fn(z_copy(b))
                return c

            lax.fori_loop(0, n_exp, per_expert, 0)
            lax.fori_loop(first_tail, total, per_tail, 0)

        each_partial_block(lambda cp: cp.start())
        each_partial_block(lambda cp: cp.wait())

    def body(g, carry):
        base = g * ROW_UNROLL
        for u in range(ROW_UNROLL):
            i = base + u
            src = h_ref.at[pl.ds(pl.multiple_of(i * nw, nw), nw)]
            for k in range(TOP_K):
                d = dest_ref[i * TOP_K + k]
                pltpu.make_async_copy(src, xs_out.at[pl.ds(pl.multiple_of(d * nw, nw), nw)],
                                      sem).start(priority=k % 2)
        return carry

    lax.fori_loop(0, td // ROW_UNROLL, body, 0)
    for _ in range(TOP_K):
        pltpu.make_async_copy(h_ref, xs_out.at[pl.ds(0, td * nw)], sem).wait()


def _dispatch_call(blk_start, nblk, dest, hp, n_rows, nw, td, bm):
    t = hp.shape[0] // nw
    grid_spec = pltpu.PrefetchScalarGridSpec(
        num_scalar_prefetch=2,
        grid=(t // td,),
        in_specs=[
            pl.BlockSpec((TOP_K * td,), lambda i, bs, nb: (i,), memory_space=pltpu.SMEM),
            pl.BlockSpec((td * nw, LANE), lambda i, bs, nb: (i, 0)),
        ],
        out_specs=pl.BlockSpec(memory_space=pl.ANY),
        scratch_shapes=[pltpu.VMEM((bm * nw, LANE), hp.dtype), pltpu.SemaphoreType.DMA,
                        pltpu.SemaphoreType.DMA],
    )
    return pl.pallas_call(
        functools.partial(_dispatch_kernel, nw),
        grid_spec=grid_spec,
        out_shape=jax.ShapeDtypeStruct((n_rows * nw, LANE), hp.dtype),
        compiler_params=pltpu.CompilerParams(dimension_semantics=("arbitrary",),
                                             vmem_limit_bytes=VMEM_LIMIT),
        name="dispatch",
    )(blk_start, nblk, dest, hp)


CAST_ROWS = 64


def _expert_kernel(bs_ref, nblk_ref, wgu_ref, bgu_ref, wd_ref, bd_ref, xs_hbm, y_hbm,
                   wgu_bf, wd_bf, xbuf, ybuf, xsem, ysem):
    e = pl.program_id(0)
    n_exp = pl.num_programs(0)
    d = wgu_ref.shape[1]
    f = wd_ref.shape[1]
    nw = d // 2 // LANE
    bm = xbuf.shape[1] // nw
    n = nblk_ref[e]
    blk0 = bs_ref[e]

    def x_copy(j, slot):
        row0 = pl.multiple_of((blk0 + j) * (bm * nw), bm * nw)
        return pltpu.make_async_copy(xs_hbm.at[pl.ds(row0, bm * nw)], xbuf.at[slot], xsem.at[slot])

    def y_copy(blk, slot):
        row0 = pl.multiple_of(blk * (bm * nw), bm * nw)
        return pltpu.make_async_copy(ybuf.at[slot], y_hbm.at[pl.ds(row0, bm * nw)], ysem.at[slot])

    @pl.when(n > 0)
    def _():
        x_copy(0, 0).start(priority=1)

        def cast_gu(r, c):
            rows = pl.ds(pl.multiple_of(r * CAST_ROWS, CAST_ROWS), CAST_ROWS)
            wgu_bf[rows, :] = wgu_ref[0, rows, :].astype(BF16)
            return c

        def cast_d(r, c):
            rows = pl.ds(pl.multiple_of(r * CAST_ROWS, CAST_ROWS), CAST_ROWS)
            wd_bf[rows, :] = wd_ref[0, rows, :].astype(BF16)
            return c

        lax.fori_loop(0, d // CAST_ROWS, cast_gu, 0)
        lax.fori_loop(0, f // CAST_ROWS, cast_d, 0)

        def block(j, c):
            slot = j % 2
            x_copy(j, slot).wait()

            @pl.when(j + 1 < n)
            def _():
                x_copy(j + 1, 1 - slot).start(priority=1)

            @pl.when(j >= 2)
            def _():
                y_copy(blk0 + j - 2, slot).wait()

            w = jnp.concatenate([xbuf[slot, pl.ds(c, bm, stride=nw), :] for c in range(nw)], axis=1)
            x_lo, x_hi = _unpack_bf16_pairs(w)
            gu = (jnp.dot(x_lo.astype(BF16), wgu_bf[:d // 2, :], preferred_element_type=F32)
                  + jnp.dot(x_hi.astype(BF16), wgu_bf[d // 2:, :], preferred_element_type=F32)
                  + bgu_ref[0])
            gate = jnp.minimum(gu[:, :f], SWIGLU_LIMIT)
            up = jnp.clip(gu[:, f:], -SWIGLU_LIMIT, SWIGLU_LIMIT)
            act = gate * jax.nn.sigmoid(SWIGLU_ALPHA * gate) * (up + 1.0)
            y = jnp.dot(act.astype(BF16), wd_bf[...], preferred_element_type=F32) + bd_ref[0]
            yw = _pack_bf16_pairs(y)
            for c in range(nw):
                ybuf[slot, pl.ds(c, bm, stride=nw), :] = yw[:, c * LANE:(c + 1) * LANE]
            y_copy(blk0 + j, slot).start(priority=1)
            return c

        lax.fori_loop(0, n, block, 0)

        @pl.when(n >= 2)
        def _():
            y_copy(blk0 + n - 2, n % 2).wait()

        y_copy(blk0 + n - 1, (n - 1) % 2).wait()

    @pl.when(e == n_exp - 1)
    def _():
        first = blk0 + n
        total = y_hbm.shape[0] // (bm * nw)
        ybuf[0] = jnp.zeros(ybuf.shape[1:], ybuf.dtype)

        def fill(b, c):
            y_copy(b, 0).start()
            return c

        def drain(b, c):
            y_copy(b, 0).wait()
            return c

        lax.fori_loop(first, total, fill, 0)
        lax.fori_loop(first, total, drain, 0)


def _expert_call(blk_start, nblk, xs, wgu, bgu, wd, bd, bm):
    e, d, f2 = wgu.shape
    f = wd.shape[1]
    nw = d // 2 // LANE
    r = xs.shape[0] // nw

    def wsel(i, bs, nb):
        return (i, 0, 0)

    grid_spec = pltpu.PrefetchScalarGridSpec(
        num_scalar_prefetch=2,
        grid=(e,),
        in_specs=[
            pl.BlockSpec((1, d, f2), wsel),
            pl.BlockSpec((1, 1, f2), wsel),
            pl.BlockSpec((1, f, d), wsel),
            pl.BlockSpec((1, 1, d), wsel),
            pl.BlockSpec(memory_space=pl.ANY),
        ],
        out_specs=pl.BlockSpec(memory_space=pl.ANY),
        scratch_shapes=[pltpu.VMEM((d, f2), BF16), pltpu.VMEM((f, d), BF16),
                        pltpu.VMEM((2, bm * nw, LANE), xs.dtype),
                        pltpu.VMEM((2, bm * nw, LANE), jnp.uint32),
                        pltpu.SemaphoreType.DMA((2,)), pltpu.SemaphoreType.DMA((2,))],
    )
    return pl.pallas_call(
        _expert_kernel,
        grid_spec=grid_spec,
        out_shape=jax.ShapeDtypeStruct((r * nw, LANE), jnp.uint32),
        compiler_params=pltpu.CompilerParams(dimension_semantics=("arbitrary",),
                                             vmem_limit_bytes=VMEM_LIMIT),
        name="experts",
    )(blk_start, nblk, wgu, bgu, wd, bd, xs)


def _combine_kernel(dest_ref, dest_next_ref, x1_ref, gate_ref, gfin_ref, yb_hbm, o_ref, ybuf, sem):
    tf, d = x1_ref.shape
    ns = d // 2 // LANE
    j = pl.program_id(0)
    nsteps = pl.num_programs(0)
    slot = j % 2

    def issue(idx_ref, s):
        def body(g, carry):
            base = g * ROW_UNROLL
            for u in range(ROW_UNROLL):
                i = base + u
                for k in range(TOP_K):
                    r = idx_ref[i * TOP_K + k]
                    pltpu.make_async_copy(yb_hbm.at[pl.ds(pl.multiple_of(r * ns, ns), ns)],
                                          ybuf.at[s, k, pl.ds(pl.multiple_of(i * ns, ns), ns)],
                                          sem.at[s]).start(priority=k % 2)
            return carry

        lax.fori_loop(0, tf // ROW_UNROLL, body, 0)

    @pl.when(j == 0)
    def _():
        issue(dest_ref, 0)

    @pl.when(j + 1 < nsteps)
    def _():
        issue(dest_next_ref, 1 - slot)

    for k in range(TOP_K):
        pltpu.make_async_copy(yb_hbm.at[pl.ds(0, tf * ns)], ybuf.at[slot, k], sem.at[slot]).wait()
    g = gate_ref[...]
    y = x1_ref[...]
    for k in range(TOP_K):
        yw = jnp.concatenate([ybuf[slot, k, pl.ds(c, tf, stride=ns), :] for c in range(ns)], axis=1)
        y = y + g[:, k:k + 1] * jnp.concatenate(_unpack_bf16_pairs(yw), axis=1)
    o_ref[...] = _rms(y, gfin_ref[...], d)


def _combine_call(dest, x1, gates_t, gfin, yb, tf):
    t, d = x1.shape
    nsteps = t // tf
    return pl.pallas_call(
        _combine_kernel,
        grid=(nsteps,),
        in_specs=[
            pl.BlockSpec((TOP_K * tf,), lambda i: (i,), memory_space=pltpu.SMEM),
            pl.BlockSpec((TOP_K * tf,), lambda i: (jnp.minimum(i + 1, nsteps - 1),),
                         memory_space=pltpu.SMEM),
            pl.BlockSpec((tf, d), lambda i: (i, 0)),
            pl.BlockSpec((tf, TOP_K), lambda i: (i, 0)),
            pl.BlockSpec(gfin.shape, lambda i: (0, 0)),
            pl.BlockSpec(memory_space=pl.ANY),
        ],
        out_specs=pl.BlockSpec((tf, d), lambda i: (i, 0)),
        out_shape=jax.ShapeDtypeStruct((t, d), F32),
        scratch_shapes=[pltpu.VMEM((2, TOP_K, tf * (d // 2 // LANE), LANE), jnp.uint32),
                        pltpu.SemaphoreType.DMA((2,))],
        compiler_params=pltpu.CompilerParams(dimension_semantics=("arbitrary",),
                                             vmem_limit_bytes=VMEM_LIMIT),
        name="combine",
    )(dest, dest, x1, gates_t, gfin, yb)


def _pad_head_cols(w, widths_in, place):
    k = w.shape[0]
    per = sum(widths_in)
    blocks = []
    for hh in range(N_HEADS):
        blocks.append(place(w[:, hh * per:(hh + 1) * per]))
    return jnp.concatenate(blocks, axis=1)


def _layer(x2, pos3, batch, seq, g_attn_norm, w_in, w_pool, b_pool, pool_scale, g_q_a, w_q_b,
           g_kv_a, w_kv_b, g_out_pool, g_out_attn, w_out, g_ffn_norm, w_router, b_router,
           w_gate_up, b_gate_up, w_down, b_down, tiles):
    tm, tq, tmo, td, bm, tf = tiles
    t, d = x2.shape
    d_pool = pool_scale.shape[0]
    gdim = d_pool // N_POOL_GROUPS
    q_rank = g_q_a.shape[0]
    kv_rank = g_kv_a.shape[0]
    half = QK_ROPE_DIM // 2

    o1, o2, o3 = d_pool, d_pool + q_rank, d_pool + q_rank + kv_rank
    zk = lambda n: jnp.zeros((d, n), F32)
    w_kr1, w_kr2 = w_in[:, o3:o3 + half], w_in[:, o3 + half:]
    w_in_p = jnp.concatenate([w_in[:, :o3], zk(QK_NOPE_DIM), w_kr1, w_kr2, zk(LANE - QK_DIM),
                              zk(QK_NOPE_DIM), w_kr2, w_kr1, zk(LANE - QK_DIM)], axis=1).astype(BF16)
    zq = lambda n: jnp.zeros((q_rank, n), F32)
    wq_plain = _pad_head_cols(w_q_b, (QK_DIM,),
                              lambda c: jnp.concatenate([c, zq(LANE - QK_DIM)], axis=1))
    wq_swap = _pad_head_cols(w_q_b, (QK_DIM,), lambda c: jnp.concatenate(
        [zq(QK_NOPE_DIM), c[:, QK_NOPE_DIM + half:], c[:, QK_NOPE_DIM:QK_NOPE_DIM + half],
         zq(LANE - QK_DIM)], axis=1))
    wq_p = jnp.concatenate([wq_plain, wq_swap], axis=1).astype(BF16)
    zkv = lambda n: jnp.zeros((kv_rank, n), F32)
    per = QK_NOPE_DIM + V_HEAD_DIM
    wk_p = _pad_head_cols(w_kv_b, (per,), lambda c: jnp.concatenate(
        [c[:, :QK_NOPE_DIM], zkv(LANE - QK_NOPE_DIM)], axis=1))
    wv_p = _pad_head_cols(w_kv_b, (per,), lambda c: jnp.concatenate(
        [c[:, QK_NOPE_DIM:], zkv(LANE - V_HEAD_DIM)], axis=1))
    wkv_p = jnp.concatenate([wk_p, wv_p], axis=1).astype(BF16)
    inv = ROPE_THETA ** (-jnp.arange(half, dtype=F32) / half)
    inv_col = inv.reshape(half, 1)

    row = lambda v: v.reshape(1, -1)
    grp = lambda v: v.reshape(N_POOL_GROUPS, 1, gdim)

    pool_n, q, k, v = _proj_call(
        x2, pos3, row(g_attn_norm), w_in_p, inv_col, w_pool.astype(BF16), grp(b_pool),
        grp(pool_scale), grp(g_out_pool), row(g_q_a), wq_p, row(g_kv_a), wkv_p, seq, tm)

    attn = _attn_call(q, k, v, batch, seq, tq)

    d_attn = N_HEADS * V_HEAD_DIM
    wo_pool = w_out[:d_pool].astype(BF16)
    wo_attn = w_out[d_pool:].reshape(N_HEADS, V_HEAD_DIM, d)
    wo_attn_p = jnp.concatenate([wo_attn, jnp.zeros((N_HEADS, HEAD_PAD - V_HEAD_DIM, d), F32)],
                                axis=1).reshape(N_HEADS * HEAD_PAD, d).astype(BF16)
    gattn_p = jnp.concatenate([g_out_attn.reshape(N_HEADS, V_HEAD_DIM),
                               jnp.zeros((N_HEADS, HEAD_PAD - V_HEAD_DIM), F32)], axis=1).reshape(1, -1)

    x1, hp, idx, gates, rank, counts = _outproj_call(
        pool_n, attn, x2, gattn_p, wo_pool, wo_attn_p, row(g_ffn_norm), w_router.T,
        b_router.reshape(-1, 1), tmo)

    n_assign = t * TOP_K
    cnt = counts[:, 0]
    padded = ((cnt + bm - 1) // bm) * bm
    pad_end = jnp.cumsum(padded)
    pad_start = pad_end - padded
    e_ids = jnp.arange(N_EXPERTS, dtype=jnp.int32)
    dest = rank + jnp.sum(jnp.where(idx[None] == e_ids[:, None, None], pad_start[:, None, None], 0),
                          axis=0)
    dest = dest.T.reshape(-1)
    n_rows = (n_assign // bm + N_EXPERTS) * bm
    blk_start = (pad_start // bm).astype(jnp.int32)
    nblk = (padded // bm).astype(jnp.int32)

    xs = _dispatch_call(blk_start, nblk, dest, hp, n_rows, d // 2 // LANE, td, bm)
    yb = _expert_call(blk_start, nblk, xs, w_gate_up, b_gate_up.reshape(N_EXPERTS, 1, -1),
                      w_down, b_down.reshape(N_EXPERTS, 1, -1), bm)
    return x1, dest, gates, yb


def _forward(x, positions, g_attn_norm, w_in, w_pool, b_pool, pool_scale, g_q_a, w_q_b, g_kv_a,
             w_kv_b, g_out_pool, g_out_attn, w_out, g_ffn_norm, w_router, b_router, w_gate_up,
             b_gate_up, w_down, b_down, g_final, tiles):
    batch, seq, d = x.shape
    t = batch * seq
    tm, tq, tmo, td, bm, tf = tiles
    x2 = x.reshape(t, d)
    pos3 = positions.reshape(t // tm, 1, tm)
    depth = g_attn_norm.shape[0]
    assert depth == 1
    l = 0
    x1, dest, gates, yb = _layer(
        x2, pos3, batch, seq, g_attn_norm[l], w_in[l], w_pool[l], b_pool[l], pool_scale[l],
        g_q_a[l], w_q_b[l], g_kv_a[l], w_kv_b[l], g_out_pool[l], g_out_attn[l], w_out[l],
        g_ffn_norm[l], w_router[l], b_router[l], w_gate_up[l], b_gate_up[l], w_down[l],
        b_down[l], tiles)
    out = _combine_call(dest, x1, gates.T, g_final.reshape(1, -1), yb, tf)
    return out.reshape(batch, seq, d)


def _tiles_for(seq):
    tm = min(512, seq)
    tq = min(512, seq // 2)
    tmo = min(512, seq)
    td = min(512, seq)
    bm = 256 if seq >= 2048 else 128
    tf = min(256, seq)
    return (tm, tq, tmo, td, bm, tf)


def kernel(x, positions, g_attn_norm, w_in, w_pool, b_pool, pool_scale, g_q_a, w_q_b, g_kv_a, w_kv_b, g_out_pool, g_out_attn, w_out, g_ffn_norm, w_router, b_router, w_gate_up, b_gate_up, w_down, b_down, g_final):
    tiles = _tiles_for(x.shape[1])
    return _forward(x, positions, g_attn_norm, w_in, w_pool, b_pool, pool_scale, g_q_a, w_q_b,
                    g_kv_a, w_kv_b, g_out_pool, g_out_attn, w_out, g_ffn_norm, w_router, b_router,
                    w_gate_up, b_gate_up, w_down, b_down, g_final, tiles)
```

```python
import functools

import jax
import jax.numpy as jnp
from jax import lax
from jax.experimental import pallas as pl
from jax.experimental.pallas import tpu as pltpu

POOL_WINDOWS = (2, 4, 8, 16)
N_POOL_GROUPS = 4
N_HEADS = 8
V_HEAD_DIM = 64
QK_NOPE_DIM = 64
QK_ROPE_DIM = 32
QK_DIM = QK_NOPE_DIM + QK_ROPE_DIM
ROPE_THETA = 10000.0
N_EXPERTS = 32
TOP_K = 4
SWIGLU_ALPHA = 1.702
SWIGLU_LIMIT = 7.0
RMS_EPS = 1e-6
LOG2_E = 1.4426950408889634

LANE = 128
HEAD_PAD = 128
POOL_HALO = 16

VMEM_LIMIT = 56 * 1024 * 1024

F32 = jnp.float32
BF16 = jnp.bfloat16


def _pack_bf16_pairs(x):
    n = x.shape[1] // 2
    xb = x.astype(BF16).astype(F32)
    lo = lax.bitcast_convert_type(xb[:, :n], jnp.uint32)
    hi = lax.bitcast_convert_type(xb[:, n:], jnp.uint32)
    return (lo >> 16) | (hi & jnp.uint32(0xFFFF0000))


def _unpack_bf16_pairs(w):
    lo = lax.bitcast_convert_type(w << 16, F32)
    hi = lax.bitcast_convert_type(w & jnp.uint32(0xFFFF0000), F32)
    return lo, hi


def _rms(x, g, n):
    ms = jnp.sum(x * x, axis=-1, keepdims=True) * (1.0 / n)
    return x * lax.rsqrt(ms + RMS_EPS) * g


def _proj_kernel(blocks_per_seq, x_ref, pos_ref, g_attn_ref, w_in_ref, inv_ref, wpool_ref,
                 bpool_ref, pscale_ref, gpool_ref, gq_ref, wq_ref, gkv_ref, wkv_ref,
                 pool_out, q_out, k_out, v_out, ext_scr):
    tm = x_ref.shape[0]
    d_model = x_ref.shape[1]
    d_pool = pool_out.shape[1]
    gdim = d_pool // N_POOL_GROUPS
    q_rank = gq_ref.shape[1]
    kv_rank = gkv_ref.shape[1]
    i = pl.program_id(0)
    blk_in_seq = i % blocks_per_seq

    x = x_ref[...]
    h = _rms(x, g_attn_ref[...], d_model).astype(BF16)
    proj = jnp.dot(h, w_in_ref[...], preferred_element_type=F32)
    u = proj[:, :d_pool]
    qlat = proj[:, d_pool:d_pool + q_rank]
    kvlat = proj[:, d_pool + q_rank:d_pool + q_rank + kv_rank]
    o_kr = d_pool + q_rank + kv_rank
    krb = proj[:, o_kr:o_kr + LANE]
    krs = proj[:, o_kr + LANE:]

    @pl.when(blk_in_seq == 0)
    def _():
        ext_scr[0:POOL_HALO, :] = jnp.zeros((POOL_HALO, d_pool), F32)

    @pl.when(blk_in_seq != 0)
    def _():
        ext_scr[0:POOL_HALO, :] = ext_scr[tm:tm + POOL_HALO, :]

    ext_scr[POOL_HALO:, :] = u
    t_in_seq = (blk_in_seq * tm + lax.broadcasted_iota(jnp.int32, (tm, gdim), 0) + 1).astype(F32)
    ys = []
    ssq = jnp.zeros((tm, 1), F32)
    for g, w in enumerate(POOL_WINDOWS):
        a = ext_scr[:, g * gdim:(g + 1) * gdim]
        shift = 1
        while shift < w:
            a = a + pltpu.roll(a, shift, axis=0)
            shift *= 2
        win = a[POOL_HALO:, :]
        cnt = jnp.minimum(t_in_seq, float(w))
        ug = u[:, g * gdim:(g + 1) * gdim]
        diff = (win / cnt - ug).astype(BF16)
        mixed = jnp.dot(diff, wpool_ref[g], preferred_element_type=F32) + bpool_ref[g]
        y = mixed * pscale_ref[g]
        ssq = ssq + jnp.sum(y * y, axis=-1, keepdims=True)
        ys.append(y)
    rinv = lax.rsqrt(ssq * (1.0 / d_pool) + RMS_EPS)
    for g in range(N_POOL_GROUPS):
        pool_out[:, g * gdim:(g + 1) * gdim] = (ys[g] * rinv * gpool_ref[g]).astype(pool_out.dtype)

    pos = pos_ref[0].astype(F32)
    ang_t = inv_ref[...] * pos
    cos_t = jnp.cos(ang_t)
    sin_t = jnp.sin(ang_t)
    half = QK_ROPE_DIM // 2
    ones = jnp.ones((QK_NOPE_DIM, tm), F32)
    c_t = jnp.concatenate([ones, cos_t, cos_t, jnp.ones((LANE - QK_DIM, tm), F32)], axis=0)
    s_t = jnp.concatenate([jnp.zeros((QK_NOPE_DIM, tm), F32), -sin_t, sin_t,
                           jnp.zeros((LANE - QK_DIM, tm), F32)], axis=0)
    c = c_t.T
    s = s_t.T

    def rope(blk, blk_swapped, c_, s_):
        return blk * c_ + blk_swapped * s_

    qn = _rms(qlat, gq_ref[...], q_rank).astype(BF16)
    q2 = jnp.dot(qn, wq_ref[...], preferred_element_type=F32)
    scale = QK_DIM ** -0.5 * LOG2_E
    cq, sq = c * scale, s * scale
    hp = N_HEADS * HEAD_PAD
    for hh in range(N_HEADS):
        sl = slice(hh * HEAD_PAD, (hh + 1) * HEAD_PAD)
        sw = slice(hp + hh * HEAD_PAD, hp + (hh + 1) * HEAD_PAD)
        q_out[:, sl] = rope(q2[:, sl], q2[:, sw], cq, sq).astype(q_out.dtype)

    kvn = _rms(kvlat, gkv_ref[...], kv_rank).astype(BF16)
    kv = jnp.dot(kvn, wkv_ref[...], preferred_element_type=F32)
    kr = rope(krb, krs, c, s)
    for hh in range(N_HEADS):
        sl = slice(hh * HEAD_PAD, (hh + 1) * HEAD_PAD)
        k_out[:, sl] = (kv[:, sl] + kr).astype(k_out.dtype)
    vlane = lax.broadcasted_iota(jnp.int32, (1, N_HEADS * HEAD_PAD), 1) % HEAD_PAD
    ones_col = (vlane == V_HEAD_DIM).astype(F32)
    v_out[0] = (kv[:, N_HEADS * HEAD_PAD:] + ones_col).T.astype(v_out.dtype)


def _proj_call(x2, pos3, g_attn, w_in_p, inv_col, wpool, bpool, pscale, gpool, gq, wq_p, gkv, wkv_p,
               seq, tm):
    t, d = x2.shape
    d_pool = pscale.shape[0] * pscale.shape[2]
    hp = N_HEADS * HEAD_PAD
    nblk = t // tm
    const2 = lambda i: (0, 0)
    const3 = lambda i: (0, 0, 0)
    in_specs = [
        pl.BlockSpec((tm, d), lambda i: (i, 0)),
        pl.BlockSpec((1, 1, tm), lambda i: (i, 0, 0)),
        pl.BlockSpec(g_attn.shape, const2),
        pl.BlockSpec(w_in_p.shape, const2),
        pl.BlockSpec(inv_col.shape, const2),
        pl.BlockSpec(wpool.shape, const3),
        pl.BlockSpec(bpool.shape, const3),
        pl.BlockSpec(pscale.shape, const3),
        pl.BlockSpec(gpool.shape, const3),
        pl.BlockSpec(gq.shape, const2),
        pl.BlockSpec(wq_p.shape, const2),
        pl.BlockSpec(gkv.shape, const2),
        pl.BlockSpec(wkv_p.shape, const2),
    ]
    out_shape = [
        jax.ShapeDtypeStruct((t, d_pool), BF16),
        jax.ShapeDtypeStruct((t, hp), BF16),
        jax.ShapeDtypeStruct((t, hp), BF16),
        jax.ShapeDtypeStruct((nblk, hp, tm), BF16),
    ]
    out_specs = [
        pl.BlockSpec((tm, d_pool), lambda i: (i, 0)),
        pl.BlockSpec((tm, hp), lambda i: (i, 0)),
        pl.BlockSpec((tm, hp), lambda i: (i, 0)),
        pl.BlockSpec((1, hp, tm), lambda i: (i, 0, 0)),
    ]
    return pl.pallas_call(
        functools.partial(_proj_kernel, seq // tm),
        grid=(nblk,),
        in_specs=in_specs,
        out_specs=out_specs,
        out_shape=out_shape,
        scratch_shapes=[pltpu.VMEM((tm + POOL_HALO, d_pool), F32)],
        compiler_params=pltpu.CompilerParams(dimension_semantics=("arbitrary",),
                                             vmem_limit_bytes=VMEM_LIMIT),
        name="proj",
    )(x2, pos3, g_attn, w_in_p, inv_col, wpool, bpool, pscale, gpool, gq, wq_p, gkv, wkv_p)


ROWSUM_ROW = V_HEAD_DIM


def _attn_kernel(q_ref, k_ref, vt_ref, o_ref, acc_ref):
    tq = q_ref.shape[0]
    th = vt_ref.shape[2]
    nsub = tq // th
    qi = pl.program_id(2)
    acc_ref[...] = jnp.zeros(acc_ref.shape, F32)

    def step(kc, m_prev, q_lo, masked):
        width = tq - q_lo
        start = pl.multiple_of(kc * th, th)
        k = k_ref[pl.ds(start, th), :]
        st = lax.dot_general(k, q_ref[q_lo:, :], (((1,), (1,)), ((), ())),
                             preferred_element_type=F32)
        if masked:
            key = lax.broadcasted_iota(jnp.int32, (th, width), 0)
            qry = lax.broadcasted_iota(jnp.int32, (th, width), 1)
            st = jnp.where(key <= qry, st, -jnp.inf)
        m_new = jnp.maximum(m_prev, jnp.max(st, axis=0, keepdims=True))
        alpha = jnp.exp2(m_prev - m_new)
        pt = jnp.exp2(st - m_new).astype(BF16)
        acc_ref[:, q_lo:] = alpha * acc_ref[:, q_lo:] + jnp.dot(vt_ref[kc], pt,
                                                                preferred_element_type=F32)
        return m_new

    def two_chunks(j, mp):
        return step(2 * j + 1, step(2 * j, mp, 0, False), 0, False)

    m = lax.fori_loop(0, (nsub // 2) * qi, two_chunks, jnp.full((1, tq), -jnp.inf, F32))
    for c in range(nsub):
        m = step(nsub * qi + c, m, c * th, True)[:, th:]

    acc = acc_ref[...]
    out = (acc / acc[ROWSUM_ROW:ROWSUM_ROW + 1, :]).T
    lane = lax.broadcasted_iota(jnp.int32, out.shape, 1)
    o_ref[...] = jnp.where(lane < V_HEAD_DIM, out, 0.0).astype(o_ref.dtype)


ATTN_Q_CHUNKS = 8


def _attn_call(q, k, vt, batch, seq, th):
    t, hp = q.shape
    tq = min(ATTN_Q_CHUNKS * th, seq)
    nq = seq // tq
    nchunk = seq // th
    return pl.pallas_call(
        _attn_kernel,
        grid=(batch, N_HEADS, nq),
        in_specs=[
            pl.BlockSpec((tq, HEAD_PAD), lambda b, h, qi: (b * nq + qi, h)),
            pl.BlockSpec((seq, HEAD_PAD), lambda b, h, qi: (b, h)),
            pl.BlockSpec((nchunk, HEAD_PAD, th), lambda b, h, qi: (b, h, 0)),
        ],
        out_specs=pl.BlockSpec((tq, HEAD_PAD), lambda b, h, qi: (b * nq + qi, h)),
        out_shape=jax.ShapeDtypeStruct((t, hp), F32),
        scratch_shapes=[pltpu.VMEM((HEAD_PAD, tq), F32)],
        compiler_params=pltpu.CompilerParams(
            dimension_semantics=("arbitrary", "arbitrary", "arbitrary"),
            vmem_limit_bytes=VMEM_LIMIT),
        name="attn",
    )(q, k, vt)


def _outproj_kernel(pool_ref, attn_ref, x_ref, gattn_ref, wop_ref, woa_ref, gffn_ref, wr_ref,
                    br_ref, x1_out, hp_out, idx_out, gate_out, rank_out, cnt_out, run_scr):
    tm, d = x_ref.shape
    d_attn = N_HEADS * V_HEAD_DIM
    i = pl.program_id(0)

    @pl.when(i == 0)
    def _():
        run_scr[...] = jnp.zeros(run_scr.shape, F32)

    attn = attn_ref[...]
    attn_n = _rms(attn, gattn_ref[...], d_attn).astype(BF16)
    mix = (jnp.dot(pool_ref[...], wop_ref[...], preferred_element_type=F32)
           + jnp.dot(attn_n, woa_ref[...], preferred_element_type=F32))
    x1 = x_ref[...] + mix
    x1_out[...] = x1
    h2 = _rms(x1, gffn_ref[...], d)

    word = _pack_bf16_pairs(h2)
    nw = d // 2 // LANE
    for c in range(nw):
        hp_out[pl.ds(c, tm, stride=nw), :] = word[:, c * LANE:(c + 1) * LANE]

    logits = lax.dot_general(wr_ref[...], h2, (((1,), (1,)), ((), ())),
                             preferred_element_type=F32,
                             precision=lax.Precision.HIGHEST) + br_ref[...]
    e_iota = lax.broadcasted_iota(jnp.int32, (N_EXPERTS, tm), 0)
    work = logits
    idxs, vals = [], []
    for _ in range(TOP_K):
        mx = jnp.max(work, axis=0, keepdims=True)
        sel = jnp.min(jnp.where(work == mx, e_iota, N_EXPERTS), axis=0, keepdims=True)
        idxs.append(sel)
        vals.append(mx)
        work = jnp.where(e_iota == sel, -jnp.inf, work)
    exps = [jnp.exp(v - vals[0]) for v in vals]
    denom = exps[0] + exps[1] + exps[2] + exps[3]
    gate_out[...] = jnp.concatenate([e / denom for e in exps], axis=0)
    idx_out[...] = jnp.concatenate(idxs, axis=0)

    onehot = jnp.zeros((N_EXPERTS, tm), F32)
    for sel in idxs:
        onehot = onehot + (e_iota == sel).astype(F32)
    r_io = lax.broadcasted_iota(jnp.int32, (tm, tm), 0)
    c_io = lax.broadcasted_iota(jnp.int32, (tm, tm), 1)
    upper = (r_io < c_io).astype(BF16)
    prefix = jnp.dot(onehot.astype(BF16), upper, preferred_element_type=F32) + run_scr[:, 0:1]
    ranks = [jnp.sum(jnp.where(e_iota == sel, prefix, 0.0), axis=0, keepdims=True) for sel in idxs]
    rank_out[...] = jnp.concatenate(ranks, axis=0).astype(jnp.int32)
    run_new = run_scr[...] + jnp.sum(onehot, axis=1, keepdims=True)
    run_scr[...] = run_new
    cnt_out[...] = run_new.astype(jnp.int32)


def _outproj_call(pool_n, attn, x2, gattn_p, wo_pool, wo_attn_p, gffn, wr_t, br_col, tm):
    t, d = x2.shape
    nblk = t // tm
    const2 = lambda i: (0, 0)
    row = lambda i: (i, 0)
    col = lambda i: (0, i)
    in_specs = [
        pl.BlockSpec((tm, pool_n.shape[1]), row),
        pl.BlockSpec((tm, attn.shape[1]), row),
        pl.BlockSpec((tm, d), row),
        pl.BlockSpec(gattn_p.shape, const2),
        pl.BlockSpec(wo_pool.shape, const2),
        pl.BlockSpec(wo_attn_p.shape, const2),
        pl.BlockSpec(gffn.shape, const2),
        pl.BlockSpec(wr_t.shape, const2),
        pl.BlockSpec(br_col.shape, const2),
    ]
    out_shape = [
        jax.ShapeDtypeStruct((t, d), F32),
        jax.ShapeDtypeStruct((t * (d // 2 // LANE), LANE), jnp.uint32),
        jax.ShapeDtypeStruct((TOP_K, t), jnp.int32),
        jax.ShapeDtypeStruct((TOP_K, t), F32),
        jax.ShapeDtypeStruct((TOP_K, t), jnp.int32),
        jax.ShapeDtypeStruct((N_EXPERTS, LANE), jnp.int32),
    ]
    out_specs = [
        pl.BlockSpec((tm, d), row),
        pl.BlockSpec((tm * (d // 2 // LANE), LANE), row),
        pl.BlockSpec((TOP_K, tm), col),
        pl.BlockSpec((TOP_K, tm), col),
        pl.BlockSpec((TOP_K, tm), col),
        pl.BlockSpec((N_EXPERTS, LANE), const2),
    ]
    return pl.pallas_call(
        _outproj_kernel,
        grid=(nblk,),
        in_specs=in_specs,
        out_specs=out_specs,
        out_shape=out_shape,
        scratch_shapes=[pltpu.VMEM((N_EXPERTS, LANE), F32)],
        compiler_params=pltpu.CompilerParams(dimension_semantics=("arbitrary",),
                                             vmem_limit_bytes=VMEM_LIMIT),
        name="outproj",
    )(pool_n, attn, x2, gattn_p, wo_pool, wo_attn_p, gffn, wr_t, br_col)


ROW_UNROLL = 8


def _dispatch_kernel(nw, bs_ref, nblk_ref, dest_ref, h_ref, xs_out, zbuf, sem, zsem):
    td = h_ref.shape[0] // nw
    blk_rows = zbuf.shape[0]
    n_exp = bs_ref.shape[0]

    @pl.when(pl.program_id(0) == 0)
    def _():
        zbuf[...] = jnp.zeros(zbuf.shape, zbuf.dtype)
        total = xs_out.shape[0] // blk_rows
        first_tail = bs_ref[n_exp - 1] + nblk_ref[n_exp - 1]

        def z_copy(blk):
            row0 = pl.multiple_of(blk * blk_rows, blk_rows)
            return pltpu.make_async_copy(zbuf, xs_out.at[pl.ds(row0, blk_rows)], zsem)

        def each_partial_block(fn):
            def per_expert(e, c):
                @pl.when(nblk_ref[e] > 0)
                def _():
                    fn(z_copy(bs_ref[e] + nblk_ref[e] - 1))
                return c

            def per_tail(b, c):
                fn(z_copy(b))
                return c

            lax.fori_loop(0, n_exp, per_expert, 0)
            lax.fori_loop(first_tail, total, per_tail, 0)

        each_partial_block(lambda cp: cp.start())
        each_partial_block(lambda cp: cp.wait())

    def body(g, carry):
        base = g * ROW_UNROLL
        for u in range(ROW_UNROLL):
            i = base + u
            src = h_ref.at[pl.ds(pl.multiple_of(i * nw, nw), nw)]
            for k in range(TOP_K):
                d = dest_ref[i * TOP_K + k]
                pltpu.make_async_copy(src, xs_out.at[pl.ds(pl.multiple_of(d * nw, nw), nw)],
                                      sem).start(priority=k % 2)
        return carry

    lax.fori_loop(0, td // ROW_UNROLL, body, 0)
    for _ in range(TOP_K):
        pltpu.make_async_copy(h_ref, xs_out.at[pl.ds(0, td * nw)], sem).wait()


def _dispatch_call(blk_start, nblk, dest, hp, n_rows, nw, td, bm):
    t = hp.shape[0] // nw
    grid_spec = pltpu.PrefetchScalarGridSpec(
        num_scalar_prefetch=2,
        grid=(t // td,),
        in_specs=[
            pl.BlockSpec((TOP_K * td,), lambda i, bs, nb: (i,), memory_space=pltpu.SMEM),
            pl.BlockSpec((td * nw, LANE), lambda i, bs, nb: (i, 0)),
        ],
        out_specs=pl.BlockSpec(memory_space=pl.ANY),
        scratch_shapes=[pltpu.VMEM((bm * nw, LANE), hp.dtype), pltpu.SemaphoreType.DMA,
                        pltpu.SemaphoreType.DMA],
    )
    return pl.pallas_call(
        functools.partial(_dispatch_kernel, nw),
        grid_spec=grid_spec,
        out_shape=jax.ShapeDtypeStruct((n_rows * nw, LANE), hp.dtype),
        compiler_params=pltpu.CompilerParams(dimension_semantics=("arbitrary",),
                                             vmem_limit_bytes=VMEM_LIMIT),
        name="dispatch",
    )(blk_start, nblk, dest, hp)


CAST_ROWS = 64


def _expert_kernel(bs_ref, nblk_ref, bgu_ref, bd_ref, wgu_hbm, wd_hbm, xs_hbm, y_hbm,
                   wgu_f32, wd_f32, wgu_bf, wd_bf, xbuf, ybuf, xsem, ysem, wsem):
    e = pl.program_id(0)
    n_exp = pl.num_programs(0)
    d = wgu_f32.shape[1]
    f = wd_f32.shape[1]
    nw = d // 2 // LANE
    bm = xbuf.shape[1] // nw
    n = nblk_ref[e]
    blk0 = bs_ref[e]
    wslot = e % 2

    def w_copies(ex, slot):
        return (pltpu.make_async_copy(wgu_hbm.at[ex], wgu_f32.at[slot], wsem.at[0, slot]),
                pltpu.make_async_copy(wd_hbm.at[ex], wd_f32.at[slot], wsem.at[1, slot]))

    def x_copy(j, slot):
        row0 = pl.multiple_of((blk0 + j) * (bm * nw), bm * nw)
        return pltpu.make_async_copy(xs_hbm.at[pl.ds(row0, bm * nw)], xbuf.at[slot], xsem.at[slot])

    def y_copy(blk, slot):
        row0 = pl.multiple_of(blk * (bm * nw), bm * nw)
        return pltpu.make_async_copy(ybuf.at[slot], y_hbm.at[pl.ds(row0, bm * nw)], ysem.at[slot])

    @pl.when(n > 0)
    def _():
        x_copy(0, 0).start()

    @pl.when(e == 0)
    def _():
        for cp in w_copies(0, 0):
            cp.start(priority=1)

    @pl.when(e + 1 < n_exp)
    def _():
        for cp in w_copies(e + 1, 1 - wslot):
            cp.start(priority=1)

    for cp in w_copies(e, wslot):
        cp.wait()

    @pl.when(n > 0)
    def _():
        def cast_gu(r, c):
            rows = pl.ds(pl.multiple_of(r * CAST_ROWS, CAST_ROWS), CAST_ROWS)
            wgu_bf[rows, :] = wgu_f32[wslot, rows, :].astype(BF16)
            return c

        def cast_d(r, c):
            rows = pl.ds(pl.multiple_of(r * CAST_ROWS, CAST_ROWS), CAST_ROWS)
            wd_bf[rows, :] = wd_f32[wslot, rows, :].astype(BF16)
            return c

        lax.fori_loop(0, d // CAST_ROWS, cast_gu, 0)
        lax.fori_loop(0, f // CAST_ROWS, cast_d, 0)

        def block(j, c):
            slot = j % 2
            x_copy(j, slot).wait()

            @pl.when(j + 1 < n)
            def _():
                x_copy(j + 1, 1 - slot).start()

            @pl.when(j >= 2)
            def _():
                y_copy(blk0 + j - 2, slot).wait()

            w = jnp.concatenate([xbuf[slot, pl.ds(c, bm, stride=nw), :] for c in range(nw)], axis=1)
            x_lo, x_hi = _unpack_bf16_pairs(w)
            gu = (jnp.dot(x_lo.astype(BF16), wgu_bf[:d // 2, :], preferred_element_type=F32)
                  + jnp.dot(x_hi.astype(BF16), wgu_bf[d // 2:, :], preferred_element_type=F32)
                  + bgu_ref[0])
            gate = jnp.minimum(gu[:, :f], SWIGLU_LIMIT)
            up = jnp.clip(gu[:, f:], -SWIGLU_LIMIT, SWIGLU_LIMIT)
            act = gate * jax.nn.sigmoid(SWIGLU_ALPHA * gate) * (up + 1.0)
            y = jnp.dot(act.astype(BF16), wd_bf[...], preferred_element_type=F32) + bd_ref[0]
            yw = _pack_bf16_pairs(y)
            for c in range(nw):
                ybuf[slot, pl.ds(c, bm, stride=nw), :] = yw[:, c * LANE:(c + 1) * LANE]
            y_copy(blk0 + j, slot).start()
            return c

        lax.fori_loop(0, n, block, 0)

        @pl.when(n >= 2)
        def _():
            y_copy(blk0 + n - 2, n % 2).wait()

        y_copy(blk0 + n - 1, (n - 1) % 2).wait()

    @pl.when(e == n_exp - 1)
    def _():
        first = blk0 + n
        total = y_hbm.shape[0] // (bm * nw)
        ybuf[0] = jnp.zeros(ybuf.shape[1:], ybuf.dtype)

        def fill(b, c):
            y_copy(b, 0).start()
            return c

        def drain(b, c):
            y_copy(b, 0).wait()
            return c

        lax.fori_loop(first, total, fill, 0)
        lax.fori_loop(first, total, drain, 0)


def _expert_call(blk_start, nblk, xs, wgu, bgu, wd, bd, bm):
    e, d, f2 = wgu.shape
    f = wd.shape[1]
    nw = d // 2 // LANE
    r = xs.shape[0] // nw

    def bsel(i, bs, nb):
        return (i, 0, 0)

    grid_spec = pltpu.PrefetchScalarGridSpec(
        num_scalar_prefetch=2,
        grid=(e,),
        in_specs=[
            pl.BlockSpec((1, 1, f2), bsel),
            pl.BlockSpec((1, 1, d), bsel),
            pl.BlockSpec(memory_space=pl.ANY),
            pl.BlockSpec(memory_space=pl.ANY),
            pl.BlockSpec(memory_space=pl.ANY),
        ],
        out_specs=pl.BlockSpec(memory_space=pl.ANY),
        scratch_shapes=[pltpu.VMEM((2, d, f2), F32), pltpu.VMEM((2, f, d), F32),
                        pltpu.VMEM((d, f2), BF16), pltpu.VMEM((f, d), BF16),
                        pltpu.VMEM((2, bm * nw, LANE), xs.dtype),
                        pltpu.VMEM((2, bm * nw, LANE), jnp.uint32),
                        pltpu.SemaphoreType.DMA((2,)), pltpu.SemaphoreType.DMA((2,)),
                        pltpu.SemaphoreType.DMA((2, 2))],
    )
    return pl.pallas_call(
        _expert_kernel,
        grid_spec=grid_spec,
        out_shape=jax.ShapeDtypeStruct((r * nw, LANE), jnp.uint32),
        compiler_params=pltpu.CompilerParams(dimension_semantics=("arbitrary",),
                                             vmem_limit_bytes=VMEM_LIMIT),
        name="experts",
    )(blk_start, nblk, bgu, bd, wgu, wd, xs)


def _combine_kernel(dest_ref, dest_next_ref, x1_ref, gate_ref, gfin_ref, yb_hbm, o_ref, ybuf, sem):
    tf, d = x1_ref.shape
    ns = d // 2 // LANE
    j = pl.program_id(0)
    nsteps = pl.num_programs(0)
    slot = j % 2

    def issue(idx_ref, s):
        def body(g, carry):
            base = g * ROW_UNROLL
            for u in range(ROW_UNROLL):
                i = base + u
                for k in range(TOP_K):
                    r = idx_ref[i * TOP_K + k]
                    pltpu.make_async_copy(yb_hbm.at[pl.ds(pl.multiple_of(r * ns, ns), ns)],
                                          ybuf.at[s, k, pl.ds(pl.multiple_of(i * ns, ns), ns)],
                                          sem.at[s]).start(priority=k % 2)
            return carry

        lax.fori_loop(0, tf // ROW_UNROLL, body, 0)

    @pl.when(j == 0)
    def _():
        issue(dest_ref, 0)

    @pl.when(j + 1 < nsteps)
    def _():
        issue(dest_next_ref, 1 - slot)

    for k in range(TOP_K):
        pltpu.make_async_copy(yb_hbm.at[pl.ds(0, tf * ns)], ybuf.at[slot, k], sem.at[slot]).wait()
    g = gate_ref[...]
    y = x1_ref[...]
    for k in range(TOP_K):
        yw = jnp.concatenate([ybuf[slot, k, pl.ds(c, tf, stride=ns), :] for c in range(ns)], axis=1)
        y = y + g[:, k:k + 1] * jnp.concatenate(_unpack_bf16_pairs(yw), axis=1)
    o_ref[...] = _rms(y, gfin_ref[...], d)


def _combine_call(dest, x1, gates_t, gfin, yb, tf):
    t, d = x1.shape
    nsteps = t // tf
    return pl.pallas_call(
        _combine_kernel,
        grid=(nsteps,),
        in_specs=[
            pl.BlockSpec((TOP_K * tf,), lambda i: (i,), memory_space=pltpu.SMEM),
            pl.BlockSpec((TOP_K * tf,), lambda i: (jnp.minimum(i + 1, nsteps - 1),),
                         memory_space=pltpu.SMEM),
            pl.BlockSpec((tf, d), lambda i: (i, 0)),
            pl.BlockSpec((tf, TOP_K), lambda i: (i, 0)),
            pl.BlockSpec(gfin.shape, lambda i: (0, 0)),
            pl.BlockSpec(memory_space=pl.ANY),
        ],
        out_specs=pl.BlockSpec((tf, d), lambda i: (i, 0)),
        out_shape=jax.ShapeDtypeStruct((t, d), F32),
        scratch_shapes=[pltpu.VMEM((2, TOP_K, tf * (d // 2 // LANE), LANE), jnp.uint32),
                        pltpu.SemaphoreType.DMA((2,))],
        compiler_params=pltpu.CompilerParams(dimension_semantics=("arbitrary",),
                                             vmem_limit_bytes=VMEM_LIMIT),
        name="combine",
    )(dest, dest, x1, gates_t, gfin, yb)


def _pad_head_cols(w, widths_in, place):
    k = w.shape[0]
    per = sum(widths_in)
    blocks = []
    for hh in range(N_HEADS):
        blocks.append(place(w[:, hh * per:(hh + 1) * per]))
    return jnp.concatenate(blocks, axis=1)


def _layer(x2, pos3, batch, seq, g_attn_norm, w_in, w_pool, b_pool, pool_scale, g_q_a, w_q_b,
           g_kv_a, w_kv_b, g_out_pool, g_out_attn, w_out, g_ffn_norm, w_router, b_router,
           w_gate_up, b_gate_up, w_down, b_down, tiles):
    tm, tq, tmo, td, bm, tf = tiles
    t, d = x2.shape
    d_pool = pool_scale.shape[0]
    gdim = d_pool // N_POOL_GROUPS
    q_rank = g_q_a.shape[0]
    kv_rank = g_kv_a.shape[0]
    half = QK_ROPE_DIM // 2

    o1, o2, o3 = d_pool, d_pool + q_rank, d_pool + q_rank + kv_rank
    zk = lambda n: jnp.zeros((d, n), F32)
    w_kr1, w_kr2 = w_in[:, o3:o3 + half], w_in[:, o3 + half:]
    w_in_p = jnp.concatenate([w_in[:, :o3], zk(QK_NOPE_DIM), w_kr1, w_kr2, zk(LANE - QK_DIM),
                              zk(QK_NOPE_DIM), w_kr2, w_kr1, zk(LANE - QK_DIM)], axis=1).astype(BF16)
    zq = lambda n: jnp.zeros((q_rank, n), F32)
    wq_plain = _pad_head_cols(w_q_b, (QK_DIM,),
                              lambda c: jnp.concatenate([c, zq(LANE - QK_DIM)], axis=1))
    wq_swap = _pad_head_cols(w_q_b, (QK_DIM,), lambda c: jnp.concatenate(
        [zq(QK_NOPE_DIM), c[:, QK_NOPE_DIM + half:], c[:, QK_NOPE_DIM:QK_NOPE_DIM + half],
         zq(LANE - QK_DIM)], axis=1))
    wq_p = jnp.concatenate([wq_plain, wq_swap], axis=1).astype(BF16)
    zkv = lambda n: jnp.zeros((kv_rank, n), F32)
    per = QK_NOPE_DIM + V_HEAD_DIM
    wk_p = _pad_head_cols(w_kv_b, (per,), lambda c: jnp.concatenate(
        [c[:, :QK_NOPE_DIM], zkv(LANE - QK_NOPE_DIM)], axis=1))
    wv_p = _pad_head_cols(w_kv_b, (per,), lambda c: jnp.concatenate(
        [c[:, QK_NOPE_DIM:], zkv(LANE - V_HEAD_DIM)], axis=1))
    wkv_p = jnp.concatenate([wk_p, wv_p], axis=1).astype(BF16)
    inv = ROPE_THETA ** (-jnp.arange(half, dtype=F32) / half)
    inv_col = inv.reshape(half, 1)

    row = lambda v: v.reshape(1, -1)
    grp = lambda v: v.reshape(N_POOL_GROUPS, 1, gdim)

    pool_n, q, k, v = _proj_call(
        x2, pos3, row(g_attn_norm), w_in_p, inv_col, w_pool.astype(BF16), grp(b_pool),
        grp(pool_scale), grp(g_out_pool), row(g_q_a), wq_p, row(g_kv_a), wkv_p, seq, tm)

    attn = _attn_call(q, k, v, batch, seq, tq)

    d_attn = N_HEADS * V_HEAD_DIM
    wo_pool = w_out[:d_pool].astype(BF16)
    wo_attn = w_out[d_pool:].reshape(N_HEADS, V_HEAD_DIM, d)
    wo_attn_p = jnp.concatenate([wo_attn, jnp.zeros((N_HEADS, HEAD_PAD - V_HEAD_DIM, d), F32)],
                                axis=1).reshape(N_HEADS * HEAD_PAD, d).astype(BF16)
    gattn_p = jnp.concatenate([g_out_attn.reshape(N_HEADS, V_HEAD_DIM),
                               jnp.zeros((N_HEADS, HEAD_PAD - V_HEAD_DIM), F32)], axis=1).reshape(1, -1)

    x1, hp, idx, gates, rank, counts = _outproj_call(
        pool_n, attn, x2, gattn_p, wo_pool, wo_attn_p, row(g_ffn_norm), w_router.T,
        b_router.reshape(-1, 1), tmo)

    n_assign = t * TOP_K
    cnt = counts[:, 0]
    padded = ((cnt + bm - 1) // bm) * bm
    pad_end = jnp.cumsum(padded)
    pad_start = pad_end - padded
    e_ids = jnp.arange(N_EXPERTS, dtype=jnp.int32)
    dest = rank + jnp.sum(jnp.where(idx[None] == e_ids[:, None, None], pad_start[:, None, None], 0),
                          axis=0)
    dest = dest.T.reshape(-1)
    n_rows = (n_assign // bm + N_EXPERTS) * bm
    blk_start = (pad_start // bm).astype(jnp.int32)
    nblk = (padded // bm).astype(jnp.int32)

    xs = _dispatch_call(blk_start, nblk, dest, hp, n_rows, d // 2 // LANE, td, bm)
    yb = _expert_call(blk_start, nblk, xs, w_gate_up, b_gate_up.reshape(N_EXPERTS, 1, -1),
                      w_down, b_down.reshape(N_EXPERTS, 1, -1), bm)
    return x1, dest, gates, yb


def _forward(x, positions, g_attn_norm, w_in, w_pool, b_pool, pool_scale, g_q_a, w_q_b, g_kv_a,
             w_kv_b, g_out_pool, g_out_attn, w_out, g_ffn_norm, w_router, b_router, w_gate_up,
             b_gate_up, w_down, b_down, g_final, tiles):
    batch, seq, d = x.shape
    t = batch * seq
    tm, tq, tmo, td, bm, tf = tiles
    x2 = x.reshape(t, d)
    pos3 = positions.reshape(t // tm, 1, tm)
    depth = g_attn_norm.shape[0]
    assert depth == 1
    l = 0
    x1, dest, gates, yb = _layer(
        x2, pos3, batch, seq, g_attn_norm[l], w_in[l], w_pool[l], b_pool[l], pool_scale[l],
        g_q_a[l], w_q_b[l], g_kv_a[l], w_kv_b[l], g_out_pool[l], g_out_attn[l], w_out[l],
        g_ffn_norm[l], w_router[l], b_router[l], w_gate_up[l], b_gate_up[l], w_down[l],
        b_down[l], tiles)
    out = _combine_call(dest, x1, gates.T, g_final.reshape(1, -1), yb, tf)
    return out.reshape(batch, seq, d)


def _tiles_for(seq):
    tm = min(512, seq)
    tq = min(512, seq // 2)
    tmo = min(512, seq)
    td = min(512, seq)
    bm = 256 if seq >= 2048 else 128
    tf = min(256, seq)
    return (tm, tq, tmo, td, bm, tf)


def kernel(x, positions, g_attn_norm, w_in, w_pool, b_pool, pool_scale, g_q_a, w_q_b, g_kv_a, w_kv_b, g_out_pool, g_out_attn, w_out, g_ffn_norm, w_router, b_router, w_gate_up, b_gate_up, w_down, b_down, g_final):
    tiles = _tiles_for(x.shape[1])
    return _forward(x, positions, g_attn_norm, w_in, w_pool, b_pool, pool_scale, g_q_a, w_q_b,
                    g_kv_a, w_kv_b, g_out_pool, g_out_attn, w_out, g_ffn_norm, w_router, b_router,
                    w_gate_up, b_gate_up, w_down, b_down, g_final, tiles)
```

```python
import functools

import jax
import jax.numpy as jnp
from jax import lax
from jax.experimental import pallas as pl
from jax.experimental.pallas import tpu as pltpu
from jax.experimental.pallas import tpu_sc as plsc

POOL_WINDOWS = (2, 4, 8, 16)
N_POOL_GROUPS = 4
N_HEADS = 8
V_HEAD_DIM = 64
QK_NOPE_DIM = 64
QK_ROPE_DIM = 32
QK_DIM = QK_NOPE_DIM + QK_ROPE_DIM
ROPE_THETA = 10000.0
N_EXPERTS = 32
TOP_K = 4
SWIGLU_ALPHA = 1.702
SWIGLU_LIMIT = 7.0
RMS_EPS = 1e-6
LOG2_E = 1.4426950408889634

LANE = 128
HEAD_PAD = 128
POOL_HALO = 16

VMEM_LIMIT = 56 * 1024 * 1024

F32 = jnp.float32
BF16 = jnp.bfloat16


def _pack_bf16_pairs(x):
    n = x.shape[1] // 2
    xb = x.astype(BF16).astype(F32)
    lo = lax.bitcast_convert_type(xb[:, :n], jnp.uint32)
    hi = lax.bitcast_convert_type(xb[:, n:], jnp.uint32)
    return (lo >> 16) | (hi & jnp.uint32(0xFFFF0000))


def _unpack_bf16_pairs(w):
    lo = lax.bitcast_convert_type(w << 16, F32)
    hi = lax.bitcast_convert_type(w & jnp.uint32(0xFFFF0000), F32)
    return lo, hi


def _rms(x, g, n):
    ms = jnp.sum(x * x, axis=-1, keepdims=True) * (1.0 / n)
    return x * lax.rsqrt(ms + RMS_EPS) * g


def _proj_kernel(blocks_per_seq, x_ref, pos_ref, g_attn_ref, w_in_ref, inv_ref, wpool_ref,
                 bpool_ref, pscale_ref, gpool_ref, gq_ref, wq_ref, gkv_ref, wkv_ref,
                 pool_out, q_out, k_out, v_out, ext_scr):
    tm = x_ref.shape[0]
    d_model = x_ref.shape[1]
    d_pool = pool_out.shape[1]
    gdim = d_pool // N_POOL_GROUPS
    q_rank = gq_ref.shape[1]
    kv_rank = gkv_ref.shape[1]
    i = pl.program_id(0)
    blk_in_seq = i % blocks_per_seq

    x = x_ref[...]
    h = _rms(x, g_attn_ref[...], d_model).astype(BF16)
    proj = jnp.dot(h, w_in_ref[...], preferred_element_type=F32)
    u = proj[:, :d_pool]
    qlat = proj[:, d_pool:d_pool + q_rank]
    kvlat = proj[:, d_pool + q_rank:d_pool + q_rank + kv_rank]
    o_kr = d_pool + q_rank + kv_rank
    krb = proj[:, o_kr:o_kr + LANE]
    krs = proj[:, o_kr + LANE:]

    @pl.when(blk_in_seq == 0)
    def _():
        ext_scr[0:POOL_HALO, :] = jnp.zeros((POOL_HALO, d_pool), F32)

    @pl.when(blk_in_seq != 0)
    def _():
        ext_scr[0:POOL_HALO, :] = ext_scr[tm:tm + POOL_HALO, :]

    ext_scr[POOL_HALO:, :] = u
    t_in_seq = (blk_in_seq * tm + lax.broadcasted_iota(jnp.int32, (tm, gdim), 0) + 1).astype(F32)
    ys = []
    ssq = jnp.zeros((tm, 1), F32)
    for g, w in enumerate(POOL_WINDOWS):
        a = ext_scr[:, g * gdim:(g + 1) * gdim]
        shift = 1
        while shift < w:
            a = a + pltpu.roll(a, shift, axis=0)
            shift *= 2
        win = a[POOL_HALO:, :]
        cnt = jnp.minimum(t_in_seq, float(w))
        ug = u[:, g * gdim:(g + 1) * gdim]
        diff = (win / cnt - ug).astype(BF16)
        mixed = jnp.dot(diff, wpool_ref[g], preferred_element_type=F32) + bpool_ref[g]
        y = mixed * pscale_ref[g]
        ssq = ssq + jnp.sum(y * y, axis=-1, keepdims=True)
        ys.append(y)
    rinv = lax.rsqrt(ssq * (1.0 / d_pool) + RMS_EPS)
    for g in range(N_POOL_GROUPS):
        pool_out[:, g * gdim:(g + 1) * gdim] = (ys[g] * rinv * gpool_ref[g]).astype(pool_out.dtype)

    pos = pos_ref[0].astype(F32)
    ang_t = inv_ref[...] * pos
    cos_t = jnp.cos(ang_t)
    sin_t = jnp.sin(ang_t)
    half = QK_ROPE_DIM // 2
    ones = jnp.ones((QK_NOPE_DIM, tm), F32)
    c_t = jnp.concatenate([ones, cos_t, cos_t, jnp.ones((LANE - QK_DIM, tm), F32)], axis=0)
    s_t = jnp.concatenate([jnp.zeros((QK_NOPE_DIM, tm), F32), -sin_t, sin_t,
                           jnp.zeros((LANE - QK_DIM, tm), F32)], axis=0)
    c = c_t.T
    s = s_t.T

    def rope(blk, blk_swapped, c_, s_):
        return blk * c_ + blk_swapped * s_

    qn = _rms(qlat, gq_ref[...], q_rank).astype(BF16)
    q2 = jnp.dot(qn, wq_ref[...], preferred_element_type=F32)
    scale = QK_DIM ** -0.5 * LOG2_E
    cq, sq = c * scale, s * scale
    hp = N_HEADS * HEAD_PAD
    for hh in range(N_HEADS):
        sl = slice(hh * HEAD_PAD, (hh + 1) * HEAD_PAD)
        sw = slice(hp + hh * HEAD_PAD, hp + (hh + 1) * HEAD_PAD)
        q_out[:, sl] = rope(q2[:, sl], q2[:, sw], cq, sq).astype(q_out.dtype)

    kvn = _rms(kvlat, gkv_ref[...], kv_rank).astype(BF16)
    kv = jnp.dot(kvn, wkv_ref[...], preferred_element_type=F32)
    kr = rope(krb, krs, c, s)
    for hh in range(N_HEADS):
        sl = slice(hh * HEAD_PAD, (hh + 1) * HEAD_PAD)
        k_out[:, sl] = (kv[:, sl] + kr).astype(k_out.dtype)
    vlane = lax.broadcasted_iota(jnp.int32, (1, N_HEADS * HEAD_PAD), 1) % HEAD_PAD
    ones_col = (vlane == V_HEAD_DIM).astype(F32)
    v_out[0] = (kv[:, N_HEADS * HEAD_PAD:] + ones_col).T.astype(v_out.dtype)


def _proj_call(x2, pos3, g_attn, w_in_p, inv_col, wpool, bpool, pscale, gpool, gq, wq_p, gkv, wkv_p,
               seq, tm):
    t, d = x2.shape
    d_pool = pscale.shape[0] * pscale.shape[2]
    hp = N_HEADS * HEAD_PAD
    nblk = t // tm
    const2 = lambda i: (0, 0)
    const3 = lambda i: (0, 0, 0)
    in_specs = [
        pl.BlockSpec((tm, d), lambda i: (i, 0)),
        pl.BlockSpec((1, 1, tm), lambda i: (i, 0, 0)),
        pl.BlockSpec(g_attn.shape, const2),
        pl.BlockSpec(w_in_p.shape, const2),
        pl.BlockSpec(inv_col.shape, const2),
        pl.BlockSpec(wpool.shape, const3),
        pl.BlockSpec(bpool.shape, const3),
        pl.BlockSpec(pscale.shape, const3),
        pl.BlockSpec(gpool.shape, const3),
        pl.BlockSpec(gq.shape, const2),
        pl.BlockSpec(wq_p.shape, const2),
        pl.BlockSpec(gkv.shape, const2),
        pl.BlockSpec(wkv_p.shape, const2),
    ]
    out_shape = [
        jax.ShapeDtypeStruct((t, d_pool), BF16),
        jax.ShapeDtypeStruct((t, hp), BF16),
        jax.ShapeDtypeStruct((t, hp), BF16),
        jax.ShapeDtypeStruct((nblk, hp, tm), BF16),
    ]
    out_specs = [
        pl.BlockSpec((tm, d_pool), lambda i: (i, 0)),
        pl.BlockSpec((tm, hp), lambda i: (i, 0)),
        pl.BlockSpec((tm, hp), lambda i: (i, 0)),
        pl.BlockSpec((1, hp, tm), lambda i: (i, 0, 0)),
    ]
    return pl.pallas_call(
        functools.partial(_proj_kernel, seq // tm),
        grid=(nblk,),
        in_specs=in_specs,
        out_specs=out_specs,
        out_shape=out_shape,
        scratch_shapes=[pltpu.VMEM((tm + POOL_HALO, d_pool), F32)],
        compiler_params=pltpu.CompilerParams(dimension_semantics=("arbitrary",),
                                             vmem_limit_bytes=VMEM_LIMIT),
        name="proj",
    )(x2, pos3, g_attn, w_in_p, inv_col, wpool, bpool, pscale, gpool, gq, wq_p, gkv, wkv_p)


ROWSUM_ROW = V_HEAD_DIM


def _attn_kernel(q_ref, k_ref, vt_ref, o_ref, acc_ref):
    tq = q_ref.shape[0]
    th = vt_ref.shape[2]
    nsub = tq // th
    qi = pl.program_id(2)
    acc_ref[...] = jnp.zeros(acc_ref.shape, F32)

    def step(kc, m_prev, q_lo, masked):
        width = tq - q_lo
        start = pl.multiple_of(kc * th, th)
        k = k_ref[pl.ds(start, th), :]
        st = lax.dot_general(k, q_ref[q_lo:, :], (((1,), (1,)), ((), ())),
                             preferred_element_type=F32)
        if masked:
            key = lax.broadcasted_iota(jnp.int32, (th, width), 0)
            qry = lax.broadcasted_iota(jnp.int32, (th, width), 1)
            st = jnp.where(key <= qry, st, -jnp.inf)
        m_new = jnp.maximum(m_prev, jnp.max(st, axis=0, keepdims=True))
        alpha = jnp.exp2(m_prev - m_new)
        pt = jnp.exp2(st - m_new).astype(BF16)
        acc_ref[:, q_lo:] = alpha * acc_ref[:, q_lo:] + jnp.dot(vt_ref[kc], pt,
                                                                preferred_element_type=F32)
        return m_new

    def two_chunks(j, mp):
        return step(2 * j + 1, step(2 * j, mp, 0, False), 0, False)

    m = lax.fori_loop(0, (nsub // 2) * qi, two_chunks, jnp.full((1, tq), -jnp.inf, F32))
    for c in range(nsub):
        m = step(nsub * qi + c, m, c * th, True)[:, th:]

    acc = acc_ref[...]
    out = (acc / acc[ROWSUM_ROW:ROWSUM_ROW + 1, :]).T
    lane = lax.broadcasted_iota(jnp.int32, out.shape, 1)
    o_ref[...] = jnp.where(lane < V_HEAD_DIM, out, 0.0).astype(o_ref.dtype)


ATTN_Q_CHUNKS = 8


def _attn_call(q, k, vt, batch, seq, th):
    t, hp = q.shape
    tq = min(ATTN_Q_CHUNKS * th, seq)
    nq = seq // tq
    nchunk = seq // th
    return pl.pallas_call(
        _attn_kernel,
        grid=(batch, N_HEADS, nq),
        in_specs=[
            pl.BlockSpec((tq, HEAD_PAD), lambda b, h, qi: (b * nq + qi, h)),
            pl.BlockSpec((seq, HEAD_PAD), lambda b, h, qi: (b, h)),
            pl.BlockSpec((nchunk, HEAD_PAD, th), lambda b, h, qi: (b, h, 0)),
        ],
        out_specs=pl.BlockSpec((tq, HEAD_PAD), lambda b, h, qi: (b * nq + qi, h)),
        out_shape=jax.ShapeDtypeStruct((t, hp), F32),
        scratch_shapes=[pltpu.VMEM((HEAD_PAD, tq), F32)],
        compiler_params=pltpu.CompilerParams(
            dimension_semantics=("arbitrary", "arbitrary", "arbitrary"),
            vmem_limit_bytes=VMEM_LIMIT),
        name="attn",
    )(q, k, vt)


def _outproj_kernel(pool_ref, attn_ref, x_ref, gattn_ref, wop_ref, woa_ref, gffn_ref, wr_ref,
                    br_ref, x1_out, hp_out, idx_out, gate_out, rank_out, cnt_out, run_scr):
    tm, d = x_ref.shape
    d_attn = N_HEADS * V_HEAD_DIM
    i = pl.program_id(0)

    @pl.when(i == 0)
    def _():
        run_scr[...] = jnp.zeros(run_scr.shape, F32)

    attn = attn_ref[...]
    attn_n = _rms(attn, gattn_ref[...], d_attn).astype(BF16)
    mix = (jnp.dot(pool_ref[...], wop_ref[...], preferred_element_type=F32)
           + jnp.dot(attn_n, woa_ref[...], preferred_element_type=F32))
    x1 = x_ref[...] + mix
    x1_out[...] = x1
    h2 = _rms(x1, gffn_ref[...], d)

    word = _pack_bf16_pairs(h2)
    nw = d // 2 // LANE
    for c in range(nw):
        hp_out[pl.ds(c, tm, stride=nw), :] = word[:, c * LANE:(c + 1) * LANE]

    logits = lax.dot_general(wr_ref[...], h2, (((1,), (1,)), ((), ())),
                             preferred_element_type=F32,
                             precision=lax.Precision.HIGHEST) + br_ref[...]
    e_iota = lax.broadcasted_iota(jnp.int32, (N_EXPERTS, tm), 0)
    work = logits
    idxs, vals = [], []
    for _ in range(TOP_K):
        mx = jnp.max(work, axis=0, keepdims=True)
        sel = jnp.min(jnp.where(work == mx, e_iota, N_EXPERTS), axis=0, keepdims=True)
        idxs.append(sel)
        vals.append(mx)
        work = jnp.where(e_iota == sel, -jnp.inf, work)
    exps = [jnp.exp(v - vals[0]) for v in vals]
    denom = exps[0] + exps[1] + exps[2] + exps[3]
    gate_out[...] = jnp.concatenate([e / denom for e in exps], axis=0)
    idx_out[...] = jnp.concatenate(idxs, axis=0)

    onehot = jnp.zeros((N_EXPERTS, tm), F32)
    for sel in idxs:
        onehot = onehot + (e_iota == sel).astype(F32)
    r_io = lax.broadcasted_iota(jnp.int32, (tm, tm), 0)
    c_io = lax.broadcasted_iota(jnp.int32, (tm, tm), 1)
    upper = (r_io < c_io).astype(BF16)
    prefix = jnp.dot(onehot.astype(BF16), upper, preferred_element_type=F32) + run_scr[:, 0:1]
    ranks = [jnp.sum(jnp.where(e_iota == sel, prefix, 0.0), axis=0, keepdims=True) for sel in idxs]
    rank_out[...] = jnp.concatenate(ranks, axis=0).astype(jnp.int32)
    run_new = run_scr[...] + jnp.sum(onehot, axis=1, keepdims=True)
    run_scr[...] = run_new
    cnt_out[...] = run_new.astype(jnp.int32)


def _outproj_call(pool_n, attn, x2, gattn_p, wo_pool, wo_attn_p, gffn, wr_t, br_col, tm):
    t, d = x2.shape
    nblk = t // tm
    const2 = lambda i: (0, 0)
    row = lambda i: (i, 0)
    col = lambda i: (0, i)
    in_specs = [
        pl.BlockSpec((tm, pool_n.shape[1]), row),
        pl.BlockSpec((tm, attn.shape[1]), row),
        pl.BlockSpec((tm, d), row),
        pl.BlockSpec(gattn_p.shape, const2),
        pl.BlockSpec(wo_pool.shape, const2),
        pl.BlockSpec(wo_attn_p.shape, const2),
        pl.BlockSpec(gffn.shape, const2),
        pl.BlockSpec(wr_t.shape, const2),
        pl.BlockSpec(br_col.shape, const2),
    ]
    out_shape = [
        jax.ShapeDtypeStruct((t, d), F32),
        jax.ShapeDtypeStruct((t * (d // 2 // LANE), LANE), jnp.uint32),
        jax.ShapeDtypeStruct((TOP_K, t), jnp.int32),
        jax.ShapeDtypeStruct((TOP_K, t), F32),
        jax.ShapeDtypeStruct((TOP_K, t), jnp.int32),
        jax.ShapeDtypeStruct((N_EXPERTS, LANE), jnp.int32),
    ]
    out_specs = [
        pl.BlockSpec((tm, d), row),
        pl.BlockSpec((tm * (d // 2 // LANE), LANE), row),
        pl.BlockSpec((TOP_K, tm), col),
        pl.BlockSpec((TOP_K, tm), col),
        pl.BlockSpec((TOP_K, tm), col),
        pl.BlockSpec((N_EXPERTS, LANE), const2),
    ]
    return pl.pallas_call(
        _outproj_kernel,
        grid=(nblk,),
        in_specs=in_specs,
        out_specs=out_specs,
        out_shape=out_shape,
        scratch_shapes=[pltpu.VMEM((N_EXPERTS, LANE), F32)],
        compiler_params=pltpu.CompilerParams(dimension_semantics=("arbitrary",),
                                             vmem_limit_bytes=VMEM_LIMIT),
        name="outproj",
    )(pool_n, attn, x2, gattn_p, wo_pool, wo_attn_p, gffn, wr_t, br_col)


ROW_UNROLL = 8


def _dispatch_kernel(nw, bs_ref, nblk_ref, dest_ref, h_ref, xs_out, zbuf, sem, zsem):
    td = h_ref.shape[0] // nw
    blk_rows = zbuf.shape[0]
    n_exp = bs_ref.shape[0]

    @pl.when(pl.program_id(0) == 0)
    def _():
        zbuf[...] = jnp.zeros(zbuf.shape, zbuf.dtype)
        total = xs_out.shape[0] // blk_rows
        first_tail = bs_ref[n_exp - 1] + nblk_ref[n_exp - 1]

        def z_copy(blk):
            row0 = pl.multiple_of(blk * blk_rows, blk_rows)
            return pltpu.make_async_copy(zbuf, xs_out.at[pl.ds(row0, blk_rows)], zsem)

        def each_partial_block(fn):
            def per_expert(e, c):
                @pl.when(nblk_ref[e] > 0)
                def _():
                    fn(z_copy(bs_ref[e] + nblk_ref[e] - 1))
                return c

            def per_tail(b, c):
                fn(z_copy(b))
                return c

            lax.fori_loop(0, n_exp, per_expert, 0)
            lax.fori_loop(first_tail, total, per_tail, 0)

        each_partial_block(lambda cp: cp.start())
        each_partial_block(lambda cp: cp.wait())

    def body(g, carry):
        base = g * ROW_UNROLL
        for u in range(ROW_UNROLL):
            i = base + u
            src = h_ref.at[pl.ds(pl.multiple_of(i * nw, nw), nw)]
            for k in range(TOP_K):
                d = dest_ref[i * TOP_K + k]
                pltpu.make_async_copy(src, xs_out.at[pl.ds(pl.multiple_of(d * nw, nw), nw)],
                                      sem).start(priority=k % 2)
        return carry

    lax.fori_loop(0, td // ROW_UNROLL, body, 0)
    for _ in range(TOP_K):
        pltpu.make_async_copy(h_ref, xs_out.at[pl.ds(0, td * nw)], sem).wait()


def _dispatch_call(blk_start, nblk, dest, hp, n_rows, nw, td, bm):
    t = hp.shape[0] // nw
    grid_spec = pltpu.PrefetchScalarGridSpec(
        num_scalar_prefetch=2,
        grid=(t // td,),
        in_specs=[
            pl.BlockSpec((TOP_K * td,), lambda i, bs, nb: (i,), memory_space=pltpu.SMEM),
            pl.BlockSpec((td * nw, LANE), lambda i, bs, nb: (i, 0)),
        ],
        out_specs=pl.BlockSpec(memory_space=pl.ANY),
        scratch_shapes=[pltpu.VMEM((bm * nw, LANE), hp.dtype), pltpu.SemaphoreType.DMA,
                        pltpu.SemaphoreType.DMA],
    )
    return pl.pallas_call(
        functools.partial(_dispatch_kernel, nw),
        grid_spec=grid_spec,
        out_shape=jax.ShapeDtypeStruct((n_rows * nw, LANE), hp.dtype),
        compiler_params=pltpu.CompilerParams(dimension_semantics=("arbitrary",),
                                             vmem_limit_bytes=VMEM_LIMIT),
        name="dispatch",
    )(blk_start, nblk, dest, hp)


SC_WINDOW = 64


def _dispatch_sc_call(dest_kt, hp_rows, n_rows):
    t, w = hp_rows.shape
    mesh = plsc.VectorSubcoreMesh(core_axis_name="core", subcore_axis_name="subcore")

    @functools.partial(pl.kernel, out_type=jax.ShapeDtypeStruct((n_rows, w), hp_rows.dtype),
                       mesh=mesh, scratch_types=[])
    def scatter_kernel(x_hbm, i_hbm, o_hbm):
        def body(x_vmem, i_vmem):
            for k in range(TOP_K):
                pltpu.sync_copy(x_vmem, o_hbm.at[i_vmem.at[0, k]])

        pltpu.emit_pipeline(
            body,
            grid=(t // SC_WINDOW,),
            in_specs=[pl.BlockSpec((SC_WINDOW, w), lambda i: (i, 0)),
                      pl.BlockSpec((1, TOP_K, SC_WINDOW), lambda i: (i, 0, 0))],
            out_specs=[],
            core_axis_name=("core", "subcore"),
            dimension_semantics=(pltpu.PARALLEL,),
        )(x_hbm, i_hbm)

    idx = dest_kt.reshape(TOP_K, t // SC_WINDOW, SC_WINDOW).transpose(1, 0, 2)
    return scatter_kernel(hp_rows, idx)


CAST_ROWS = 64


def _expert_kernel(bs_ref, nblk_ref, bgu_ref, bd_ref, wgu_hbm, wd_hbm, xs_hbm, y_hbm,
                   wgu_f32, wd_f32, wgu_bf, wd_bf, xbuf, ybuf, xsem, ysem, wsem):
    e = pl.program_id(0)
    n_exp = pl.num_programs(0)
    d = wgu_f32.shape[1]
    f = wd_f32.shape[1]
    nw = d // 2 // LANE
    bm = xbuf.shape[1] // nw
    n = nblk_ref[e]
    blk0 = bs_ref[e]
    wslot = e % 2

    def w_copies(ex, slot):
        return (pltpu.make_async_copy(wgu_hbm.at[ex], wgu_f32.at[slot], wsem.at[0, slot]),
                pltpu.make_async_copy(wd_hbm.at[ex], wd_f32.at[slot], wsem.at[1, slot]))

    def x_copy(j, slot):
        row0 = pl.multiple_of((blk0 + j) * (bm * nw), bm * nw)
        return pltpu.make_async_copy(xs_hbm.at[pl.ds(row0, bm * nw)], xbuf.at[slot], xsem.at[slot])

    def y_copy(blk, slot):
        row0 = pl.multiple_of(blk * (bm * nw), bm * nw)
        return pltpu.make_async_copy(ybuf.at[slot], y_hbm.at[pl.ds(row0, bm * nw)], ysem.at[slot])

    @pl.when(n > 0)
    def _():
        x_copy(0, 0).start()

    @pl.when(e == 0)
    def _():
        for cp in w_copies(0, 0):
            cp.start(priority=1)

    @pl.when(e + 1 < n_exp)
    def _():
        for cp in w_copies(e + 1, 1 - wslot):
            cp.start(priority=1)

    for cp in w_copies(e, wslot):
        cp.wait()

    @pl.when(n > 0)
    def _():
        def cast_gu(r, c):
            rows = pl.ds(pl.multiple_of(r * CAST_ROWS, CAST_ROWS), CAST_ROWS)
            wgu_bf[rows, :] = wgu_f32[wslot, rows, :].astype(BF16)
            return c

        def cast_d(r, c):
            rows = pl.ds(pl.multiple_of(r * CAST_ROWS, CAST_ROWS), CAST_ROWS)
            wd_bf[rows, :] = wd_f32[wslot, rows, :].astype(BF16)
            return c

        lax.fori_loop(0, d // CAST_ROWS, cast_gu, 0)
        lax.fori_loop(0, f // CAST_ROWS, cast_d, 0)

        def block(j, c):
            slot = j % 2
            x_copy(j, slot).wait()

            @pl.when(j + 1 < n)
            def _():
                x_copy(j + 1, 1 - slot).start()

            @pl.when(j >= 2)
            def _():
                y_copy(blk0 + j - 2, slot).wait()

            w = jnp.concatenate([xbuf[slot, pl.ds(c, bm, stride=nw), :] for c in range(nw)], axis=1)
            x_lo, x_hi = _unpack_bf16_pairs(w)
            gu = (jnp.dot(x_lo.astype(BF16), wgu_bf[:d // 2, :], preferred_element_type=F32)
                  + jnp.dot(x_hi.astype(BF16), wgu_bf[d // 2:, :], preferred_element_type=F32)
                  + bgu_ref[0])
            gate = jnp.minimum(gu[:, :f], SWIGLU_LIMIT)
            up = jnp.clip(gu[:, f:], -SWIGLU_LIMIT, SWIGLU_LIMIT)
            act = gate * jax.nn.sigmoid(SWIGLU_ALPHA * gate) * (up + 1.0)
            y = jnp.dot(act.astype(BF16), wd_bf[...], preferred_element_type=F32) + bd_ref[0]
            yw = _pack_bf16_pairs(y)
            for c in range(nw):
                ybuf[slot, pl.ds(c, bm, stride=nw), :] = yw[:, c * LANE:(c + 1) * LANE]
            y_copy(blk0 + j, slot).start()
            return c

        lax.fori_loop(0, n, block, 0)

        @pl.when(n >= 2)
        def _():
            y_copy(blk0 + n - 2, n % 2).wait()

        y_copy(blk0 + n - 1, (n - 1) % 2).wait()

    @pl.when(e == n_exp - 1)
    def _():
        first = blk0 + n
        total = y_hbm.shape[0] // (bm * nw)
        ybuf[0] = jnp.zeros(ybuf.shape[1:], ybuf.dtype)

        def fill(b, c):
            y_copy(b, 0).start()
            return c

        def drain(b, c):
            y_copy(b, 0).wait()
            return c

        lax.fori_loop(first, total, fill, 0)
        lax.fori_loop(first, total, drain, 0)


def _expert_call(blk_start, nblk, xs, wgu, bgu, wd, bd, bm):
    e, d, f2 = wgu.shape
    f = wd.shape[1]
    nw = d // 2 // LANE
    r = xs.shape[0] // nw

    def bsel(i, bs, nb):
        return (i, 0, 0)

    grid_spec = pltpu.PrefetchScalarGridSpec(
        num_scalar_prefetch=2,
        grid=(e,),
        in_specs=[
            pl.BlockSpec((1, 1, f2), bsel),
            pl.BlockSpec((1, 1, d), bsel),
            pl.BlockSpec(memory_space=pl.ANY),
            pl.BlockSpec(memory_space=pl.ANY),
            pl.BlockSpec(memory_space=pl.ANY),
        ],
        out_specs=pl.BlockSpec(memory_space=pl.ANY),
        scratch_shapes=[pltpu.VMEM((2, d, f2), F32), pltpu.VMEM((2, f, d), F32),
                        pltpu.VMEM((d, f2), BF16), pltpu.VMEM((f, d), BF16),
                        pltpu.VMEM((2, bm * nw, LANE), xs.dtype),
                        pltpu.VMEM((2, bm * nw, LANE), jnp.uint32),
                        pltpu.SemaphoreType.DMA((2,)), pltpu.SemaphoreType.DMA((2,)),
                        pltpu.SemaphoreType.DMA((2, 2))],
    )
    return pl.pallas_call(
        _expert_kernel,
        grid_spec=grid_spec,
        out_shape=jax.ShapeDtypeStruct((r * nw, LANE), jnp.uint32),
        compiler_params=pltpu.CompilerParams(dimension_semantics=("arbitrary",),
                                             vmem_limit_bytes=VMEM_LIMIT),
        name="experts",
    )(blk_start, nblk, bgu, bd, wgu, wd, xs)


def _combine_kernel(dest_ref, dest_next_ref, x1_ref, gate_ref, gfin_ref, yb_hbm, o_ref, ybuf, sem):
    tf, d = x1_ref.shape
    ns = d // 2 // LANE
    j = pl.program_id(0)
    nsteps = pl.num_programs(0)
    slot = j % 2

    def issue(idx_ref, s):
        def body(g, carry):
            base = g * ROW_UNROLL
            for u in range(ROW_UNROLL):
                i = base + u
                for k in range(TOP_K):
                    r = idx_ref[i * TOP_K + k]
                    pltpu.make_async_copy(yb_hbm.at[pl.ds(pl.multiple_of(r * ns, ns), ns)],
                                          ybuf.at[s, k, pl.ds(pl.multiple_of(i * ns, ns), ns)],
                                          sem.at[s]).start(priority=k % 2)
            return carry

        lax.fori_loop(0, tf // ROW_UNROLL, body, 0)

    @pl.when(j == 0)
    def _():
        issue(dest_ref, 0)

    @pl.when(j + 1 < nsteps)
    def _():
        issue(dest_next_ref, 1 - slot)

    for k in range(TOP_K):
        pltpu.make_async_copy(yb_hbm.at[pl.ds(0, tf * ns)], ybuf.at[slot, k], sem.at[slot]).wait()
    g = gate_ref[...]
    y = x1_ref[...]
    for k in range(TOP_K):
        yw = jnp.concatenate([ybuf[slot, k, pl.ds(c, tf, stride=ns), :] for c in range(ns)], axis=1)
        y = y + g[:, k:k + 1] * jnp.concatenate(_unpack_bf16_pairs(yw), axis=1)
    o_ref[...] = _rms(y, gfin_ref[...], d)


def _combine_call(dest, x1, gates_t, gfin, yb, tf):
    t, d = x1.shape
    nsteps = t // tf
    return pl.pallas_call(
        _combine_kernel,
        grid=(nsteps,),
        in_specs=[
            pl.BlockSpec((TOP_K * tf,), lambda i: (i,), memory_space=pltpu.SMEM),
            pl.BlockSpec((TOP_K * tf,), lambda i: (jnp.minimum(i + 1, nsteps - 1),),
                         memory_space=pltpu.SMEM),
            pl.BlockSpec((tf, d), lambda i: (i, 0)),
            pl.BlockSpec((tf, TOP_K), lambda i: (i, 0)),
            pl.BlockSpec(gfin.shape, lambda i: (0, 0)),
            pl.BlockSpec(memory_space=pl.ANY),
        ],
        out_specs=pl.BlockSpec((tf, d), lambda i: (i, 0)),
        out_shape=jax.ShapeDtypeStruct((t, d), F32),
        scratch_shapes=[pltpu.VMEM((2, TOP_K, tf * (d // 2 // LANE), LANE), jnp.uint32),
                        pltpu.SemaphoreType.DMA((2,))],
        compiler_params=pltpu.CompilerParams(dimension_semantics=("arbitrary",),
                                             vmem_limit_bytes=VMEM_LIMIT),
        name="combine",
    )(dest, dest, x1, gates_t, gfin, yb)


def _pad_head_cols(w, widths_in, place):
    k = w.shape[0]
    per = sum(widths_in)
    blocks = []
    for hh in range(N_HEADS):
        blocks.append(place(w[:, hh * per:(hh + 1) * per]))
    return jnp.concatenate(blocks, axis=1)


def _layer(x2, pos3, batch, seq, g_attn_norm, w_in, w_pool, b_pool, pool_scale, g_q_a, w_q_b,
           g_kv_a, w_kv_b, g_out_pool, g_out_attn, w_out, g_ffn_norm, w_router, b_router,
           w_gate_up, b_gate_up, w_down, b_down, tiles):
    tm, tq, tmo, td, bm, tf = tiles
    t, d = x2.shape
    d_pool = pool_scale.shape[0]
    gdim = d_pool // N_POOL_GROUPS
    q_rank = g_q_a.shape[0]
    kv_rank = g_kv_a.shape[0]
    half = QK_ROPE_DIM // 2

    o1, o2, o3 = d_pool, d_pool + q_rank, d_pool + q_rank + kv_rank
    zk = lambda n: jnp.zeros((d, n), F32)
    w_kr1, w_kr2 = w_in[:, o3:o3 + half], w_in[:, o3 + half:]
    w_in_p = jnp.concatenate([w_in[:, :o3], zk(QK_NOPE_DIM), w_kr1, w_kr2, zk(LANE - QK_DIM),
                              zk(QK_NOPE_DIM), w_kr2, w_kr1, zk(LANE - QK_DIM)], axis=1).astype(BF16)
    zq = lambda n: jnp.zeros((q_rank, n), F32)
    wq_plain = _pad_head_cols(w_q_b, (QK_DIM,),
                              lambda c: jnp.concatenate([c, zq(LANE - QK_DIM)], axis=1))
    wq_swap = _pad_head_cols(w_q_b, (QK_DIM,), lambda c: jnp.concatenate(
        [zq(QK_NOPE_DIM), c[:, QK_NOPE_DIM + half:], c[:, QK_NOPE_DIM:QK_NOPE_DIM + half],
         zq(LANE - QK_DIM)], axis=1))
    wq_p = jnp.concatenate([wq_plain, wq_swap], axis=1).astype(BF16)
    zkv = lambda n: jnp.zeros((kv_rank, n), F32)
    per = QK_NOPE_DIM + V_HEAD_DIM
    wk_p = _pad_head_cols(w_kv_b, (per,), lambda c: jnp.concatenate(
        [c[:, :QK_NOPE_DIM], zkv(LANE - QK_NOPE_DIM)], axis=1))
    wv_p = _pad_head_cols(w_kv_b, (per,), lambda c: jnp.concatenate(
        [c[:, QK_NOPE_DIM:], zkv(LANE - V_HEAD_DIM)], axis=1))
    wkv_p = jnp.concatenate([wk_p, wv_p], axis=1).astype(BF16)
    inv = ROPE_THETA ** (-jnp.arange(half, dtype=F32) / half)
    inv_col = inv.reshape(half, 1)

    row = lambda v: v.reshape(1, -1)
    grp = lambda v: v.reshape(N_POOL_GROUPS, 1, gdim)

    pool_n, q, k, v = _proj_call(
        x2, pos3, row(g_attn_norm), w_in_p, inv_col, w_pool.astype(BF16), grp(b_pool),
        grp(pool_scale), grp(g_out_pool), row(g_q_a), wq_p, row(g_kv_a), wkv_p, seq, tm)

    attn = _attn_call(q, k, v, batch, seq, tq)

    d_attn = N_HEADS * V_HEAD_DIM
    wo_pool = w_out[:d_pool].astype(BF16)
    wo_attn = w_out[d_pool:].reshape(N_HEADS, V_HEAD_DIM, d)
    wo_attn_p = jnp.concatenate([wo_attn, jnp.zeros((N_HEADS, HEAD_PAD - V_HEAD_DIM, d), F32)],
                                axis=1).reshape(N_HEADS * HEAD_PAD, d).astype(BF16)
    gattn_p = jnp.concatenate([g_out_attn.reshape(N_HEADS, V_HEAD_DIM),
                               jnp.zeros((N_HEADS, HEAD_PAD - V_HEAD_DIM), F32)], axis=1).reshape(1, -1)

    x1, hp, idx, gates, rank, counts = _outproj_call(
        pool_n, attn, x2, gattn_p, wo_pool, wo_attn_p, row(g_ffn_norm), w_router.T,
        b_router.reshape(-1, 1), tmo)

    n_assign = t * TOP_K
    cnt = counts[:, 0]
    padded = ((cnt + bm - 1) // bm) * bm
    pad_end = jnp.cumsum(padded)
    pad_start = pad_end - padded
    e_ids = jnp.arange(N_EXPERTS, dtype=jnp.int32)
    dest = rank + jnp.sum(jnp.where(idx[None] == e_ids[:, None, None], pad_start[:, None, None], 0),
                          axis=0)
    dest_kt = dest
    dest = dest.T.reshape(-1)
    n_rows = (n_assign // bm + N_EXPERTS) * bm
    blk_start = (pad_start // bm).astype(jnp.int32)
    nblk = (padded // bm).astype(jnp.int32)

    nw = d // 2 // LANE
    xs = _dispatch_sc_call(dest_kt, hp.reshape(t, nw * LANE), n_rows).reshape(n_rows * nw, LANE)
    yb = _expert_call(blk_start, nblk, xs, w_gate_up, b_gate_up.reshape(N_EXPERTS, 1, -1),
                      w_down, b_down.reshape(N_EXPERTS, 1, -1), bm)
    return x1, dest, gates, yb


def _forward(x, positions, g_attn_norm, w_in, w_pool, b_pool, pool_scale, g_q_a, w_q_b, g_kv_a,
             w_kv_b, g_out_pool, g_out_attn, w_out, g_ffn_norm, w_router, b_router, w_gate_up,
             b_gate_up, w_down, b_down, g_final, tiles):
    batch, seq, d = x.shape
    t = batch * seq
    tm, tq, tmo, td, bm, tf = tiles
    x2 = x.reshape(t, d)
    pos3 = positions.reshape(t // tm, 1, tm)
    depth = g_attn_norm.shape[0]
    assert depth == 1
    l = 0
    x1, dest, gates, yb = _layer(
        x2, pos3, batch, seq, g_attn_norm[l], w_in[l], w_pool[l], b_pool[l], pool_scale[l],
        g_q_a[l], w_q_b[l], g_kv_a[l], w_kv_b[l], g_out_pool[l], g_out_attn[l], w_out[l],
        g_ffn_norm[l], w_router[l], b_router[l], w_gate_up[l], b_gate_up[l], w_down[l],
        b_down[l], tiles)
    out = _combine_call(dest, x1, gates.T, g_final.reshape(1, -1), yb, tf)
    return out.reshape(batch, seq, d)


def _tiles_for(seq):
    tm = min(512, seq)
    tq = min(512, seq // 2)
    tmo = min(512, seq)
    td = min(512, seq)
    bm = 256 if seq >= 2048 else 128
    tf = min(256, seq)
    return (tm, tq, tmo, td, bm, tf)


def kernel(x, positions, g_attn_norm, w_in, w_pool, b_pool, pool_scale, g_q_a, w_q_b, g_kv_a, w_kv_b, g_out_pool, g_out_attn, w_out, g_ffn_norm, w_router, b_router, w_gate_up, b_gate_up, w_down, b_down, g_final):
    tiles = _tiles_for(x.shape[1])
    return _forward(x, positions, g_attn_norm, w_in, w_pool, b_pool, pool_scale, g_q_a, w_q_b,
                    g_kv_a, w_kv_b, g_out_pool, g_out_attn, w_out, g_ffn_norm, w_router, b_router,
                    w_gate_up, b_gate_up, w_down, b_down, g_final, tiles)
```

```python
import functools

import jax
import jax.numpy as jnp
from jax import lax
from jax.experimental import pallas as pl
from jax.experimental.pallas import tpu as pltpu
from jax.experimental.pallas import tpu_sc as plsc

POOL_WINDOWS = (2, 4, 8, 16)
N_POOL_GROUPS = 4
N_HEADS = 8
V_HEAD_DIM = 64
QK_NOPE_DIM = 64
QK_ROPE_DIM = 32
QK_DIM = QK_NOPE_DIM + QK_ROPE_DIM
ROPE_THETA = 10000.0
N_EXPERTS = 32
TOP_K = 4
SWIGLU_ALPHA = 1.702
SWIGLU_LIMIT = 7.0
RMS_EPS = 1e-6
LOG2_E = 1.4426950408889634

LANE = 128
HEAD_PAD = 128
POOL_HALO = 16

VMEM_LIMIT = 56 * 1024 * 1024

F32 = jnp.float32
BF16 = jnp.bfloat16


def _pack_bf16_pairs(x):
    n = x.shape[1] // 2
    xb = x.astype(BF16).astype(F32)
    lo = lax.bitcast_convert_type(xb[:, :n], jnp.uint32)
    hi = lax.bitcast_convert_type(xb[:, n:], jnp.uint32)
    return (lo >> 16) | (hi & jnp.uint32(0xFFFF0000))


def _unpack_bf16_pairs(w):
    lo = lax.bitcast_convert_type(w << 16, F32)
    hi = lax.bitcast_convert_type(w & jnp.uint32(0xFFFF0000), F32)
    return lo, hi


def _rms(x, g, n):
    ms = jnp.sum(x * x, axis=-1, keepdims=True) * (1.0 / n)
    return x * lax.rsqrt(ms + RMS_EPS) * g


def _proj_kernel(blocks_per_seq, x_ref, pos_ref, g_attn_ref, w_in_ref, inv_ref, wpool_ref,
                 bpool_ref, pscale_ref, gpool_ref, gq_ref, wq_ref, gkv_ref, wkv_ref,
                 pool_out, q_out, k_out, v_out, ext_scr):
    tm = x_ref.shape[0]
    d_model = x_ref.shape[1]
    d_pool = pool_out.shape[1]
    gdim = d_pool // N_POOL_GROUPS
    q_rank = gq_ref.shape[1]
    kv_rank = gkv_ref.shape[1]
    i = pl.program_id(0)
    blk_in_seq = i % blocks_per_seq

    x = x_ref[...]
    h = _rms(x, g_attn_ref[...], d_model).astype(BF16)
    proj = jnp.dot(h, w_in_ref[...], preferred_element_type=F32)
    u = proj[:, :d_pool]
    qlat = proj[:, d_pool:d_pool + q_rank]
    kvlat = proj[:, d_pool + q_rank:d_pool + q_rank + kv_rank]
    o_kr = d_pool + q_rank + kv_rank
    krb = proj[:, o_kr:o_kr + LANE]
    krs = proj[:, o_kr + LANE:]

    @pl.when(blk_in_seq == 0)
    def _():
        ext_scr[0:POOL_HALO, :] = jnp.zeros((POOL_HALO, d_pool), F32)

    @pl.when(blk_in_seq != 0)
    def _():
        ext_scr[0:POOL_HALO, :] = ext_scr[tm:tm + POOL_HALO, :]

    ext_scr[POOL_HALO:, :] = u
    t_in_seq = (blk_in_seq * tm + lax.broadcasted_iota(jnp.int32, (tm, gdim), 0) + 1).astype(F32)
    ys = []
    ssq = jnp.zeros((tm, 1), F32)
    for g, w in enumerate(POOL_WINDOWS):
        a = ext_scr[:, g * gdim:(g + 1) * gdim]
        shift = 1
        while shift < w:
            a = a + pltpu.roll(a, shift, axis=0)
            shift *= 2
        win = a[POOL_HALO:, :]
        cnt = jnp.minimum(t_in_seq, float(w))
        ug = u[:, g * gdim:(g + 1) * gdim]
        diff = (win / cnt - ug).astype(BF16)
        mixed = jnp.dot(diff, wpool_ref[g], preferred_element_type=F32) + bpool_ref[g]
        y = mixed * pscale_ref[g]
        ssq = ssq + jnp.sum(y * y, axis=-1, keepdims=True)
        ys.append(y)
    rinv = lax.rsqrt(ssq * (1.0 / d_pool) + RMS_EPS)
    for g in range(N_POOL_GROUPS):
        pool_out[:, g * gdim:(g + 1) * gdim] = (ys[g] * rinv * gpool_ref[g]).astype(pool_out.dtype)

    pos = pos_ref[0].astype(F32)
    ang_t = inv_ref[...] * pos
    cos_t = jnp.cos(ang_t)
    sin_t = jnp.sin(ang_t)
    half = QK_ROPE_DIM // 2
    ones = jnp.ones((QK_NOPE_DIM, tm), F32)
    c_t = jnp.concatenate([ones, cos_t, cos_t, jnp.ones((LANE - QK_DIM, tm), F32)], axis=0)
    s_t = jnp.concatenate([jnp.zeros((QK_NOPE_DIM, tm), F32), -sin_t, sin_t,
                           jnp.zeros((LANE - QK_DIM, tm), F32)], axis=0)
    c = c_t.T
    s = s_t.T

    def rope(blk, blk_swapped, c_, s_):
        return blk * c_ + blk_swapped * s_

    qn = _rms(qlat, gq_ref[...], q_rank).astype(BF16)
    q2 = jnp.dot(qn, wq_ref[...], preferred_element_type=F32)
    scale = QK_DIM ** -0.5 * LOG2_E
    cq, sq = c * scale, s * scale
    hp = N_HEADS * HEAD_PAD
    for hh in range(N_HEADS):
        sl = slice(hh * HEAD_PAD, (hh + 1) * HEAD_PAD)
        sw = slice(hp + hh * HEAD_PAD, hp + (hh + 1) * HEAD_PAD)
        q_out[:, sl] = rope(q2[:, sl], q2[:, sw], cq, sq).astype(q_out.dtype)

    kvn = _rms(kvlat, gkv_ref[...], kv_rank).astype(BF16)
    kv = jnp.dot(kvn, wkv_ref[...], preferred_element_type=F32)
    kr = rope(krb, krs, c, s)
    for hh in range(N_HEADS):
        sl = slice(hh * HEAD_PAD, (hh + 1) * HEAD_PAD)
        k_out[:, sl] = (kv[:, sl] + kr).astype(k_out.dtype)
    vlane = lax.broadcasted_iota(jnp.int32, (1, N_HEADS * HEAD_PAD), 1) % HEAD_PAD
    ones_col = (vlane == V_HEAD_DIM).astype(F32)
    v_out[0] = (kv[:, N_HEADS * HEAD_PAD:] + ones_col).T.astype(v_out.dtype)


def _proj_call(x2, pos3, g_attn, w_in_p, inv_col, wpool, bpool, pscale, gpool, gq, wq_p, gkv, wkv_p,
               seq, tm):
    t, d = x2.shape
    d_pool = pscale.shape[0] * pscale.shape[2]
    hp = N_HEADS * HEAD_PAD
    nblk = t // tm
    const2 = lambda i: (0, 0)
    const3 = lambda i: (0, 0, 0)
    in_specs = [
        pl.BlockSpec((tm, d), lambda i: (i, 0)),
        pl.BlockSpec((1, 1, tm), lambda i: (i, 0, 0)),
        pl.BlockSpec(g_attn.shape, const2),
        pl.BlockSpec(w_in_p.shape, const2),
        pl.BlockSpec(inv_col.shape, const2),
        pl.BlockSpec(wpool.shape, const3),
        pl.BlockSpec(bpool.shape, const3),
        pl.BlockSpec(pscale.shape, const3),
        pl.BlockSpec(gpool.shape, const3),
        pl.BlockSpec(gq.shape, const2),
        pl.BlockSpec(wq_p.shape, const2),
        pl.BlockSpec(gkv.shape, const2),
        pl.BlockSpec(wkv_p.shape, const2),
    ]
    out_shape = [
        jax.ShapeDtypeStruct((t, d_pool), BF16),
        jax.ShapeDtypeStruct((t, hp), BF16),
        jax.ShapeDtypeStruct((t, hp), BF16),
        jax.ShapeDtypeStruct((nblk, hp, tm), BF16),
    ]
    out_specs = [
        pl.BlockSpec((tm, d_pool), lambda i: (i, 0)),
        pl.BlockSpec((tm, hp), lambda i: (i, 0)),
        pl.BlockSpec((tm, hp), lambda i: (i, 0)),
        pl.BlockSpec((1, hp, tm), lambda i: (i, 0, 0)),
    ]
    return pl.pallas_call(
        functools.partial(_proj_kernel, seq // tm),
        grid=(nblk,),
        in_specs=in_specs,
        out_specs=out_specs,
        out_shape=out_shape,
        scratch_shapes=[pltpu.VMEM((tm + POOL_HALO, d_pool), F32)],
        compiler_params=pltpu.CompilerParams(dimension_semantics=("arbitrary",),
                                             vmem_limit_bytes=VMEM_LIMIT),
        name="proj",
    )(x2, pos3, g_attn, w_in_p, inv_col, wpool, bpool, pscale, gpool, gq, wq_p, gkv, wkv_p)


ROWSUM_ROW = V_HEAD_DIM


def _attn_kernel(q_ref, k_ref, vt_ref, o_ref, acc_ref):
    tq = q_ref.shape[0]
    th = vt_ref.shape[2]
    nsub = tq // th
    qi = pl.program_id(2)
    acc_ref[...] = jnp.zeros(acc_ref.shape, F32)

    def step(kc, m_prev, q_lo, masked):
        width = tq - q_lo
        start = pl.multiple_of(kc * th, th)
        k = k_ref[pl.ds(start, th), :]
        st = lax.dot_general(k, q_ref[q_lo:, :], (((1,), (1,)), ((), ())),
                             preferred_element_type=F32)
        if masked:
            key = lax.broadcasted_iota(jnp.int32, (th, width), 0)
            qry = lax.broadcasted_iota(jnp.int32, (th, width), 1)
            st = jnp.where(key <= qry, st, -jnp.inf)
        m_new = jnp.maximum(m_prev, jnp.max(st, axis=0, keepdims=True))
        alpha = jnp.exp2(m_prev - m_new)
        pt = jnp.exp2(st - m_new).astype(BF16)
        acc_ref[:, q_lo:] = alpha * acc_ref[:, q_lo:] + jnp.dot(vt_ref[kc], pt,
                                                                preferred_element_type=F32)
        return m_new

    def two_chunks(j, mp):
        return step(2 * j + 1, step(2 * j, mp, 0, False), 0, False)

    m = lax.fori_loop(0, (nsub // 2) * qi, two_chunks, jnp.full((1, tq), -jnp.inf, F32))
    for c in range(nsub):
        m = step(nsub * qi + c, m, c * th, True)[:, th:]

    acc = acc_ref[...]
    out = (acc / acc[ROWSUM_ROW:ROWSUM_ROW + 1, :]).T
    lane = lax.broadcasted_iota(jnp.int32, out.shape, 1)
    o_ref[...] = jnp.where(lane < V_HEAD_DIM, out, 0.0).astype(o_ref.dtype)


ATTN_Q_CHUNKS = 8


def _attn_call(q, k, vt, batch, seq, th):
    t, hp = q.shape
    tq = min(ATTN_Q_CHUNKS * th, seq)
    nq = seq // tq
    nchunk = seq // th
    return pl.pallas_call(
        _attn_kernel,
        grid=(batch, N_HEADS, nq),
        in_specs=[
            pl.BlockSpec((tq, HEAD_PAD), lambda b, h, qi: (b * nq + qi, h)),
            pl.BlockSpec((seq, HEAD_PAD), lambda b, h, qi: (b, h)),
            pl.BlockSpec((nchunk, HEAD_PAD, th), lambda b, h, qi: (b, h, 0)),
        ],
        out_specs=pl.BlockSpec((tq, HEAD_PAD), lambda b, h, qi: (b * nq + qi, h)),
        out_shape=jax.ShapeDtypeStruct((t, hp), F32),
        scratch_shapes=[pltpu.VMEM((HEAD_PAD, tq), F32)],
        compiler_params=pltpu.CompilerParams(
            dimension_semantics=("arbitrary", "arbitrary", "arbitrary"),
            vmem_limit_bytes=VMEM_LIMIT),
        name="attn",
    )(q, k, vt)


def _outproj_kernel(pool_ref, attn_ref, x_ref, gattn_ref, wop_ref, woa_ref, gffn_ref, wr_ref,
                    br_ref, x1_out, hp_out, idx_out, gate_out, rank_out, cnt_out, run_scr):
    tm, d = x_ref.shape
    d_attn = N_HEADS * V_HEAD_DIM
    i = pl.program_id(0)

    @pl.when(i == 0)
    def _():
        run_scr[...] = jnp.zeros(run_scr.shape, F32)

    attn = attn_ref[...]
    attn_n = _rms(attn, gattn_ref[...], d_attn).astype(BF16)
    mix = (jnp.dot(pool_ref[...], wop_ref[...], preferred_element_type=F32)
           + jnp.dot(attn_n, woa_ref[...], preferred_element_type=F32))
    x1 = x_ref[...] + mix
    x1_out[...] = x1
    h2 = _rms(x1, gffn_ref[...], d)

    hp_out[...] = _pack_bf16_pairs(h2)

    logits = lax.dot_general(wr_ref[...], h2, (((1,), (1,)), ((), ())),
                             preferred_element_type=F32,
                             precision=lax.Precision.HIGHEST) + br_ref[...]
    e_iota = lax.broadcasted_iota(jnp.int32, (N_EXPERTS, tm), 0)
    work = logits
    idxs, vals = [], []
    for _ in range(TOP_K):
        mx = jnp.max(work, axis=0, keepdims=True)
        sel = jnp.min(jnp.where(work == mx, e_iota, N_EXPERTS), axis=0, keepdims=True)
        idxs.append(sel)
        vals.append(mx)
        work = jnp.where(e_iota == sel, -jnp.inf, work)
    exps = [jnp.exp(v - vals[0]) for v in vals]
    denom = exps[0] + exps[1] + exps[2] + exps[3]
    gate_out[...] = jnp.concatenate([e / denom for e in exps], axis=0)
    idx_out[...] = jnp.concatenate(idxs, axis=0)

    onehot = jnp.zeros((N_EXPERTS, tm), F32)
    for sel in idxs:
        onehot = onehot + (e_iota == sel).astype(F32)
    r_io = lax.broadcasted_iota(jnp.int32, (tm, tm), 0)
    c_io = lax.broadcasted_iota(jnp.int32, (tm, tm), 1)
    upper = (r_io < c_io).astype(BF16)
    prefix = jnp.dot(onehot.astype(BF16), upper, preferred_element_type=F32) + run_scr[:, 0:1]
    ranks = [jnp.sum(jnp.where(e_iota == sel, prefix, 0.0), axis=0, keepdims=True) for sel in idxs]
    rank_out[...] = jnp.concatenate(ranks, axis=0).astype(jnp.int32)
    run_new = run_scr[...] + jnp.sum(onehot, axis=1, keepdims=True)
    run_scr[...] = run_new
    cnt_out[...] = run_new.astype(jnp.int32)


def _outproj_call(pool_n, attn, x2, gattn_p, wo_pool, wo_attn_p, gffn, wr_t, br_col, tm):
    t, d = x2.shape
    nblk = t // tm
    const2 = lambda i: (0, 0)
    row = lambda i: (i, 0)
    col = lambda i: (0, i)
    in_specs = [
        pl.BlockSpec((tm, pool_n.shape[1]), row),
        pl.BlockSpec((tm, attn.shape[1]), row),
        pl.BlockSpec((tm, d), row),
        pl.BlockSpec(gattn_p.shape, const2),
        pl.BlockSpec(wo_pool.shape, const2),
        pl.BlockSpec(wo_attn_p.shape, const2),
        pl.BlockSpec(gffn.shape, const2),
        pl.BlockSpec(wr_t.shape, const2),
        pl.BlockSpec(br_col.shape, const2),
    ]
    out_shape = [
        jax.ShapeDtypeStruct((t, d), F32),
        jax.ShapeDtypeStruct((t, d // 2), jnp.uint32),
        jax.ShapeDtypeStruct((TOP_K, t), jnp.int32),
        jax.ShapeDtypeStruct((TOP_K, t), F32),
        jax.ShapeDtypeStruct((TOP_K, t), jnp.int32),
        jax.ShapeDtypeStruct((N_EXPERTS, LANE), jnp.int32),
    ]
    out_specs = [
        pl.BlockSpec((tm, d), row),
        pl.BlockSpec((tm, d // 2), row),
        pl.BlockSpec((TOP_K, tm), col),
        pl.BlockSpec((TOP_K, tm), col),
        pl.BlockSpec((TOP_K, tm), col),
        pl.BlockSpec((N_EXPERTS, LANE), const2),
    ]
    return pl.pallas_call(
        _outproj_kernel,
        grid=(nblk,),
        in_specs=in_specs,
        out_specs=out_specs,
        out_shape=out_shape,
        scratch_shapes=[pltpu.VMEM((N_EXPERTS, LANE), F32)],
        compiler_params=pltpu.CompilerParams(dimension_semantics=("arbitrary",),
                                             vmem_limit_bytes=VMEM_LIMIT),
        name="outproj",
    )(pool_n, attn, x2, gattn_p, wo_pool, wo_attn_p, gffn, wr_t, br_col)


ROW_UNROLL = 8


def _dispatch_kernel(nw, bs_ref, nblk_ref, dest_ref, h_ref, xs_out, zbuf, sem, zsem):
    td = h_ref.shape[0] // nw
    blk_rows = zbuf.shape[0]
    n_exp = bs_ref.shape[0]

    @pl.when(pl.program_id(0) == 0)
    def _():
        zbuf[...] = jnp.zeros(zbuf.shape, zbuf.dtype)
        total = xs_out.shape[0] // blk_rows
        first_tail = bs_ref[n_exp - 1] + nblk_ref[n_exp - 1]

        def z_copy(blk):
            row0 = pl.multiple_of(blk * blk_rows, blk_rows)
            return pltpu.make_async_copy(zbuf, xs_out.at[pl.ds(row0, blk_rows)], zsem)

        def each_partial_block(fn):
            def per_expert(e, c):
                @pl.when(nblk_ref[e] > 0)
                def _():
                    fn(z_copy(bs_ref[e] + nblk_ref[e] - 1))
                return c

            def per_tail(b, c):
                fn(z_copy(b))
                return c

            lax.fori_loop(0, n_exp, per_expert, 0)
            lax.fori_loop(first_tail, total, per_tail, 0)

        each_partial_block(lambda cp: cp.start())
        each_partial_block(lambda cp: cp.wait())

    def body(g, carry):
        base = g * ROW_UNROLL
        for u in range(ROW_UNROLL):
            i = base + u
            src = h_ref.at[pl.ds(pl.multiple_of(i * nw, nw), nw)]
            for k in range(TOP_K):
                d = dest_ref[i * TOP_K + k]
                pltpu.make_async_copy(src, xs_out.at[pl.ds(pl.multiple_of(d * nw, nw), nw)],
                                      sem).start(priority=k % 2)
        return carry

    lax.fori_loop(0, td // ROW_UNROLL, body, 0)
    for _ in range(TOP_K):
        pltpu.make_async_copy(h_ref, xs_out.at[pl.ds(0, td * nw)], sem).wait()


def _dispatch_call(blk_start, nblk, dest, hp, n_rows, nw, td, bm):
    t = hp.shape[0] // nw
    grid_spec = pltpu.PrefetchScalarGridSpec(
        num_scalar_prefetch=2,
        grid=(t // td,),
        in_specs=[
            pl.BlockSpec((TOP_K * td,), lambda i, bs, nb: (i,), memory_space=pltpu.SMEM),
            pl.BlockSpec((td * nw, LANE), lambda i, bs, nb: (i, 0)),
        ],
        out_specs=pl.BlockSpec(memory_space=pl.ANY),
        scratch_shapes=[pltpu.VMEM((bm * nw, LANE), hp.dtype), pltpu.SemaphoreType.DMA,
                        pltpu.SemaphoreType.DMA],
    )
    return pl.pallas_call(
        functools.partial(_dispatch_kernel, nw),
        grid_spec=grid_spec,
        out_shape=jax.ShapeDtypeStruct((n_rows * nw, LANE), hp.dtype),
        compiler_params=pltpu.CompilerParams(dimension_semantics=("arbitrary",),
                                             vmem_limit_bytes=VMEM_LIMIT),
        name="dispatch",
    )(blk_start, nblk, dest, hp)


SC_WINDOW = 64


def _window_indices(dest_kt):
    t = dest_kt.shape[1]
    return dest_kt.reshape(TOP_K, t // SC_WINDOW, SC_WINDOW).transpose(1, 0, 2)


def _dispatch_sc_call(idx, hp_rows, n_rows):
    t, w = hp_rows.shape
    mesh = plsc.VectorSubcoreMesh(core_axis_name="core", subcore_axis_name="subcore")

    @functools.partial(pl.kernel, out_type=jax.ShapeDtypeStruct((n_rows, w), hp_rows.dtype),
                       mesh=mesh, scratch_types=[])
    def scatter_kernel(x_hbm, i_hbm, o_hbm):
        def body(x_vmem, i_vmem):
            for k in range(TOP_K):
                pltpu.sync_copy(x_vmem, o_hbm.at[i_vmem.at[0, k]])

        pltpu.emit_pipeline(
            body,
            grid=(t // SC_WINDOW,),
            in_specs=[pl.BlockSpec((SC_WINDOW, w), lambda i: (i, 0)),
                      pl.BlockSpec((1, TOP_K, SC_WINDOW), lambda i: (i, 0, 0))],
            out_specs=[],
            core_axis_name=("core", "subcore"),
            dimension_semantics=(pltpu.PARALLEL,),
        )(x_hbm, i_hbm)

    return scatter_kernel(hp_rows, idx)


def _gather_sc_call(idx, rows):
    nwin, _, _ = idx.shape
    t = nwin * SC_WINDOW
    wd = rows.shape[1]
    mesh = plsc.VectorSubcoreMesh(core_axis_name="core", subcore_axis_name="subcore")

    @functools.partial(pl.kernel, out_type=jax.ShapeDtypeStruct((TOP_K, t, wd), rows.dtype),
                       mesh=mesh, scratch_types=[])
    def gather_kernel(y_hbm, i_hbm, o_hbm):
        def body(i_vmem, o_vmem):
            pltpu.sync_copy(y_hbm.at[i_vmem.at[0, 0]], o_vmem.at[0])

        pltpu.emit_pipeline(
            body,
            grid=(nwin, TOP_K),
            in_specs=[pl.BlockSpec((1, 1, SC_WINDOW), lambda i, k: (i * TOP_K + k, 0, 0))],
            out_specs=[pl.BlockSpec((1, SC_WINDOW, wd), lambda i, k: (k, i, 0))],
            core_axis_name=("core", "subcore"),
            dimension_semantics=(pltpu.PARALLEL, pltpu.PARALLEL),
        )(i_hbm, o_hbm)

    return gather_kernel(rows, idx.reshape(nwin * TOP_K, 1, SC_WINDOW))


def _combine_sum_kernel(x1_ref, gate_ref, gfin_ref, yg_ref, o_ref):
    d = x1_ref.shape[1]
    g = gate_ref[...]
    y = x1_ref[...]
    for k in range(TOP_K):
        y = y + g[:, k:k + 1] * jnp.concatenate(_unpack_bf16_pairs(yg_ref[k]), axis=1)
    o_ref[...] = _rms(y, gfin_ref[...], d)


def _combine_sum_call(x1, gates_t, gfin, yg, tf):
    t, d = x1.shape
    return pl.pallas_call(
        _combine_sum_kernel,
        grid=(t // tf,),
        in_specs=[
            pl.BlockSpec((tf, d), lambda i: (i, 0)),
            pl.BlockSpec((tf, TOP_K), lambda i: (i, 0)),
            pl.BlockSpec(gfin.shape, lambda i: (0, 0)),
            pl.BlockSpec((TOP_K, tf, d // 2), lambda i: (0, i, 0)),
        ],
        out_specs=pl.BlockSpec((tf, d), lambda i: (i, 0)),
        out_shape=jax.ShapeDtypeStruct((t, d), F32),
        compiler_params=pltpu.CompilerParams(dimension_semantics=("arbitrary",),
                                             vmem_limit_bytes=VMEM_LIMIT),
        name="combine",
    )(x1, gates_t, gfin, yg)


CAST_ROWS = 64


def _expert_kernel(bs_ref, nblk_ref, cnt_ref, bgu_ref, bd_ref, wgu_hbm, wd_hbm, xs_hbm, y_hbm,
                   wgu_f32, wd_f32, wgu_bf, wd_bf, xbuf, ybuf, xsem, ysem, wsem):
    e = pl.program_id(0)
    n_exp = pl.num_programs(0)
    d = wgu_f32.shape[1]
    f = wd_f32.shape[1]
    bm = xbuf.shape[1]
    n = nblk_ref[e]
    blk0 = bs_ref[e]
    wslot = e % 2

    def w_copies(ex, slot):
        return (pltpu.make_async_copy(wgu_hbm.at[ex], wgu_f32.at[slot], wsem.at[0, slot]),
                pltpu.make_async_copy(wd_hbm.at[ex], wd_f32.at[slot], wsem.at[1, slot]))

    def x_copy(j, slot):
        row0 = pl.multiple_of((blk0 + j) * bm, bm)
        return pltpu.make_async_copy(xs_hbm.at[pl.ds(row0, bm)], xbuf.at[slot], xsem.at[slot])

    def y_copy(blk, slot):
        row0 = pl.multiple_of(blk * bm, bm)
        return pltpu.make_async_copy(ybuf.at[slot], y_hbm.at[pl.ds(row0, bm)], ysem.at[slot])

    @pl.when(n > 0)
    def _():
        x_copy(0, 0).start()

    @pl.when(e == 0)
    def _():
        for cp in w_copies(0, 0):
            cp.start(priority=1)

    @pl.when(e + 1 < n_exp)
    def _():
        for cp in w_copies(e + 1, 1 - wslot):
            cp.start(priority=1)

    for cp in w_copies(e, wslot):
        cp.wait()

    @pl.when(n > 0)
    def _():
        def cast_gu(r, c):
            rows = pl.ds(pl.multiple_of(r * CAST_ROWS, CAST_ROWS), CAST_ROWS)
            wgu_bf[rows, :] = wgu_f32[wslot, rows, :].astype(BF16)
            return c

        def cast_d(r, c):
            rows = pl.ds(pl.multiple_of(r * CAST_ROWS, CAST_ROWS), CAST_ROWS)
            wd_bf[rows, :] = wd_f32[wslot, rows, :].astype(BF16)
            return c

        lax.fori_loop(0, d // CAST_ROWS, cast_gu, 0)
        lax.fori_loop(0, f // CAST_ROWS, cast_d, 0)

        def block(j, c):
            slot = j % 2
            x_copy(j, slot).wait()

            @pl.when(j + 1 < n)
            def _():
                x_copy(j + 1, 1 - slot).start()

            @pl.when(j >= 2)
            def _():
                y_copy(blk0 + j - 2, slot).wait()

            row = lax.broadcasted_iota(jnp.int32, (bm, 1), 0)
            w = jnp.where(row < cnt_ref[e] - j * bm, xbuf[slot], jnp.uint32(0))
            x_lo, x_hi = _unpack_bf16_pairs(w)
            gu = (jnp.dot(x_lo.astype(BF16), wgu_bf[:d // 2, :], preferred_element_type=F32)
                  + jnp.dot(x_hi.astype(BF16), wgu_bf[d // 2:, :], preferred_element_type=F32)
                  + bgu_ref[0])
            gate = jnp.minimum(gu[:, :f], SWIGLU_LIMIT)
            up = jnp.clip(gu[:, f:], -SWIGLU_LIMIT, SWIGLU_LIMIT)
            act = gate * jax.nn.sigmoid(SWIGLU_ALPHA * gate) * (up + 1.0)
            y = jnp.dot(act.astype(BF16), wd_bf[...], preferred_element_type=F32) + bd_ref[0]
            ybuf[slot] = _pack_bf16_pairs(y)
            y_copy(blk0 + j, slot).start()
            return c

        lax.fori_loop(0, n, block, 0)

        @pl.when(n >= 2)
        def _():
            y_copy(blk0 + n - 2, n % 2).wait()

        y_copy(blk0 + n - 1, (n - 1) % 2).wait()

    @pl.when(e == n_exp - 1)
    def _():
        first = blk0 + n
        total = y_hbm.shape[0] // bm
        ybuf[0] = jnp.zeros(ybuf.shape[1:], ybuf.dtype)

        def fill(b, c):
            y_copy(b, 0).start()
            return c

        def drain(b, c):
            y_copy(b, 0).wait()
            return c

        lax.fori_loop(first, total, fill, 0)
        lax.fori_loop(first, total, drain, 0)


def _expert_call(blk_start, nblk, cnt, xs, wgu, bgu, wd, bd, bm):
    e, d, f2 = wgu.shape
    f = wd.shape[1]
    r, dw = xs.shape

    def bsel(i, bs, nb, ct):
        return (i, 0, 0)

    grid_spec = pltpu.PrefetchScalarGridSpec(
        num_scalar_prefetch=3,
        grid=(e,),
        in_specs=[
            pl.BlockSpec((1, 1, f2), bsel),
            pl.BlockSpec((1, 1, d), bsel),
            pl.BlockSpec(memory_space=pl.ANY),
            pl.BlockSpec(memory_space=pl.ANY),
            pl.BlockSpec(memory_space=pl.ANY),
        ],
        out_specs=pl.BlockSpec(memory_space=pl.ANY),
        scratch_shapes=[pltpu.VMEM((2, d, f2), F32), pltpu.VMEM((2, f, d), F32),
                        pltpu.VMEM((d, f2), BF16), pltpu.VMEM((f, d), BF16),
                        pltpu.VMEM((2, bm, dw), xs.dtype),
                        pltpu.VMEM((2, bm, dw), jnp.uint32),
                        pltpu.SemaphoreType.DMA((2,)), pltpu.SemaphoreType.DMA((2,)),
                        pltpu.SemaphoreType.DMA((2, 2))],
    )
    return pl.pallas_call(
        _expert_kernel,
        grid_spec=grid_spec,
        out_shape=jax.ShapeDtypeStruct((r, dw), jnp.uint32),
        compiler_params=pltpu.CompilerParams(dimension_semantics=("arbitrary",),
                                             vmem_limit_bytes=VMEM_LIMIT),
        name="experts",
    )(blk_start, nblk, cnt, bgu, bd, wgu, wd, xs)


def _combine_kernel(dest_ref, dest_next_ref, x1_ref, gate_ref, gfin_ref, yb_hbm, o_ref, ybuf, sem):
    tf, d = x1_ref.shape
    ns = d // 2 // LANE
    j = pl.program_id(0)
    nsteps = pl.num_programs(0)
    slot = j % 2

    def issue(idx_ref, s):
        def body(g, carry):
            base = g * ROW_UNROLL
            for u in range(ROW_UNROLL):
                i = base + u
                for k in range(TOP_K):
                    r = idx_ref[i * TOP_K + k]
                    pltpu.make_async_copy(yb_hbm.at[pl.ds(pl.multiple_of(r * ns, ns), ns)],
                                          ybuf.at[s, k, pl.ds(pl.multiple_of(i * ns, ns), ns)],
                                          sem.at[s]).start(priority=k % 2)
            return carry

        lax.fori_loop(0, tf // ROW_UNROLL, body, 0)

    @pl.when(j == 0)
    def _():
        issue(dest_ref, 0)

    @pl.when(j + 1 < nsteps)
    def _():
        issue(dest_next_ref, 1 - slot)

    for k in range(TOP_K):
        pltpu.make_async_copy(yb_hbm.at[pl.ds(0, tf * ns)], ybuf.at[slot, k], sem.at[slot]).wait()
    g = gate_ref[...]
    y = x1_ref[...]
    for k in range(TOP_K):
        yw = jnp.concatenate([ybuf[slot, k, pl.ds(c, tf, stride=ns), :] for c in range(ns)], axis=1)
        y = y + g[:, k:k + 1] * jnp.concatenate(_unpack_bf16_pairs(yw), axis=1)
    o_ref[...] = _rms(y, gfin_ref[...], d)


def _combine_call(dest, x1, gates_t, gfin, yb, tf):
    t, d = x1.shape
    nsteps = t // tf
    return pl.pallas_call(
        _combine_kernel,
        grid=(nsteps,),
        in_specs=[
            pl.BlockSpec((TOP_K * tf,), lambda i: (i,), memory_space=pltpu.SMEM),
            pl.BlockSpec((TOP_K * tf,), lambda i: (jnp.minimum(i + 1, nsteps - 1),),
                         memory_space=pltpu.SMEM),
            pl.BlockSpec((tf, d), lambda i: (i, 0)),
            pl.BlockSpec((tf, TOP_K), lambda i: (i, 0)),
            pl.BlockSpec(gfin.shape, lambda i: (0, 0)),
            pl.BlockSpec(memory_space=pl.ANY),
        ],
        out_specs=pl.BlockSpec((tf, d), lambda i: (i, 0)),
        out_shape=jax.ShapeDtypeStruct((t, d), F32),
        scratch_shapes=[pltpu.VMEM((2, TOP_K, tf * (d // 2 // LANE), LANE), jnp.uint32),
                        pltpu.SemaphoreType.DMA((2,))],
        compiler_params=pltpu.CompilerParams(dimension_semantics=("arbitrary",),
                                             vmem_limit_bytes=VMEM_LIMIT),
        name="combine",
    )(dest, dest, x1, gates_t, gfin, yb)


def _pad_head_cols(w, widths_in, place):
    k = w.shape[0]
    per = sum(widths_in)
    blocks = []
    for hh in range(N_HEADS):
        blocks.append(place(w[:, hh * per:(hh + 1) * per]))
    return jnp.concatenate(blocks, axis=1)


def _layer(x2, pos3, batch, seq, g_attn_norm, w_in, w_pool, b_pool, pool_scale, g_q_a, w_q_b,
           g_kv_a, w_kv_b, g_out_pool, g_out_attn, w_out, g_ffn_norm, w_router, b_router,
           w_gate_up, b_gate_up, w_down, b_down, tiles):
    tm, tq, tmo, td, bm, tf = tiles
    t, d = x2.shape
    d_pool = pool_scale.shape[0]
    gdim = d_pool // N_POOL_GROUPS
    q_rank = g_q_a.shape[0]
    kv_rank = g_kv_a.shape[0]
    half = QK_ROPE_DIM // 2

    o1, o2, o3 = d_pool, d_pool + q_rank, d_pool + q_rank + kv_rank
    zk = lambda n: jnp.zeros((d, n), F32)
    w_kr1, w_kr2 = w_in[:, o3:o3 + half], w_in[:, o3 + half:]
    w_in_p = jnp.concatenate([w_in[:, :o3], zk(QK_NOPE_DIM), w_kr1, w_kr2, zk(LANE - QK_DIM),
                              zk(QK_NOPE_DIM), w_kr2, w_kr1, zk(LANE - QK_DIM)], axis=1).astype(BF16)
    zq = lambda n: jnp.zeros((q_rank, n), F32)
    wq_plain = _pad_head_cols(w_q_b, (QK_DIM,),
                              lambda c: jnp.concatenate([c, zq(LANE - QK_DIM)], axis=1))
    wq_swap = _pad_head_cols(w_q_b, (QK_DIM,), lambda c: jnp.concatenate(
        [zq(QK_NOPE_DIM), c[:, QK_NOPE_DIM + half:], c[:, QK_NOPE_DIM:QK_NOPE_DIM + half],
         zq(LANE - QK_DIM)], axis=1))
    wq_p = jnp.concatenate([wq_plain, wq_swap], axis=1).astype(BF16)
    zkv = lambda n: jnp.zeros((kv_rank, n), F32)
    per = QK_NOPE_DIM + V_HEAD_DIM
    wk_p = _pad_head_cols(w_kv_b, (per,), lambda c: jnp.concatenate(
        [c[:, :QK_NOPE_DIM], zkv(LANE - QK_NOPE_DIM)], axis=1))
    wv_p = _pad_head_cols(w_kv_b, (per,), lambda c: jnp.concatenate(
        [c[:, QK_NOPE_DIM:], zkv(LANE - V_HEAD_DIM)], axis=1))
    wkv_p = jnp.concatenate([wk_p, wv_p], axis=1).astype(BF16)
    inv = ROPE_THETA ** (-jnp.arange(half, dtype=F32) / half)
    inv_col = inv.reshape(half, 1)

    row = lambda v: v.reshape(1, -1)
    grp = lambda v: v.reshape(N_POOL_GROUPS, 1, gdim)

    pool_n, q, k, v = _proj_call(
        x2, pos3, row(g_attn_norm), w_in_p, inv_col, w_pool.astype(BF16), grp(b_pool),
        grp(pool_scale), grp(g_out_pool), row(g_q_a), wq_p, row(g_kv_a), wkv_p, seq, tm)

    attn = _attn_call(q, k, v, batch, seq, tq)

    d_attn = N_HEADS * V_HEAD_DIM
    wo_pool = w_out[:d_pool].astype(BF16)
    wo_attn = w_out[d_pool:].reshape(N_HEADS, V_HEAD_DIM, d)
    wo_attn_p = jnp.concatenate([wo_attn, jnp.zeros((N_HEADS, HEAD_PAD - V_HEAD_DIM, d), F32)],
                                axis=1).reshape(N_HEADS * HEAD_PAD, d).astype(BF16)
    gattn_p = jnp.concatenate([g_out_attn.reshape(N_HEADS, V_HEAD_DIM),
                               jnp.zeros((N_HEADS, HEAD_PAD - V_HEAD_DIM), F32)], axis=1).reshape(1, -1)

    x1, hp, idx, gates, rank, counts = _outproj_call(
        pool_n, attn, x2, gattn_p, wo_pool, wo_attn_p, row(g_ffn_norm), w_router.T,
        b_router.reshape(-1, 1), tmo)

    n_assign = t * TOP_K
    cnt = counts[:, 0]
    padded = ((cnt + bm - 1) // bm) * bm
    pad_end = jnp.cumsum(padded)
    pad_start = pad_end - padded
    e_ids = jnp.arange(N_EXPERTS, dtype=jnp.int32)
    dest = rank + jnp.sum(jnp.where(idx[None] == e_ids[:, None, None], pad_start[:, None, None], 0),
                          axis=0)
    win_idx = _window_indices(dest)
    n_rows = (n_assign // bm + N_EXPERTS) * bm
    blk_start = (pad_start // bm).astype(jnp.int32)
    nblk = (padded // bm).astype(jnp.int32)

    xs = _dispatch_sc_call(win_idx, hp, n_rows)
    yb = _expert_call(blk_start, nblk, cnt.astype(jnp.int32), xs, w_gate_up,
                      b_gate_up.reshape(N_EXPERTS, 1, -1), w_down, b_down.reshape(N_EXPERTS, 1, -1), bm)
    yg = _gather_sc_call(win_idx, yb)
    return x1, gates, yg


def _forward(x, positions, g_attn_norm, w_in, w_pool, b_pool, pool_scale, g_q_a, w_q_b, g_kv_a,
             w_kv_b, g_out_pool, g_out_attn, w_out, g_ffn_norm, w_router, b_router, w_gate_up,
             b_gate_up, w_down, b_down, g_final, tiles):
    batch, seq, d = x.shape
    t = batch * seq
    tm, tq, tmo, td, bm, tf = tiles
    x2 = x.reshape(t, d)
    pos3 = positions.reshape(t // tm, 1, tm)
    depth = g_attn_norm.shape[0]
    assert depth == 1
    l = 0
    x1, gates, yg = _layer(
        x2, pos3, batch, seq, g_attn_norm[l], w_in[l], w_pool[l], b_pool[l], pool_scale[l],
        g_q_a[l], w_q_b[l], g_kv_a[l], w_kv_b[l], g_out_pool[l], g_out_attn[l], w_out[l],
        g_ffn_norm[l], w_router[l], b_router[l], w_gate_up[l], b_gate_up[l], w_down[l],
        b_down[l], tiles)
    out = _combine_sum_call(x1, gates.T, g_final.reshape(1, -1), yg, tf)
    return out.reshape(batch, seq, d)


def _tiles_for(seq):
    tm = min(512, seq)
    tq = min(512, seq // 2)
    tmo = min(512, seq)
    td = min(512, seq)
    bm = 256 if seq >= 2048 else 128
    tf = min(256, seq)
    return (tm, tq, tmo, td, bm, tf)


def kernel(x, positions, g_attn_norm, w_in, w_pool, b_pool, pool_scale, g_q_a, w_q_b, g_kv_a, w_kv_b, g_out_pool, g_out_attn, w_out, g_ffn_norm, w_router, b_router, w_gate_up, b_gate_up, w_down, b_down, g_final):
    tiles = _tiles_for(x.shape[1])
    return _forward(x, positions, g_attn_norm, w_in, w_pool, b_pool, pool_scale, g_q_a, w_q_b,
                    g_kv_a, w_kv_b, g_out_pool, g_out_attn, w_out, g_ffn_norm, w_router, b_router,
                    w_gate_up, b_gate_up, w_down, b_down, g_final, tiles)
```

```python
import functools

import jax
import jax.numpy as jnp
from jax import lax
from jax.experimental import pallas as pl
from jax.experimental.pallas import tpu as pltpu
from jax.experimental.pallas import tpu_sc as plsc

POOL_WINDOWS = (2, 4, 8, 16)
N_POOL_GROUPS = 4
N_HEADS = 8
V_HEAD_DIM = 64
QK_NOPE_DIM = 64
QK_ROPE_DIM = 32
QK_DIM = QK_NOPE_DIM + QK_ROPE_DIM
ROPE_THETA = 10000.0
N_EXPERTS = 32
TOP_K = 4
SWIGLU_ALPHA = 1.702
SWIGLU_LIMIT = 7.0
RMS_EPS = 1e-6
LOG2_E = 1.4426950408889634

LANE = 128
HEAD_PAD = 128
POOL_HALO = 16

VMEM_LIMIT = 56 * 1024 * 1024

F32 = jnp.float32
BF16 = jnp.bfloat16


def _pack_bf16_pairs(x):
    n = x.shape[1] // 2
    xb = x.astype(BF16).astype(F32)
    lo = lax.bitcast_convert_type(xb[:, :n], jnp.uint32)
    hi = lax.bitcast_convert_type(xb[:, n:], jnp.uint32)
    return (lo >> 16) | (hi & jnp.uint32(0xFFFF0000))


def _unpack_bf16_pairs(w):
    lo = lax.bitcast_convert_type(w << 16, F32)
    hi = lax.bitcast_convert_type(w & jnp.uint32(0xFFFF0000), F32)
    return lo, hi


def _rms(x, g, n):
    ms = jnp.sum(x * x, axis=-1, keepdims=True) * (1.0 / n)
    return x * lax.rsqrt(ms + RMS_EPS) * g


def _proj_kernel(blocks_per_seq, x_ref, pos_ref, g_attn_ref, w_in_ref, inv_ref, wpool_ref,
                 bpool_ref, pscale_ref, gpool_ref, gq_ref, wq_ref, gkv_ref, wkv_ref,
                 pool_out, q_out, k_out, v_out, ext_scr):
    tm = x_ref.shape[0]
    d_model = x_ref.shape[1]
    d_pool = pool_out.shape[1]
    gdim = d_pool // N_POOL_GROUPS
    q_rank = gq_ref.shape[1]
    kv_rank = gkv_ref.shape[1]
    i = pl.program_id(0)
    blk_in_seq = i % blocks_per_seq

    x = x_ref[...]
    h = _rms(x, g_attn_ref[...], d_model).astype(BF16)
    proj = jnp.dot(h, w_in_ref[...], preferred_element_type=F32)
    u = proj[:, :d_pool]
    qlat = proj[:, d_pool:d_pool + q_rank]
    kvlat = proj[:, d_pool + q_rank:d_pool + q_rank + kv_rank]
    o_kr = d_pool + q_rank + kv_rank
    krb = proj[:, o_kr:o_kr + LANE]
    krs = proj[:, o_kr + LANE:]

    @pl.when(blk_in_seq == 0)
    def _():
        ext_scr[0:POOL_HALO, :] = jnp.zeros((POOL_HALO, d_pool), F32)

    @pl.when(blk_in_seq != 0)
    def _():
        ext_scr[0:POOL_HALO, :] = ext_scr[tm:tm + POOL_HALO, :]

    ext_scr[POOL_HALO:, :] = u
    t_in_seq = (blk_in_seq * tm + lax.broadcasted_iota(jnp.int32, (tm, gdim), 0) + 1).astype(F32)
    ys = []
    ssq = jnp.zeros((tm, 1), F32)
    for g, w in enumerate(POOL_WINDOWS):
        a = ext_scr[:, g * gdim:(g + 1) * gdim]
        shift = 1
        while shift < w:
            a = a + pltpu.roll(a, shift, axis=0)
            shift *= 2
        win = a[POOL_HALO:, :]
        cnt = jnp.minimum(t_in_seq, float(w))
        ug = u[:, g * gdim:(g + 1) * gdim]
        diff = (win / cnt - ug).astype(BF16)
        mixed = jnp.dot(diff, wpool_ref[g], preferred_element_type=F32) + bpool_ref[g]
        y = mixed * pscale_ref[g]
        ssq = ssq + jnp.sum(y * y, axis=-1, keepdims=True)
        ys.append(y)
    rinv = lax.rsqrt(ssq * (1.0 / d_pool) + RMS_EPS)
    for g in range(N_POOL_GROUPS):
        pool_out[:, g * gdim:(g + 1) * gdim] = (ys[g] * rinv * gpool_ref[g]).astype(pool_out.dtype)

    pos = pos_ref[0].astype(F32)
    ang_t = inv_ref[...] * pos
    cos_t = jnp.cos(ang_t)
    sin_t = jnp.sin(ang_t)
    half = QK_ROPE_DIM // 2
    ones = jnp.ones((QK_NOPE_DIM, tm), F32)
    c_t = jnp.concatenate([ones, cos_t, cos_t, jnp.ones((LANE - QK_DIM, tm), F32)], axis=0)
    s_t = jnp.concatenate([jnp.zeros((QK_NOPE_DIM, tm), F32), -sin_t, sin_t,
                           jnp.zeros((LANE - QK_DIM, tm), F32)], axis=0)
    c = c_t.T
    s = s_t.T

    def rope(blk, blk_swapped, c_, s_):
        return blk * c_ + blk_swapped * s_

    qn = _rms(qlat, gq_ref[...], q_rank).astype(BF16)
    q2 = jnp.dot(qn, wq_ref[...], preferred_element_type=F32)
    scale = QK_DIM ** -0.5 * LOG2_E
    cq, sq = c * scale, s * scale
    hp = N_HEADS * HEAD_PAD
    for hh in range(N_HEADS):
        sl = slice(hh * HEAD_PAD, (hh + 1) * HEAD_PAD)
        sw = slice(hp + hh * HEAD_PAD, hp + (hh + 1) * HEAD_PAD)
        q_out[:, sl] = rope(q2[:, sl], q2[:, sw], cq, sq).astype(q_out.dtype)

    kvn = _rms(kvlat, gkv_ref[...], kv_rank).astype(BF16)
    kv = jnp.dot(kvn, wkv_ref[...], preferred_element_type=F32)
    kr = rope(krb, krs, c, s)
    for hh in range(N_HEADS):
        sl = slice(hh * HEAD_PAD, (hh + 1) * HEAD_PAD)
        k_out[:, sl] = (kv[:, sl] + kr).astype(k_out.dtype)
    vlane = lax.broadcasted_iota(jnp.int32, (1, N_HEADS * HEAD_PAD), 1) % HEAD_PAD
    ones_col = (vlane == V_HEAD_DIM).astype(F32)
    v_out[0] = (kv[:, N_HEADS * HEAD_PAD:] + ones_col).T.astype(v_out.dtype)


def _proj_call(x2, pos3, g_attn, w_in_p, inv_col, wpool, bpool, pscale, gpool, gq, wq_p, gkv, wkv_p,
               seq, tm):
    t, d = x2.shape
    d_pool = pscale.shape[0] * pscale.shape[2]
    hp = N_HEADS * HEAD_PAD
    nblk = t // tm
    const2 = lambda i: (0, 0)
    const3 = lambda i: (0, 0, 0)
    in_specs = [
        pl.BlockSpec((tm, d), lambda i: (i, 0)),
        pl.BlockSpec((1, 1, tm), lambda i: (i, 0, 0)),
        pl.BlockSpec(g_attn.shape, const2),
        pl.BlockSpec(w_in_p.shape, const2),
        pl.BlockSpec(inv_col.shape, const2),
        pl.BlockSpec(wpool.shape, const3),
        pl.BlockSpec(bpool.shape, const3),
        pl.BlockSpec(pscale.shape, const3),
        pl.BlockSpec(gpool.shape, const3),
        pl.BlockSpec(gq.shape, const2),
        pl.BlockSpec(wq_p.shape, const2),
        pl.BlockSpec(gkv.shape, const2),
        pl.BlockSpec(wkv_p.shape, const2),
    ]
    out_shape = [
        jax.ShapeDtypeStruct((t, d_pool), BF16),
        jax.ShapeDtypeStruct((t, hp), BF16),
        jax.ShapeDtypeStruct((t, hp), BF16),
        jax.ShapeDtypeStruct((nblk, hp, tm), BF16),
    ]
    out_specs = [
        pl.BlockSpec((tm, d_pool), lambda i: (i, 0)),
        pl.BlockSpec((tm, hp), lambda i: (i, 0)),
        pl.BlockSpec((tm, hp), lambda i: (i, 0)),
        pl.BlockSpec((1, hp, tm), lambda i: (i, 0, 0)),
    ]
    return pl.pallas_call(
        functools.partial(_proj_kernel, seq // tm),
        grid=(nblk,),
        in_specs=in_specs,
        out_specs=out_specs,
        out_shape=out_shape,
        scratch_shapes=[pltpu.VMEM((tm + POOL_HALO, d_pool), F32)],
        compiler_params=pltpu.CompilerParams(dimension_semantics=("arbitrary",),
                                             vmem_limit_bytes=VMEM_LIMIT),
        name="proj",
    )(x2, pos3, g_attn, w_in_p, inv_col, wpool, bpool, pscale, gpool, gq, wq_p, gkv, wkv_p)


ROWSUM_ROW = V_HEAD_DIM


def _attn_kernel(q_ref, k_ref, vt_ref, o_ref, acc_ref):
    tq = q_ref.shape[0]
    th = vt_ref.shape[2]
    nsub = tq // th
    qi = pl.program_id(2)
    acc_ref[...] = jnp.zeros(acc_ref.shape, F32)

    def step(kc, m_prev, q_lo, masked):
        width = tq - q_lo
        start = pl.multiple_of(kc * th, th)
        k = k_ref[pl.ds(start, th), :]
        st = lax.dot_general(k, q_ref[q_lo:, :], (((1,), (1,)), ((), ())),
                             preferred_element_type=F32)
        if masked:
            key = lax.broadcasted_iota(jnp.int32, (th, width), 0)
            qry = lax.broadcasted_iota(jnp.int32, (th, width), 1)
            st = jnp.where(key <= qry, st, -jnp.inf)
        m_new = jnp.maximum(m_prev, jnp.max(st, axis=0, keepdims=True))
        alpha = jnp.exp2(m_prev - m_new)
        pt = jnp.exp2(st - m_new).astype(BF16)
        acc_ref[:, q_lo:] = alpha * acc_ref[:, q_lo:] + jnp.dot(vt_ref[kc], pt,
                                                                preferred_element_type=F32)
        return m_new

    def two_chunks(j, mp):
        return step(2 * j + 1, step(2 * j, mp, 0, False), 0, False)

    m = lax.fori_loop(0, (nsub // 2) * qi, two_chunks, jnp.full((1, tq), -jnp.inf, F32))
    for c in range(nsub):
        m = step(nsub * qi + c, m, c * th, True)[:, th:]

    acc = acc_ref[...]
    out = (acc / acc[ROWSUM_ROW:ROWSUM_ROW + 1, :]).T
    lane = lax.broadcasted_iota(jnp.int32, out.shape, 1)
    o_ref[...] = jnp.where(lane < V_HEAD_DIM, out, 0.0).astype(o_ref.dtype)


ATTN_Q_CHUNKS = 8


def _attn_call(q, k, vt, batch, seq, th):
    t, hp = q.shape
    tq = min(ATTN_Q_CHUNKS * th, seq)
    nq = seq // tq
    nchunk = seq // th
    return pl.pallas_call(
        _attn_kernel,
        grid=(batch, N_HEADS, nq),
        in_specs=[
            pl.BlockSpec((tq, HEAD_PAD), lambda b, h, qi: (b * nq + qi, h)),
            pl.BlockSpec((seq, HEAD_PAD), lambda b, h, qi: (b, h)),
            pl.BlockSpec((nchunk, HEAD_PAD, th), lambda b, h, qi: (b, h, 0)),
        ],
        out_specs=pl.BlockSpec((tq, HEAD_PAD), lambda b, h, qi: (b * nq + qi, h)),
        out_shape=jax.ShapeDtypeStruct((t, hp), F32),
        scratch_shapes=[pltpu.VMEM((HEAD_PAD, tq), F32)],
        compiler_params=pltpu.CompilerParams(
            dimension_semantics=("arbitrary", "arbitrary", "arbitrary"),
            vmem_limit_bytes=VMEM_LIMIT),
        name="attn",
    )(q, k, vt)


def _outproj_kernel(pool_ref, attn_ref, x_ref, gattn_ref, wop_ref, woa_ref, gffn_ref, wr_ref,
                    br_ref, x1_out, hp_out, idx_out, gate_out, rank_out, cnt_out, run_scr):
    tm, d = x_ref.shape
    d_attn = N_HEADS * V_HEAD_DIM
    i = pl.program_id(0)

    @pl.when(i == 0)
    def _():
        run_scr[...] = jnp.zeros(run_scr.shape, F32)

    attn = attn_ref[...]
    attn_n = _rms(attn, gattn_ref[...], d_attn).astype(BF16)
    mix = (jnp.dot(pool_ref[...], wop_ref[...], preferred_element_type=F32)
           + jnp.dot(attn_n, woa_ref[...], preferred_element_type=F32))
    x1 = x_ref[...] + mix
    x1_out[...] = x1
    h2 = _rms(x1, gffn_ref[...], d)

    hp_out[...] = _pack_bf16_pairs(h2)

    logits = lax.dot_general(wr_ref[...], h2, (((1,), (1,)), ((), ())),
                             preferred_element_type=F32,
                             precision=lax.Precision.HIGHEST) + br_ref[...]
    e_iota = lax.broadcasted_iota(jnp.int32, (N_EXPERTS, tm), 0)
    work = logits
    idxs, vals = [], []
    for _ in range(TOP_K):
        mx = jnp.max(work, axis=0, keepdims=True)
        sel = jnp.min(jnp.where(work == mx, e_iota, N_EXPERTS), axis=0, keepdims=True)
        idxs.append(sel)
        vals.append(mx)
        work = jnp.where(e_iota == sel, -jnp.inf, work)
    exps = [jnp.exp(v - vals[0]) for v in vals]
    denom = exps[0] + exps[1] + exps[2] + exps[3]
    gate_out[...] = jnp.concatenate([e / denom for e in exps], axis=0)
    idx_out[...] = jnp.concatenate(idxs, axis=0)

    onehot = jnp.zeros((N_EXPERTS, tm), F32)
    for sel in idxs:
        onehot = onehot + (e_iota == sel).astype(F32)
    r_io = lax.broadcasted_iota(jnp.int32, (tm, tm), 0)
    c_io = lax.broadcasted_iota(jnp.int32, (tm, tm), 1)
    upper = (r_io < c_io).astype(BF16)
    prefix = jnp.dot(onehot.astype(BF16), upper, preferred_element_type=F32) + run_scr[:, 0:1]
    ranks = [jnp.sum(jnp.where(e_iota == sel, prefix, 0.0), axis=0, keepdims=True) for sel in idxs]
    rank_out[...] = jnp.concatenate(ranks, axis=0).astype(jnp.int32)
    run_new = run_scr[...] + jnp.sum(onehot, axis=1, keepdims=True)
    run_scr[...] = run_new
    cnt_out[...] = run_new.astype(jnp.int32)


def _outproj_call(pool_n, attn, x2, gattn_p, wo_pool, wo_attn_p, gffn, wr_t, br_col, tm):
    t, d = x2.shape
    nblk = t // tm
    const2 = lambda i: (0, 0)
    row = lambda i: (i, 0)
    col = lambda i: (0, i)
    in_specs = [
        pl.BlockSpec((tm, pool_n.shape[1]), row),
        pl.BlockSpec((tm, attn.shape[1]), row),
        pl.BlockSpec((tm, d), row),
        pl.BlockSpec(gattn_p.shape, const2),
        pl.BlockSpec(wo_pool.shape, const2),
        pl.BlockSpec(wo_attn_p.shape, const2),
        pl.BlockSpec(gffn.shape, const2),
        pl.BlockSpec(wr_t.shape, const2),
        pl.BlockSpec(br_col.shape, const2),
    ]
    out_shape = [
        jax.ShapeDtypeStruct((t, d), F32),
        jax.ShapeDtypeStruct((t, d // 2), jnp.uint32),
        jax.ShapeDtypeStruct((TOP_K, t), jnp.int32),
        jax.ShapeDtypeStruct((TOP_K, t), F32),
        jax.ShapeDtypeStruct((TOP_K, t), jnp.int32),
        jax.ShapeDtypeStruct((N_EXPERTS, LANE), jnp.int32),
    ]
    out_specs = [
        pl.BlockSpec((tm, d), row),
        pl.BlockSpec((tm, d // 2), row),
        pl.BlockSpec((TOP_K, tm), col),
        pl.BlockSpec((TOP_K, tm), col),
        pl.BlockSpec((TOP_K, tm), col),
        pl.BlockSpec((N_EXPERTS, LANE), const2),
    ]
    return pl.pallas_call(
        _outproj_kernel,
        grid=(nblk,),
        in_specs=in_specs,
        out_specs=out_specs,
        out_shape=out_shape,
        scratch_shapes=[pltpu.VMEM((N_EXPERTS, LANE), F32)],
        compiler_params=pltpu.CompilerParams(dimension_semantics=("arbitrary",),
                                             vmem_limit_bytes=VMEM_LIMIT),
        name="outproj",
    )(pool_n, attn, x2, gattn_p, wo_pool, wo_attn_p, gffn, wr_t, br_col)


SC_WINDOW = 64


def _window_indices(dest_kt):
    t = dest_kt.shape[1]
    return dest_kt.reshape(TOP_K, t // SC_WINDOW, SC_WINDOW).transpose(1, 0, 2)


def _dispatch_sc_call(idx, hp_rows, n_rows):
    t, w = hp_rows.shape
    mesh = plsc.VectorSubcoreMesh(core_axis_name="core", subcore_axis_name="subcore")

    @functools.partial(pl.kernel, out_type=jax.ShapeDtypeStruct((n_rows, w), hp_rows.dtype),
                       mesh=mesh, scratch_types=[])
    def scatter_kernel(x_hbm, i_hbm, o_hbm):
        def body(x_vmem, i_vmem):
            for k in range(TOP_K):
                pltpu.sync_copy(x_vmem, o_hbm.at[i_vmem.at[0, k]])

        pltpu.emit_pipeline(
            body,
            grid=(t // SC_WINDOW,),
            in_specs=[pl.BlockSpec((SC_WINDOW, w), lambda i: (i, 0)),
                      pl.BlockSpec((1, TOP_K, SC_WINDOW), lambda i: (i, 0, 0))],
            out_specs=[],
            core_axis_name=("core", "subcore"),
            dimension_semantics=(pltpu.PARALLEL,),
        )(x_hbm, i_hbm)

    return scatter_kernel(hp_rows, idx)


def _gather_sc_call(idx, rows):
    nwin, _, _ = idx.shape
    t = nwin * SC_WINDOW
    wd = rows.shape[1]
    mesh = plsc.VectorSubcoreMesh(core_axis_name="core", subcore_axis_name="subcore")

    @functools.partial(pl.kernel, out_type=jax.ShapeDtypeStruct((TOP_K, t, wd), rows.dtype),
                       mesh=mesh, scratch_types=[])
    def gather_kernel(y_hbm, i_hbm, o_hbm):
        def body(i_vmem, o_vmem):
            pltpu.sync_copy(y_hbm.at[i_vmem.at[0, 0]], o_vmem.at[0])

        pltpu.emit_pipeline(
            body,
            grid=(nwin, TOP_K),
            in_specs=[pl.BlockSpec((1, 1, SC_WINDOW), lambda i, k: (i * TOP_K + k, 0, 0))],
            out_specs=[pl.BlockSpec((1, SC_WINDOW, wd), lambda i, k: (k, i, 0))],
            core_axis_name=("core", "subcore"),
            dimension_semantics=(pltpu.PARALLEL, pltpu.PARALLEL),
        )(i_hbm, o_hbm)

    return gather_kernel(rows, idx.reshape(nwin * TOP_K, 1, SC_WINDOW))


def _combine_sum_kernel(x1_ref, gate_ref, gfin_ref, yg_ref, o_ref):
    d = x1_ref.shape[1]
    g = gate_ref[...]
    y = x1_ref[...]
    for k in range(TOP_K):
        y = y + g[:, k:k + 1] * jnp.concatenate(_unpack_bf16_pairs(yg_ref[k]), axis=1)
    o_ref[...] = _rms(y, gfin_ref[...], d)


def _combine_sum_call(x1, gates_t, gfin, yg, tf):
    t, d = x1.shape
    return pl.pallas_call(
        _combine_sum_kernel,
        grid=(t // tf,),
        in_specs=[
            pl.BlockSpec((tf, d), lambda i: (i, 0)),
            pl.BlockSpec((tf, TOP_K), lambda i: (i, 0)),
            pl.BlockSpec(gfin.shape, lambda i: (0, 0)),
            pl.BlockSpec((TOP_K, tf, d // 2), lambda i: (0, i, 0)),
        ],
        out_specs=pl.BlockSpec((tf, d), lambda i: (i, 0)),
        out_shape=jax.ShapeDtypeStruct((t, d), F32),
        compiler_params=pltpu.CompilerParams(dimension_semantics=("arbitrary",),
                                             vmem_limit_bytes=VMEM_LIMIT),
        name="combine",
    )(x1, gates_t, gfin, yg)


CAST_ROWS = 64


def _expert_kernel(bs_ref, nblk_ref, cnt_ref, bgu_ref, bd_ref, wgu_hbm, wd_hbm, xs_hbm, y_hbm,
                   wgu_f32, wd_f32, wgu_bf, wd_bf, xbuf, ybuf, xsem, ysem, wsem):
    e = pl.program_id(0)
    n_exp = pl.num_programs(0)
    d = wgu_f32.shape[1]
    f = wd_f32.shape[1]
    bm = xbuf.shape[1]
    n = nblk_ref[e]
    blk0 = bs_ref[e]
    wslot = e % 2

    def w_copies(ex, slot):
        return (pltpu.make_async_copy(wgu_hbm.at[ex], wgu_f32.at[slot], wsem.at[0, slot]),
                pltpu.make_async_copy(wd_hbm.at[ex], wd_f32.at[slot], wsem.at[1, slot]))

    def x_copy(j, slot):
        row0 = pl.multiple_of((blk0 + j) * bm, bm)
        return pltpu.make_async_copy(xs_hbm.at[pl.ds(row0, bm)], xbuf.at[slot], xsem.at[slot])

    def y_copy(blk, slot):
        row0 = pl.multiple_of(blk * bm, bm)
        return pltpu.make_async_copy(ybuf.at[slot], y_hbm.at[pl.ds(row0, bm)], ysem.at[slot])

    @pl.when(n > 0)
    def _():
        x_copy(0, 0).start()

    @pl.when(e == 0)
    def _():
        for cp in w_copies(0, 0):
            cp.start(priority=1)

    @pl.when(e + 1 < n_exp)
    def _():
        for cp in w_copies(e + 1, 1 - wslot):
            cp.start(priority=1)

    for cp in w_copies(e, wslot):
        cp.wait()

    @pl.when(n > 0)
    def _():
        def cast_gu(r, c):
            rows = pl.ds(pl.multiple_of(r * CAST_ROWS, CAST_ROWS), CAST_ROWS)
            wgu_bf[rows, :] = wgu_f32[wslot, rows, :].astype(BF16)
            return c

        def cast_d(r, c):
            rows = pl.ds(pl.multiple_of(r * CAST_ROWS, CAST_ROWS), CAST_ROWS)
            wd_bf[rows, :] = wd_f32[wslot, rows, :].astype(BF16)
            return c

        lax.fori_loop(0, d // CAST_ROWS, cast_gu, 0)
        lax.fori_loop(0, f // CAST_ROWS, cast_d, 0)

        def block(j, c):
            slot = j % 2
            x_copy(j, slot).wait()

            @pl.when(j + 1 < n)
            def _():
                x_copy(j + 1, 1 - slot).start()

            @pl.when(j >= 2)
            def _():
                y_copy(blk0 + j - 2, slot).wait()

            row = lax.broadcasted_iota(jnp.int32, (bm, 1), 0)
            w = jnp.where(row < cnt_ref[e] - j * bm, xbuf[slot], jnp.uint32(0))
            x_lo, x_hi = _unpack_bf16_pairs(w)
            gu = (jnp.dot(x_lo.astype(BF16), wgu_bf[:d // 2, :], preferred_element_type=F32)
                  + jnp.dot(x_hi.astype(BF16), wgu_bf[d // 2:, :], preferred_element_type=F32)
                  + bgu_ref[0])
            gate = jnp.minimum(gu[:, :f], SWIGLU_LIMIT)
            up = jnp.clip(gu[:, f:], -SWIGLU_LIMIT, SWIGLU_LIMIT)
            act = gate * jax.nn.sigmoid(SWIGLU_ALPHA * gate) * (up + 1.0)
            y = jnp.dot(act.astype(BF16), wd_bf[...], preferred_element_type=F32) + bd_ref[0]
            ybuf[slot] = _pack_bf16_pairs(y)
            y_copy(blk0 + j, slot).start()
            return c

        lax.fori_loop(0, n, block, 0)

        @pl.when(n >= 2)
        def _():
            y_copy(blk0 + n - 2, n % 2).wait()

        y_copy(blk0 + n - 1, (n - 1) % 2).wait()

    @pl.when(e == n_exp - 1)
    def _():
        first = blk0 + n
        total = y_hbm.shape[0] // bm
        ybuf[0] = jnp.zeros(ybuf.shape[1:], ybuf.dtype)

        def fill(b, c):
            y_copy(b, 0).start()
            return c

        def drain(b, c):
            y_copy(b, 0).wait()
            return c

        lax.fori_loop(first, total, fill, 0)
        lax.fori_loop(first, total, drain, 0)


def _expert_call(blk_start, nblk, cnt, xs, wgu, bgu, wd, bd, bm):
    e, d, f2 = wgu.shape
    f = wd.shape[1]
    r, dw = xs.shape

    def bsel(i, bs, nb, ct):
        return (i, 0, 0)

    grid_spec = pltpu.PrefetchScalarGridSpec(
        num_scalar_prefetch=3,
        grid=(e,),
        in_specs=[
            pl.BlockSpec((1, 1, f2), bsel),
            pl.BlockSpec((1, 1, d), bsel),
            pl.BlockSpec(memory_space=pl.ANY),
            pl.BlockSpec(memory_space=pl.ANY),
            pl.BlockSpec(memory_space=pl.ANY),
        ],
        out_specs=pl.BlockSpec(memory_space=pl.ANY),
        scratch_shapes=[pltpu.VMEM((2, d, f2), F32), pltpu.VMEM((2, f, d), F32),
                        pltpu.VMEM((d, f2), BF16), pltpu.VMEM((f, d), BF16),
                        pltpu.VMEM((2, bm, dw), xs.dtype),
                        pltpu.VMEM((2, bm, dw), jnp.uint32),
                        pltpu.SemaphoreType.DMA((2,)), pltpu.SemaphoreType.DMA((2,)),
                        pltpu.SemaphoreType.DMA((2, 2))],
    )
    return pl.pallas_call(
        _expert_kernel,
        grid_spec=grid_spec,
        out_shape=jax.ShapeDtypeStruct((r, dw), jnp.uint32),
        compiler_params=pltpu.CompilerParams(dimension_semantics=("arbitrary",),
                                             vmem_limit_bytes=VMEM_LIMIT),
        name="experts",
    )(blk_start, nblk, cnt, bgu, bd, wgu, wd, xs)


def _pad_head_cols(w, widths_in, place):
    k = w.shape[0]
    per = sum(widths_in)
    blocks = []
    for hh in range(N_HEADS):
        blocks.append(place(w[:, hh * per:(hh + 1) * per]))
    return jnp.concatenate(blocks, axis=1)


def _layer(x2, pos3, batch, seq, g_attn_norm, w_in, w_pool, b_pool, pool_scale, g_q_a, w_q_b,
           g_kv_a, w_kv_b, g_out_pool, g_out_attn, w_out, g_ffn_norm, w_router, b_router,
           w_gate_up, b_gate_up, w_down, b_down, tiles):
    tm, tmo, bm, tf = tiles
    t, d = x2.shape
    d_pool = pool_scale.shape[0]
    gdim = d_pool // N_POOL_GROUPS
    q_rank = g_q_a.shape[0]
    kv_rank = g_kv_a.shape[0]
    half = QK_ROPE_DIM // 2

    o1, o2, o3 = d_pool, d_pool + q_rank, d_pool + q_rank + kv_rank
    zk = lambda n: jnp.zeros((d, n), F32)
    w_kr1, w_kr2 = w_in[:, o3:o3 + half], w_in[:, o3 + half:]
    w_in_p = jnp.concatenate([w_in[:, :o3], zk(QK_NOPE_DIM), w_kr1, w_kr2, zk(LANE - QK_DIM),
                              zk(QK_NOPE_DIM), w_kr2, w_kr1, zk(LANE - QK_DIM)], axis=1).astype(BF16)
    zq = lambda n: jnp.zeros((q_rank, n), F32)
    wq_plain = _pad_head_cols(w_q_b, (QK_DIM,),
                              lambda c: jnp.concatenate([c, zq(LANE - QK_DIM)], axis=1))
    wq_swap = _pad_head_cols(w_q_b, (QK_DIM,), lambda c: jnp.concatenate(
        [zq(QK_NOPE_DIM), c[:, QK_NOPE_DIM + half:], c[:, QK_NOPE_DIM:QK_NOPE_DIM + half],
         zq(LANE - QK_DIM)], axis=1))
    wq_p = jnp.concatenate([wq_plain, wq_swap], axis=1).astype(BF16)
    zkv = lambda n: jnp.zeros((kv_rank, n), F32)
    per = QK_NOPE_DIM + V_HEAD_DIM
    wk_p = _pad_head_cols(w_kv_b, (per,), lambda c: jnp.concatenate(
        [c[:, :QK_NOPE_DIM], zkv(LANE - QK_NOPE_DIM)], axis=1))
    wv_p = _pad_head_cols(w_kv_b, (per,), lambda c: jnp.concatenate(
        [c[:, QK_NOPE_DIM:], zkv(LANE - V_HEAD_DIM)], axis=1))
    wkv_p = jnp.concatenate([wk_p, wv_p], axis=1).astype(BF16)
    inv = ROPE_THETA ** (-jnp.arange(half, dtype=F32) / half)
    inv_col = inv.reshape(half, 1)

    row = lambda v: v.reshape(1, -1)
    grp = lambda v: v.reshape(N_POOL_GROUPS, 1, gdim)

    pool_n, q, k, v = _proj_call(
        x2, pos3, row(g_attn_norm), w_in_p, inv_col, w_pool.astype(BF16), grp(b_pool),
        grp(pool_scale), grp(g_out_pool), row(g_q_a), wq_p, row(g_kv_a), wkv_p, seq, tm)

    attn = _attn_call(q, k, v, batch, seq, tm)

    d_attn = N_HEADS * V_HEAD_DIM
    wo_pool = w_out[:d_pool].astype(BF16)
    wo_attn = w_out[d_pool:].reshape(N_HEADS, V_HEAD_DIM, d)
    wo_attn_p = jnp.concatenate([wo_attn, jnp.zeros((N_HEADS, HEAD_PAD - V_HEAD_DIM, d), F32)],
                                axis=1).reshape(N_HEADS * HEAD_PAD, d).astype(BF16)
    gattn_p = jnp.concatenate([g_out_attn.reshape(N_HEADS, V_HEAD_DIM),
                               jnp.zeros((N_HEADS, HEAD_PAD - V_HEAD_DIM), F32)], axis=1).reshape(1, -1)

    x1, hp, idx, gates, rank, counts = _outproj_call(
        pool_n, attn, x2, gattn_p, wo_pool, wo_attn_p, row(g_ffn_norm), w_router.T,
        b_router.reshape(-1, 1), tmo)

    n_assign = t * TOP_K
    cnt = counts[:, 0]
    padded = ((cnt + bm - 1) // bm) * bm
    pad_end = jnp.cumsum(padded)
    pad_start = pad_end - padded
    e_ids = jnp.arange(N_EXPERTS, dtype=jnp.int32)
    dest = rank + jnp.sum(jnp.where(idx[None] == e_ids[:, None, None], pad_start[:, None, None], 0),
                          axis=0)
    win_idx = _window_indices(dest)
    n_rows = (n_assign // bm + N_EXPERTS) * bm
    blk_start = (pad_start // bm).astype(jnp.int32)
    nblk = (padded // bm).astype(jnp.int32)

    xs = _dispatch_sc_call(win_idx, hp, n_rows)
    yb = _expert_call(blk_start, nblk, cnt.astype(jnp.int32), xs, w_gate_up,
                      b_gate_up.reshape(N_EXPERTS, 1, -1), w_down, b_down.reshape(N_EXPERTS, 1, -1), bm)
    yg = _gather_sc_call(win_idx, yb)
    return x1, gates, yg


def _forward(x, positions, g_attn_norm, w_in, w_pool, b_pool, pool_scale, g_q_a, w_q_b, g_kv_a,
             w_kv_b, g_out_pool, g_out_attn, w_out, g_ffn_norm, w_router, b_router, w_gate_up,
             b_gate_up, w_down, b_down, g_final, tiles):
    batch, seq, d = x.shape
    t = batch * seq
    tm, tmo, bm, tf = tiles
    x2 = x.reshape(t, d)
    pos3 = positions.reshape(t // tm, 1, tm)
    depth = g_attn_norm.shape[0]
    assert depth == 1
    l = 0
    x1, gates, yg = _layer(
        x2, pos3, batch, seq, g_attn_norm[l], w_in[l], w_pool[l], b_pool[l], pool_scale[l],
        g_q_a[l], w_q_b[l], g_kv_a[l], w_kv_b[l], g_out_pool[l], g_out_attn[l], w_out[l],
        g_ffn_norm[l], w_router[l], b_router[l], w_gate_up[l], b_gate_up[l], w_down[l],
        b_down[l], tiles)
    out = _combine_sum_call(x1, gates.T, g_final.reshape(1, -1), yg, tf)
    return out.reshape(batch, seq, d)


def _tiles_for(seq):
    tm = min(512, seq // 2)
    tmo = min(512, seq)
    bm = 256 if seq >= 2048 else 128
    tf = min(512, seq)
    return (tm, tmo, bm, tf)


def kernel(x, positions, g_attn_norm, w_in, w_pool, b_pool, pool_scale, g_q_a, w_q_b, g_kv_a, w_kv_b, g_out_pool, g_out_attn, w_out, g_ffn_norm, w_router, b_router, w_gate_up, b_gate_up, w_down, b_down, g_final):
    tiles = _tiles_for(x.shape[1])
    return _forward(x, positions, g_attn_norm, w_in, w_pool, b_pool, pool_scale, g_q_a, w_q_b,
                    g_kv_a, w_kv_b, g_out_pool, g_out_attn, w_out, g_ffn_norm, w_router, b_router,
                    w_gate_up, b_gate_up, w_down, b_down, g_final, tiles)
```

```python
import functools

import jax
import jax.numpy as jnp
from jax import lax
from jax.experimental import pallas as pl
from jax.experimental.pallas import tpu as pltpu
from jax.experimental.pallas import tpu_sc as plsc

POOL_WINDOWS = (2, 4, 8, 16)
N_POOL_GROUPS = 4
N_HEADS = 8
V_HEAD_DIM = 64
QK_NOPE_DIM = 64
QK_ROPE_DIM = 32
QK_DIM = QK_NOPE_DIM + QK_ROPE_DIM
ROPE_THETA = 10000.0
N_EXPERTS = 32
TOP_K = 4
SWIGLU_ALPHA = 1.702
SWIGLU_LIMIT = 7.0
RMS_EPS = 1e-6
LOG2_E = 1.4426950408889634

LANE = 128
HEAD_PAD = 128
POOL_HALO = 16

VMEM_LIMIT = 56 * 1024 * 1024

F32 = jnp.float32
BF16 = jnp.bfloat16


def _pack_bf16_pairs(x):
    n = x.shape[1] // 2
    xb = x.astype(BF16).astype(F32)
    lo = lax.bitcast_convert_type(xb[:, :n], jnp.uint32)
    hi = lax.bitcast_convert_type(xb[:, n:], jnp.uint32)
    return (lo >> 16) | (hi & jnp.uint32(0xFFFF0000))


def _unpack_bf16_pairs(w):
    lo = lax.bitcast_convert_type(w << 16, F32)
    hi = lax.bitcast_convert_type(w & jnp.uint32(0xFFFF0000), F32)
    return lo, hi


def _rms(x, g, n):
    ms = jnp.sum(x * x, axis=-1, keepdims=True) * (1.0 / n)
    return x * lax.rsqrt(ms + RMS_EPS) * g


def _proj_kernel(blocks_per_seq, x_ref, pos_ref, g_attn_ref, w_in_ref, inv_ref, wpool_ref,
                 bpool_ref, pscale_ref, gpool_ref, gq_ref, wq_ref, gkv_ref, wkv_ref,
                 pool_out, q_out, k_out, v_out, ext_scr):
    tm = x_ref.shape[0]
    d_model = x_ref.shape[1]
    d_pool = pool_out.shape[1]
    gdim = d_pool // N_POOL_GROUPS
    q_rank = gq_ref.shape[1]
    kv_rank = gkv_ref.shape[1]
    i = pl.program_id(0)
    blk_in_seq = i % blocks_per_seq

    x = x_ref[...]
    h = _rms(x, g_attn_ref[...], d_model).astype(BF16)
    proj = jnp.dot(h, w_in_ref[...], preferred_element_type=F32)
    u = proj[:, :d_pool]
    qlat = proj[:, d_pool:d_pool + q_rank]
    kvlat = proj[:, d_pool + q_rank:d_pool + q_rank + kv_rank]
    o_kr = d_pool + q_rank + kv_rank
    krb = proj[:, o_kr:o_kr + LANE]
    krs = proj[:, o_kr + LANE:]

    @pl.when(blk_in_seq == 0)
    def _():
        ext_scr[0:POOL_HALO, :] = jnp.zeros((POOL_HALO, d_pool), F32)

    @pl.when(blk_in_seq != 0)
    def _():
        ext_scr[0:POOL_HALO, :] = ext_scr[tm:tm + POOL_HALO, :]

    ext_scr[POOL_HALO:, :] = u
    t_in_seq = (blk_in_seq * tm + lax.broadcasted_iota(jnp.int32, (tm, gdim), 0) + 1).astype(F32)
    ys = []
    ssq = jnp.zeros((tm, 1), F32)
    for g, w in enumerate(POOL_WINDOWS):
        a = ext_scr[:, g * gdim:(g + 1) * gdim]
        shift = 1
        while shift < w:
            a = a + pltpu.roll(a, shift, axis=0)
            shift *= 2
        win = a[POOL_HALO:, :]
        cnt = jnp.minimum(t_in_seq, float(w))
        ug = u[:, g * gdim:(g + 1) * gdim]
        diff = (win / cnt - ug).astype(BF16)
        mixed = jnp.dot(diff, wpool_ref[g], preferred_element_type=F32) + bpool_ref[g]
        y = mixed * pscale_ref[g]
        ssq = ssq + jnp.sum(y * y, axis=-1, keepdims=True)
        ys.append(y)
    rinv = lax.rsqrt(ssq * (1.0 / d_pool) + RMS_EPS)
    for g in range(N_POOL_GROUPS):
        pool_out[:, g * gdim:(g + 1) * gdim] = (ys[g] * rinv * gpool_ref[g]).astype(pool_out.dtype)

    pos = pos_ref[0].astype(F32)
    ang_t = inv_ref[...] * pos
    cos_t = jnp.cos(ang_t)
    sin_t = jnp.sin(ang_t)
    half = QK_ROPE_DIM // 2
    ones = jnp.ones((QK_NOPE_DIM, tm), F32)
    c_t = jnp.concatenate([ones, cos_t, cos_t, jnp.ones((LANE - QK_DIM, tm), F32)], axis=0)
    s_t = jnp.concatenate([jnp.zeros((QK_NOPE_DIM, tm), F32), -sin_t, sin_t,
                           jnp.zeros((LANE - QK_DIM, tm), F32)], axis=0)
    c = c_t.T
    s = s_t.T

    def rope(blk, blk_swapped, c_, s_):
        return blk * c_ + blk_swapped * s_

    qn = _rms(qlat, gq_ref[...], q_rank).astype(BF16)
    q2 = jnp.dot(qn, wq_ref[...], preferred_element_type=F32)
    scale = QK_DIM ** -0.5 * LOG2_E
    cq, sq = c * scale, s * scale
    hp = N_HEADS * HEAD_PAD
    for hh in range(N_HEADS):
        sl = slice(hh * HEAD_PAD, (hh + 1) * HEAD_PAD)
        sw = slice(hp + hh * HEAD_PAD, hp + (hh + 1) * HEAD_PAD)
        q_out[:, sl] = rope(q2[:, sl], q2[:, sw], cq, sq).astype(q_out.dtype)

    kvn = _rms(kvlat, gkv_ref[...], kv_rank).astype(BF16)
    kv = jnp.dot(kvn, wkv_ref[...], preferred_element_type=F32)
    kr = rope(krb, krs, c, s)
    for hh in range(N_HEADS):
        sl = slice(hh * HEAD_PAD, (hh + 1) * HEAD_PAD)
        k_out[:, sl] = (kv[:, sl] + kr).astype(k_out.dtype)
    vlane = lax.broadcasted_iota(jnp.int32, (1, N_HEADS * HEAD_PAD), 1) % HEAD_PAD
    ones_col = (vlane == V_HEAD_DIM).astype(F32)
    v_out[0] = (kv[:, N_HEADS * HEAD_PAD:] + ones_col).T.astype(v_out.dtype)


def _proj_call(x2, pos3, g_attn, w_in_p, inv_col, wpool, bpool, pscale, gpool, gq, wq_p, gkv, wkv_p,
               seq, tm):
    t, d = x2.shape
    d_pool = pscale.shape[0] * pscale.shape[2]
    hp = N_HEADS * HEAD_PAD
    nblk = t // tm
    const2 = lambda i: (0, 0)
    const3 = lambda i: (0, 0, 0)
    in_specs = [
        pl.BlockSpec((tm, d), lambda i: (i, 0)),
        pl.BlockSpec((1, 1, tm), lambda i: (i, 0, 0)),
        pl.BlockSpec(g_attn.shape, const2),
        pl.BlockSpec(w_in_p.shape, const2),
        pl.BlockSpec(inv_col.shape, const2),
        pl.BlockSpec(wpool.shape, const3),
        pl.BlockSpec(bpool.shape, const3),
        pl.BlockSpec(pscale.shape, const3),
        pl.BlockSpec(gpool.shape, const3),
        pl.BlockSpec(gq.shape, const2),
        pl.BlockSpec(wq_p.shape, const2),
        pl.BlockSpec(gkv.shape, const2),
        pl.BlockSpec(wkv_p.shape, const2),
    ]
    out_shape = [
        jax.ShapeDtypeStruct((t, d_pool), BF16),
        jax.ShapeDtypeStruct((t, hp), BF16),
        jax.ShapeDtypeStruct((t, hp), BF16),
        jax.ShapeDtypeStruct((nblk, hp, tm), BF16),
    ]
    out_specs = [
        pl.BlockSpec((tm, d_pool), lambda i: (i, 0)),
        pl.BlockSpec((tm, hp), lambda i: (i, 0)),
        pl.BlockSpec((tm, hp), lambda i: (i, 0)),
        pl.BlockSpec((1, hp, tm), lambda i: (i, 0, 0)),
    ]
    return pl.pallas_call(
        functools.partial(_proj_kernel, seq // tm),
        grid=(nblk,),
        in_specs=in_specs,
        out_specs=out_specs,
        out_shape=out_shape,
        scratch_shapes=[pltpu.VMEM((tm + POOL_HALO, d_pool), F32)],
        compiler_params=pltpu.CompilerParams(dimension_semantics=("arbitrary",),
                                             vmem_limit_bytes=VMEM_LIMIT),
        name="proj",
    )(x2, pos3, g_attn, w_in_p, inv_col, wpool, bpool, pscale, gpool, gq, wq_p, gkv, wkv_p)


ROWSUM_ROW = V_HEAD_DIM


def _attn_kernel(q_ref, k_ref, vt_ref, o_ref, acc_ref):
    tq = q_ref.shape[0]
    th = vt_ref.shape[2]
    nsub = tq // th
    qi = pl.program_id(2)
    acc_ref[...] = jnp.zeros(acc_ref.shape, F32)

    def step(kc, m_prev, q_lo, masked):
        width = tq - q_lo
        start = pl.multiple_of(kc * th, th)
        k = k_ref[pl.ds(start, th), :]
        st = lax.dot_general(k, q_ref[q_lo:, :], (((1,), (1,)), ((), ())),
                             preferred_element_type=F32)
        if masked:
            key = lax.broadcasted_iota(jnp.int32, (th, width), 0)
            qry = lax.broadcasted_iota(jnp.int32, (th, width), 1)
            st = jnp.where(key <= qry, st, -jnp.inf)
        m_new = jnp.maximum(m_prev, jnp.max(st, axis=0, keepdims=True))
        alpha = jnp.exp2(m_prev - m_new)
        pt = jnp.exp2(st - m_new).astype(BF16)
        acc_ref[:, q_lo:] = alpha * acc_ref[:, q_lo:] + jnp.dot(vt_ref[kc], pt,
                                                                preferred_element_type=F32)
        return m_new

    def two_chunks(j, mp):
        return step(2 * j + 1, step(2 * j, mp, 0, False), 0, False)

    m = lax.fori_loop(0, (nsub // 2) * qi, two_chunks, jnp.full((1, tq), -jnp.inf, F32))
    for c in range(nsub):
        m = step(nsub * qi + c, m, c * th, True)[:, th:]

    acc = acc_ref[...]
    out = (acc / acc[ROWSUM_ROW:ROWSUM_ROW + 1, :]).T
    lane = lax.broadcasted_iota(jnp.int32, out.shape, 1)
    o_ref[...] = jnp.where(lane < V_HEAD_DIM, out, 0.0).astype(o_ref.dtype)


ATTN_Q_CHUNKS = 8


def _attn_call(q, k, vt, batch, seq, th):
    t, hp = q.shape
    tq = min(ATTN_Q_CHUNKS * th, seq)
    nq = seq // tq
    nchunk = seq // th
    return pl.pallas_call(
        _attn_kernel,
        grid=(batch, N_HEADS, nq),
        in_specs=[
            pl.BlockSpec((tq, HEAD_PAD), lambda b, h, qi: (b * nq + qi, h)),
            pl.BlockSpec((seq, HEAD_PAD), lambda b, h, qi: (b, h)),
            pl.BlockSpec((nchunk, HEAD_PAD, th), lambda b, h, qi: (b, h, 0)),
        ],
        out_specs=pl.BlockSpec((tq, HEAD_PAD), lambda b, h, qi: (b * nq + qi, h)),
        out_shape=jax.ShapeDtypeStruct((t, hp), F32),
        scratch_shapes=[pltpu.VMEM((HEAD_PAD, tq), F32)],
        compiler_params=pltpu.CompilerParams(
            dimension_semantics=("arbitrary", "arbitrary", "arbitrary"),
            vmem_limit_bytes=VMEM_LIMIT),
        name="attn",
    )(q, k, vt)


def _outproj_kernel(pool_ref, attn_ref, x_ref, gattn_ref, wop_ref, woa_ref, gffn_ref, wr_ref,
                    br_ref, x1_out, hp_out, idx_out, gate_out, rank_out, cnt_out, run_scr):
    tm, d = x_ref.shape
    d_attn = N_HEADS * V_HEAD_DIM
    i = pl.program_id(0)

    @pl.when(i == 0)
    def _():
        run_scr[...] = jnp.zeros(run_scr.shape, F32)

    attn = attn_ref[...]
    attn_n = _rms(attn, gattn_ref[...], d_attn).astype(BF16)
    mix = (jnp.dot(pool_ref[...], wop_ref[...], preferred_element_type=F32)
           + jnp.dot(attn_n, woa_ref[...], preferred_element_type=F32))
    x1 = x_ref[...] + mix
    x1_out[...] = x1
    h2 = _rms(x1, gffn_ref[...], d)

    hp_out[...] = _pack_bf16_pairs(h2)

    logits = lax.dot_general(wr_ref[...], h2, (((1,), (1,)), ((), ())),
                             preferred_element_type=F32,
                             precision=lax.Precision.HIGHEST) + br_ref[...]
    e_iota = lax.broadcasted_iota(jnp.int32, (N_EXPERTS, tm), 0)
    work = logits
    idxs, vals = [], []
    for _ in range(TOP_K):
        mx = jnp.max(work, axis=0, keepdims=True)
        sel = jnp.min(jnp.where(work == mx, e_iota, N_EXPERTS), axis=0, keepdims=True)
        idxs.append(sel)
        vals.append(mx)
        work = jnp.where(e_iota == sel, -jnp.inf, work)
    exps = [jnp.exp(v - vals[0]) for v in vals]
    denom = exps[0] + exps[1] + exps[2] + exps[3]
    gate_out[...] = jnp.concatenate([e / denom for e in exps], axis=0)
    idx_out[...] = jnp.concatenate(idxs, axis=0)

    onehot = jnp.zeros((N_EXPERTS, tm), F32)
    for sel in idxs:
        onehot = onehot + (e_iota == sel).astype(F32)
    r_io = lax.broadcasted_iota(jnp.int32, (tm, tm), 0)
    c_io = lax.broadcasted_iota(jnp.int32, (tm, tm), 1)
    upper = (r_io < c_io).astype(BF16)
    prefix = jnp.dot(onehot.astype(BF16), upper, preferred_element_type=F32) + run_scr[:, 0:1]
    ranks = [jnp.sum(jnp.where(e_iota == sel, prefix, 0.0), axis=0, keepdims=True) for sel in idxs]
    rank_out[...] = jnp.concatenate(ranks, axis=0).astype(jnp.int32)
    run_new = run_scr[...] + jnp.sum(onehot, axis=1, keepdims=True)
    run_scr[...] = run_new
    cnt_out[...] = run_new.astype(jnp.int32)


def _outproj_call(pool_n, attn, x2, gattn_p, wo_pool, wo_attn_p, gffn, wr_t, br_col, tm):
    t, d = x2.shape
    nblk = t // tm
    const2 = lambda i: (0, 0)
    row = lambda i: (i, 0)
    col = lambda i: (0, i)
    in_specs = [
        pl.BlockSpec((tm, pool_n.shape[1]), row),
        pl.BlockSpec((tm, attn.shape[1]), row),
        pl.BlockSpec((tm, d), row),
        pl.BlockSpec(gattn_p.shape, const2),
        pl.BlockSpec(wo_pool.shape, const2),
        pl.BlockSpec(wo_attn_p.shape, const2),
        pl.BlockSpec(gffn.shape, const2),
        pl.BlockSpec(wr_t.shape, const2),
        pl.BlockSpec(br_col.shape, const2),
    ]
    out_shape = [
        jax.ShapeDtypeStruct((t, d), F32),
        jax.ShapeDtypeStruct((t, d // 2), jnp.uint32),
        jax.ShapeDtypeStruct((TOP_K, t), jnp.int32),
        jax.ShapeDtypeStruct((TOP_K, t), F32),
        jax.ShapeDtypeStruct((TOP_K, t), jnp.int32),
        jax.ShapeDtypeStruct((N_EXPERTS, LANE), jnp.int32),
    ]
    out_specs = [
        pl.BlockSpec((tm, d), row),
        pl.BlockSpec((tm, d // 2), row),
        pl.BlockSpec((TOP_K, tm), col),
        pl.BlockSpec((TOP_K, tm), col),
        pl.BlockSpec((TOP_K, tm), col),
        pl.BlockSpec((N_EXPERTS, LANE), const2),
    ]
    return pl.pallas_call(
        _outproj_kernel,
        grid=(nblk,),
        in_specs=in_specs,
        out_specs=out_specs,
        out_shape=out_shape,
        scratch_shapes=[pltpu.VMEM((N_EXPERTS, LANE), F32)],
        compiler_params=pltpu.CompilerParams(dimension_semantics=("arbitrary",),
                                             vmem_limit_bytes=VMEM_LIMIT),
        name="outproj",
    )(pool_n, attn, x2, gattn_p, wo_pool, wo_attn_p, gffn, wr_t, br_col)


SC_WINDOW = 64


def _window_indices(dest_kt):
    t = dest_kt.shape[1]
    return dest_kt.reshape(TOP_K, t // SC_WINDOW, SC_WINDOW).transpose(1, 0, 2)


def _dispatch_sc_call(idx, hp_rows, n_rows):
    t, w = hp_rows.shape
    mesh = plsc.VectorSubcoreMesh(core_axis_name="core", subcore_axis_name="subcore")

    @functools.partial(pl.kernel, out_type=jax.ShapeDtypeStruct((n_rows, w), hp_rows.dtype),
                       mesh=mesh, scratch_types=[])
    def scatter_kernel(x_hbm, i_hbm, o_hbm):
        def body(x_vmem, i_vmem):
            for k in range(TOP_K):
                pltpu.sync_copy(x_vmem, o_hbm.at[i_vmem.at[0, k]])

        pltpu.emit_pipeline(
            body,
            grid=(t // SC_WINDOW,),
            in_specs=[pl.BlockSpec((SC_WINDOW, w), lambda i: (i, 0)),
                      pl.BlockSpec((1, TOP_K, SC_WINDOW), lambda i: (i, 0, 0))],
            out_specs=[],
            core_axis_name=("core", "subcore"),
            dimension_semantics=(pltpu.PARALLEL,),
        )(x_hbm, i_hbm)

    return scatter_kernel(hp_rows, idx)


def _gather_sc_call(idx, rows):
    nwin, _, _ = idx.shape
    t = nwin * SC_WINDOW
    wd = rows.shape[1]
    mesh = plsc.VectorSubcoreMesh(core_axis_name="core", subcore_axis_name="subcore")

    @functools.partial(pl.kernel, out_type=jax.ShapeDtypeStruct((TOP_K, t, wd), rows.dtype),
                       mesh=mesh, scratch_types=[])
    def gather_kernel(y_hbm, i_hbm, o_hbm):
        def body(i_vmem, o_vmem):
            pltpu.sync_copy(y_hbm.at[i_vmem.at[0, 0]], o_vmem.at[0])

        pltpu.emit_pipeline(
            body,
            grid=(nwin, TOP_K),
            in_specs=[pl.BlockSpec((1, 1, SC_WINDOW), lambda i, k: (i * TOP_K + k, 0, 0))],
            out_specs=[pl.BlockSpec((1, SC_WINDOW, wd), lambda i, k: (k, i, 0))],
            core_axis_name=("core", "subcore"),
            dimension_semantics=(pltpu.PARALLEL, pltpu.PARALLEL),
        )(i_hbm, o_hbm)

    return gather_kernel(rows, idx.reshape(nwin * TOP_K, 1, SC_WINDOW))


def _combine_sum_kernel(x1_ref, gate_ref, gfin_ref, yg_ref, o_ref):
    d = x1_ref.shape[1]
    g = gate_ref[...]
    y = x1_ref[...]
    for k in range(TOP_K):
        y = y + g[:, k:k + 1] * jnp.concatenate(_unpack_bf16_pairs(yg_ref[k]), axis=1)
    o_ref[...] = _rms(y, gfin_ref[...], d)


def _combine_sum_call(x1, gates_t, gfin, yg, tf):
    t, d = x1.shape
    return pl.pallas_call(
        _combine_sum_kernel,
        grid=(t // tf,),
        in_specs=[
            pl.BlockSpec((tf, d), lambda i: (i, 0)),
            pl.BlockSpec((tf, TOP_K), lambda i: (i, 0)),
            pl.BlockSpec(gfin.shape, lambda i: (0, 0)),
            pl.BlockSpec((TOP_K, tf, d // 2), lambda i: (0, i, 0)),
        ],
        out_specs=pl.BlockSpec((tf, d), lambda i: (i, 0)),
        out_shape=jax.ShapeDtypeStruct((t, d), F32),
        compiler_params=pltpu.CompilerParams(dimension_semantics=("arbitrary",),
                                             vmem_limit_bytes=VMEM_LIMIT),
        name="combine",
    )(x1, gates_t, gfin, yg)


CAST_ROWS = 64


def _expert_kernel(bs_ref, nblk_ref, cnt_ref, bgu_ref, bd_ref, wgu_hbm, wd_hbm, xs_hbm, y_hbm,
                   wgu_f32, wd_f32, wgu_bf, wd_bf, xbuf, ybuf, xsem, ysem, wsem):
    e = pl.program_id(0)
    n_exp = pl.num_programs(0)
    d = wgu_f32.shape[1]
    f = wd_f32.shape[1]
    bm = xbuf.shape[1]
    n = nblk_ref[e]
    blk0 = bs_ref[e]
    wslot = e % 2

    def w_copies(ex, slot):
        return (pltpu.make_async_copy(wgu_hbm.at[ex], wgu_f32.at[slot], wsem.at[0, slot]),
                pltpu.make_async_copy(wd_hbm.at[ex], wd_f32.at[slot], wsem.at[1, slot]))

    n_valid = bs_ref[n_exp - 1] + nblk_ref[n_exp - 1]

    def x_copy(g):
        row0 = pl.multiple_of(g * bm, bm)
        return pltpu.make_async_copy(xs_hbm.at[pl.ds(row0, bm)], xbuf.at[g % 2], xsem.at[g % 2])

    def y_copy(g):
        row0 = pl.multiple_of(g * bm, bm)
        return pltpu.make_async_copy(ybuf.at[g % 2], y_hbm.at[pl.ds(row0, bm)], ysem.at[g % 2])

    @pl.when(e == 0)
    def _():
        x_copy(0).start()
        for cp in w_copies(0, 0):
            cp.start(priority=1)

    @pl.when(e + 1 < n_exp)
    def _():
        for cp in w_copies(e + 1, 1 - wslot):
            cp.start(priority=1)

    for cp in w_copies(e, wslot):
        cp.wait()

    @pl.when(n > 0)
    def _():
        def cast_gu(r, c):
            rows = pl.ds(pl.multiple_of(r * CAST_ROWS, CAST_ROWS), CAST_ROWS)
            wgu_bf[rows, :] = wgu_f32[wslot, rows, :].astype(BF16)
            return c

        def cast_d(r, c):
            rows = pl.ds(pl.multiple_of(r * CAST_ROWS, CAST_ROWS), CAST_ROWS)
            wd_bf[rows, :] = wd_f32[wslot, rows, :].astype(BF16)
            return c

        lax.fori_loop(0, d // CAST_ROWS, cast_gu, 0)
        lax.fori_loop(0, f // CAST_ROWS, cast_d, 0)

        def block(j, c):
            g = blk0 + j
            slot = g % 2
            x_copy(g).wait()

            @pl.when(g + 1 < n_valid)
            def _():
                x_copy(g + 1).start()

            @pl.when(g >= 2)
            def _():
                y_copy(g - 2).wait()

            row = lax.broadcasted_iota(jnp.int32, (bm, 1), 0)
            w = jnp.where(row < cnt_ref[e] - j * bm, xbuf[slot], jnp.uint32(0))
            x_lo, x_hi = _unpack_bf16_pairs(w)
            gu = (jnp.dot(x_lo.astype(BF16), wgu_bf[:d // 2, :], preferred_element_type=F32)
                  + jnp.dot(x_hi.astype(BF16), wgu_bf[d // 2:, :], preferred_element_type=F32)
                  + bgu_ref[0])
            gate = jnp.minimum(gu[:, :f], SWIGLU_LIMIT)
            up = jnp.clip(gu[:, f:], -SWIGLU_LIMIT, SWIGLU_LIMIT)
            act = gate * jax.nn.sigmoid(SWIGLU_ALPHA * gate) * (up + 1.0)
            y = jnp.dot(act.astype(BF16), wd_bf[...], preferred_element_type=F32) + bd_ref[0]
            ybuf[slot] = _pack_bf16_pairs(y)
            y_copy(g).start()
            return c

        lax.fori_loop(0, n, block, 0)

    @pl.when(e == n_exp - 1)
    def _():
        @pl.when(n_valid >= 2)
        def _():
            y_copy(n_valid - 2).wait()

        @pl.when(n_valid >= 1)
        def _():
            y_copy(n_valid - 1).wait()

        total = y_hbm.shape[0] // bm
        ybuf[0] = jnp.zeros(ybuf.shape[1:], ybuf.dtype)

        def z_copy(b):
            row0 = pl.multiple_of(b * bm, bm)
            return pltpu.make_async_copy(ybuf.at[0], y_hbm.at[pl.ds(row0, bm)], ysem.at[0])

        def fill(b, c):
            z_copy(b).start()
            return c

        def drain(b, c):
            z_copy(b).wait()
            return c

        lax.fori_loop(n_valid, total, fill, 0)
        lax.fori_loop(n_valid, total, drain, 0)


def _expert_call(blk_start, nblk, cnt, xs, wgu, bgu, wd, bd, bm):
    e, d, f2 = wgu.shape
    f = wd.shape[1]
    r, dw = xs.shape

    def bsel(i, bs, nb, ct):
        return (i, 0, 0)

    grid_spec = pltpu.PrefetchScalarGridSpec(
        num_scalar_prefetch=3,
        grid=(e,),
        in_specs=[
            pl.BlockSpec((1, 1, f2), bsel),
            pl.BlockSpec((1, 1, d), bsel),
            pl.BlockSpec(memory_space=pl.ANY),
            pl.BlockSpec(memory_space=pl.ANY),
            pl.BlockSpec(memory_space=pl.ANY),
        ],
        out_specs=pl.BlockSpec(memory_space=pl.ANY),
        scratch_shapes=[pltpu.VMEM((2, d, f2), F32), pltpu.VMEM((2, f, d), F32),
                        pltpu.VMEM((d, f2), BF16), pltpu.VMEM((f, d), BF16),
                        pltpu.VMEM((2, bm, dw), xs.dtype),
                        pltpu.VMEM((2, bm, dw), jnp.uint32),
                        pltpu.SemaphoreType.DMA((2,)), pltpu.SemaphoreType.DMA((2,)),
                        pltpu.SemaphoreType.DMA((2, 2))],
    )
    return pl.pallas_call(
        _expert_kernel,
        grid_spec=grid_spec,
        out_shape=jax.ShapeDtypeStruct((r, dw), jnp.uint32),
        compiler_params=pltpu.CompilerParams(dimension_semantics=("arbitrary",),
                                             vmem_limit_bytes=VMEM_LIMIT),
        name="experts",
    )(blk_start, nblk, cnt, bgu, bd, wgu, wd, xs)


def _pad_head_cols(w, widths_in, place):
    k = w.shape[0]
    per = sum(widths_in)
    blocks = []
    for hh in range(N_HEADS):
        blocks.append(place(w[:, hh * per:(hh + 1) * per]))
    return jnp.concatenate(blocks, axis=1)


def _layer(x2, pos3, batch, seq, g_attn_norm, w_in, w_pool, b_pool, pool_scale, g_q_a, w_q_b,
           g_kv_a, w_kv_b, g_out_pool, g_out_attn, w_out, g_ffn_norm, w_router, b_router,
           w_gate_up, b_gate_up, w_down, b_down, tiles):
    tm, tmo, bm, tf = tiles
    t, d = x2.shape
    d_pool = pool_scale.shape[0]
    gdim = d_pool // N_POOL_GROUPS
    q_rank = g_q_a.shape[0]
    kv_rank = g_kv_a.shape[0]
    half = QK_ROPE_DIM // 2

    o1, o2, o3 = d_pool, d_pool + q_rank, d_pool + q_rank + kv_rank
    zk = lambda n: jnp.zeros((d, n), F32)
    w_kr1, w_kr2 = w_in[:, o3:o3 + half], w_in[:, o3 + half:]
    w_in_p = jnp.concatenate([w_in[:, :o3], zk(QK_NOPE_DIM), w_kr1, w_kr2, zk(LANE - QK_DIM),
                              zk(QK_NOPE_DIM), w_kr2, w_kr1, zk(LANE - QK_DIM)], axis=1).astype(BF16)
    zq = lambda n: jnp.zeros((q_rank, n), F32)
    wq_plain = _pad_head_cols(w_q_b, (QK_DIM,),
                              lambda c: jnp.concatenate([c, zq(LANE - QK_DIM)], axis=1))
    wq_swap = _pad_head_cols(w_q_b, (QK_DIM,), lambda c: jnp.concatenate(
        [zq(QK_NOPE_DIM), c[:, QK_NOPE_DIM + half:], c[:, QK_NOPE_DIM:QK_NOPE_DIM + half],
         zq(LANE - QK_DIM)], axis=1))
    wq_p = jnp.concatenate([wq_plain, wq_swap], axis=1).astype(BF16)
    zkv = lambda n: jnp.zeros((kv_rank, n), F32)
    per = QK_NOPE_DIM + V_HEAD_DIM
    wk_p = _pad_head_cols(w_kv_b, (per,), lambda c: jnp.concatenate(
        [c[:, :QK_NOPE_DIM], zkv(LANE - QK_NOPE_DIM)], axis=1))
    wv_p = _pad_head_cols(w_kv_b, (per,), lambda c: jnp.concatenate(
        [c[:, QK_NOPE_DIM:], zkv(LANE - V_HEAD_DIM)], axis=1))
    wkv_p = jnp.concatenate([wk_p, wv_p], axis=1).astype(BF16)
    inv = ROPE_THETA ** (-jnp.arange(half, dtype=F32) / half)
    inv_col = inv.reshape(half, 1)

    row = lambda v: v.reshape(1, -1)
    grp = lambda v: v.reshape(N_POOL_GROUPS, 1, gdim)

    pool_n, q, k, v = _proj_call(
        x2, pos3, row(g_attn_norm), w_in_p, inv_col, w_pool.astype(BF16), grp(b_pool),
        grp(pool_scale), grp(g_out_pool), row(g_q_a), wq_p, row(g_kv_a), wkv_p, seq, tm)

    attn = _attn_call(q, k, v, batch, seq, tm)

    d_attn = N_HEADS * V_HEAD_DIM
    wo_pool = w_out[:d_pool].astype(BF16)
    wo_attn = w_out[d_pool:].reshape(N_HEADS, V_HEAD_DIM, d)
    wo_attn_p = jnp.concatenate([wo_attn, jnp.zeros((N_HEADS, HEAD_PAD - V_HEAD_DIM, d), F32)],
                                axis=1).reshape(N_HEADS * HEAD_PAD, d).astype(BF16)
    gattn_p = jnp.concatenate([g_out_attn.reshape(N_HEADS, V_HEAD_DIM),
                               jnp.zeros((N_HEADS, HEAD_PAD - V_HEAD_DIM), F32)], axis=1).reshape(1, -1)

    x1, hp, idx, gates, rank, counts = _outproj_call(
        pool_n, attn, x2, gattn_p, wo_pool, wo_attn_p, row(g_ffn_norm), w_router.T,
        b_router.reshape(-1, 1), tmo)

    n_assign = t * TOP_K
    cnt = counts[:, 0]
    padded = ((cnt + bm - 1) // bm) * bm
    pad_end = jnp.cumsum(padded)
    pad_start = pad_end - padded
    e_ids = jnp.arange(N_EXPERTS, dtype=jnp.int32)
    dest = rank + jnp.sum(jnp.where(idx[None] == e_ids[:, None, None], pad_start[:, None, None], 0),
                          axis=0)
    win_idx = _window_indices(dest)
    n_rows = (n_assign // bm + N_EXPERTS) * bm
    blk_start = (pad_start // bm).astype(jnp.int32)
    nblk = (padded // bm).astype(jnp.int32)

    xs = _dispatch_sc_call(win_idx, hp, n_rows)
    yb = _expert_call(blk_start, nblk, cnt.astype(jnp.int32), xs, w_gate_up,
                      b_gate_up.reshape(N_EXPERTS, 1, -1), w_down, b_down.reshape(N_EXPERTS, 1, -1), bm)
    yg = _gather_sc_call(win_idx, yb)
    return x1, gates, yg


def _forward(x, positions, g_attn_norm, w_in, w_pool, b_pool, pool_scale, g_q_a, w_q_b, g_kv_a,
             w_kv_b, g_out_pool, g_out_attn, w_out, g_ffn_norm, w_router, b_router, w_gate_up,
             b_gate_up, w_down, b_down, g_final, tiles):
    batch, seq, d = x.shape
    t = batch * seq
    tm, tmo, bm, tf = tiles
    x2 = x.reshape(t, d)
    pos3 = positions.reshape(t // tm, 1, tm)
    depth = g_attn_norm.shape[0]
    assert depth == 1
    l = 0
    x1, gates, yg = _layer(
        x2, pos3, batch, seq, g_attn_norm[l], w_in[l], w_pool[l], b_pool[l], pool_scale[l],
        g_q_a[l], w_q_b[l], g_kv_a[l], w_kv_b[l], g_out_pool[l], g_out_attn[l], w_out[l],
        g_ffn_norm[l], w_router[l], b_router[l], w_gate_up[l], b_gate_up[l], w_down[l],
        b_down[l], tiles)
    out = _combine_sum_call(x1, gates.T, g_final.reshape(1, -1), yg, tf)
    return out.reshape(batch, seq, d)


def _tiles_for(seq):
    tm = min(512, seq // 2)
    tmo = min(512, seq)
    bm = 256 if seq >= 2048 else 128
    tf = min(512, seq)
    return (tm, tmo, bm, tf)


def kernel(x, positions, g_attn_norm, w_in, w_pool, b_pool, pool_scale, g_q_a, w_q_b, g_kv_a, w_kv_b, g_out_pool, g_out_attn, w_out, g_ffn_norm, w_router, b_router, w_gate_up, b_gate_up, w_down, b_down, g_final):
    tiles = _tiles_for(x.shape[1])
    return _forward(x, positions, g_attn_norm, w_in, w_pool, b_pool, pool_scale, g_q_a, w_q_b,
                    g_kv_a, w_kv_b, g_out_pool, g_out_attn, w_out, g_ffn_norm, w_router, b_router,
                    w_gate_up, b_gate_up, w_down, b_down, g_final, tiles)
```

```python
import functools

import jax
import jax.numpy as jnp
from jax import lax
from jax.experimental import pallas as pl
from jax.experimental.pallas import tpu as pltpu
from jax.experimental.pallas import tpu_sc as plsc

POOL_WINDOWS = (2, 4, 8, 16)
N_POOL_GROUPS = 4
N_HEADS = 8
V_HEAD_DIM = 64
QK_NOPE_DIM = 64
QK_ROPE_DIM = 32
QK_DIM = QK_NOPE_DIM + QK_ROPE_DIM
ROPE_THETA = 10000.0
N_EXPERTS = 32
TOP_K = 4
SWIGLU_ALPHA = 1.702
SWIGLU_LIMIT = 7.0
RMS_EPS = 1e-6
LOG2_E = 1.4426950408889634

LANE = 128
HEAD_PAD = 128
POOL_HALO = 16

VMEM_LIMIT = 56 * 1024 * 1024

F32 = jnp.float32
BF16 = jnp.bfloat16


def _pack_bf16_pairs(x):
    n = x.shape[1] // 2
    xb = x.astype(BF16).astype(F32)
    lo = lax.bitcast_convert_type(xb[:, :n], jnp.uint32)
    hi = lax.bitcast_convert_type(xb[:, n:], jnp.uint32)
    return (lo >> 16) | (hi & jnp.uint32(0xFFFF0000))


def _unpack_bf16_pairs(w):
    lo = lax.bitcast_convert_type(w << 16, F32)
    hi = lax.bitcast_convert_type(w & jnp.uint32(0xFFFF0000), F32)
    return lo, hi


def _rms(x, g, n):
    ms = jnp.sum(x * x, axis=-1, keepdims=True) * (1.0 / n)
    return x * lax.rsqrt(ms + RMS_EPS) * g


def _proj_kernel(blocks_per_seq, x_ref, pos_ref, g_attn_ref, w_in_ref, inv_ref, wpool_ref,
                 bpool_ref, pscale_ref, gpool_ref, gq_ref, wq_ref, gkv_ref, wkv_ref,
                 pool_out, q_out, k_out, v_out, ext_scr):
    tm = x_ref.shape[0]
    d_model = x_ref.shape[1]
    d_pool = pool_out.shape[1]
    gdim = d_pool // N_POOL_GROUPS
    q_rank = gq_ref.shape[1]
    kv_rank = gkv_ref.shape[1]
    i = pl.program_id(0)
    blk_in_seq = i % blocks_per_seq

    x = x_ref[...]
    h = _rms(x, g_attn_ref[...], d_model).astype(BF16)
    proj = jnp.dot(h, w_in_ref[...], preferred_element_type=F32)
    u = proj[:, :d_pool]
    qlat = proj[:, d_pool:d_pool + q_rank]
    kvlat = proj[:, d_pool + q_rank:d_pool + q_rank + kv_rank]
    o_kr = d_pool + q_rank + kv_rank
    krb = proj[:, o_kr:o_kr + LANE]
    krs = proj[:, o_kr + LANE:]

    @pl.when(blk_in_seq == 0)
    def _():
        ext_scr[0:POOL_HALO, :] = jnp.zeros((POOL_HALO, d_pool), F32)

    @pl.when(blk_in_seq != 0)
    def _():
        ext_scr[0:POOL_HALO, :] = ext_scr[tm:tm + POOL_HALO, :]

    ext_scr[POOL_HALO:, :] = u
    t_in_seq = (blk_in_seq * tm + lax.broadcasted_iota(jnp.int32, (tm, gdim), 0) + 1).astype(F32)
    ys = []
    ssq = jnp.zeros((tm, 1), F32)
    for g, w in enumerate(POOL_WINDOWS):
        a = ext_scr[:, g * gdim:(g + 1) * gdim]
        shift = 1
        while shift < w:
            a = a + pltpu.roll(a, shift, axis=0)
            shift *= 2
        win = a[POOL_HALO:, :]
        cnt = jnp.minimum(t_in_seq, float(w))
        ug = u[:, g * gdim:(g + 1) * gdim]
        diff = (win / cnt - ug).astype(BF16)
        mixed = jnp.dot(diff, wpool_ref[g], preferred_element_type=F32) + bpool_ref[g]
        y = mixed * pscale_ref[g]
        ssq = ssq + jnp.sum(y * y, axis=-1, keepdims=True)
        ys.append(y)
    rinv = lax.rsqrt(ssq * (1.0 / d_pool) + RMS_EPS)
    for g in range(N_POOL_GROUPS):
        pool_out[:, g * gdim:(g + 1) * gdim] = (ys[g] * rinv * gpool_ref[g]).astype(pool_out.dtype)

    pos = pos_ref[0].astype(F32)
    ang_t = inv_ref[...] * pos
    cos_t = jnp.cos(ang_t)
    sin_t = jnp.sin(ang_t)
    half = QK_ROPE_DIM // 2
    ones = jnp.ones((QK_NOPE_DIM, tm), F32)
    c_t = jnp.concatenate([ones, cos_t, cos_t, jnp.ones((LANE - QK_DIM, tm), F32)], axis=0)
    s_t = jnp.concatenate([jnp.zeros((QK_NOPE_DIM, tm), F32), -sin_t, sin_t,
                           jnp.zeros((LANE - QK_DIM, tm), F32)], axis=0)
    c = c_t.T
    s = s_t.T

    def rope(blk, blk_swapped, c_, s_):
        return blk * c_ + blk_swapped * s_

    qn = _rms(qlat, gq_ref[...], q_rank).astype(BF16)
    q2 = jnp.dot(qn, wq_ref[...], preferred_element_type=F32)
    scale = QK_DIM ** -0.5 * LOG2_E
    cq, sq = c * scale, s * scale
    hp = N_HEADS * HEAD_PAD
    for hh in range(N_HEADS):
        sl = slice(hh * HEAD_PAD, (hh + 1) * HEAD_PAD)
        sw = slice(hp + hh * HEAD_PAD, hp + (hh + 1) * HEAD_PAD)
        q_out[:, sl] = rope(q2[:, sl], q2[:, sw], cq, sq).astype(q_out.dtype)

    kvn = _rms(kvlat, gkv_ref[...], kv_rank).astype(BF16)
    kv = jnp.dot(kvn, wkv_ref[...], preferred_element_type=F32)
    kr = rope(krb, krs, c, s)
    for hh in range(N_HEADS):
        sl = slice(hh * HEAD_PAD, (hh + 1) * HEAD_PAD)
        k_out[:, sl] = (kv[:, sl] + kr).astype(k_out.dtype)
    vlane = lax.broadcasted_iota(jnp.int32, (1, N_HEADS * HEAD_PAD), 1) % HEAD_PAD
    ones_col = (vlane == V_HEAD_DIM).astype(F32)
    v_out[0] = (kv[:, N_HEADS * HEAD_PAD:] + ones_col).T.astype(v_out.dtype)


def _proj_call(x2, pos3, g_attn, w_in_p, inv_col, wpool, bpool, pscale, gpool, gq, wq_p, gkv, wkv_p,
               seq, tm):
    t, d = x2.shape
    d_pool = pscale.shape[0] * pscale.shape[2]
    hp = N_HEADS * HEAD_PAD
    nblk = t // tm
    const2 = lambda i: (0, 0)
    const3 = lambda i: (0, 0, 0)
    in_specs = [
        pl.BlockSpec((tm, d), lambda i: (i, 0)),
        pl.BlockSpec((1, 1, tm), lambda i: (i, 0, 0)),
        pl.BlockSpec(g_attn.shape, const2),
        pl.BlockSpec(w_in_p.shape, const2),
        pl.BlockSpec(inv_col.shape, const2),
        pl.BlockSpec(wpool.shape, const3),
        pl.BlockSpec(bpool.shape, const3),
        pl.BlockSpec(pscale.shape, const3),
        pl.BlockSpec(gpool.shape, const3),
        pl.BlockSpec(gq.shape, const2),
        pl.BlockSpec(wq_p.shape, const2),
        pl.BlockSpec(gkv.shape, const2),
        pl.BlockSpec(wkv_p.shape, const2),
    ]
    out_shape = [
        jax.ShapeDtypeStruct((t, d_pool), BF16),
        jax.ShapeDtypeStruct((t, hp), BF16),
        jax.ShapeDtypeStruct((t, hp), BF16),
        jax.ShapeDtypeStruct((nblk, hp, tm), BF16),
    ]
    out_specs = [
        pl.BlockSpec((tm, d_pool), lambda i: (i, 0)),
        pl.BlockSpec((tm, hp), lambda i: (i, 0)),
        pl.BlockSpec((tm, hp), lambda i: (i, 0)),
        pl.BlockSpec((1, hp, tm), lambda i: (i, 0, 0)),
    ]
    return pl.pallas_call(
        functools.partial(_proj_kernel, seq // tm),
        grid=(nblk,),
        in_specs=in_specs,
        out_specs=out_specs,
        out_shape=out_shape,
        scratch_shapes=[pltpu.VMEM((tm + POOL_HALO, d_pool), F32)],
        compiler_params=pltpu.CompilerParams(dimension_semantics=("arbitrary",),
                                             vmem_limit_bytes=VMEM_LIMIT),
        name="proj",
    )(x2, pos3, g_attn, w_in_p, inv_col, wpool, bpool, pscale, gpool, gq, wq_p, gkv, wkv_p)


ROWSUM_ROW = V_HEAD_DIM


def _attn_kernel(q_ref, k_ref, vt_ref, o_ref, acc_ref):
    tq = q_ref.shape[0]
    th = vt_ref.shape[2]
    nsub = tq // th
    qi = pl.program_id(2)
    acc_ref[...] = jnp.zeros(acc_ref.shape, F32)

    def step(kc, m_prev, q_lo, masked):
        width = tq - q_lo
        start = pl.multiple_of(kc * th, th)
        k = k_ref[pl.ds(start, th), :]
        st = lax.dot_general(k, q_ref[q_lo:, :], (((1,), (1,)), ((), ())),
                             preferred_element_type=F32)
        if masked:
            key = lax.broadcasted_iota(jnp.int32, (th, width), 0)
            qry = lax.broadcasted_iota(jnp.int32, (th, width), 1)
            st = jnp.where(key <= qry, st, -jnp.inf)
        m_new = jnp.maximum(m_prev, jnp.max(st, axis=0, keepdims=True))
        alpha = jnp.exp2(m_prev - m_new)
        pt = jnp.exp2(st - m_new).astype(BF16)
        acc_ref[:, q_lo:] = alpha * acc_ref[:, q_lo:] + jnp.dot(vt_ref[kc], pt,
                                                                preferred_element_type=F32)
        return m_new

    def two_chunks(j, mp):
        return step(2 * j + 1, step(2 * j, mp, 0, False), 0, False)

    m = lax.fori_loop(0, (nsub // 2) * qi, two_chunks, jnp.full((1, tq), -jnp.inf, F32))
    for c in range(nsub):
        m = step(nsub * qi + c, m, c * th, True)[:, th:]

    acc = acc_ref[...]
    out = (acc / acc[ROWSUM_ROW:ROWSUM_ROW + 1, :]).T
    lane = lax.broadcasted_iota(jnp.int32, out.shape, 1)
    o_ref[...] = jnp.where(lane < V_HEAD_DIM, out, 0.0).astype(o_ref.dtype)


ATTN_Q_CHUNKS = 8


def _attn_call(q, k, vt, batch, seq, th):
    t, hp = q.shape
    tq = min(ATTN_Q_CHUNKS * th, seq)
    nq = seq // tq
    nchunk = seq // th
    return pl.pallas_call(
        _attn_kernel,
        grid=(batch, N_HEADS, nq),
        in_specs=[
            pl.BlockSpec((tq, HEAD_PAD), lambda b, h, qi: (b * nq + qi, h)),
            pl.BlockSpec((seq, HEAD_PAD), lambda b, h, qi: (b, h)),
            pl.BlockSpec((nchunk, HEAD_PAD, th), lambda b, h, qi: (b, h, 0)),
        ],
        out_specs=pl.BlockSpec((tq, HEAD_PAD), lambda b, h, qi: (b * nq + qi, h)),
        out_shape=jax.ShapeDtypeStruct((t, hp), F32),
        scratch_shapes=[pltpu.VMEM((HEAD_PAD, tq), F32)],
        compiler_params=pltpu.CompilerParams(
            dimension_semantics=("arbitrary", "arbitrary", "arbitrary"),
            vmem_limit_bytes=VMEM_LIMIT),
        name="attn",
    )(q, k, vt)


def _outproj_kernel(pool_ref, attn_ref, x_ref, gattn_ref, wop_ref, woa_ref, gffn_ref, wr_ref,
                    br_ref, x1_out, hp_out, idx_out, gate_out, rank_out, cnt_out, run_scr):
    tm, d = x_ref.shape
    d_attn = N_HEADS * V_HEAD_DIM
    i = pl.program_id(0)

    @pl.when(i == 0)
    def _():
        run_scr[...] = jnp.zeros(run_scr.shape, F32)

    attn = attn_ref[...]
    attn_n = _rms(attn, gattn_ref[...], d_attn).astype(BF16)
    mix = (jnp.dot(pool_ref[...], wop_ref[...], preferred_element_type=F32)
           + jnp.dot(attn_n, woa_ref[...], preferred_element_type=F32))
    x1 = x_ref[...] + mix
    x1_out[...] = x1
    h2 = _rms(x1, gffn_ref[...], d)

    hp_out[...] = _pack_bf16_pairs(h2)

    logits = lax.dot_general(wr_ref[...], h2, (((1,), (1,)), ((), ())),
                             preferred_element_type=F32,
                             precision=lax.Precision.HIGHEST) + br_ref[...]
    e_iota = lax.broadcasted_iota(jnp.int32, (N_EXPERTS, tm), 0)
    work = logits
    idxs, vals = [], []
    for _ in range(TOP_K):
        mx = jnp.max(work, axis=0, keepdims=True)
        sel = jnp.min(jnp.where(work == mx, e_iota, N_EXPERTS), axis=0, keepdims=True)
        idxs.append(sel)
        vals.append(mx)
        work = jnp.where(e_iota == sel, -jnp.inf, work)
    exps = [jnp.exp(v - vals[0]) for v in vals]
    denom = exps[0] + exps[1] + exps[2] + exps[3]
    gate_out[...] = jnp.concatenate([e / denom for e in exps], axis=0)
    idx_out[...] = jnp.concatenate(idxs, axis=0)

    onehot = jnp.zeros((N_EXPERTS, tm), F32)
    for sel in idxs:
        onehot = onehot + (e_iota == sel).astype(F32)
    r_io = lax.broadcasted_iota(jnp.int32, (tm, tm), 0)
    c_io = lax.broadcasted_iota(jnp.int32, (tm, tm), 1)
    upper = (r_io < c_io).astype(BF16)
    prefix = jnp.dot(onehot.astype(BF16), upper, preferred_element_type=F32) + run_scr[:, 0:1]
    ranks = [jnp.sum(jnp.where(e_iota == sel, prefix, 0.0), axis=0, keepdims=True) for sel in idxs]
    rank_out[...] = jnp.concatenate(ranks, axis=0).astype(jnp.int32)
    run_new = run_scr[...] + jnp.sum(onehot, axis=1, keepdims=True)
    run_scr[...] = run_new
    cnt_out[...] = run_new.astype(jnp.int32)


def _outproj_call(pool_n, attn, x2, gattn_p, wo_pool, wo_attn_p, gffn, wr_t, br_col, tm):
    t, d = x2.shape
    nblk = t // tm
    const2 = lambda i: (0, 0)
    row = lambda i: (i, 0)
    col = lambda i: (0, i)
    in_specs = [
        pl.BlockSpec((tm, pool_n.shape[1]), row),
        pl.BlockSpec((tm, attn.shape[1]), row),
        pl.BlockSpec((tm, d), row),
        pl.BlockSpec(gattn_p.shape, const2),
        pl.BlockSpec(wo_pool.shape, const2),
        pl.BlockSpec(wo_attn_p.shape, const2),
        pl.BlockSpec(gffn.shape, const2),
        pl.BlockSpec(wr_t.shape, const2),
        pl.BlockSpec(br_col.shape, const2),
    ]
    out_shape = [
        jax.ShapeDtypeStruct((t, d), F32),
        jax.ShapeDtypeStruct((t, d // 2), jnp.uint32),
        jax.ShapeDtypeStruct((TOP_K, t), jnp.int32),
        jax.ShapeDtypeStruct((TOP_K, t), F32),
        jax.ShapeDtypeStruct((TOP_K, t), jnp.int32),
        jax.ShapeDtypeStruct((N_EXPERTS, LANE), jnp.int32),
    ]
    out_specs = [
        pl.BlockSpec((tm, d), row),
        pl.BlockSpec((tm, d // 2), row),
        pl.BlockSpec((TOP_K, tm), col),
        pl.BlockSpec((TOP_K, tm), col),
        pl.BlockSpec((TOP_K, tm), col),
        pl.BlockSpec((N_EXPERTS, LANE), const2),
    ]
    return pl.pallas_call(
        _outproj_kernel,
        grid=(nblk,),
        in_specs=in_specs,
        out_specs=out_specs,
        out_shape=out_shape,
        scratch_shapes=[pltpu.VMEM((N_EXPERTS, LANE), F32)],
        compiler_params=pltpu.CompilerParams(dimension_semantics=("arbitrary",),
                                             vmem_limit_bytes=VMEM_LIMIT),
        name="outproj",
    )(pool_n, attn, x2, gattn_p, wo_pool, wo_attn_p, gffn, wr_t, br_col)


SC_WINDOW = 64
COMBINE_PARTS = 2


def _window_indices(dest_kt):
    t = dest_kt.shape[1]
    return dest_kt.reshape(TOP_K, t // SC_WINDOW, SC_WINDOW).transpose(1, 0, 2)


def _dispatch_sc_call(idx, hp_rows, n_rows):
    t, w = hp_rows.shape
    mesh = plsc.VectorSubcoreMesh(core_axis_name="core", subcore_axis_name="subcore")

    @functools.partial(pl.kernel, out_type=jax.ShapeDtypeStruct((n_rows, w), hp_rows.dtype),
                       mesh=mesh, scratch_types=[])
    def scatter_kernel(x_hbm, i_hbm, o_hbm):
        def body(x_vmem, i_vmem):
            for k in range(TOP_K):
                pltpu.sync_copy(x_vmem, o_hbm.at[i_vmem.at[0, k]])

        pltpu.emit_pipeline(
            body,
            grid=(t // SC_WINDOW,),
            in_specs=[pl.BlockSpec((SC_WINDOW, w), lambda i: (i, 0)),
                      pl.BlockSpec((1, TOP_K, SC_WINDOW), lambda i: (i, 0, 0))],
            out_specs=[],
            core_axis_name=("core", "subcore"),
            dimension_semantics=(pltpu.PARALLEL,),
        )(x_hbm, i_hbm)

    return scatter_kernel(hp_rows, idx)


def _gather_sc_call(idx, rows):
    nwin, _, _ = idx.shape
    t = nwin * SC_WINDOW
    wd = rows.shape[1]
    mesh = plsc.VectorSubcoreMesh(core_axis_name="core", subcore_axis_name="subcore")

    @functools.partial(pl.kernel, out_type=jax.ShapeDtypeStruct((TOP_K, t, wd), rows.dtype),
                       mesh=mesh, scratch_types=[])
    def gather_kernel(y_hbm, i_hbm, o_hbm):
        def body(i_vmem, o_vmem):
            pltpu.sync_copy(y_hbm.at[i_vmem.at[0, 0]], o_vmem.at[0])

        pltpu.emit_pipeline(
            body,
            grid=(nwin, TOP_K),
            in_specs=[pl.BlockSpec((1, 1, SC_WINDOW), lambda i, k: (i * TOP_K + k, 0, 0))],
            out_specs=[pl.BlockSpec((1, SC_WINDOW, wd), lambda i, k: (k, i, 0))],
            core_axis_name=("core", "subcore"),
            dimension_semantics=(pltpu.PARALLEL, pltpu.PARALLEL),
        )(i_hbm, o_hbm)

    return gather_kernel(rows, idx.reshape(nwin * TOP_K, 1, SC_WINDOW))


def _combine_sum_kernel(x1_ref, gate_ref, gfin_ref, yg_ref, *rest):
    o_ref = rest[-1]
    d = x1_ref.shape[1]
    g = gate_ref[...]
    y = x1_ref[...]
    for k in range(TOP_K):
        y = y + g[:, k:k + 1] * jnp.concatenate(_unpack_bf16_pairs(yg_ref[k]), axis=1)
    o_ref[...] = _rms(y, gfin_ref[...], d)


def _combine_sum_call(x1, gates_t, gfin, yg, tf, first_block, out_so_far=None):
    t, d = x1.shape
    n = yg.shape[1]
    here = lambda i: (first_block + i, 0)
    in_specs = [
        pl.BlockSpec((tf, d), here),
        pl.BlockSpec((tf, TOP_K), here),
        pl.BlockSpec(gfin.shape, lambda i: (0, 0)),
        pl.BlockSpec((TOP_K, tf, d // 2), lambda i: (0, i, 0)),
    ]
    args = [x1, gates_t, gfin, yg]
    aliases = {}
    if out_so_far is not None:
        in_specs.append(pl.BlockSpec(memory_space=pl.ANY))
        args.append(out_so_far)
        aliases = {len(args) - 1: 0}
    return pl.pallas_call(
        _combine_sum_kernel,
        grid=(n // tf,),
        in_specs=in_specs,
        out_specs=pl.BlockSpec((tf, d), here),
        out_shape=jax.ShapeDtypeStruct((t, d), F32),
        input_output_aliases=aliases,
        compiler_params=pltpu.CompilerParams(dimension_semantics=("arbitrary",),
                                             vmem_limit_bytes=VMEM_LIMIT),
        name="combine",
    )(*args)


CAST_ROWS = 64


def _expert_kernel(bs_ref, nblk_ref, cnt_ref, bgu_ref, bd_ref, wgu_hbm, wd_hbm, xs_hbm, y_hbm,
                   wgu_f32, wd_f32, wgu_bf, wd_bf, xbuf, ybuf, xsem, ysem, wsem):
    e = pl.program_id(0)
    n_exp = pl.num_programs(0)
    d = wgu_f32.shape[1]
    f = wd_f32.shape[1]
    bm = xbuf.shape[1]
    n = nblk_ref[e]
    blk0 = bs_ref[e]
    wslot = e % 2

    def w_copies(ex, slot):
        return (pltpu.make_async_copy(wgu_hbm.at[ex], wgu_f32.at[slot], wsem.at[0, slot]),
                pltpu.make_async_copy(wd_hbm.at[ex], wd_f32.at[slot], wsem.at[1, slot]))

    n_valid = bs_ref[n_exp - 1] + nblk_ref[n_exp - 1]

    def x_copy(g):
        row0 = pl.multiple_of(g * bm, bm)
        return pltpu.make_async_copy(xs_hbm.at[pl.ds(row0, bm)], xbuf.at[g % 2], xsem.at[g % 2])

    def y_copy(g):
        row0 = pl.multiple_of(g * bm, bm)
        return pltpu.make_async_copy(ybuf.at[g % 2], y_hbm.at[pl.ds(row0, bm)], ysem.at[g % 2])

    @pl.when(e == 0)
    def _():
        x_copy(0).start()
        for cp in w_copies(0, 0):
            cp.start(priority=1)

    @pl.when(e + 1 < n_exp)
    def _():
        for cp in w_copies(e + 1, 1 - wslot):
            cp.start(priority=1)

    for cp in w_copies(e, wslot):
        cp.wait()

    @pl.when(n > 0)
    def _():
        def cast_gu(r, c):
            rows = pl.ds(pl.multiple_of(r * CAST_ROWS, CAST_ROWS), CAST_ROWS)
            wgu_bf[rows, :] = wgu_f32[wslot, rows, :].astype(BF16)
            return c

        def cast_d(r, c):
            rows = pl.ds(pl.multiple_of(r * CAST_ROWS, CAST_ROWS), CAST_ROWS)
            wd_bf[rows, :] = wd_f32[wslot, rows, :].astype(BF16)
            return c

        lax.fori_loop(0, d // CAST_ROWS, cast_gu, 0)
        lax.fori_loop(0, f // CAST_ROWS, cast_d, 0)

        def block(j, c):
            g = blk0 + j
            slot = g % 2
            x_copy(g).wait()

            @pl.when(g + 1 < n_valid)
            def _():
                x_copy(g + 1).start()

            @pl.when(g >= 2)
            def _():
                y_copy(g - 2).wait()

            row = lax.broadcasted_iota(jnp.int32, (bm, 1), 0)
            w = jnp.where(row < cnt_ref[e] - j * bm, xbuf[slot], jnp.uint32(0))
            x_lo, x_hi = _unpack_bf16_pairs(w)
            gu = (jnp.dot(x_lo.astype(BF16), wgu_bf[:d // 2, :], preferred_element_type=F32)
                  + jnp.dot(x_hi.astype(BF16), wgu_bf[d // 2:, :], preferred_element_type=F32)
                  + bgu_ref[0])
            gate = jnp.minimum(gu[:, :f], SWIGLU_LIMIT)
            up = jnp.clip(gu[:, f:], -SWIGLU_LIMIT, SWIGLU_LIMIT)
            act = gate * jax.nn.sigmoid(SWIGLU_ALPHA * gate) * (up + 1.0)
            y = jnp.dot(act.astype(BF16), wd_bf[...], preferred_element_type=F32) + bd_ref[0]
            ybuf[slot] = _pack_bf16_pairs(y)
            y_copy(g).start()
            return c

        lax.fori_loop(0, n, block, 0)

    @pl.when(e == n_exp - 1)
    def _():
        @pl.when(n_valid >= 2)
        def _():
            y_copy(n_valid - 2).wait()

        @pl.when(n_valid >= 1)
        def _():
            y_copy(n_valid - 1).wait()

        total = y_hbm.shape[0] // bm
        ybuf[0] = jnp.zeros(ybuf.shape[1:], ybuf.dtype)

        def z_copy(b):
            row0 = pl.multiple_of(b * bm, bm)
            return pltpu.make_async_copy(ybuf.at[0], y_hbm.at[pl.ds(row0, bm)], ysem.at[0])

        def fill(b, c):
            z_copy(b).start()
            return c

        def drain(b, c):
            z_copy(b).wait()
            return c

        lax.fori_loop(n_valid, total, fill, 0)
        lax.fori_loop(n_valid, total, drain, 0)


def _expert_call(blk_start, nblk, cnt, xs, wgu, bgu, wd, bd, bm):
    e, d, f2 = wgu.shape
    f = wd.shape[1]
    r, dw = xs.shape

    def bsel(i, bs, nb, ct):
        return (i, 0, 0)

    grid_spec = pltpu.PrefetchScalarGridSpec(
        num_scalar_prefetch=3,
        grid=(e,),
        in_specs=[
            pl.BlockSpec((1, 1, f2), bsel),
            pl.BlockSpec((1, 1, d), bsel),
            pl.BlockSpec(memory_space=pl.ANY),
            pl.BlockSpec(memory_space=pl.ANY),
            pl.BlockSpec(memory_space=pl.ANY),
        ],
        out_specs=pl.BlockSpec(memory_space=pl.ANY),
        scratch_shapes=[pltpu.VMEM((2, d, f2), F32), pltpu.VMEM((2, f, d), F32),
                        pltpu.VMEM((d, f2), BF16), pltpu.VMEM((f, d), BF16),
                        pltpu.VMEM((2, bm, dw), xs.dtype),
                        pltpu.VMEM((2, bm, dw), jnp.uint32),
                        pltpu.SemaphoreType.DMA((2,)), pltpu.SemaphoreType.DMA((2,)),
                        pltpu.SemaphoreType.DMA((2, 2))],
    )
    return pl.pallas_call(
        _expert_kernel,
        grid_spec=grid_spec,
        out_shape=jax.ShapeDtypeStruct((r, dw), jnp.uint32),
        compiler_params=pltpu.CompilerParams(dimension_semantics=("arbitrary",),
                                             vmem_limit_bytes=VMEM_LIMIT),
        name="experts",
    )(blk_start, nblk, cnt, bgu, bd, wgu, wd, xs)


def _pad_head_cols(w, widths_in, place):
    k = w.shape[0]
    per = sum(widths_in)
    blocks = []
    for hh in range(N_HEADS):
        blocks.append(place(w[:, hh * per:(hh + 1) * per]))
    return jnp.concatenate(blocks, axis=1)


def _layer(x2, pos3, batch, seq, g_attn_norm, w_in, w_pool, b_pool, pool_scale, g_q_a, w_q_b,
           g_kv_a, w_kv_b, g_out_pool, g_out_attn, w_out, g_ffn_norm, w_router, b_router,
           w_gate_up, b_gate_up, w_down, b_down, tiles):
    tm, tmo, bm, tf = tiles
    t, d = x2.shape
    d_pool = pool_scale.shape[0]
    gdim = d_pool // N_POOL_GROUPS
    q_rank = g_q_a.shape[0]
    kv_rank = g_kv_a.shape[0]
    half = QK_ROPE_DIM // 2

    o1, o2, o3 = d_pool, d_pool + q_rank, d_pool + q_rank + kv_rank
    zk = lambda n: jnp.zeros((d, n), F32)
    w_kr1, w_kr2 = w_in[:, o3:o3 + half], w_in[:, o3 + half:]
    w_in_p = jnp.concatenate([w_in[:, :o3], zk(QK_NOPE_DIM), w_kr1, w_kr2, zk(LANE - QK_DIM),
                              zk(QK_NOPE_DIM), w_kr2, w_kr1, zk(LANE - QK_DIM)], axis=1).astype(BF16)
    zq = lambda n: jnp.zeros((q_rank, n), F32)
    wq_plain = _pad_head_cols(w_q_b, (QK_DIM,),
                              lambda c: jnp.concatenate([c, zq(LANE - QK_DIM)], axis=1))
    wq_swap = _pad_head_cols(w_q_b, (QK_DIM,), lambda c: jnp.concatenate(
        [zq(QK_NOPE_DIM), c[:, QK_NOPE_DIM + half:], c[:, QK_NOPE_DIM:QK_NOPE_DIM + half],
         zq(LANE - QK_DIM)], axis=1))
    wq_p = jnp.concatenate([wq_plain, wq_swap], axis=1).astype(BF16)
    zkv = lambda n: jnp.zeros((kv_rank, n), F32)
    per = QK_NOPE_DIM + V_HEAD_DIM
    wk_p = _pad_head_cols(w_kv_b, (per,), lambda c: jnp.concatenate(
        [c[:, :QK_NOPE_DIM], zkv(LANE - QK_NOPE_DIM)], axis=1))
    wv_p = _pad_head_cols(w_kv_b, (per,), lambda c: jnp.concatenate(
        [c[:, QK_NOPE_DIM:], zkv(LANE - V_HEAD_DIM)], axis=1))
    wkv_p = jnp.concatenate([wk_p, wv_p], axis=1).astype(BF16)
    inv = ROPE_THETA ** (-jnp.arange(half, dtype=F32) / half)
    inv_col = inv.reshape(half, 1)

    row = lambda v: v.reshape(1, -1)
    grp = lambda v: v.reshape(N_POOL_GROUPS, 1, gdim)

    pool_n, q, k, v = _proj_call(
        x2, pos3, row(g_attn_norm), w_in_p, inv_col, w_pool.astype(BF16), grp(b_pool),
        grp(pool_scale), grp(g_out_pool), row(g_q_a), wq_p, row(g_kv_a), wkv_p, seq, tm)

    attn = _attn_call(q, k, v, batch, seq, tm)

    d_attn = N_HEADS * V_HEAD_DIM
    wo_pool = w_out[:d_pool].astype(BF16)
    wo_attn = w_out[d_pool:].reshape(N_HEADS, V_HEAD_DIM, d)
    wo_attn_p = jnp.concatenate([wo_attn, jnp.zeros((N_HEADS, HEAD_PAD - V_HEAD_DIM, d), F32)],
                                axis=1).reshape(N_HEADS * HEAD_PAD, d).astype(BF16)
    gattn_p = jnp.concatenate([g_out_attn.reshape(N_HEADS, V_HEAD_DIM),
                               jnp.zeros((N_HEADS, HEAD_PAD - V_HEAD_DIM), F32)], axis=1).reshape(1, -1)

    x1, hp, idx, gates, rank, counts = _outproj_call(
        pool_n, attn, x2, gattn_p, wo_pool, wo_attn_p, row(g_ffn_norm), w_router.T,
        b_router.reshape(-1, 1), tmo)

    n_assign = t * TOP_K
    cnt = counts[:, 0]
    padded = ((cnt + bm - 1) // bm) * bm
    pad_end = jnp.cumsum(padded)
    pad_start = pad_end - padded
    e_ids = jnp.arange(N_EXPERTS, dtype=jnp.int32)
    dest = rank + jnp.sum(jnp.where(idx[None] == e_ids[:, None, None], pad_start[:, None, None], 0),
                          axis=0)
    win_idx = _window_indices(dest)
    n_rows = (n_assign // bm + N_EXPERTS) * bm
    blk_start = (pad_start // bm).astype(jnp.int32)
    nblk = (padded // bm).astype(jnp.int32)

    xs = _dispatch_sc_call(win_idx, hp, n_rows)
    yb = _expert_call(blk_start, nblk, cnt.astype(jnp.int32), xs, w_gate_up,
                      b_gate_up.reshape(N_EXPERTS, 1, -1), w_down, b_down.reshape(N_EXPERTS, 1, -1), bm)
    return x1, gates, yb, win_idx


def _forward(x, positions, g_attn_norm, w_in, w_pool, b_pool, pool_scale, g_q_a, w_q_b, g_kv_a,
             w_kv_b, g_out_pool, g_out_attn, w_out, g_ffn_norm, w_router, b_router, w_gate_up,
             b_gate_up, w_down, b_down, g_final, tiles):
    batch, seq, d = x.shape
    t = batch * seq
    tm, tmo, bm, tf = tiles
    x2 = x.reshape(t, d)
    pos3 = positions.reshape(t // tm, 1, tm)
    depth = g_attn_norm.shape[0]
    assert depth == 1
    l = 0
    x1, gates, yb, win_idx = _layer(
        x2, pos3, batch, seq, g_attn_norm[l], w_in[l], w_pool[l], b_pool[l], pool_scale[l],
        g_q_a[l], w_q_b[l], g_kv_a[l], w_kv_b[l], g_out_pool[l], g_out_attn[l], w_out[l],
        g_ffn_norm[l], w_router[l], b_router[l], w_gate_up[l], b_gate_up[l], w_down[l],
        b_down[l], tiles)
    gates_t = gates.T
    gfin = g_final.reshape(1, -1)
    wins = win_idx.shape[0] // COMBINE_PARTS
    out = None
    for p in range(COMBINE_PARTS):
        yg = _gather_sc_call(win_idx[p * wins:(p + 1) * wins], yb)
        out = _combine_sum_call(x1, gates_t, gfin, yg, tf, p * (wins * SC_WINDOW // tf), out)
    return out.reshape(batch, seq, d)


def _tiles_for(seq):
    tm = min(512, seq // 2)
    tmo = min(512, seq)
    bm = 256 if seq >= 2048 else 128
    tf = min(512, seq)
    return (tm, tmo, bm, tf)


def kernel(x, positions, g_attn_norm, w_in, w_pool, b_pool, pool_scale, g_q_a, w_q_b, g_kv_a, w_kv_b, g_out_pool, g_out_attn, w_out, g_ffn_norm, w_router, b_router, w_gate_up, b_gate_up, w_down, b_down, g_final):
    tiles = _tiles_for(x.shape[1])
    return _forward(x, positions, g_attn_norm, w_in, w_pool, b_pool, pool_scale, g_q_a, w_q_b,
                    g_kv_a, w_kv_b, g_out_pool, g_out_attn, w_out, g_ffn_norm, w_router, b_router,
                    w_gate_up, b_gate_up, w_down, b_down, g_final, tiles)
```

```python
import functools

import jax
import jax.numpy as jnp
from jax import lax
from jax.experimental import pallas as pl
from jax.experimental.pallas import tpu as pltpu
from jax.experimental.pallas import tpu_sc as plsc

POOL_WINDOWS = (2, 4, 8, 16)
N_POOL_GROUPS = 4
N_HEADS = 8
V_HEAD_DIM = 64
QK_NOPE_DIM = 64
QK_ROPE_DIM = 32
QK_DIM = QK_NOPE_DIM + QK_ROPE_DIM
ROPE_THETA = 10000.0
N_EXPERTS = 32
TOP_K = 4
SWIGLU_ALPHA = 1.702
SWIGLU_LIMIT = 7.0
RMS_EPS = 1e-6
LOG2_E = 1.4426950408889634

LANE = 128
HEAD_PAD = 128
POOL_HALO = 16

VMEM_LIMIT = 56 * 1024 * 1024

F32 = jnp.float32
BF16 = jnp.bfloat16


def _pack_bf16_pairs(x):
    n = x.shape[1] // 2
    xb = x.astype(BF16).astype(F32)
    lo = lax.bitcast_convert_type(xb[:, :n], jnp.uint32)
    hi = lax.bitcast_convert_type(xb[:, n:], jnp.uint32)
    return (lo >> 16) | (hi & jnp.uint32(0xFFFF0000))


def _unpack_bf16_pairs(w):
    lo = lax.bitcast_convert_type(w << 16, F32)
    hi = lax.bitcast_convert_type(w & jnp.uint32(0xFFFF0000), F32)
    return lo, hi


def _rms(x, g, n):
    ms = jnp.sum(x * x, axis=-1, keepdims=True) * (1.0 / n)
    return x * lax.rsqrt(ms + RMS_EPS) * g


def _proj_kernel(blocks_per_seq, x_ref, pos_ref, g_attn_ref, w_in_ref, inv_ref, wpool_ref,
                 bpool_ref, pscale_ref, gpool_ref, gq_ref, wq_ref, gkv_ref, wkv_ref,
                 pool_out, q_out, k_out, v_out, ext_scr):
    tm = x_ref.shape[0]
    d_model = x_ref.shape[1]
    d_pool = pool_out.shape[1]
    gdim = d_pool // N_POOL_GROUPS
    q_rank = gq_ref.shape[1]
    kv_rank = gkv_ref.shape[1]
    i = pl.program_id(0)
    blk_in_seq = i % blocks_per_seq

    x = x_ref[...]
    h = _rms(x, g_attn_ref[...], d_model).astype(BF16)
    proj = jnp.dot(h, w_in_ref[...], preferred_element_type=F32)
    u = proj[:, :d_pool]
    qlat = proj[:, d_pool:d_pool + q_rank]
    kvlat = proj[:, d_pool + q_rank:d_pool + q_rank + kv_rank]
    o_kr = d_pool + q_rank + kv_rank
    krb = proj[:, o_kr:o_kr + LANE]
    krs = proj[:, o_kr + LANE:]

    @pl.when(blk_in_seq == 0)
    def _():
        ext_scr[0:POOL_HALO, :] = jnp.zeros((POOL_HALO, d_pool), F32)

    @pl.when(blk_in_seq != 0)
    def _():
        ext_scr[0:POOL_HALO, :] = ext_scr[tm:tm + POOL_HALO, :]

    ext_scr[POOL_HALO:, :] = u
    t_in_seq = (blk_in_seq * tm + lax.broadcasted_iota(jnp.int32, (tm, gdim), 0) + 1).astype(F32)
    ys = []
    ssq = jnp.zeros((tm, 1), F32)
    for g, w in enumerate(POOL_WINDOWS):
        a = ext_scr[:, g * gdim:(g + 1) * gdim]
        shift = 1
        while shift < w:
            a = a + pltpu.roll(a, shift, axis=0)
            shift *= 2
        win = a[POOL_HALO:, :]
        cnt = jnp.minimum(t_in_seq, float(w))
        ug = u[:, g * gdim:(g + 1) * gdim]
        diff = (win / cnt - ug).astype(BF16)
        mixed = jnp.dot(diff, wpool_ref[g], preferred_element_type=F32) + bpool_ref[g]
        y = mixed * pscale_ref[g]
        ssq = ssq + jnp.sum(y * y, axis=-1, keepdims=True)
        ys.append(y)
    rinv = lax.rsqrt(ssq * (1.0 / d_pool) + RMS_EPS)
    for g in range(N_POOL_GROUPS):
        pool_out[:, g * gdim:(g + 1) * gdim] = (ys[g] * rinv * gpool_ref[g]).astype(pool_out.dtype)

    pos = pos_ref[0].astype(F32)
    ang_t = inv_ref[...] * pos
    cos_t = jnp.cos(ang_t)
    sin_t = jnp.sin(ang_t)
    half = QK_ROPE_DIM // 2
    ones = jnp.ones((QK_NOPE_DIM, tm), F32)
    c_t = jnp.concatenate([ones, cos_t, cos_t, jnp.ones((LANE - QK_DIM, tm), F32)], axis=0)
    s_t = jnp.concatenate([jnp.zeros((QK_NOPE_DIM, tm), F32), -sin_t, sin_t,
                           jnp.zeros((LANE - QK_DIM, tm), F32)], axis=0)
    c = c_t.T
    s = s_t.T

    def rope(blk, blk_swapped, c_, s_):
        return blk * c_ + blk_swapped * s_

    qn = _rms(qlat, gq_ref[...], q_rank).astype(BF16)
    q2 = jnp.dot(qn, wq_ref[...], preferred_element_type=F32)
    scale = QK_DIM ** -0.5 * LOG2_E
    cq, sq = c * scale, s * scale
    hp = N_HEADS * HEAD_PAD
    for hh in range(N_HEADS):
        sl = slice(hh * HEAD_PAD, (hh + 1) * HEAD_PAD)
        sw = slice(hp + hh * HEAD_PAD, hp + (hh + 1) * HEAD_PAD)
        q_out[:, sl] = rope(q2[:, sl], q2[:, sw], cq, sq).astype(q_out.dtype)

    kvn = _rms(kvlat, gkv_ref[...], kv_rank).astype(BF16)
    kv = jnp.dot(kvn, wkv_ref[...], preferred_element_type=F32)
    kr = rope(krb, krs, c, s)
    for hh in range(N_HEADS):
        sl = slice(hh * HEAD_PAD, (hh + 1) * HEAD_PAD)
        k_out[:, sl] = (kv[:, sl] + kr).astype(k_out.dtype)
    vlane = lax.broadcasted_iota(jnp.int32, (1, N_HEADS * HEAD_PAD), 1) % HEAD_PAD
    ones_col = (vlane == V_HEAD_DIM).astype(F32)
    v_out[0] = (kv[:, N_HEADS * HEAD_PAD:] + ones_col).T.astype(v_out.dtype)


def _proj_call(x2, pos3, g_attn, w_in_p, inv_col, wpool, bpool, pscale, gpool, gq, wq_p, gkv, wkv_p,
               seq, tm):
    t, d = x2.shape
    d_pool = pscale.shape[0] * pscale.shape[2]
    hp = N_HEADS * HEAD_PAD
    nblk = t // tm
    const2 = lambda i: (0, 0)
    const3 = lambda i: (0, 0, 0)
    in_specs = [
        pl.BlockSpec((tm, d), lambda i: (i, 0)),
        pl.BlockSpec((1, 1, tm), lambda i: (i, 0, 0)),
        pl.BlockSpec(g_attn.shape, const2),
        pl.BlockSpec(w_in_p.shape, const2),
        pl.BlockSpec(inv_col.shape, const2),
        pl.BlockSpec(wpool.shape, const3),
        pl.BlockSpec(bpool.shape, const3),
        pl.BlockSpec(pscale.shape, const3),
        pl.BlockSpec(gpool.shape, const3),
        pl.BlockSpec(gq.shape, const2),
        pl.BlockSpec(wq_p.shape, const2),
        pl.BlockSpec(gkv.shape, const2),
        pl.BlockSpec(wkv_p.shape, const2),
    ]
    out_shape = [
        jax.ShapeDtypeStruct((t, d_pool), BF16),
        jax.ShapeDtypeStruct((t, hp), BF16),
        jax.ShapeDtypeStruct((t, hp), BF16),
        jax.ShapeDtypeStruct((nblk, hp, tm), BF16),
    ]
    out_specs = [
        pl.BlockSpec((tm, d_pool), lambda i: (i, 0)),
        pl.BlockSpec((tm, hp), lambda i: (i, 0)),
        pl.BlockSpec((tm, hp), lambda i: (i, 0)),
        pl.BlockSpec((1, hp, tm), lambda i: (i, 0, 0)),
    ]
    return pl.pallas_call(
        functools.partial(_proj_kernel, seq // tm),
        grid=(nblk,),
        in_specs=in_specs,
        out_specs=out_specs,
        out_shape=out_shape,
        scratch_shapes=[pltpu.VMEM((tm + POOL_HALO, d_pool), F32)],
        compiler_params=pltpu.CompilerParams(dimension_semantics=("arbitrary",),
                                             vmem_limit_bytes=VMEM_LIMIT),
        name="proj",
    )(x2, pos3, g_attn, w_in_p, inv_col, wpool, bpool, pscale, gpool, gq, wq_p, gkv, wkv_p)


ROWSUM_ROW = V_HEAD_DIM


def _attn_kernel(q_ref, k_ref, vt_ref, o_ref, acc_ref):
    tq = q_ref.shape[0]
    th = vt_ref.shape[2]
    nsub = tq // th
    qi = pl.program_id(2)
    acc_ref[...] = jnp.zeros(acc_ref.shape, F32)

    def step(kc, m_prev, q_lo, masked):
        width = tq - q_lo
        start = pl.multiple_of(kc * th, th)
        k = k_ref[pl.ds(start, th), :]
        st = lax.dot_general(k, q_ref[q_lo:, :], (((1,), (1,)), ((), ())),
                             preferred_element_type=F32)
        if masked:
            key = lax.broadcasted_iota(jnp.int32, (th, width), 0)
            qry = lax.broadcasted_iota(jnp.int32, (th, width), 1)
            st = jnp.where(key <= qry, st, -jnp.inf)
        m_new = jnp.maximum(m_prev, jnp.max(st, axis=0, keepdims=True))
        alpha = jnp.exp2(m_prev - m_new)
        pt = jnp.exp2(st - m_new).astype(BF16)
        acc_ref[:, q_lo:] = alpha * acc_ref[:, q_lo:] + jnp.dot(vt_ref[kc], pt,
                                                                preferred_element_type=F32)
        return m_new

    def two_chunks(j, mp):
        return step(2 * j + 1, step(2 * j, mp, 0, False), 0, False)

    m = lax.fori_loop(0, (nsub // 2) * qi, two_chunks, jnp.full((1, tq), -jnp.inf, F32))
    for c in range(nsub):
        m = step(nsub * qi + c, m, c * th, True)[:, th:]

    acc = acc_ref[...]
    out = (acc / acc[ROWSUM_ROW:ROWSUM_ROW + 1, :]).T
    lane = lax.broadcasted_iota(jnp.int32, out.shape, 1)
    o_ref[...] = jnp.where(lane < V_HEAD_DIM, out, 0.0).astype(o_ref.dtype)


ATTN_Q_CHUNKS = 8


def _attn_call(q, k, vt, batch, seq, th):
    t, hp = q.shape
    tq = min(ATTN_Q_CHUNKS * th, seq)
    nq = seq // tq
    nchunk = seq // th
    return pl.pallas_call(
        _attn_kernel,
        grid=(batch, N_HEADS, nq),
        in_specs=[
            pl.BlockSpec((tq, HEAD_PAD), lambda b, h, qi: (b * nq + qi, h)),
            pl.BlockSpec((seq, HEAD_PAD), lambda b, h, qi: (b, h)),
            pl.BlockSpec((nchunk, HEAD_PAD, th), lambda b, h, qi: (b, h, 0)),
        ],
        out_specs=pl.BlockSpec((tq, HEAD_PAD), lambda b, h, qi: (b * nq + qi, h)),
        out_shape=jax.ShapeDtypeStruct((t, hp), F32),
        scratch_shapes=[pltpu.VMEM((HEAD_PAD, tq), F32)],
        compiler_params=pltpu.CompilerParams(
            dimension_semantics=("arbitrary", "arbitrary", "arbitrary"),
            vmem_limit_bytes=VMEM_LIMIT),
        name="attn",
    )(q, k, vt)


def _outproj_kernel(pool_ref, attn_ref, x_ref, gattn_ref, wop_ref, woa_ref, gffn_ref, wr_ref,
                    br_ref, x1_out, hp_out, idx_out, gate_out, rank_out, cnt_out, run_scr):
    tm, d = x_ref.shape
    d_attn = N_HEADS * V_HEAD_DIM
    i = pl.program_id(0)

    @pl.when(i == 0)
    def _():
        run_scr[...] = jnp.zeros(run_scr.shape, F32)

    attn = attn_ref[...]
    attn_n = _rms(attn, gattn_ref[...], d_attn).astype(BF16)
    mix = (jnp.dot(pool_ref[...], wop_ref[...], preferred_element_type=F32)
           + jnp.dot(attn_n, woa_ref[...], preferred_element_type=F32))
    x1 = x_ref[...] + mix
    x1_out[...] = x1
    h2 = _rms(x1, gffn_ref[...], d)

    hp_out[...] = _pack_bf16_pairs(h2)

    logits = lax.dot_general(wr_ref[...], h2, (((1,), (1,)), ((), ())),
                             preferred_element_type=F32,
                             precision=lax.Precision.HIGHEST) + br_ref[...]
    e_iota = lax.broadcasted_iota(jnp.int32, (N_EXPERTS, tm), 0)
    work = logits
    idxs, vals = [], []
    for _ in range(TOP_K):
        mx = jnp.max(work, axis=0, keepdims=True)
        sel = jnp.min(jnp.where(work == mx, e_iota, N_EXPERTS), axis=0, keepdims=True)
        idxs.append(sel)
        vals.append(mx)
        work = jnp.where(e_iota == sel, -jnp.inf, work)
    exps = [jnp.exp(v - vals[0]) for v in vals]
    denom = exps[0] + exps[1] + exps[2] + exps[3]
    gate_out[...] = jnp.concatenate([e / denom for e in exps], axis=0)
    idx_out[...] = jnp.concatenate(idxs, axis=0)

    onehot = jnp.zeros((N_EXPERTS, tm), F32)
    for sel in idxs:
        onehot = onehot + (e_iota == sel).astype(F32)
    r_io = lax.broadcasted_iota(jnp.int32, (tm, tm), 0)
    c_io = lax.broadcasted_iota(jnp.int32, (tm, tm), 1)
    upper = (r_io < c_io).astype(BF16)
    prefix = jnp.dot(onehot.astype(BF16), upper, preferred_element_type=F32) + run_scr[:, 0:1]
    ranks = [jnp.sum(jnp.where(e_iota == sel, prefix, 0.0), axis=0, keepdims=True) for sel in idxs]
    rank_out[...] = jnp.concatenate(ranks, axis=0).astype(jnp.int32)
    run_new = run_scr[...] + jnp.sum(onehot, axis=1, keepdims=True)
    run_scr[...] = run_new
    cnt_out[...] = run_new.astype(jnp.int32)


def _outproj_call(pool_n, attn, x2, gattn_p, wo_pool, wo_attn_p, gffn, wr_t, br_col, tm):
    t, d = x2.shape
    nblk = t // tm
    const2 = lambda i: (0, 0)
    row = lambda i: (i, 0)
    col = lambda i: (0, i)
    in_specs = [
        pl.BlockSpec((tm, pool_n.shape[1]), row),
        pl.BlockSpec((tm, attn.shape[1]), row),
        pl.BlockSpec((tm, d), row),
        pl.BlockSpec(gattn_p.shape, const2),
        pl.BlockSpec(wo_pool.shape, const2),
        pl.BlockSpec(wo_attn_p.shape, const2),
        pl.BlockSpec(gffn.shape, const2),
        pl.BlockSpec(wr_t.shape, const2),
        pl.BlockSpec(br_col.shape, const2),
    ]
    out_shape = [
        jax.ShapeDtypeStruct((t, d), F32),
        jax.ShapeDtypeStruct((t, d // 2), jnp.uint32),
        jax.ShapeDtypeStruct((TOP_K, t), jnp.int32),
        jax.ShapeDtypeStruct((TOP_K, t), F32),
        jax.ShapeDtypeStruct((TOP_K, t), jnp.int32),
        jax.ShapeDtypeStruct((N_EXPERTS, LANE), jnp.int32),
    ]
    out_specs = [
        pl.BlockSpec((tm, d), row),
        pl.BlockSpec((tm, d // 2), row),
        pl.BlockSpec((TOP_K, tm), col),
        pl.BlockSpec((TOP_K, tm), col),
        pl.BlockSpec((TOP_K, tm), col),
        pl.BlockSpec((N_EXPERTS, LANE), const2),
    ]
    return pl.pallas_call(
        _outproj_kernel,
        grid=(nblk,),
        in_specs=in_specs,
        out_specs=out_specs,
        out_shape=out_shape,
        scratch_shapes=[pltpu.VMEM((N_EXPERTS, LANE), F32)],
        compiler_params=pltpu.CompilerParams(dimension_semantics=("arbitrary",),
                                             vmem_limit_bytes=VMEM_LIMIT),
        name="outproj",
    )(pool_n, attn, x2, gattn_p, wo_pool, wo_attn_p, gffn, wr_t, br_col)


SC_WINDOW = 64


def _window_indices(dest_kt):
    t = dest_kt.shape[1]
    return dest_kt.reshape(TOP_K, t // SC_WINDOW, SC_WINDOW).transpose(1, 0, 2)


def _dispatch_sc_call(idx, hp_rows, n_rows):
    t, w = hp_rows.shape
    mesh = plsc.VectorSubcoreMesh(core_axis_name="core", subcore_axis_name="subcore")

    @functools.partial(pl.kernel, out_type=jax.ShapeDtypeStruct((n_rows, w), hp_rows.dtype),
                       mesh=mesh, scratch_types=[])
    def scatter_kernel(x_hbm, i_hbm, o_hbm):
        def body(x_vmem, i_vmem):
            for k in range(TOP_K):
                pltpu.sync_copy(x_vmem, o_hbm.at[i_vmem.at[0, k]])

        pltpu.emit_pipeline(
            body,
            grid=(t // SC_WINDOW,),
            in_specs=[pl.BlockSpec((SC_WINDOW, w), lambda i: (i, 0)),
                      pl.BlockSpec((1, TOP_K, SC_WINDOW), lambda i: (i, 0, 0))],
            out_specs=[],
            core_axis_name=("core", "subcore"),
            dimension_semantics=(pltpu.PARALLEL,),
        )(x_hbm, i_hbm)

    return scatter_kernel(hp_rows, idx)


def _gather_sc_call(idx, rows):
    nwin, _, _ = idx.shape
    t = nwin * SC_WINDOW
    wd = rows.shape[1]
    mesh = plsc.VectorSubcoreMesh(core_axis_name="core", subcore_axis_name="subcore")

    @functools.partial(pl.kernel, out_type=jax.ShapeDtypeStruct((TOP_K, t, wd), rows.dtype),
                       mesh=mesh, scratch_types=[])
    def gather_kernel(y_hbm, i_hbm, o_hbm):
        def body(i_vmem, o_vmem):
            pltpu.sync_copy(y_hbm.at[i_vmem.at[0, 0]], o_vmem.at[0])

        pltpu.emit_pipeline(
            body,
            grid=(nwin, TOP_K),
            in_specs=[pl.BlockSpec((1, 1, SC_WINDOW), lambda i, k: (i * TOP_K + k, 0, 0))],
            out_specs=[pl.BlockSpec((1, SC_WINDOW, wd), lambda i, k: (k, i, 0))],
            core_axis_name=("core", "subcore"),
            dimension_semantics=(pltpu.PARALLEL, pltpu.PARALLEL),
        )(i_hbm, o_hbm)

    return gather_kernel(rows, idx.reshape(nwin * TOP_K, 1, SC_WINDOW))


def _combine_sum_kernel(x1_ref, gate_ref, gfin_ref, yg_ref, o_ref):
    d = x1_ref.shape[1]
    g = gate_ref[...]
    y = x1_ref[...]
    for k in range(TOP_K):
        y = y + g[:, k:k + 1] * jnp.concatenate(_unpack_bf16_pairs(yg_ref[k]), axis=1)
    o_ref[...] = _rms(y, gfin_ref[...], d)


def _combine_sum_call(x1, gates_t, gfin, yg, tf):
    t, d = x1.shape
    return pl.pallas_call(
        _combine_sum_kernel,
        grid=(t // tf,),
        in_specs=[
            pl.BlockSpec((tf, d), lambda i: (i, 0)),
            pl.BlockSpec((tf, TOP_K), lambda i: (i, 0)),
            pl.BlockSpec(gfin.shape, lambda i: (0, 0)),
            pl.BlockSpec((TOP_K, tf, d // 2), lambda i: (0, i, 0)),
        ],
        out_specs=pl.BlockSpec((tf, d), lambda i: (i, 0)),
        out_shape=jax.ShapeDtypeStruct((t, d), F32),
        compiler_params=pltpu.CompilerParams(dimension_semantics=("arbitrary",),
                                             vmem_limit_bytes=VMEM_LIMIT),
        name="combine",
    )(x1, gates_t, gfin, yg)


CAST_ROWS = 64


def _expert_kernel(bs_ref, nblk_ref, cnt_ref, bgu_ref, bd_ref, wgu_hbm, wd_hbm, xs_hbm, y_hbm,
                   wgu_f32, wd_f32, wgu_bf, wd_bf, xbuf, ybuf, xsem, ysem, wsem):
    e = pl.program_id(0)
    n_exp = pl.num_programs(0)
    d = wgu_f32.shape[1]
    f = wd_f32.shape[1]
    bm = xbuf.shape[1]
    n = nblk_ref[e]
    blk0 = bs_ref[e]
    wslot = e % 2

    def w_copies(ex, slot):
        return (pltpu.make_async_copy(wgu_hbm.at[ex], wgu_f32.at[slot], wsem.at[0, slot]),
                pltpu.make_async_copy(wd_hbm.at[ex], wd_f32.at[slot], wsem.at[1, slot]))

    n_valid = bs_ref[n_exp - 1] + nblk_ref[n_exp - 1]

    def x_copy(g):
        row0 = pl.multiple_of(g * bm, bm)
        return pltpu.make_async_copy(xs_hbm.at[pl.ds(row0, bm)], xbuf.at[g % 2], xsem.at[g % 2])

    def y_copy(g):
        row0 = pl.multiple_of(g * bm, bm)
        return pltpu.make_async_copy(ybuf.at[g % 2], y_hbm.at[pl.ds(row0, bm)], ysem.at[g % 2])

    @pl.when(e == 0)
    def _():
        x_copy(0).start()
        for cp in w_copies(0, 0):
            cp.start(priority=1)

    @pl.when(e + 1 < n_exp)
    def _():
        for cp in w_copies(e + 1, 1 - wslot):
            cp.start(priority=1)

    for cp in w_copies(e, wslot):
        cp.wait()

    @pl.when(n > 0)
    def _():
        def cast_gu(r, c):
            rows = pl.ds(pl.multiple_of(r * CAST_ROWS, CAST_ROWS), CAST_ROWS)
            wgu_bf[rows, :] = wgu_f32[wslot, rows, :].astype(BF16)
            return c

        def cast_d(r, c):
            rows = pl.ds(pl.multiple_of(r * CAST_ROWS, CAST_ROWS), CAST_ROWS)
            wd_bf[rows, :] = wd_f32[wslot, rows, :].astype(BF16)
            return c

        def block(j, c):
            g = blk0 + j
            slot = g % 2
            x_copy(g).wait()

            @pl.when(g + 1 < n_valid)
            def _():
                x_copy(g + 1).start()

            @pl.when(g >= 2)
            def _():
                y_copy(g - 2).wait()

            row = lax.broadcasted_iota(jnp.int32, (bm, 1), 0)
            w = jnp.where(row < cnt_ref[e] - j * bm, xbuf[slot], jnp.uint32(0))
            x_lo, x_hi = _unpack_bf16_pairs(w)
            gu = (jnp.dot(x_lo, wgu_f32[wslot, :d // 2, :], preferred_element_type=F32)
                  + jnp.dot(x_hi, wgu_f32[wslot, d // 2:, :], preferred_element_type=F32)
                  + bgu_ref[0])
            gate = jnp.minimum(gu[:, :f], SWIGLU_LIMIT)
            up = jnp.clip(gu[:, f:], -SWIGLU_LIMIT, SWIGLU_LIMIT)
            act = gate * jax.nn.sigmoid(SWIGLU_ALPHA * gate) * (up + 1.0)
            y = jnp.dot(act, wd_f32[wslot], preferred_element_type=F32) + bd_ref[0]
            ybuf[slot] = _pack_bf16_pairs(y)
            y_copy(g).start()
            return c

        lax.fori_loop(0, n, block, 0)

    @pl.when(e == n_exp - 1)
    def _():
        @pl.when(n_valid >= 2)
        def _():
            y_copy(n_valid - 2).wait()

        @pl.when(n_valid >= 1)
        def _():
            y_copy(n_valid - 1).wait()

        total = y_hbm.shape[0] // bm
        ybuf[0] = jnp.zeros(ybuf.shape[1:], ybuf.dtype)

        def z_copy(b):
            row0 = pl.multiple_of(b * bm, bm)
            return pltpu.make_async_copy(ybuf.at[0], y_hbm.at[pl.ds(row0, bm)], ysem.at[0])

        def fill(b, c):
            z_copy(b).start()
            return c

        def drain(b, c):
            z_copy(b).wait()
            return c

        lax.fori_loop(n_valid, total, fill, 0)
        lax.fori_loop(n_valid, total, drain, 0)


def _expert_call(blk_start, nblk, cnt, xs, wgu, bgu, wd, bd, bm):
    e, d, f2 = wgu.shape
    f = wd.shape[1]
    r, dw = xs.shape

    def bsel(i, bs, nb, ct):
        return (i, 0, 0)

    grid_spec = pltpu.PrefetchScalarGridSpec(
        num_scalar_prefetch=3,
        grid=(e,),
        in_specs=[
            pl.BlockSpec((1, 1, f2), bsel),
            pl.BlockSpec((1, 1, d), bsel),
            pl.BlockSpec(memory_space=pl.ANY),
            pl.BlockSpec(memory_space=pl.ANY),
            pl.BlockSpec(memory_space=pl.ANY),
        ],
        out_specs=pl.BlockSpec(memory_space=pl.ANY),
        scratch_shapes=[pltpu.VMEM((2, d, f2), F32), pltpu.VMEM((2, f, d), F32),
                        pltpu.VMEM((d, f2), BF16), pltpu.VMEM((f, d), BF16),
                        pltpu.VMEM((2, bm, dw), xs.dtype),
                        pltpu.VMEM((2, bm, dw), jnp.uint32),
                        pltpu.SemaphoreType.DMA((2,)), pltpu.SemaphoreType.DMA((2,)),
                        pltpu.SemaphoreType.DMA((2, 2))],
    )
    return pl.pallas_call(
        _expert_kernel,
        grid_spec=grid_spec,
        out_shape=jax.ShapeDtypeStruct((r, dw), jnp.uint32),
        compiler_params=pltpu.CompilerParams(dimension_semantics=("arbitrary",),
                                             vmem_limit_bytes=VMEM_LIMIT),
        name="experts",
    )(blk_start, nblk, cnt, bgu, bd, wgu, wd, xs)


def _pad_head_cols(w, widths_in, place):
    k = w.shape[0]
    per = sum(widths_in)
    blocks = []
    for hh in range(N_HEADS):
        blocks.append(place(w[:, hh * per:(hh + 1) * per]))
    return jnp.concatenate(blocks, axis=1)


def _layer(x2, pos3, batch, seq, g_attn_norm, w_in, w_pool, b_pool, pool_scale, g_q_a, w_q_b,
           g_kv_a, w_kv_b, g_out_pool, g_out_attn, w_out, g_ffn_norm, w_router, b_router,
           w_gate_up, b_gate_up, w_down, b_down, tiles):
    tm, tmo, bm, tf = tiles
    t, d = x2.shape
    d_pool = pool_scale.shape[0]
    gdim = d_pool // N_POOL_GROUPS
    q_rank = g_q_a.shape[0]
    kv_rank = g_kv_a.shape[0]
    half = QK_ROPE_DIM // 2

    o1, o2, o3 = d_pool, d_pool + q_rank, d_pool + q_rank + kv_rank
    zk = lambda n: jnp.zeros((d, n), F32)
    w_kr1, w_kr2 = w_in[:, o3:o3 + half], w_in[:, o3 + half:]
    w_in_p = jnp.concatenate([w_in[:, :o3], zk(QK_NOPE_DIM), w_kr1, w_kr2, zk(LANE - QK_DIM),
                              zk(QK_NOPE_DIM), w_kr2, w_kr1, zk(LANE - QK_DIM)], axis=1).astype(BF16)
    zq = lambda n: jnp.zeros((q_rank, n), F32)
    wq_plain = _pad_head_cols(w_q_b, (QK_DIM,),
                              lambda c: jnp.concatenate([c, zq(LANE - QK_DIM)], axis=1))
    wq_swap = _pad_head_cols(w_q_b, (QK_DIM,), lambda c: jnp.concatenate(
        [zq(QK_NOPE_DIM), c[:, QK_NOPE_DIM + half:], c[:, QK_NOPE_DIM:QK_NOPE_DIM + half],
         zq(LANE - QK_DIM)], axis=1))
    wq_p = jnp.concatenate([wq_plain, wq_swap], axis=1).astype(BF16)
    zkv = lambda n: jnp.zeros((kv_rank, n), F32)
    per = QK_NOPE_DIM + V_HEAD_DIM
    wk_p = _pad_head_cols(w_kv_b, (per,), lambda c: jnp.concatenate(
        [c[:, :QK_NOPE_DIM], zkv(LANE - QK_NOPE_DIM)], axis=1))
    wv_p = _pad_head_cols(w_kv_b, (per,), lambda c: jnp.concatenate(
        [c[:, QK_NOPE_DIM:], zkv(LANE - V_HEAD_DIM)], axis=1))
    wkv_p = jnp.concatenate([wk_p, wv_p], axis=1).astype(BF16)
    inv = ROPE_THETA ** (-jnp.arange(half, dtype=F32) / half)
    inv_col = inv.reshape(half, 1)

    row = lambda v: v.reshape(1, -1)
    grp = lambda v: v.reshape(N_POOL_GROUPS, 1, gdim)

    pool_n, q, k, v = _proj_call(
        x2, pos3, row(g_attn_norm), w_in_p, inv_col, w_pool.astype(BF16), grp(b_pool),
        grp(pool_scale), grp(g_out_pool), row(g_q_a), wq_p, row(g_kv_a), wkv_p, seq, tm)

    attn = _attn_call(q, k, v, batch, seq, tm)

    d_attn = N_HEADS * V_HEAD_DIM
    wo_pool = w_out[:d_pool].astype(BF16)
    wo_attn = w_out[d_pool:].reshape(N_HEADS, V_HEAD_DIM, d)
    wo_attn_p = jnp.concatenate([wo_attn, jnp.zeros((N_HEADS, HEAD_PAD - V_HEAD_DIM, d), F32)],
                                axis=1).reshape(N_HEADS * HEAD_PAD, d).astype(BF16)
    gattn_p = jnp.concatenate([g_out_attn.reshape(N_HEADS, V_HEAD_DIM),
                               jnp.zeros((N_HEADS, HEAD_PAD - V_HEAD_DIM), F32)], axis=1).reshape(1, -1)

    x1, hp, idx, gates, rank, counts = _outproj_call(
        pool_n, attn, x2, gattn_p, wo_pool, wo_attn_p, row(g_ffn_norm), w_router.T,
        b_router.reshape(-1, 1), tmo)

    n_assign = t * TOP_K
    cnt = counts[:, 0]
    padded = ((cnt + bm - 1) // bm) * bm
    pad_end = jnp.cumsum(padded)
    pad_start = pad_end - padded
    e_ids = jnp.arange(N_EXPERTS, dtype=jnp.int32)
    dest = rank + jnp.sum(jnp.where(idx[None] == e_ids[:, None, None], pad_start[:, None, None], 0),
                          axis=0)
    win_idx = _window_indices(dest)
    n_rows = (n_assign // bm + N_EXPERTS) * bm
    blk_start = (pad_start // bm).astype(jnp.int32)
    nblk = (padded // bm).astype(jnp.int32)

    xs = _dispatch_sc_call(win_idx, hp, n_rows)
    yb = _expert_call(blk_start, nblk, cnt.astype(jnp.int32), xs, w_gate_up,
                      b_gate_up.reshape(N_EXPERTS, 1, -1), w_down, b_down.reshape(N_EXPERTS, 1, -1), bm)
    yg = _gather_sc_call(win_idx, yb)
    return x1, gates, yg


def _forward(x, positions, g_attn_norm, w_in, w_pool, b_pool, pool_scale, g_q_a, w_q_b, g_kv_a,
             w_kv_b, g_out_pool, g_out_attn, w_out, g_ffn_norm, w_router, b_router, w_gate_up,
             b_gate_up, w_down, b_down, g_final, tiles):
    batch, seq, d = x.shape
    t = batch * seq
    tm, tmo, bm, tf = tiles
    x2 = x.reshape(t, d)
    pos3 = positions.reshape(t // tm, 1, tm)
    depth = g_attn_norm.shape[0]
    assert depth == 1
    l = 0
    x1, gates, yg = _layer(
        x2, pos3, batch, seq, g_attn_norm[l], w_in[l], w_pool[l], b_pool[l], pool_scale[l],
        g_q_a[l], w_q_b[l], g_kv_a[l], w_kv_b[l], g_out_pool[l], g_out_attn[l], w_out[l],
        g_ffn_norm[l], w_router[l], b_router[l], w_gate_up[l], b_gate_up[l], w_down[l],
        b_down[l], tiles)
    out = _combine_sum_call(x1, gates.T, g_final.reshape(1, -1), yg, tf)
    return out.reshape(batch, seq, d)


def _tiles_for(seq):
    tm = min(512, seq // 2)
    tmo = min(512, seq)
    bm = 256 if seq >= 2048 else 128
    tf = min(512, seq)
    return (tm, tmo, bm, tf)


def kernel(x, positions, g_attn_norm, w_in, w_pool, b_pool, pool_scale, g_q_a, w_q_b, g_kv_a, w_kv_b, g_out_pool, g_out_attn, w_out, g_ffn_norm, w_router, b_router, w_gate_up, b_gate_up, w_down, b_down, g_final):
    tiles = _tiles_for(x.shape[1])
    return _forward(x, positions, g_attn_norm, w_in, w_pool, b_pool, pool_scale, g_q_a, w_q_b,
                    g_kv_a, w_kv_b, g_out_pool, g_out_attn, w_out, g_ffn_norm, w_router, b_router,
                    w_gate_up, b_gate_up, w_down, b_down, g_final, tiles)
```
